```python
import math
import jax, jax.numpy as jnp
from jax import lax
import numpy as np

D_MODEL = 1024
BATCH = 8
SEQ = 4096
DEPTH = 2

HEAD_DIM = 64
N_META = 16
BLOCK_Q = 128
SB_HEADS = 4
SP_HEADS = 4
IDX_HEADS = 8
IDX_DIM = 32
TOPK_MAX = 256
DF_HEADS = 4
N_BUCKETS = 32
MAX_DISTANCE = 128
N_BIAS_HEADS = SP_HEADS + DF_HEADS
D_FF = 2816
CONV_WIDTH = 3
EPS = 1e-6
NEG_INF = -1e30

SB_W = SB_HEADS * HEAD_DIM
SP_W = SP_HEADS * HEAD_DIM
DF_W = DF_HEADS * 2 * HEAD_DIM
SPLITS = (SB_W, SB_W, SB_W,
          SP_W, SP_W, SP_W,
          IDX_HEADS * IDX_DIM, IDX_DIM, IDX_HEADS,
          DF_W, DF_W, DF_W,
          D_MODEL, D_MODEL, D_MODEL)
D_IN = sum(SPLITS)

kernel_name = 'hybrid_gated_sb_dsa_diff_convglu'


def _rmsnorm(a, gain):
    af = a.astype(jnp.float32)
    af = af * lax.rsqrt(jnp.mean(af * af, axis=-1, keepdims=True) + EPS)
    return (af * gain.astype(jnp.float32)).astype(a.dtype)


def _split_cols(z):
    out, off = [], 0
    for w in SPLITS:
        out.append(z[..., off:off + w])
        off += w
    return out


def _to_blocks(a):
    b, tp = a.shape[:2]
    return jnp.moveaxis(a.reshape((b, tp // BLOCK_Q, BLOCK_Q) + a.shape[2:]), 1, 0)


def _from_blocks(a):
    nb, b = a.shape[:2]
    return jnp.moveaxis(a, 0, 1).reshape((b, nb * BLOCK_Q) + a.shape[3:])


def _t5_bucket(rel):
    n = jnp.maximum(rel, 0)
    max_exact = N_BUCKETS // 2
    nf = jnp.maximum(n, 1).astype(jnp.float32)
    large = max_exact + (jnp.log(nf / max_exact) / math.log(MAX_DISTANCE / max_exact)
                         * (N_BUCKETS - max_exact)).astype(jnp.int32)
    return jnp.where(n < max_exact, n, jnp.minimum(large, N_BUCKETS - 1))


def _stick_breaking(q, k, v):
    tp, dh = q.shape[1], q.shape[-1]
    scale = dh ** -0.5
    kf, vf = k.astype(jnp.float32), v.astype(jnp.float32)
    key_pos = jnp.arange(tp)

    def block(args):
        qb, start = args
        qpos = start + jnp.arange(BLOCK_Q)
        z = jnp.einsum('bqhd,bkhd->bhqk', qb.astype(jnp.float32), kf) * scale
        mask = key_pos[None, :] < qpos[:, None]
        log_1m_beta = jnp.where(mask, jax.nn.log_sigmoid(-z), 0.0)
        between = lax.cumsum(log_1m_beta, axis=3, reverse=True) - log_1m_beta
        w = jnp.where(mask, jnp.exp(jax.nn.log_sigmoid(z) + between), 0.0)
        return jnp.einsum('bhqk,bkhd->bqhd', w, vf)

    nb = tp // BLOCK_Q
    out = lax.map(block, (_to_blocks(q), jnp.arange(nb) * BLOCK_Q))
    return _from_blocks(out).astype(v.dtype)


def _indexed_sparse_attention(q, k, v, q_ix, k_ix, w_ix, bias_table, top_k):
    tp, h, dh = q.shape[1], q.shape[2], q.shape[3]
    scale = dh ** -0.5
    k_ixf = k_ix.astype(jnp.float32)
    key_pos = jnp.arange(tp)
    table = bias_table.astype(jnp.float32)

    def block(args):
        qb, qib, wb, start = args
        qpos = start + jnp.arange(BLOCK_Q)
        dots = jnp.einsum('bqjd,bkd->bqjk', qib.astype(jnp.float32), k_ixf)
        score = jnp.einsum('bqj,bqjk->bqk', wb.astype(jnp.float32), jax.nn.relu(dots))
        score = jnp.where((key_pos[None, :] <= qpos[:, None])[None], score, NEG_INF)
        _, sel = lax.top_k(score, top_k)
        rel = qpos[None, :, None] - sel
        valid = rel >= 0
        k_sel = jax.vmap(lambda kk, ii: kk[ii])(k, sel)
        v_sel = jax.vmap(lambda vv, ii: vv[ii])(v, sel)
        s = jnp.einsum('bqhd,bqkhd->bhqk', qb.astype(jnp.float32),
                       k_sel.astype(jnp.float32)) * scale
        s = s + jnp.moveaxis(table[_t5_bucket(rel)], -1, 1)
        s = jnp.where(valid[:, None], s, NEG_INF)
        p = jax.nn.softmax(s, axis=-1)
        return jnp.einsum('bhqk,bqkhd->bqhd', p, v_sel.astype(jnp.float32))

    nb = tp // BLOCK_Q
    out = lax.map(block, (_to_blocks(q), _to_blocks(q_ix), _to_blocks(w_ix),
                          jnp.arange(nb) * BLOCK_Q))
    return _from_blocks(out).astype(v.dtype)


def _differential_attention(q1, q2, k1, k2, v, lam, bias_table):
    tp, dh = q1.shape[1], q1.shape[-1]
    scale = dh ** -0.5
    k1f, k2f, vf = k1.astype(jnp.float32), k2.astype(jnp.float32), v.astype(jnp.float32)
    key_pos = jnp.arange(tp)
    table = bias_table.astype(jnp.float32)

    def block(args):
        q1b, q2b, start = args
        qpos = start + jnp.arange(BLOCK_Q)
        mask = key_pos[None, :] <= qpos[:, None]
        bias = jnp.moveaxis(table[_t5_bucket(qpos[:, None] - key_pos[None, :])], -1, 0)

        def attn_map(qb, kf):
            s = jnp.einsum('bqhd,bkhd->bhqk', qb.astype(jnp.float32), kf) * scale + bias
            return jax.nn.softmax(jnp.where(mask, s, NEG_INF), axis=-1)

        p = attn_map(q1b, k1f) - lam * attn_map(q2b, k2f)
        return jnp.einsum('bhqk,bkhd->bqhd', p, vf)

    nb = tp // BLOCK_Q
    out = lax.map(block, (_to_blocks(q1), _to_blocks(q2), jnp.arange(nb) * BLOCK_Q))
    return _from_blocks(out).astype(v.dtype)


def _mixer(u, layer, w_in, b_gate, q_norm_sp, k_norm_sp, q_norm_df, k_norm_df,
           lam_q1, lam_k1, lam_q2, lam_k2, subln_df, w_br_sb, w_br_sp, w_br_df, w_out,
           rel_bias, top_k):
    b, tp, _ = u.shape
    (q_sb, k_sb, v_sb, q_sp, k_sp, v_sp, q_ix, k_ix, w_ix,
     q_df, k_df, v_df, g_sb, g_sp, g_df) = _split_cols(u @ w_in)

    def hd(a, n):
        return a.reshape(b, tp, n, -1)

    y_sb = _stick_breaking(hd(q_sb, SB_HEADS), hd(k_sb, SB_HEADS), hd(v_sb, SB_HEADS))
    y_sp = _indexed_sparse_attention(_rmsnorm(hd(q_sp, SP_HEADS), q_norm_sp),
                                     _rmsnorm(hd(k_sp, SP_HEADS), k_norm_sp),
                                     hd(v_sp, SP_HEADS), hd(q_ix, IDX_HEADS), k_ix, w_ix,
                                     rel_bias[:, :SP_HEADS], top_k)
    q_df = _rmsnorm(q_df.reshape(b, tp, DF_HEADS, 2, HEAD_DIM), q_norm_df)
    k_df = _rmsnorm(k_df.reshape(b, tp, DF_HEADS, 2, HEAD_DIM), k_norm_df)
    lam_init = 0.8 - 0.6 * math.exp(-0.3 * layer)
    lam = (jnp.exp(jnp.sum(lam_q1.astype(jnp.float32) * lam_k1.astype(jnp.float32)))
           - jnp.exp(jnp.sum(lam_q2.astype(jnp.float32) * lam_k2.astype(jnp.float32)))
           + lam_init)
    y_df = _differential_attention(q_df[..., 0, :], q_df[..., 1, :], k_df[..., 0, :],
                                   k_df[..., 1, :], hd(v_df, DF_HEADS), lam,
                                   rel_bias[:, SP_HEADS:])
    y_df = _rmsnorm(y_df, subln_df) * (1.0 - lam_init)

    merged = (jax.nn.sigmoid(g_sb + b_gate[:D_MODEL]) * (y_sb.reshape(b, tp, -1) @ w_br_sb)
              + jax.nn.sigmoid(g_sp + b_gate[D_MODEL:2 * D_MODEL]) * (y_sp.reshape(b, tp, -1) @ w_br_sp)
              + jax.nn.sigmoid(g_df + b_gate[2 * D_MODEL:]) * (y_df.reshape(b, tp, -1) @ w_br_df))
    return merged @ w_out


def _conv_ffn(u, w_up, conv_w, conv_b, w_down):
    tp = u.shape[1]
    z = u @ w_up
    gate, val = z[..., :D_FF], z[..., D_FF:]
    gp = jnp.pad(gate, ((0, 0), (CONV_WIDTH - 1, 0), (0, 0)))
    conv = conv_b + sum(gp[:, i:i + tp] * conv_w[i] for i in range(CONV_WIDTH))
    return (jax.nn.silu(conv) * val) @ w_down


def setup_inputs(seed: int = 0) -> dict:
    key = jax.random.key(seed)
    ks = jax.random.split(key, 24)

    def n(k, shape, s):
        return jax.random.normal(k, shape, jnp.float32) * s

    return {
        'x': n(ks[0], (BATCH, SEQ, D_MODEL), 1.0),
        'meta_tokens': n(ks[1], (N_META, D_MODEL), 1.0),
        'rel_bias': n(ks[2], (N_BUCKETS, N_BIAS_HEADS), 0.5),
        'attn_norm': 1.0 + n(ks[3], (DEPTH, D_MODEL), 0.01),
        'w_in': n(ks[4], (DEPTH, D_MODEL, D_IN), D_MODEL ** -0.5),
        'b_gate': n(ks[5], (DEPTH, 3 * D_MODEL), 0.02),
        'q_norm_sp': 1.0 + n(ks[6], (DEPTH, HEAD_DIM), 0.01),
        'k_norm_sp': 1.0 + n(ks[7], (DEPTH, HEAD_DIM), 0.01),
        'q_norm_df': 1.0 + n(ks[8], (DEPTH, HEAD_DIM), 0.01),
        'k_norm_df': 1.0 + n(ks[9], (DEPTH, HEAD_DIM), 0.01),
        'lam_q1': n(ks[10], (DEPTH, HEAD_DIM), 0.1),
        'lam_k1': n(ks[11], (DEPTH, HEAD_DIM), 0.1),
        'lam_q2': n(ks[12], (DEPTH, HEAD_DIM), 0.1),
        'lam_k2': n(ks[13], (DEPTH, HEAD_DIM), 0.1),
        'subln_df': 1.0 + n(ks[14], (DEPTH, 2 * HEAD_DIM), 0.01),
        'w_br_sb': n(ks[15], (DEPTH, SB_W, D_MODEL), SB_W ** -0.5),
        'w_br_sp': n(ks[16], (DEPTH, SP_W, D_MODEL), SP_W ** -0.5),
        'w_br_df': n(ks[17], (DEPTH, DF_W, D_MODEL), DF_W ** -0.5),
        'w_out': n(ks[18], (DEPTH, D_MODEL, D_MODEL), D_MODEL ** -0.5),
        'ffn_norm': 1.0 + n(ks[19], (DEPTH, D_MODEL), 0.01),
        'w_up': n(ks[20], (DEPTH, D_MODEL, 2 * D_FF), D_MODEL ** -0.5),
        'conv_w': n(ks[21], (DEPTH, CONV_WIDTH, D_FF), CONV_WIDTH ** -0.5),
        'conv_b': n(ks[22], (DEPTH, D_FF), 0.02),
        'w_down': n(ks[23], (DEPTH, D_FF, D_MODEL), D_FF ** -0.5),
    }


def reference(x, meta_tokens, rel_bias, attn_norm, w_in, b_gate, q_norm_sp, k_norm_sp,
              q_norm_df, k_norm_df, lam_q1, lam_k1, lam_q2, lam_k2, subln_df, w_br_sb,
              w_br_sp, w_br_df, w_out, ffn_norm, w_up, conv_w, conv_b, w_down):
    b, s, d = x.shape
    t = N_META + s
    tp = -(-t // BLOCK_Q) * BLOCK_Q
    top_k = min(TOPK_MAX, t // 4)
    meta = jnp.broadcast_to(meta_tokens[None].astype(x.dtype), (b, N_META, d))
    h = jnp.concatenate([meta, x, jnp.zeros((b, tp - t, d), x.dtype)], axis=1)
    for l in range(DEPTH):
        h = h + _mixer(_rmsnorm(h, attn_norm[l]), l, w_in[l], b_gate[l], q_norm_sp[l],
                       k_norm_sp[l], q_norm_df[l], k_norm_df[l], lam_q1[l], lam_k1[l],
                       lam_q2[l], lam_k2[l], subln_df[l], w_br_sb[l], w_br_sp[l],
                       w_br_df[l], w_out[l], rel_bias, top_k)
        h = h + _conv_ffn(_rmsnorm(h, ffn_norm[l]), w_up[l], conv_w[l], conv_b[l], w_down[l])
    return h[:, N_META:t]
```

```python
import functools
import math

import numpy as np
import jax
import jax.numpy as jnp
from jax import lax
from jax.experimental import pallas as pl
from jax.experimental.pallas import tpu as pltpu

D_MODEL = 1024
HEAD_DIM = 64
N_META = 16
BLK = 128
SB_HEADS = 4
SP_HEADS = 4
IDX_HEADS = 8
IDX_DIM = 32
TOPK_MAX = 256
DF_HEADS = 4
N_BUCKETS = 32
MAX_DISTANCE = 128
D_FF = 2816
EPS = 1e-6
M_INIT = -1e30
INT_MIN = -2 ** 31

G_SB, G_SP, G_DF = 0, 1024, 2048
Q_SB, K_SB, V_SB = 3072, 3328, 3584
Q_SP, K_SP, V_SP = 3840, 4096, 4352
Q_IX, KW_IX = 4608, 4864
Q_DF, K_DF, V_DF = 5120, 5632, 6144
NZ = 6656
W_IX_LANE = IDX_DIM

VMEM_LIMIT = 56 * 1024 * 1024

F32 = jnp.float32
BF16 = jnp.bfloat16
NT_DIMS = (((1,), (1,)), ((), ()))


def _nt_dot(a, b):
    return lax.dot_general(a, b, NT_DIMS, preferred_element_type=F32)


def _dot(a, b):
    return jnp.dot(a, b, preferred_element_type=F32)


def _params(*sem):
    return pltpu.CompilerParams(dimension_semantics=sem, vmem_limit_bytes=VMEM_LIMIT)


def _pick_rows(m, cap):
    for c in (2048, 1024, 512, 384, 256, 128):
        if c <= cap and m % c == 0:
            return c
    raise ValueError(f"row count {m} is not a multiple of {BLK}")


def _bucket_np(rel):
    n = np.maximum(rel, 0)
    max_exact = N_BUCKETS // 2
    nf = np.maximum(n, 1).astype(np.float32)
    large = max_exact + (np.log(nf / np.float32(max_exact)) / np.float32(math.log(MAX_DISTANCE / max_exact))
                         * np.float32(N_BUCKETS - max_exact)).astype(np.int32)
    return np.where(n < max_exact, n, np.minimum(large, N_BUCKETS - 1)).astype(np.int32)


def _bucket_tiles():
    tq = np.arange(BLK)[:, None]
    tk = np.arange(BLK)[None, :]
    return np.stack([_bucket_np(tq - tk), _bucket_np(BLK + tq - tk)])


def _sb_prefix_matrix():
    sp = np.arange(2 * BLK)[:, None] % BLK
    c = np.arange(2 * BLK)[None, :]
    return np.where(c < BLK, sp > c, True).astype(np.float32)


def _ix_select_matrices():
    c = np.arange(256)[:, None]
    col = np.arange(256)[None, :]
    rep = ((c < IDX_DIM) & (c == col % IDX_DIM)).astype(np.float32)
    col8 = np.arange(IDX_HEADS * BLK)[None, :]
    bcast = (c == W_IX_LANE + col8 // BLK).astype(np.float32)
    s1 = np.arange(BLK)[:, None]
    s0 = np.arange(BLK)[None, :]
    before = (s1 < s0).astype(np.float32)
    return rep, bcast, before


def _head_rmsnorm128(xf, gain, lo):
    ss = xf * xf
    s_lo = jnp.sum(jnp.where(lo, ss, 0.0), axis=-1, keepdims=True)
    s_hi = jnp.sum(jnp.where(lo, 0.0, ss), axis=-1, keepdims=True)
    ms = jnp.where(lo, s_lo, s_hi) * (1.0 / HEAD_DIM)
    return xf * lax.rsqrt(ms + EPS) * gain


def _bias_tile(tab_ref, bk, head):
    acc = jnp.zeros(bk.shape, F32)
    for b in range(N_BUCKETS):
        acc = jnp.where(bk == b, tab_ref[b, head], acc)
    return acc


def _softmax_step(s, vj, m_ref, l_ref, acc_ref):
    m_old = m_ref[...]
    m_new = jnp.maximum(m_old, jnp.max(s, axis=-1, keepdims=True))
    alpha = jnp.exp(m_old - m_new)
    p = jnp.exp(s - m_new)
    l_ref[...] = alpha * l_ref[...] + jnp.sum(p, axis=-1, keepdims=True)
    w = acc_ref.shape[-1] // BLK
    a = alpha if w == 1 else jnp.concatenate([alpha] * w, axis=1)
    acc_ref[...] = a * acc_ref[...] + _dot(p.astype(BF16), vj)
    m_ref[...] = m_new


def _in_proj_kernel(h_ref, g_ref, w_ref, o_ref, u_ref):
    @pl.when(pl.program_id(1) == 0)
    def _():
        x = h_ref[...]
        ms = jnp.mean(x * x, axis=-1, keepdims=True)
        u_ref[...] = (x * lax.rsqrt(ms + EPS) * g_ref[...]).astype(BF16)

    o_ref[...] = _dot(u_ref[...], w_ref[...]).astype(o_ref.dtype)


def _in_proj(h, gain, w):
    m, d = h.shape
    n = w.shape[1]
    tm = _pick_rows(m, 1024)
    tn = 512
    return pl.pallas_call(
        _in_proj_kernel,
        out_shape=jax.ShapeDtypeStruct((m, n), BF16),
        grid=(m // tm, n // tn),
        in_specs=[pl.BlockSpec((tm, d), lambda i, j: (i, 0)),
                  pl.BlockSpec((1, d), lambda i, j: (0, 0)),
                  pl.BlockSpec((d, tn), lambda i, j: (0, j))],
        out_specs=pl.BlockSpec((tm, tn), lambda i, j: (i, j)),
        scratch_shapes=[pltpu.VMEM((tm, d), BF16)],
        compiler_params=_params("parallel", "arbitrary"),
        name="in_proj",
    )(h, gain.reshape(1, d), w)


def _sb_kernel(q_ref, k_ref, v_ref, u_ref, o_ref, tot_ref, acc_ref):
    i = pl.program_id(1)
    nh = SB_HEADS
    lane = lax.broadcasted_iota(jnp.int32, (BLK, nh * HEAD_DIM), 1)
    head_of_lane = lane // HEAD_DIM
    q = q_ref[0].astype(F32) * (HEAD_DIM ** -0.5)
    qs = jnp.concatenate([jnp.where(head_of_lane == h, q, 0.0) for h in range(nh)], axis=0).astype(BF16)
    tot_ref[...] = jnp.zeros_like(tot_ref)
    acc_ref[...] = jnp.zeros_like(acc_ref)

    def tile(j, diag):
        r = pl.multiple_of(j * BLK, BLK)
        kj = k_ref[0, pl.ds(r, BLK), :]
        vj = v_ref[0, pl.ds(r, BLK), :]
        z = _nt_dot(qs, kj)
        sp = jnp.maximum(z, 0.0) + jnp.log(1.0 + jnp.exp(-jnp.abs(z)))
        l1m = -sp
        if diag:
            row = lax.broadcasted_iota(jnp.int32, z.shape, 0) & (BLK - 1)
            col = lax.broadcasted_iota(jnp.int32, z.shape, 1)
            mask = col < row
            l1m = jnp.where(mask, l1m, 0.0)
        hi = l1m.astype(BF16)
        lo = (l1m - hi.astype(F32)).astype(BF16)
        rs = _dot(jnp.concatenate([hi, lo], axis=1), u_ref[...])
        w = jnp.exp((z - sp) + rs[:, :BLK] + tot_ref[...])
        if diag:
            w = jnp.where(mask, w, 0.0)
        acc_ref[...] += _dot(w.astype(BF16), vj)
        tot_ref[...] += rs[:, BLK:]

    tile(i, True)

    def body(it, c):
        tile(i - 1 - it, False)
        return c

    lax.fori_loop(0, i, body, 0)

    a = acc_ref[...]
    out = a[:BLK]
    for h in range(1, nh):
        out = jnp.where(head_of_lane == h, a[h * BLK:(h + 1) * BLK], out)
    o_ref[0] = out.astype(o_ref.dtype)


def _sb_attn(z3, u_mat):
    b, tp, _ = z3.shape
    nq = tp // BLK
    w = SB_HEADS * HEAD_DIM
    return pl.pallas_call(
        _sb_kernel,
        out_shape=jax.ShapeDtypeStruct((b, tp, w), BF16),
        grid=(b, nq),
        in_specs=[pl.BlockSpec((1, BLK, w), lambda bb, i: (bb, i, Q_SB // w)),
                  pl.BlockSpec((1, tp, w), lambda bb, i: (bb, 0, K_SB // w)),
                  pl.BlockSpec((1, tp, w), lambda bb, i: (bb, 0, V_SB // w)),
                  pl.BlockSpec((2 * BLK, 2 * BLK), lambda bb, i: (0, 0))],
        out_specs=pl.BlockSpec((1, BLK, w), lambda bb, i: (bb, i, 0)),
        scratch_shapes=[pltpu.VMEM((SB_HEADS * BLK, BLK), F32),
                        pltpu.VMEM((SB_HEADS * BLK, w), F32)],
        compiler_params=_params("parallel", "arbitrary"),
        name="sb_attn",
    )(z3, z3, z3, u_mat)


def _df_kernel(tab_ref, q_ref, k_ref, v_ref, qg_ref, kg_ref, bkt_ref, lamv_ref, sub_ref, o_ref,
               kn_ref, bias_ref, m_ref, l_ref, acc_ref, *, nq, lam_init, head_off):
    h = pl.program_id(1)
    i = pl.program_id(2)
    lane = lax.broadcasted_iota(jnp.int32, (BLK, BLK), 1)
    lo = lane < HEAD_DIM

    @pl.when(i == 0)
    def _prep():
        def kbody(c, carry):
            r = pl.multiple_of(c * BLK, BLK)
            kf = k_ref[0, pl.ds(r, BLK), :].astype(F32)
            kn_ref[pl.ds(r, BLK), :] = _head_rmsnorm128(kf, kg_ref[...], lo).astype(BF16)
            return carry

        lax.fori_loop(0, nq, kbody, 0)
        for t in range(2):
            bt = _bias_tile(tab_ref, bkt_ref[t], head_off + h)
            bias_ref[t] = jnp.concatenate([bt, bt], axis=0)
        far = jnp.full((BLK, BLK), tab_ref[N_BUCKETS - 1, head_off + h], F32)
        bias_ref[2] = jnp.concatenate([far, far], axis=0)

    qn = _head_rmsnorm128(q_ref[0].astype(F32), qg_ref[...], lo) * (HEAD_DIM ** -0.5)
    qs = jnp.concatenate([jnp.where(lo, qn, 0.0), jnp.where(lo, 0.0, qn)], axis=0).astype(BF16)
    m_ref[...] = jnp.full(m_ref.shape, M_INIT, F32)
    l_ref[...] = jnp.zeros_like(l_ref)
    acc_ref[...] = jnp.zeros_like(acc_ref)

    def tile(j, bias_idx, causal):
        r = pl.multiple_of(j * BLK, BLK)
        s = _nt_dot(qs, kn_ref[pl.ds(r, BLK), :]) + bias_ref[bias_idx]
        if causal:
            row = lax.broadcasted_iota(jnp.int32, s.shape, 0) & (BLK - 1)
            col = lax.broadcasted_iota(jnp.int32, s.shape, 1)
            s = jnp.where(col <= row, s, -jnp.inf)
        _softmax_step(s, v_ref[0, pl.ds(r, BLK), :], m_ref, l_ref, acc_ref)

    def far_body(j, c):
        tile(j, 2, False)
        return c

    lax.fori_loop(0, jnp.maximum(i - 1, 0), far_body, 0)

    @pl.when(i >= 1)
    def _():
        tile(i - 1, 1, False)

    tile(i, 0, True)

    a = acc_ref[...] / l_ref[...]
    lv = lamv_ref[...]
    lam = (jnp.exp(jnp.sum(lv[0:1] * lv[1:2], axis=-1, keepdims=True))
           - jnp.exp(jnp.sum(lv[2:3] * lv[3:4], axis=-1, keepdims=True)) + lam_init)
    y = a[:BLK] - lam * a[BLK:]
    y = y * lax.rsqrt(jnp.mean(y * y, axis=-1, keepdims=True) + EPS) * sub_ref[...]
    o_ref[0] = (y * (1.0 - lam_init)).astype(o_ref.dtype)


def _df_attn(z3, rel_bias, qg, kg, bkt, lamv, subln, lam_init):
    b, tp, _ = z3.shape
    nq = tp // BLK
    kern = functools.partial(_df_kernel, nq=nq, lam_init=lam_init, head_off=SP_HEADS)
    vec = lambda bb, h, i: (0, 0)
    return pl.pallas_call(
        kern,
        out_shape=jax.ShapeDtypeStruct((b, tp, DF_HEADS * BLK), BF16),
        grid=(b, DF_HEADS, nq),
        in_specs=[pl.BlockSpec(memory_space=pltpu.SMEM),
                  pl.BlockSpec((1, BLK, BLK), lambda bb, h, i: (bb, i, Q_DF // BLK + h)),
                  pl.BlockSpec((1, tp, BLK), lambda bb, h, i: (bb, 0, K_DF // BLK + h)),
                  pl.BlockSpec((1, tp, BLK), lambda bb, h, i: (bb, 0, V_DF // BLK + h)),
                  pl.BlockSpec((1, BLK), vec),
                  pl.BlockSpec((1, BLK), vec),
                  pl.BlockSpec((2, BLK, BLK), lambda bb, h, i: (0, 0, 0)),
                  pl.BlockSpec((8, BLK), vec),
                  pl.BlockSpec((1, BLK), vec)],
        out_specs=pl.BlockSpec((1, BLK, BLK), lambda bb, h, i: (bb, i, h)),
        scratch_shapes=[pltpu.VMEM((tp, BLK), BF16),
                        pltpu.VMEM((3, 2 * BLK, BLK), F32),
                        pltpu.VMEM((2 * BLK, BLK), F32),
                        pltpu.VMEM((2 * BLK, BLK), F32),
                        pltpu.VMEM((2 * BLK, BLK), F32)],
        compiler_params=_params("parallel", "parallel", "arbitrary"),
        name="df_attn",
    )(rel_bias, z3, z3, z3, qg, kg, bkt, lamv, subln)


def _sp_kernel(tab_ref, q_ref, k_ref, v_ref, qix_ref, kwq_ref, kwk_ref, qg_ref, kg_ref, bkt_ref,
               rep_ref, bcast_ref, before_ref, o_ref,
               kn_ref, kx_ref, bias_ref, keys_ref, thr_ref, wb_ref, m_ref, l_ref, acc_ref,
               *, nq, top_k):
    i = pl.program_id(1)
    nh = SP_HEADS
    w = nh * HEAD_DIM
    lane = lax.broadcasted_iota(jnp.int32, (BLK, w), 1)
    head_of_lane = lane // HEAD_DIM
    lo = lax.broadcasted_iota(jnp.int32, (BLK, BLK), 1) < HEAD_DIM

    def norm256(xf, g):
        return jnp.concatenate([_head_rmsnorm128(xf[:, :BLK], g[:, :BLK], lo),
                                _head_rmsnorm128(xf[:, BLK:], g[:, BLK:], lo)], axis=1)

    @pl.when(i == 0)
    def _prep():
        def kbody(c, carry):
            r = pl.multiple_of(c * BLK, BLK)
            kn_ref[pl.ds(r, BLK), :] = norm256(k_ref[0, pl.ds(r, BLK), :].astype(F32), kg_ref[...]).astype(BF16)
            kx_ref[pl.ds(r, BLK), :] = _dot(kwk_ref[0, pl.ds(r, BLK), :], rep_ref[...]).astype(BF16)
            return carry

        lax.fori_loop(0, nq, kbody, 0)
        for t in range(2):
            bias_ref[t] = jnp.concatenate([_bias_tile(tab_ref, bkt_ref[t], h) for h in range(nh)], axis=0)
        bias_ref[2] = jnp.concatenate([jnp.full((BLK, BLK), tab_ref[N_BUCKETS - 1, h], F32) for h in range(nh)],
                                      axis=0)

    row = lax.broadcasted_iota(jnp.int32, (BLK, BLK), 0)
    col = lax.broadcasted_iota(jnp.int32, (BLK, BLK), 1)
    causal = col <= row

    wb_ref[...] = _dot(kwq_ref[0], bcast_ref[...])
    qix = qix_ref[0].astype(F32)
    ix_head = lane // IDX_DIM
    qx = jnp.concatenate([jnp.where(ix_head == h, qix, 0.0) for h in range(IDX_HEADS)], axis=0).astype(BF16)

    def score_tile(j, diag):
        r = pl.multiple_of(j * BLK, BLK)
        d = _nt_dot(qx, kx_ref[pl.ds(r, BLK), :])
        sc = jnp.zeros((BLK, BLK), F32)
        for h in range(IDX_HEADS):
            sc = sc + wb_ref[:, h * BLK:(h + 1) * BLK] * jnp.maximum(d[h * BLK:(h + 1) * BLK], 0.0)
        sc = jnp.where(sc == 0.0, 0.0, sc)
        bits = lax.bitcast_convert_type(sc, jnp.int32)
        key = jnp.where(bits < 0, bits ^ jnp.int32(0x7FFFFFFF), bits)
        if diag:
            key = jnp.where(causal, key, jnp.int32(INT_MIN))
        keys_ref[j] = key

    def score_body(j, c):
        score_tile(j, False)
        return c

    lax.fori_loop(0, i, score_body, 0)
    score_tile(i, True)

    nchunk = i + 1

    def count_ge(cand):
        def cbody(j, c):
            return c + jnp.where(keys_ref[j] >= cand, 1.0, 0.0)

        c = lax.fori_loop(0, nchunk, cbody, jnp.zeros((BLK, BLK), F32))
        return jnp.sum(c, axis=-1, keepdims=True)

    def bit_body(b, carry):
        cur, cnt_cur = carry
        cand = cur + jnp.left_shift(jnp.int32(1), 31 - b)
        cnt = count_ge(cand)
        ok = cnt >= float(top_k)
        return jnp.where(ok, cand, cur), jnp.where(ok, cnt, cnt_cur)

    cur0 = jnp.full((BLK, BLK), INT_MIN, jnp.int32)
    cnt0 = jnp.zeros((BLK, BLK), F32) + (nchunk * BLK).astype(F32)
    thr, cge = lax.fori_loop(0, 32, bit_body, (cur0, cnt0))
    thr_ref[...] = thr

    tie = jnp.where((cge > float(top_k)) & (thr > INT_MIN), 1, 0)

    @pl.when(jnp.max(tie) > 0)
    def _ties():
        def gbody(j, c):
            return c + jnp.where(keys_ref[j] > thr, 1.0, 0.0)

        cgt = jnp.sum(lax.fori_loop(0, nchunk, gbody, jnp.zeros((BLK, BLK), F32)), axis=-1, keepdims=True)
        need = float(top_k) - cgt

        def tbody(j, run):
            kj = keys_ref[j]
            eq = kj == thr
            eqf = jnp.where(eq, 1.0, 0.0)
            rank = run + _dot(eqf.astype(BF16), before_ref[...])
            keep = jnp.where(kj > thr, 1, jnp.where(eq & (rank < need), 1, -1))
            keys_ref[j] = keep.astype(jnp.int32)
            return run + jnp.sum(eqf, axis=-1, keepdims=True)

        lax.fori_loop(0, nchunk, tbody, jnp.zeros((BLK, BLK), F32))
        thr_ref[...] = jnp.zeros_like(thr_ref)

    qn = norm256(q_ref[0].astype(F32), qg_ref[...]) * (HEAD_DIM ** -0.5)
    qs = jnp.concatenate([jnp.where(head_of_lane == h, qn, 0.0) for h in range(nh)], axis=0).astype(BF16)
    m_ref[...] = jnp.full(m_ref.shape, M_INIT, F32)
    l_ref[...] = jnp.zeros_like(l_ref)
    acc_ref[...] = jnp.zeros_like(acc_ref)

    def tile(j, bias_idx, diag):
        r = pl.multiple_of(j * BLK, BLK)
        s = _nt_dot(qs, kn_ref[pl.ds(r, BLK), :]) + bias_ref[bias_idx]
        sel = keys_ref[j] >= thr_ref[...]
        if diag:
            sel = sel & causal
        s = jnp.concatenate([jnp.where(sel, s[h * BLK:(h + 1) * BLK], -jnp.inf) for h in range(nh)], axis=0)
        _softmax_step(s, v_ref[0, pl.ds(r, BLK), :], m_ref, l_ref, acc_ref)

    def far_body(j, c):
        tile(j, 2, False)
        return c

    lax.fori_loop(0, jnp.maximum(i - 1, 0), far_body, 0)

    @pl.when(i >= 1)
    def _():
        tile(i - 1, 1, False)

    tile(i, 0, True)

    inv = 1.0 / l_ref[...]
    a = acc_ref[...] * jnp.concatenate([inv, inv], axis=1)
    out = a[:BLK]
    for h in range(1, nh):
        out = jnp.where(head_of_lane == h, a[h * BLK:(h + 1) * BLK], out)
    o_ref[0] = out.astype(o_ref.dtype)


def _sp_attn(z3, rel_bias, qg, kg, bkt, rep, bcast, before, top_k):
    b, tp, _ = z3.shape
    nq = tp // BLK
    w = SP_HEADS * HEAD_DIM
    kern = functools.partial(_sp_kernel, nq=nq, top_k=top_k)
    c2 = lambda bb, i: (0, 0)
    return pl.pallas_call(
        kern,
        out_shape=jax.ShapeDtypeStruct((b, tp, w), BF16),
        grid=(b, nq),
        in_specs=[pl.BlockSpec(memory_space=pltpu.SMEM),
                  pl.BlockSpec((1, BLK, w), lambda bb, i: (bb, i, Q_SP // w)),
                  pl.BlockSpec((1, tp, w), lambda bb, i: (bb, 0, K_SP // w)),
                  pl.BlockSpec((1, tp, w), lambda bb, i: (bb, 0, V_SP // w)),
                  pl.BlockSpec((1, BLK, w), lambda bb, i: (bb, i, Q_IX // w)),
                  pl.BlockSpec((1, BLK, w), lambda bb, i: (bb, i, KW_IX // w)),
                  pl.BlockSpec((1, tp, w), lambda bb, i: (bb, 0, KW_IX // w)),
                  pl.BlockSpec((1, w), c2),
                  pl.BlockSpec((1, w), c2),
                  pl.BlockSpec((2, BLK, BLK), lambda bb, i: (0, 0, 0)),
                  pl.BlockSpec((w, w), c2),
                  pl.BlockSpec((w, IDX_HEADS * BLK), c2),
                  pl.BlockSpec((BLK, BLK), c2)],
        out_specs=pl.BlockSpec((1, BLK, w), lambda bb, i: (bb, i, 0)),
        scratch_shapes=[pltpu.VMEM((tp, w), BF16),
                        pltpu.VMEM((tp, w), BF16),
                        pltpu.VMEM((3, SP_HEADS * BLK, BLK), F32),
                        pltpu.VMEM((nq, BLK, BLK), jnp.int32),
                        pltpu.VMEM((BLK, BLK), jnp.int32),
                        pltpu.VMEM((BLK, IDX_HEADS * BLK), F32),
                        pltpu.VMEM((SP_HEADS * BLK, BLK), F32),
                        pltpu.VMEM((SP_HEADS * BLK, BLK), F32),
                        pltpu.VMEM((SP_HEADS * BLK, w), F32)],
        compiler_params=_params("parallel", "arbitrary"),
        name="sp_attn",
    )(rel_bias, z3, z3, z3, z3, z3, z3, qg, kg, bkt, rep, bcast, before)


def _mix_kernel(h_ref, gsb_ref, gsp_ref, gdf_ref, bg_ref, ysb_ref, ysp_ref, ydf_ref,
                wsb_ref, wsp_ref, wdf_ref, wo_ref, o_ref):
    def branch(g_ref, k, y_ref, w_ref):
        gate = jax.nn.sigmoid(g_ref[...].astype(F32) + bg_ref[:, k * D_MODEL:(k + 1) * D_MODEL])
        return gate * _dot(y_ref[...], w_ref[...])

    merged = (branch(gsb_ref, 0, ysb_ref, wsb_ref) + branch(gsp_ref, 1, ysp_ref, wsp_ref)
              + branch(gdf_ref, 2, ydf_ref, wdf_ref))
    o_ref[...] = h_ref[...] + _dot(merged.astype(BF16), wo_ref[...])


def _mix_out(h, z, b_gate, y_sb, y_sp, y_df, w_sb, w_sp, w_df, w_o):
    m, d = h.shape
    tm = _pick_rows(m, 512)
    row = lambda i: (i, 0)
    fixed = lambda i: (0, 0)
    return pl.pallas_call(
        _mix_kernel,
        out_shape=jax.ShapeDtypeStruct((m, d), F32),
        grid=(m // tm,),
        in_specs=[pl.BlockSpec((tm, d), row),
                  pl.BlockSpec((tm, d), lambda i: (i, G_SB // D_MODEL)),
                  pl.BlockSpec((tm, d), lambda i: (i, G_SP // D_MODEL)),
                  pl.BlockSpec((tm, d), lambda i: (i, G_DF // D_MODEL)),
                  pl.BlockSpec((1, 3 * d), fixed),
                  pl.BlockSpec((tm, y_sb.shape[1]), row),
                  pl.BlockSpec((tm, y_sp.shape[1]), row),
                  pl.BlockSpec((tm, y_df.shape[1]), row),
                  pl.BlockSpec(w_sb.shape, fixed),
                  pl.BlockSpec(w_sp.shape, fixed),
                  pl.BlockSpec(w_df.shape, fixed),
                  pl.BlockSpec(w_o.shape, fixed)],
        out_specs=pl.BlockSpec((tm, d), row),
        compiler_params=_params("parallel"),
        name="mix_out",
    )(h, z, z, z, b_gate.reshape(1, 3 * d), y_sb, y_sp, y_df, w_sb, w_sp, w_df, w_o)


def _ffn_kernel(h_ref, g_ref, wg_ref, wv_ref, cw_ref, cb_ref, wd_ref, o_ref,
                u_ref, gbuf_ref, carry_ref, acc_ref, *, tm, tp):
    r = pl.program_id(0)
    f = pl.program_id(1)

    @pl.when(f == 0)
    def _():
        x = h_ref[...]
        ms = jnp.mean(x * x, axis=-1, keepdims=True)
        u_ref[...] = (x * lax.rsqrt(ms + EPS) * g_ref[...]).astype(BF16)
        acc_ref[...] = jnp.zeros_like(acc_ref)

    u = u_ref[...]
    gate = _dot(u, wg_ref[...])
    val = _dot(u, wv_ref[...])
    @pl.when(r == 0)
    def _():
        carry_ref[f] = jnp.zeros((8, gate.shape[1]), F32)

    gbuf_ref[0:8] = carry_ref[f]
    gbuf_ref[8:8 + tm] = gate
    carry_ref[f] = gate[tm - 8:tm]
    seq_start = lax.rem(tp - lax.rem(r * tm, tp), tp)
    local = lax.broadcasted_iota(jnp.int32, (tm, 1), 0)
    g1 = jnp.where(local != seq_start, gbuf_ref[7:7 + tm], 0.0)
    g2 = jnp.where((local != seq_start) & (local != seq_start + 1), gbuf_ref[6:6 + tm], 0.0)
    cw = cw_ref[...]
    conv = cb_ref[...] + cw[0:1] * g2 + cw[1:2] * g1 + cw[2:3] * gate
    act = conv * jax.nn.sigmoid(conv) * val
    acc_ref[...] += _dot(act.astype(BF16), wd_ref[...])

    @pl.when(f == pl.num_programs(1) - 1)
    def _():
        o_ref[...] = h_ref[...] + acc_ref[...]


def _ffn(h, gain, w_up, conv_w, conv_b, w_down, tp):
    m, d = h.shape
    tm = _pick_rows(m, 512)
    tf = D_FF // 2
    nf = D_FF // tf
    kern = functools.partial(_ffn_kernel, tm=tm, tp=tp)
    return pl.pallas_call(
        kern,
        out_shape=jax.ShapeDtypeStruct((m, d), F32),
        grid=(m // tm, nf),
        in_specs=[pl.BlockSpec((tm, d), lambda r, f: (r, 0)),
                  pl.BlockSpec((1, d), lambda r, f: (0, 0)),
                  pl.BlockSpec((d, tf), lambda r, f: (0, f)),
                  pl.BlockSpec((d, tf), lambda r, f: (0, nf + f)),
                  pl.BlockSpec((8, tf), lambda r, f: (0, f)),
                  pl.BlockSpec((1, tf), lambda r, f: (0, f)),
                  pl.BlockSpec((tf, d), lambda r, f: (f, 0))],
        out_specs=pl.BlockSpec((tm, d), lambda r, f: (r, 0)),
        scratch_shapes=[pltpu.VMEM((tm, d), BF16),
                        pltpu.VMEM((tm + 8, tf), F32),
                        pltpu.VMEM((nf, 8, tf), F32),
                        pltpu.VMEM((tm, d), F32)],
        compiler_params=_params("arbitrary", "arbitrary"),
        name="conv_ffn",
    )(h, gain.reshape(1, d), w_up, w_up, conv_w, conv_b.reshape(1, D_FF), w_down)


def _permute_w_in(w):
    n_attn = KW_IX - Q_SB + IDX_DIM + IDX_HEADS
    n_gate = 3 * D_MODEL
    n_df = 3 * DF_HEADS * 2 * HEAD_DIM
    gates = w[:, n_attn + n_df:]
    attn = w[:, :n_attn]
    pad = jnp.zeros((w.shape[0], Q_DF - Q_SB - n_attn), w.dtype)
    df = w[:, n_attn:n_attn + n_df]
    out = jnp.concatenate([gates, attn, pad, df], axis=1)
    assert gates.shape[1] == n_gate and out.shape[1] == NZ
    return out


def kernel(x, meta_tokens, rel_bias, attn_norm, w_in, b_gate, q_norm_sp, k_norm_sp, q_norm_df, k_norm_df, lam_q1, lam_k1, lam_q2, lam_k2, subln_df, w_br_sb, w_br_sp, w_br_df, w_out, ffn_norm, w_up, conv_w, conv_b, w_down):
    b, s, d = x.shape
    depth = w_in.shape[0]
    t = N_META + s
    tp = -(-t // BLK) * BLK
    top_k = min(TOPK_MAX, t // 4)
    m = b * tp

    meta = jnp.broadcast_to(meta_tokens[None].astype(x.dtype), (b, N_META, d))
    h = jnp.concatenate([meta, x, jnp.zeros((b, tp - t, d), x.dtype)], axis=1).reshape(m, d)

    bkt = jnp.asarray(_bucket_tiles())
    u_mat = jnp.asarray(_sb_prefix_matrix(), BF16)
    rep, bcast, before = (jnp.asarray(a, BF16) for a in _ix_select_matrices())
    rel_bias = rel_bias.astype(F32)

    for l in range(depth):
        lam_init = 0.8 - 0.6 * math.exp(-0.3 * l)
        z = _in_proj(h, attn_norm[l], _permute_w_in(w_in[l]).astype(BF16))
        z3 = z.reshape(b, tp, NZ)
        y_sb = _sb_attn(z3, u_mat)
        y_sp = _sp_attn(z3, rel_bias,
                        jnp.tile(q_norm_sp[l].astype(F32), SP_HEADS).reshape(1, -1),
                        jnp.tile(k_norm_sp[l].astype(F32), SP_HEADS).reshape(1, -1),
                        bkt, rep, bcast, before, top_k)
        lamv = jnp.zeros((8, BLK), F32).at[:4, :HEAD_DIM].set(
            jnp.stack([lam_q1[l], lam_k1[l], lam_q2[l], lam_k2[l]]).astype(F32))
        y_df = _df_attn(z3, rel_bias,
                        jnp.tile(q_norm_df[l].astype(F32), 2).reshape(1, -1),
                        jnp.tile(k_norm_df[l].astype(F32), 2).reshape(1, -1),
                        bkt, lamv, subln_df[l].astype(F32).reshape(1, -1), lam_init)
        h = _mix_out(h, z, b_gate[l], y_sb.reshape(m, -1), y_sp.reshape(m, -1), y_df.reshape(m, -1),
                     w_br_sb[l].astype(BF16), w_br_sp[l].astype(BF16), w_br_df[l].astype(BF16),
                     w_out[l].astype(BF16))
        cw = jnp.zeros((8, D_FF), F32).at[:conv_w.shape[1]].set(conv_w[l])
        h = _ffn(h, ffn_norm[l], w_up[l].astype(BF16), cw, conv_b[l], w_down[l].astype(BF16), tp)

    return h.reshape(b, tp, d)[:, N_META:t]
```

```python
import functools
import math

import numpy as np
import jax
import jax.numpy as jnp
from jax import lax
from jax.experimental import pallas as pl
from jax.experimental.pallas import tpu as pltpu

D_MODEL = 1024
HEAD_DIM = 64
N_META = 16
BLK = 128
SB_HEADS = 4
SP_HEADS = 4
IDX_HEADS = 8
IDX_DIM = 32
TOPK_MAX = 256
DF_HEADS = 4
N_BUCKETS = 32
MAX_DISTANCE = 128
D_FF = 2816
EPS = 1e-6
M_INIT = -1e30
INT_MIN = -2 ** 31

G_SB, G_SP, G_DF = 0, 1024, 2048
Q_SB, K_SB, V_SB = 3072, 3328, 3584
Q_SP, K_SP, V_SP = 3840, 4096, 4352
Q_IX, KW_IX = 4608, 4864
Q_DF, K_DF, V_DF = 5120, 5632, 6144
NZ = 6656
W_IX_LANE = IDX_DIM

VMEM_LIMIT = 56 * 1024 * 1024

F32 = jnp.float32
BF16 = jnp.bfloat16
NT_DIMS = (((1,), (1,)), ((), ()))


def _nt_dot(a, b):
    return lax.dot_general(a, b, NT_DIMS, preferred_element_type=F32)


def _dot(a, b):
    return jnp.dot(a, b, preferred_element_type=F32)


def _params(*sem):
    return pltpu.CompilerParams(dimension_semantics=sem, vmem_limit_bytes=VMEM_LIMIT)


def _pick_rows(m, cap):
    for c in (2048, 1024, 512, 384, 256, 128):
        if c <= cap and m % c == 0:
            return c
    raise ValueError(f"row count {m} is not a multiple of {BLK}")


def _bucket_np(rel):
    n = np.maximum(rel, 0)
    max_exact = N_BUCKETS // 2
    nf = np.maximum(n, 1).astype(np.float32)
    large = max_exact + (np.log(nf / np.float32(max_exact)) / np.float32(math.log(MAX_DISTANCE / max_exact))
                         * np.float32(N_BUCKETS - max_exact)).astype(np.int32)
    return np.where(n < max_exact, n, np.minimum(large, N_BUCKETS - 1)).astype(np.int32)


def _bucket_tiles():
    tq = np.arange(BLK)[:, None]
    tk = np.arange(BLK)[None, :]
    return np.stack([_bucket_np(tq - tk), _bucket_np(BLK + tq - tk)])


def _sb_prefix_matrix():
    sp = np.arange(2 * BLK)[:, None] % BLK
    c = np.arange(2 * BLK)[None, :]
    return np.where(c < BLK, sp > c, True).astype(np.float32)


def _ix_select_matrices():
    c = np.arange(256)[:, None]
    col = np.arange(256)[None, :]
    rep = ((c < IDX_DIM) & (c == col % IDX_DIM)).astype(np.float32)
    col8 = np.arange(IDX_HEADS * BLK)[None, :]
    bcast = (c == W_IX_LANE + col8 // BLK).astype(np.float32)
    s1 = np.arange(BLK)[:, None]
    s0 = np.arange(BLK)[None, :]
    before = (s1 < s0).astype(np.float32)
    return rep, bcast, before


def _head_rmsnorm128(xf, gain, lo):
    ss = xf * xf
    s_lo = jnp.sum(jnp.where(lo, ss, 0.0), axis=-1, keepdims=True)
    s_hi = jnp.sum(jnp.where(lo, 0.0, ss), axis=-1, keepdims=True)
    ms = jnp.where(lo, s_lo, s_hi) * (1.0 / HEAD_DIM)
    return xf * lax.rsqrt(ms + EPS) * gain


def _bias_tile(tab_ref, bk, head):
    acc = jnp.zeros(bk.shape, F32)
    for b in range(N_BUCKETS):
        acc = jnp.where(bk == b, tab_ref[b, head], acc)
    return acc


def _softmax_chunk(tiles, v, m_ref, l_ref, acc_ref):
    w = acc_ref.shape[-1] // BLK
    p_rows = []
    alphas = []
    for rb, row in enumerate(tiles):
        rows = slice(rb * BLK, (rb + 1) * BLK)
        m_old = m_ref[rows]
        mx = functools.reduce(jnp.maximum, row)
        m_new = jnp.maximum(m_old, jnp.max(mx, axis=-1, keepdims=True))
        alpha = jnp.exp(m_old - m_new)
        ps = [jnp.exp(t - m_new) for t in row]
        l_ref[rows] = alpha * l_ref[rows] + jnp.sum(functools.reduce(jnp.add, ps), axis=-1, keepdims=True)
        m_ref[rows] = m_new
        p_rows.append(jnp.concatenate([t.astype(BF16) for t in ps], axis=1) if len(ps) > 1 else ps[0].astype(BF16))
        alphas.append(alpha if w == 1 else jnp.concatenate([alpha] * w, axis=1))
    p = jnp.concatenate(p_rows, axis=0)
    acc_ref[...] = jnp.concatenate(alphas, axis=0) * acc_ref[...] + _dot(p, v)


def _split_tiles(s):
    return [[s[rb * BLK:(rb + 1) * BLK, c * BLK:(c + 1) * BLK] for c in range(s.shape[1] // BLK)]
            for rb in range(s.shape[0] // BLK)]


def _for_far_blocks(n_far, chunk_fn, descending=False):
    n4 = n_far // 4
    rem = n_far - 4 * n4

    def fours():
        def body(c, carry):
            chunk_fn((n4 - 1 - c if descending else c) * 4, 4)
            return carry

        lax.fori_loop(0, n4, body, 0)

    def two():
        @pl.when((rem & 2) != 0)
        def _():
            chunk_fn(4 * n4, 2)

    def one():
        @pl.when((rem & 1) != 0)
        def _():
            chunk_fn(4 * n4 + (rem & 2), 1)

    for step in ((one, two, fours) if descending else (fours, two, one)):
        step()


def _in_proj_kernel(h_ref, g_ref, w_ref, o_ref, u_ref):
    @pl.when(pl.program_id(1) == 0)
    def _():
        x = h_ref[...]
        ms = jnp.mean(x * x, axis=-1, keepdims=True)
        u_ref[...] = (x * lax.rsqrt(ms + EPS) * g_ref[...]).astype(BF16)

    o_ref[...] = _dot(u_ref[...], w_ref[...]).astype(o_ref.dtype)


def _in_proj(h, gain, w):
    m, d = h.shape
    n = w.shape[1]
    tm = _pick_rows(m, 1024)
    tn = 512
    return pl.pallas_call(
        _in_proj_kernel,
        out_shape=jax.ShapeDtypeStruct((m, n), BF16),
        grid=(m // tm, n // tn),
        in_specs=[pl.BlockSpec((tm, d), lambda i, j: (i, 0)),
                  pl.BlockSpec((1, d), lambda i, j: (0, 0)),
                  pl.BlockSpec((d, tn), lambda i, j: (0, j))],
        out_specs=pl.BlockSpec((tm, tn), lambda i, j: (i, j)),
        scratch_shapes=[pltpu.VMEM((tm, d), BF16)],
        compiler_params=_params("parallel", "arbitrary"),
        name="in_proj",
    )(h, gain.reshape(1, d), w)


def _sb_kernel(q_ref, k_ref, v_ref, u_ref, o_ref, tot_ref, acc_ref):
    i = pl.program_id(1)
    nh = SB_HEADS
    lane = lax.broadcasted_iota(jnp.int32, (BLK, nh * HEAD_DIM), 1)
    head_of_lane = lane // HEAD_DIM
    q = q_ref[0].astype(F32) * (HEAD_DIM ** -0.5)
    qs = jnp.concatenate([jnp.where(head_of_lane == h, q, 0.0) for h in range(nh)], axis=0).astype(BF16)
    tot_ref[...] = jnp.zeros_like(tot_ref)
    acc_ref[...] = jnp.zeros_like(acc_ref)

    def chunk(first, n, diag=False):
        r = pl.multiple_of(first * BLK, BLK)
        z_all = _nt_dot(qs, k_ref[0, pl.ds(r, n * BLK), :])
        run = tot_ref[...]
        ws = [None] * n
        for c in reversed(range(n)):
            z = z_all[:, c * BLK:(c + 1) * BLK]
            sp = jnp.maximum(z, 0.0) + jnp.log(1.0 + jnp.exp(-jnp.abs(z)))
            l1m = -sp
            if diag:
                row = lax.broadcasted_iota(jnp.int32, z.shape, 0) & (BLK - 1)
                col = lax.broadcasted_iota(jnp.int32, z.shape, 1)
                mask = col < row
                l1m = jnp.where(mask, l1m, 0.0)
            hi = l1m.astype(BF16)
            lo = (l1m - hi.astype(F32)).astype(BF16)
            rs = _dot(jnp.concatenate([hi, lo], axis=1), u_ref[...])
            w = jnp.exp((z - sp) + rs[:, :BLK] + run)
            if diag:
                w = jnp.where(mask, w, 0.0)
            ws[c] = w.astype(BF16)
            run = run + rs[:, BLK:]
        tot_ref[...] = run
        p = ws[0] if n == 1 else jnp.concatenate(ws, axis=1)
        acc_ref[...] += _dot(p, v_ref[0, pl.ds(r, n * BLK), :])

    chunk(i, 1, diag=True)
    _for_far_blocks(i, chunk, descending=True)

    a = acc_ref[...]
    out = a[:BLK]
    for h in range(1, nh):
        out = jnp.where(head_of_lane == h, a[h * BLK:(h + 1) * BLK], out)
    o_ref[0] = out.astype(o_ref.dtype)


def _sb_attn(z3, u_mat):
    b, tp, _ = z3.shape
    nq = tp // BLK
    w = SB_HEADS * HEAD_DIM
    return pl.pallas_call(
        _sb_kernel,
        out_shape=jax.ShapeDtypeStruct((b, tp, w), BF16),
        grid=(b, nq),
        in_specs=[pl.BlockSpec((1, BLK, w), lambda bb, i: (bb, i, Q_SB // w)),
                  pl.BlockSpec((1, tp, w), lambda bb, i: (bb, 0, K_SB // w)),
                  pl.BlockSpec((1, tp, w), lambda bb, i: (bb, 0, V_SB // w)),
                  pl.BlockSpec((2 * BLK, 2 * BLK), lambda bb, i: (0, 0))],
        out_specs=pl.BlockSpec((1, BLK, w), lambda bb, i: (bb, i, 0)),
        scratch_shapes=[pltpu.VMEM((SB_HEADS * BLK, BLK), F32),
                        pltpu.VMEM((SB_HEADS * BLK, w), F32)],
        compiler_params=_params("parallel", "arbitrary"),
        name="sb_attn",
    )(z3, z3, z3, u_mat)


def _df_kernel(tab_ref, q_ref, k_ref, v_ref, qg_ref, kg_ref, bkt_ref, lamv_ref, sub_ref, o_ref,
               kn_ref, bias_ref, m_ref, l_ref, acc_ref, *, nq, lam_init, head_off):
    h = pl.program_id(1)
    i = pl.program_id(2)
    lane = lax.broadcasted_iota(jnp.int32, (BLK, BLK), 1)
    lo = lane < HEAD_DIM

    @pl.when(i == 0)
    def _prep():
        def kbody(c, carry):
            r = pl.multiple_of(c * BLK, BLK)
            kf = k_ref[0, pl.ds(r, BLK), :].astype(F32)
            kn_ref[pl.ds(r, BLK), :] = _head_rmsnorm128(kf, kg_ref[...], lo).astype(BF16)
            return carry

        lax.fori_loop(0, nq, kbody, 0)
        bt = jnp.concatenate([_bias_tile(tab_ref, bkt_ref[1], head_off + h),
                              _bias_tile(tab_ref, bkt_ref[0], head_off + h)], axis=1)
        bt = bt - tab_ref[N_BUCKETS - 1, head_off + h]
        bias_ref[...] = jnp.concatenate([bt, bt], axis=0)

    qn = _head_rmsnorm128(q_ref[0].astype(F32), qg_ref[...], lo) * (HEAD_DIM ** -0.5)
    qs = jnp.concatenate([jnp.where(lo, qn, 0.0), jnp.where(lo, 0.0, qn)], axis=0).astype(BF16)
    m_ref[...] = jnp.full(m_ref.shape, M_INIT, F32)
    l_ref[...] = jnp.zeros_like(l_ref)
    acc_ref[...] = jnp.zeros_like(acc_ref)

    def far_chunk(first, n):
        r = pl.multiple_of(first * BLK, BLK)
        s = _nt_dot(qs, kn_ref[pl.ds(r, n * BLK), :])
        _softmax_chunk(_split_tiles(s), v_ref[0, pl.ds(r, n * BLK), :], m_ref, l_ref, acc_ref)

    def near_chunk(first, n):
        r = pl.multiple_of(first * BLK, BLK)
        s = _nt_dot(qs, kn_ref[pl.ds(r, n * BLK), :]) + bias_ref[:, (2 - n) * BLK:]
        row = lax.broadcasted_iota(jnp.int32, s.shape, 0) & (BLK - 1)
        col = lax.broadcasted_iota(jnp.int32, s.shape, 1) - (n - 1) * BLK
        s = jnp.where(col <= row, s, -jnp.inf)
        _softmax_chunk(_split_tiles(s), v_ref[0, pl.ds(r, n * BLK), :], m_ref, l_ref, acc_ref)

    _for_far_blocks(jnp.maximum(i - 1, 0), far_chunk)

    @pl.when(i >= 1)
    def _():
        near_chunk(i - 1, 2)

    @pl.when(i == 0)
    def _():
        near_chunk(0, 1)

    a = acc_ref[...] / l_ref[...]
    lv = lamv_ref[...]
    lam = (jnp.exp(jnp.sum(lv[0:1] * lv[1:2], axis=-1, keepdims=True))
           - jnp.exp(jnp.sum(lv[2:3] * lv[3:4], axis=-1, keepdims=True)) + lam_init)
    y = a[:BLK] - lam * a[BLK:]
    y = y * lax.rsqrt(jnp.mean(y * y, axis=-1, keepdims=True) + EPS) * sub_ref[...]
    o_ref[0] = (y * (1.0 - lam_init)).astype(o_ref.dtype)


def _df_attn(z3, rel_bias, qg, kg, bkt, lamv, subln, lam_init):
    b, tp, _ = z3.shape
    nq = tp // BLK
    kern = functools.partial(_df_kernel, nq=nq, lam_init=lam_init, head_off=SP_HEADS)
    vec = lambda bb, h, i: (0, 0)
    return pl.pallas_call(
        kern,
        out_shape=jax.ShapeDtypeStruct((b, tp, DF_HEADS * BLK), BF16),
        grid=(b, DF_HEADS, nq),
        in_specs=[pl.BlockSpec(memory_space=pltpu.SMEM),
                  pl.BlockSpec((1, BLK, BLK), lambda bb, h, i: (bb, i, Q_DF // BLK + h)),
                  pl.BlockSpec((1, tp, BLK), lambda bb, h, i: (bb, 0, K_DF // BLK + h)),
                  pl.BlockSpec((1, tp, BLK), lambda bb, h, i: (bb, 0, V_DF // BLK + h)),
                  pl.BlockSpec((1, BLK), vec),
                  pl.BlockSpec((1, BLK), vec),
                  pl.BlockSpec((2, BLK, BLK), lambda bb, h, i: (0, 0, 0)),
                  pl.BlockSpec((8, BLK), vec),
                  pl.BlockSpec((1, BLK), vec)],
        out_specs=pl.BlockSpec((1, BLK, BLK), lambda bb, h, i: (bb, i, h)),
        scratch_shapes=[pltpu.VMEM((tp, BLK), BF16),
                        pltpu.VMEM((2 * BLK, 2 * BLK), F32),
                        pltpu.VMEM((2 * BLK, BLK), F32),
                        pltpu.VMEM((2 * BLK, BLK), F32),
                        pltpu.VMEM((2 * BLK, BLK), F32)],
        compiler_params=_params("parallel", "parallel", "arbitrary"),
        name="df_attn",
    )(rel_bias, z3, z3, z3, qg, kg, bkt, lamv, subln)


def _sp_kernel(tab_ref, q_ref, k_ref, v_ref, qix_ref, kwq_ref, kwk_ref, qg_ref, kg_ref, bkt_ref,
               rep_ref, bcast_ref, before_ref, o_ref,
               kn_ref, kx_ref, bias_ref, keys_ref, madd_ref, thr_ref, wb_ref, m_ref, l_ref, acc_ref,
               *, nq, top_k):
    i = pl.program_id(1)
    nh = SP_HEADS
    w = nh * HEAD_DIM
    lane = lax.broadcasted_iota(jnp.int32, (BLK, w), 1)
    head_of_lane = lane // HEAD_DIM
    lo = lax.broadcasted_iota(jnp.int32, (BLK, BLK), 1) < HEAD_DIM

    def norm256(xf, g):
        return jnp.concatenate([_head_rmsnorm128(xf[:, :BLK], g[:, :BLK], lo),
                                _head_rmsnorm128(xf[:, BLK:], g[:, BLK:], lo)], axis=1)

    @pl.when(i == 0)
    def _prep():
        def kbody(c, carry):
            r = pl.multiple_of(c * BLK, BLK)
            kn_ref[pl.ds(r, BLK), :] = norm256(k_ref[0, pl.ds(r, BLK), :].astype(F32), kg_ref[...]).astype(BF16)
            kx_ref[pl.ds(r, BLK), :] = _dot(kwk_ref[0, pl.ds(r, BLK), :], rep_ref[...]).astype(BF16)
            return carry

        lax.fori_loop(0, nq, kbody, 0)
        bias_ref[...] = jnp.concatenate(
            [jnp.concatenate([_bias_tile(tab_ref, bkt_ref[1], h), _bias_tile(tab_ref, bkt_ref[0], h)], axis=1)
             - tab_ref[N_BUCKETS - 1, h] for h in range(nh)], axis=0)

    row = lax.broadcasted_iota(jnp.int32, (BLK, BLK), 0)
    col = lax.broadcasted_iota(jnp.int32, (BLK, BLK), 1)
    causal = col <= row

    wb_ref[...] = _dot(kwq_ref[0], bcast_ref[...])
    qix = qix_ref[0].astype(F32)
    ix_head = lane // IDX_DIM
    qx = jnp.concatenate([jnp.where(ix_head == h, qix, 0.0) for h in range(IDX_HEADS)], axis=0).astype(BF16)

    def score_tile(j, diag):
        r = pl.multiple_of(j * BLK, BLK)
        d = _nt_dot(qx, kx_ref[pl.ds(r, BLK), :])
        sc = jnp.zeros((BLK, BLK), F32)
        for h in range(IDX_HEADS):
            sc = sc + wb_ref[:, h * BLK:(h + 1) * BLK] * jnp.maximum(d[h * BLK:(h + 1) * BLK], 0.0)
        sc = jnp.where(sc == 0.0, 0.0, sc)
        bits = lax.bitcast_convert_type(sc, jnp.int32)
        key = jnp.where(bits < 0, bits ^ jnp.int32(0x7FFFFFFF), bits)
        if diag:
            key = jnp.where(causal, key, jnp.int32(INT_MIN))
        keys_ref[j] = key

    def score_body(j, c):
        score_tile(j, False)
        return c

    lax.fori_loop(0, i, score_body, 0)
    score_tile(i, True)

    nchunk = i + 1

    def count_ge(cand):
        def cbody(j, c):
            return c + jnp.where(keys_ref[j] >= cand, 1.0, 0.0)

        c = lax.fori_loop(0, nchunk, cbody, jnp.zeros((BLK, BLK), F32))
        return jnp.sum(c, axis=-1, keepdims=True)

    def bit_body(b, carry):
        cur, cnt_cur = carry
        cand = cur + jnp.left_shift(jnp.int32(1), 31 - b)
        cnt = count_ge(cand)
        ok = cnt >= float(top_k)
        return jnp.where(ok, cand, cur), jnp.where(ok, cnt, cnt_cur)

    cur0 = jnp.full((BLK, BLK), INT_MIN, jnp.int32)
    cnt0 = jnp.zeros((BLK, BLK), F32) + (nchunk * BLK).astype(F32)
    thr, cge = lax.fori_loop(0, 32, bit_body, (cur0, cnt0))
    thr_ref[...] = thr

    tie = jnp.where((cge > float(top_k)) & (thr > INT_MIN), 1, 0)

    @pl.when(jnp.max(tie) > 0)
    def _ties():
        def gbody(j, c):
            return c + jnp.where(keys_ref[j] > thr, 1.0, 0.0)

        cgt = jnp.sum(lax.fori_loop(0, nchunk, gbody, jnp.zeros((BLK, BLK), F32)), axis=-1, keepdims=True)
        need = float(top_k) - cgt

        def tbody(j, run):
            kj = keys_ref[j]
            eq = kj == thr
            eqf = jnp.where(eq, 1.0, 0.0)
            rank = run + _dot(eqf.astype(BF16), before_ref[...])
            keep = jnp.where(kj > thr, 1, jnp.where(eq & (rank < need), 1, -1))
            keys_ref[j] = keep.astype(jnp.int32)
            return run + jnp.sum(eqf, axis=-1, keepdims=True)

        lax.fori_loop(0, nchunk, tbody, jnp.zeros((BLK, BLK), F32))
        thr_ref[...] = jnp.zeros_like(thr_ref)

    qn = norm256(q_ref[0].astype(F32), qg_ref[...]) * (HEAD_DIM ** -0.5)
    qs = jnp.concatenate([jnp.where(head_of_lane == h, qn, 0.0) for h in range(nh)], axis=0).astype(BF16)
    m_ref[...] = jnp.full(m_ref.shape, M_INIT, F32)
    l_ref[...] = jnp.zeros_like(l_ref)
    acc_ref[...] = jnp.zeros_like(acc_ref)

    thr_sel = thr_ref[...]

    def mask_body(j, c):
        madd_ref[j] = jnp.where(keys_ref[j] >= thr_sel, 0.0, -jnp.inf)
        return c

    lax.fori_loop(0, i, mask_body, 0)
    madd_ref[i] = jnp.where((keys_ref[i] >= thr_sel) & causal, 0.0, -jnp.inf)

    def far_chunk(first, n):
        r = pl.multiple_of(first * BLK, BLK)
        s = _nt_dot(qs, kn_ref[pl.ds(r, n * BLK), :])
        tiles = [[s[h * BLK:(h + 1) * BLK, c * BLK:(c + 1) * BLK] + madd_ref[first + c] for c in range(n)]
                 for h in range(nh)]
        _softmax_chunk(tiles, v_ref[0, pl.ds(r, n * BLK), :], m_ref, l_ref, acc_ref)

    def near_chunk(first, n):
        r = pl.multiple_of(first * BLK, BLK)
        s = _nt_dot(qs, kn_ref[pl.ds(r, n * BLK), :]) + bias_ref[:, (2 - n) * BLK:]
        tiles = [[s[h * BLK:(h + 1) * BLK, c * BLK:(c + 1) * BLK] + madd_ref[first + c] for c in range(n)]
                 for h in range(nh)]
        _softmax_chunk(tiles, v_ref[0, pl.ds(r, n * BLK), :], m_ref, l_ref, acc_ref)

    _for_far_blocks(jnp.maximum(i - 1, 0), far_chunk)

    @pl.when(i >= 1)
    def _():
        near_chunk(i - 1, 2)

    @pl.when(i == 0)
    def _():
        near_chunk(0, 1)

    inv = 1.0 / l_ref[...]
    a = acc_ref[...] * jnp.concatenate([inv, inv], axis=1)
    out = a[:BLK]
    for h in range(1, nh):
        out = jnp.where(head_of_lane == h, a[h * BLK:(h + 1) * BLK], out)
    o_ref[0] = out.astype(o_ref.dtype)


def _sp_attn(z3, rel_bias, qg, kg, bkt, rep, bcast, before, top_k):
    b, tp, _ = z3.shape
    nq = tp // BLK
    w = SP_HEADS * HEAD_DIM
    kern = functools.partial(_sp_kernel, nq=nq, top_k=top_k)
    c2 = lambda bb, i: (0, 0)
    return pl.pallas_call(
        kern,
        out_shape=jax.ShapeDtypeStruct((b, tp, w), BF16),
        grid=(b, nq),
        in_specs=[pl.BlockSpec(memory_space=pltpu.SMEM),
                  pl.BlockSpec((1, BLK, w), lambda bb, i: (bb, i, Q_SP // w)),
                  pl.BlockSpec((1, tp, w), lambda bb, i: (bb, 0, K_SP // w)),
                  pl.BlockSpec((1, tp, w), lambda bb, i: (bb, 0, V_SP // w)),
                  pl.BlockSpec((1, BLK, w), lambda bb, i: (bb, i, Q_IX // w)),
                  pl.BlockSpec((1, BLK, w), lambda bb, i: (bb, i, KW_IX // w)),
                  pl.BlockSpec((1, tp, w), lambda bb, i: (bb, 0, KW_IX // w)),
                  pl.BlockSpec((1, w), c2),
                  pl.BlockSpec((1, w), c2),
                  pl.BlockSpec((2, BLK, BLK), lambda bb, i: (0, 0, 0)),
                  pl.BlockSpec((w, w), c2),
                  pl.BlockSpec((w, IDX_HEADS * BLK), c2),
                  pl.BlockSpec((BLK, BLK), c2)],
        out_specs=pl.BlockSpec((1, BLK, w), lambda bb, i: (bb, i, 0)),
        scratch_shapes=[pltpu.VMEM((tp, w), BF16),
                        pltpu.VMEM((tp, w), BF16),
                        pltpu.VMEM((SP_HEADS * BLK, 2 * BLK), F32),
                        pltpu.VMEM((nq, BLK, BLK), jnp.int32),
                        pltpu.VMEM((nq, BLK, BLK), F32),
                        pltpu.VMEM((BLK, BLK), jnp.int32),
                        pltpu.VMEM((BLK, IDX_HEADS * BLK), F32),
                        pltpu.VMEM((SP_HEADS * BLK, BLK), F32),
                        pltpu.VMEM((SP_HEADS * BLK, BLK), F32),
                        pltpu.VMEM((SP_HEADS * BLK, w), F32)],
        compiler_params=_params("parallel", "arbitrary"),
        name="sp_attn",
    )(rel_bias, z3, z3, z3, z3, z3, z3, qg, kg, bkt, rep, bcast, before)


def _mix_kernel(h_ref, gsb_ref, gsp_ref, gdf_ref, bg_ref, ysb_ref, ysp_ref, ydf_ref,
                wsb_ref, wsp_ref, wdf_ref, wo_ref, o_ref):
    def branch(g_ref, k, y_ref, w_ref):
        gate = jax.nn.sigmoid(g_ref[...].astype(F32) + bg_ref[:, k * D_MODEL:(k + 1) * D_MODEL])
        return gate * _dot(y_ref[...], w_ref[...])

    merged = (branch(gsb_ref, 0, ysb_ref, wsb_ref) + branch(gsp_ref, 1, ysp_ref, wsp_ref)
              + branch(gdf_ref, 2, ydf_ref, wdf_ref))
    o_ref[...] = h_ref[...] + _dot(merged.astype(BF16), wo_ref[...])


def _mix_out(h, z, b_gate, y_sb, y_sp, y_df, w_sb, w_sp, w_df, w_o):
    m, d = h.shape
    tm = _pick_rows(m, 512)
    row = lambda i: (i, 0)
    fixed = lambda i: (0, 0)
    return pl.pallas_call(
        _mix_kernel,
        out_shape=jax.ShapeDtypeStruct((m, d), F32),
        grid=(m // tm,),
        in_specs=[pl.BlockSpec((tm, d), row),
                  pl.BlockSpec((tm, d), lambda i: (i, G_SB // D_MODEL)),
                  pl.BlockSpec((tm, d), lambda i: (i, G_SP // D_MODEL)),
                  pl.BlockSpec((tm, d), lambda i: (i, G_DF // D_MODEL)),
                  pl.BlockSpec((1, 3 * d), fixed),
                  pl.BlockSpec((tm, y_sb.shape[1]), row),
                  pl.BlockSpec((tm, y_sp.shape[1]), row),
                  pl.BlockSpec((tm, y_df.shape[1]), row),
                  pl.BlockSpec(w_sb.shape, fixed),
                  pl.BlockSpec(w_sp.shape, fixed),
                  pl.BlockSpec(w_df.shape, fixed),
                  pl.BlockSpec(w_o.shape, fixed)],
        out_specs=pl.BlockSpec((tm, d), row),
        compiler_params=_params("parallel"),
        name="mix_out",
    )(h, z, z, z, b_gate.reshape(1, 3 * d), y_sb, y_sp, y_df, w_sb, w_sp, w_df, w_o)


def _ffn_kernel(h_ref, g_ref, wg_ref, wv_ref, cw_ref, cb_ref, wd_ref, o_ref,
                u_ref, gbuf_ref, carry_ref, acc_ref, *, tm, tp):
    r = pl.program_id(0)
    f = pl.program_id(1)

    @pl.when(f == 0)
    def _():
        x = h_ref[...]
        ms = jnp.mean(x * x, axis=-1, keepdims=True)
        u_ref[...] = (x * lax.rsqrt(ms + EPS) * g_ref[...]).astype(BF16)
        acc_ref[...] = jnp.zeros_like(acc_ref)

    u = u_ref[...]
    gate = _dot(u, wg_ref[...])
    val = _dot(u, wv_ref[...])
    @pl.when(r == 0)
    def _():
        carry_ref[f] = jnp.zeros((8, gate.shape[1]), F32)

    gbuf_ref[0:8] = carry_ref[f]
    gbuf_ref[8:8 + tm] = gate
    carry_ref[f] = gate[tm - 8:tm]
    seq_start = lax.rem(tp - lax.rem(r * tm, tp), tp)
    local = lax.broadcasted_iota(jnp.int32, (tm, 1), 0)
    g1 = jnp.where(local != seq_start, gbuf_ref[7:7 + tm], 0.0)
    g2 = jnp.where((local != seq_start) & (local != seq_start + 1), gbuf_ref[6:6 + tm], 0.0)
    cw = cw_ref[...]
    conv = cb_ref[...] + cw[0:1] * g2 + cw[1:2] * g1 + cw[2:3] * gate
    act = conv * jax.nn.sigmoid(conv) * val
    acc_ref[...] += _dot(act.astype(BF16), wd_ref[...])

    @pl.when(f == pl.num_programs(1) - 1)
    def _():
        o_ref[...] = h_ref[...] + acc_ref[...]


def _ffn(h, gain, w_up, conv_w, conv_b, w_down, tp):
    m, d = h.shape
    tm = _pick_rows(m, 512)
    tf = D_FF // 2
    nf = D_FF // tf
    kern = functools.partial(_ffn_kernel, tm=tm, tp=tp)
    return pl.pallas_call(
        kern,
        out_shape=jax.ShapeDtypeStruct((m, d), F32),
        grid=(m // tm, nf),
        in_specs=[pl.BlockSpec((tm, d), lambda r, f: (r, 0)),
                  pl.BlockSpec((1, d), lambda r, f: (0, 0)),
                  pl.BlockSpec((d, tf), lambda r, f: (0, f)),
                  pl.BlockSpec((d, tf), lambda r, f: (0, nf + f)),
                  pl.BlockSpec((8, tf), lambda r, f: (0, f)),
                  pl.BlockSpec((1, tf), lambda r, f: (0, f)),
                  pl.BlockSpec((tf, d), lambda r, f: (f, 0))],
        out_specs=pl.BlockSpec((tm, d), lambda r, f: (r, 0)),
        scratch_shapes=[pltpu.VMEM((tm, d), BF16),
                        pltpu.VMEM((tm + 8, tf), F32),
                        pltpu.VMEM((nf, 8, tf), F32),
                        pltpu.VMEM((tm, d), F32)],
        compiler_params=_params("arbitrary", "arbitrary"),
        name="conv_ffn",
    )(h, gain.reshape(1, d), w_up, w_up, conv_w, conv_b.reshape(1, D_FF), w_down)


def _permute_w_in(w):
    n_attn = KW_IX - Q_SB + IDX_DIM + IDX_HEADS
    n_gate = 3 * D_MODEL
    n_df = 3 * DF_HEADS * 2 * HEAD_DIM
    gates = w[:, n_attn + n_df:]
    attn = w[:, :n_attn]
    pad = jnp.zeros((w.shape[0], Q_DF - Q_SB - n_attn), w.dtype)
    df = w[:, n_attn:n_attn + n_df]
    out = jnp.concatenate([gates, attn, pad, df], axis=1)
    assert gates.shape[1] == n_gate and out.shape[1] == NZ
    return out


def kernel(x, meta_tokens, rel_bias, attn_norm, w_in, b_gate, q_norm_sp, k_norm_sp, q_norm_df, k_norm_df, lam_q1, lam_k1, lam_q2, lam_k2, subln_df, w_br_sb, w_br_sp, w_br_df, w_out, ffn_norm, w_up, conv_w, conv_b, w_down):
    b, s, d = x.shape
    depth = w_in.shape[0]
    t = N_META + s
    tp = -(-t // BLK) * BLK
    top_k = min(TOPK_MAX, t // 4)
    m = b * tp

    meta = jnp.broadcast_to(meta_tokens[None].astype(x.dtype), (b, N_META, d))
    h = jnp.concatenate([meta, x, jnp.zeros((b, tp - t, d), x.dtype)], axis=1).reshape(m, d)

    bkt = jnp.asarray(_bucket_tiles())
    u_mat = jnp.asarray(_sb_prefix_matrix(), BF16)
    rep, bcast, before = (jnp.asarray(a, BF16) for a in _ix_select_matrices())
    rel_bias = rel_bias.astype(F32)

    for l in range(depth):
        lam_init = 0.8 - 0.6 * math.exp(-0.3 * l)
        z = _in_proj(h, attn_norm[l], _permute_w_in(w_in[l]).astype(BF16))
        z3 = z.reshape(b, tp, NZ)
        y_sb = _sb_attn(z3, u_mat)
        y_sp = _sp_attn(z3, rel_bias,
                        jnp.tile(q_norm_sp[l].astype(F32), SP_HEADS).reshape(1, -1),
                        jnp.tile(k_norm_sp[l].astype(F32), SP_HEADS).reshape(1, -1),
                        bkt, rep, bcast, before, top_k)
        lamv = jnp.zeros((8, BLK), F32).at[:4, :HEAD_DIM].set(
            jnp.stack([lam_q1[l], lam_k1[l], lam_q2[l], lam_k2[l]]).astype(F32))
        y_df = _df_attn(z3, rel_bias,
                        jnp.tile(q_norm_df[l].astype(F32), 2).reshape(1, -1),
                        jnp.tile(k_norm_df[l].astype(F32), 2).reshape(1, -1),
                        bkt, lamv, subln_df[l].astype(F32).reshape(1, -1), lam_init)
        h = _mix_out(h, z, b_gate[l], y_sb.reshape(m, -1), y_sp.reshape(m, -1), y_df.reshape(m, -1),
                     w_br_sb[l].astype(BF16), w_br_sp[l].astype(BF16), w_br_df[l].astype(BF16),
                     w_out[l].astype(BF16))
        cw = jnp.zeros((8, D_FF), F32).at[:conv_w.shape[1]].set(conv_w[l])
        h = _ffn(h, ffn_norm[l], w_up[l].astype(BF16), cw, conv_b[l], w_down[l].astype(BF16), tp)

    return h.reshape(b, tp, d)[:, N_META:t]
```

```python
import functools
import math

import numpy as np
import jax
import jax.numpy as jnp
from jax import lax
from jax.experimental import pallas as pl
from jax.experimental.pallas import tpu as pltpu

D_MODEL = 1024
HEAD_DIM = 64
N_META = 16
BLK = 128
QB = 256
SB_HEADS = 4
SP_HEADS = 4
IDX_HEADS = 8
IDX_DIM = 32
TOPK_MAX = 256
DF_HEADS = 4
N_BUCKETS = 32
MAX_DISTANCE = 128
D_FF = 2816
EPS = 1e-6
M_INIT = -1e30
INT_MIN = -2 ** 31
ROW_GROUP = 64
SB_FAR = 2
SP_FAR = 2
DF_FAR = 4

G_SB, G_SP, G_DF = 0, 1024, 2048
Q_SB, K_SB, V_SB = 3072, 3328, 3584
Q_SP, K_SP, V_SP = 3840, 4096, 4352
Q_IX, KW_IX = 4608, 4864
Q_DF, K_DF, V_DF = 5120, 5632, 6144
NZ = 6656
W_IX_LANE = IDX_DIM

VMEM_LIMIT = 56 * 1024 * 1024

F32 = jnp.float32
BF16 = jnp.bfloat16
NT_DIMS = (((1,), (1,)), ((), ()))


def _nt_dot(a, b):
    return lax.dot_general(a, b, NT_DIMS, preferred_element_type=F32)


def _dot(a, b):
    return jnp.dot(a, b, preferred_element_type=F32)


def _params(*sem):
    return pltpu.CompilerParams(dimension_semantics=sem, vmem_limit_bytes=VMEM_LIMIT)


def _pick_rows(m, cap):
    for c in (2048, 1024, 512, 256):
        if c <= cap and m % c == 0:
            return c
    raise ValueError(f"row count {m} is not a multiple of {QB}")


def _bucket_np(rel):
    n = np.maximum(rel, 0)
    max_exact = N_BUCKETS // 2
    nf = np.maximum(n, 1).astype(np.float32)
    large = max_exact + (np.log(nf / np.float32(max_exact)) / np.float32(math.log(MAX_DISTANCE / max_exact))
                         * np.float32(N_BUCKETS - max_exact)).astype(np.int32)
    return np.where(n < max_exact, n, np.minimum(large, N_BUCKETS - 1)).astype(np.int32)


def _bucket_tile():
    tq = np.arange(QB)[:, None]
    c = np.arange(2 * QB)[None, :]
    return _bucket_np(tq - c + QB)


def _sb_prefix_matrix():
    sp = np.arange(2 * BLK)[:, None] % BLK
    c = np.arange(2 * BLK)[None, :]
    return np.where(c < BLK, sp > c, True).astype(np.float32)


def _ix_select_matrices():
    c = np.arange(QB)[:, None]
    col = np.arange(QB)[None, :]
    rep = ((c < IDX_DIM) & (c == col % IDX_DIM)).astype(np.float32)
    col8 = np.arange(IDX_HEADS * BLK)[None, :]
    bcast = (c == W_IX_LANE + col8 // BLK).astype(np.float32)
    before = (c < col).astype(np.float32)
    return rep, bcast, before


def _lane_iota(shape):
    return lax.broadcasted_iota(jnp.int32, shape, len(shape) - 1)


def _row_iota(shape):
    return lax.broadcasted_iota(jnp.int32, shape, 0)


def _head_rmsnorm128(xf, gain):
    lo = _lane_iota((1, BLK)) < HEAD_DIM
    ss = xf * xf
    s_lo = jnp.sum(jnp.where(lo, ss, 0.0), axis=-1, keepdims=True)
    s_hi = jnp.sum(jnp.where(lo, 0.0, ss), axis=-1, keepdims=True)
    ms = jnp.where(lo, s_lo, s_hi) * (1.0 / HEAD_DIM)
    return xf * lax.rsqrt(ms + EPS) * gain


def _bias_tile(tab_ref, bk, head):
    far = tab_ref[N_BUCKETS - 1, head]
    acc = jnp.zeros(bk.shape, F32)
    for b in range(N_BUCKETS - 1):
        acc = jnp.where(bk == b, tab_ref[b, head] - far, acc)
    return acc


def _softmax_piece(s, v, m_ref, l_ref, acc_ref):
    parts = [s[:, c * BLK:(c + 1) * BLK] for c in range(s.shape[1] // BLK)]
    m_old = m_ref[...]
    m_new = jnp.maximum(m_old, jnp.max(functools.reduce(jnp.maximum, parts), axis=-1, keepdims=True))
    alpha = jnp.exp(m_old - m_new)
    ps = [jnp.exp(t - m_new) for t in parts]
    l_ref[...] = alpha * l_ref[...] + jnp.sum(functools.reduce(jnp.add, ps), axis=-1, keepdims=True)
    m_ref[...] = m_new
    p = jnp.concatenate([t.astype(BF16) for t in ps], axis=1)
    w = acc_ref.shape[-1] // BLK
    a = alpha if w == 1 else jnp.concatenate([alpha] * w, axis=1)
    acc_ref[...] = a * acc_ref[...] + _dot(p, v)


def _for_far_pieces(n_far, width, piece_fn):
    def body(p, carry):
        hi = n_far - p * width
        piece_fn(jnp.maximum(hi - width, 0), hi)
        return carry

    lax.fori_loop(0, (n_far + width - 1) // width, body, 0)


def _new_key_mask(start, hi, width):
    col = _lane_iota((1, width * QB))
    return jnp.where(col < (hi - start) * QB, 0.0, -jnp.inf)


def _in_proj_kernel(h_ref, g_ref, w_ref, o_ref, u_ref):
    @pl.when(pl.program_id(1) == 0)
    def _():
        x = h_ref[...]
        ms = jnp.mean(x * x, axis=-1, keepdims=True)
        u_ref[...] = (x * lax.rsqrt(ms + EPS) * g_ref[...]).astype(BF16)

    o_ref[...] = _dot(u_ref[...], w_ref[...]).astype(o_ref.dtype)


def _in_proj(h, gain, w):
    m, d = h.shape
    n = w.shape[1]
    tm = _pick_rows(m, 1024)
    tn = 512
    return pl.pallas_call(
        _in_proj_kernel,
        out_shape=jax.ShapeDtypeStruct((m, n), BF16),
        grid=(m // tm, n // tn),
        in_specs=[pl.BlockSpec((tm, d), lambda i, j: (i, 0)),
                  pl.BlockSpec((1, d), lambda i, j: (0, 0)),
                  pl.BlockSpec((d, tn), lambda i, j: (0, j))],
        out_specs=pl.BlockSpec((tm, tn), lambda i, j: (i, j)),
        scratch_shapes=[pltpu.VMEM((tm, d), BF16)],
        compiler_params=_params("parallel", "arbitrary"),
        name="in_proj",
    )(h, gain.reshape(1, d), w)


def _sb_kernel(q_ref, k_ref, v_ref, u_ref, o_ref, tot_ref, acc_ref):
    i = pl.program_id(1)
    nh = SB_HEADS
    head_of_lane = _lane_iota((QB, nh * HEAD_DIM)) // HEAD_DIM
    q = q_ref[0].astype(F32) * (HEAD_DIM ** -0.5)
    qs = jnp.concatenate([jnp.where(head_of_lane == h, q, 0.0) for h in range(nh)], axis=0).astype(BF16)
    tot_ref[...] = jnp.zeros_like(tot_ref)
    acc_ref[...] = jnp.zeros_like(acc_ref)

    def piece(start, n_blocks, hi=None):
        r = pl.multiple_of(start * QB, QB)
        z_all = _nt_dot(qs, k_ref[0, pl.ds(r, n_blocks * QB), :])
        if hi is not None:
            z_all = z_all + _new_key_mask(start, hi, n_blocks)
        run = tot_ref[...]
        n_sub = n_blocks * QB // BLK
        ws = [None] * n_sub
        for c in reversed(range(n_sub)):
            z = z_all[:, c * BLK:(c + 1) * BLK]
            sp = jnp.maximum(z, 0.0) + jnp.log(1.0 + jnp.exp(-jnp.abs(z)))
            l1m = -sp
            if hi is None:
                mask = (_lane_iota(z.shape) + c * BLK) < (_row_iota(z.shape) & (QB - 1))
                l1m = jnp.where(mask, l1m, 0.0)
            l1m_hi = l1m.astype(BF16)
            l1m_lo = (l1m - l1m_hi.astype(F32)).astype(BF16)
            rs = _dot(jnp.concatenate([l1m_hi, l1m_lo], axis=1), u_ref[...])
            w = jnp.exp((z - sp) + rs[:, :BLK] + run)
            if hi is None:
                w = jnp.where(mask, w, 0.0)
            ws[c] = w.astype(BF16)
            run = run + rs[:, BLK:]
        tot_ref[...] = run
        acc_ref[...] += _dot(jnp.concatenate(ws, axis=1), v_ref[0, pl.ds(r, n_blocks * QB), :])

    piece(i, 1)
    _for_far_pieces(i, SB_FAR, lambda start, hi: piece(start, SB_FAR, hi))

    a = acc_ref[...]
    out = a[:QB]
    for h in range(1, nh):
        out = jnp.where(head_of_lane == h, a[h * QB:(h + 1) * QB], out)
    o_ref[0] = out.astype(o_ref.dtype)


def _sb_attn(z3, u_mat):
    b, tp, _ = z3.shape
    nq = tp // QB
    assert nq >= SB_FAR
    w = SB_HEADS * HEAD_DIM
    return pl.pallas_call(
        _sb_kernel,
        out_shape=jax.ShapeDtypeStruct((b, tp, w), BF16),
        grid=(b, nq),
        in_specs=[pl.BlockSpec((1, QB, w), lambda bb, i: (bb, i, Q_SB // w)),
                  pl.BlockSpec((1, tp, w), lambda bb, i: (bb, 0, K_SB // w)),
                  pl.BlockSpec((1, tp, w), lambda bb, i: (bb, 0, V_SB // w)),
                  pl.BlockSpec((2 * BLK, 2 * BLK), lambda bb, i: (0, 0))],
        out_specs=pl.BlockSpec((1, QB, w), lambda bb, i: (bb, i, 0)),
        scratch_shapes=[pltpu.VMEM((SB_HEADS * QB, BLK), F32),
                        pltpu.VMEM((SB_HEADS * QB, w), F32)],
        compiler_params=_params("parallel", "arbitrary"),
        name="sb_attn",
    )(z3, z3, z3, u_mat)


def _df_kernel(tab_ref, q_ref, k_ref, v_ref, qg_ref, kg_ref, bkt_ref, lamv_ref, sub_ref, o_ref,
               kn_ref, bias_ref, m_ref, l_ref, acc_ref, *, nq, lam_init, head_off):
    h = pl.program_id(1)
    i = pl.program_id(2)
    lo = _lane_iota((1, BLK)) < HEAD_DIM

    @pl.when(i == 0)
    def _prep():
        def kbody(c, carry):
            r = pl.multiple_of(c * QB, QB)
            kf = k_ref[0, pl.ds(r, QB), :].astype(F32)
            kn_ref[pl.ds(r, QB), :] = _head_rmsnorm128(kf, kg_ref[...]).astype(BF16)
            return carry

        lax.fori_loop(0, nq, kbody, 0)
        bias_ref[...] = _bias_tile(tab_ref, bkt_ref[...], head_off + h)

    qn = _head_rmsnorm128(q_ref[0].astype(F32), qg_ref[...]) * (HEAD_DIM ** -0.5)
    qs = jnp.concatenate([jnp.where(lo, qn, 0.0), jnp.where(lo, 0.0, qn)], axis=0).astype(BF16)
    m_ref[...] = jnp.full(m_ref.shape, M_INIT, F32)
    l_ref[...] = jnp.zeros_like(l_ref)
    acc_ref[...] = jnp.zeros_like(acc_ref)

    def far_piece(start, hi):
        r = pl.multiple_of(start * QB, QB)
        s = _nt_dot(qs, kn_ref[pl.ds(r, DF_FAR * QB), :]) + _new_key_mask(start, hi, DF_FAR)
        _softmax_piece(s, v_ref[0, pl.ds(r, DF_FAR * QB), :], m_ref, l_ref, acc_ref)

    def near_piece(first, n):
        r = pl.multiple_of(first * QB, QB)
        b = bias_ref[:, (2 - n) * QB:]
        s = _nt_dot(qs, kn_ref[pl.ds(r, n * QB), :]) + jnp.concatenate([b, b], axis=0)
        causal = (_lane_iota(s.shape) - (n - 1) * QB) <= (_row_iota(s.shape) & (QB - 1))
        s = jnp.where(causal, s, -jnp.inf)
        _softmax_piece(s, v_ref[0, pl.ds(r, n * QB), :], m_ref, l_ref, acc_ref)

    _for_far_pieces(jnp.maximum(i - 1, 0), DF_FAR, far_piece)

    @pl.when(i >= 1)
    def _():
        near_piece(i - 1, 2)

    @pl.when(i == 0)
    def _():
        near_piece(0, 1)

    a = acc_ref[...] / l_ref[...]
    lv = lamv_ref[...]
    lam = (jnp.exp(jnp.sum(lv[0:1] * lv[1:2], axis=-1, keepdims=True))
           - jnp.exp(jnp.sum(lv[2:3] * lv[3:4], axis=-1, keepdims=True)) + lam_init)
    y = a[:QB] - lam * a[QB:]
    y = y * lax.rsqrt(jnp.mean(y * y, axis=-1, keepdims=True) + EPS) * sub_ref[...]
    o_ref[0] = (y * (1.0 - lam_init)).astype(o_ref.dtype)


def _df_attn(z3, rel_bias, qg, kg, bkt, lamv, subln, lam_init):
    b, tp, _ = z3.shape
    nq = tp // QB
    assert nq >= DF_FAR
    kern = functools.partial(_df_kernel, nq=nq, lam_init=lam_init, head_off=SP_HEADS)
    vec = lambda bb, h, i: (0, 0)
    return pl.pallas_call(
        kern,
        out_shape=jax.ShapeDtypeStruct((b, tp, DF_HEADS * BLK), BF16),
        grid=(b, DF_HEADS, nq),
        in_specs=[pl.BlockSpec(memory_space=pltpu.SMEM),
                  pl.BlockSpec((1, QB, BLK), lambda bb, h, i: (bb, i, Q_DF // BLK + h)),
                  pl.BlockSpec((1, tp, BLK), lambda bb, h, i: (bb, 0, K_DF // BLK + h)),
                  pl.BlockSpec((1, tp, BLK), lambda bb, h, i: (bb, 0, V_DF // BLK + h)),
                  pl.BlockSpec((1, BLK), vec),
                  pl.BlockSpec((1, BLK), vec),
                  pl.BlockSpec((QB, 2 * QB), vec),
                  pl.BlockSpec((8, BLK), vec),
                  pl.BlockSpec((1, BLK), vec)],
        out_specs=pl.BlockSpec((1, QB, BLK), lambda bb, h, i: (bb, i, h)),
        scratch_shapes=[pltpu.VMEM((tp, BLK), BF16),
                        pltpu.VMEM((QB, 2 * QB), F32),
                        pltpu.VMEM((2 * QB, BLK), F32),
                        pltpu.VMEM((2 * QB, BLK), F32),
                        pltpu.VMEM((2 * QB, BLK), F32)],
        compiler_params=_params("parallel", "parallel", "arbitrary"),
        name="df_attn",
    )(rel_bias, z3, z3, z3, qg, kg, bkt, lamv, subln)


def _sp_kernel(tab_ref, q_ref, k_ref, v_ref, qix_ref, kwq_ref, kwk_ref, qg_ref, kg_ref, bkt_ref,
               rep_ref, bcast_ref, before_ref, o_ref,
               kn_ref, kx_ref, bias_ref, keys_ref, thr_ref, wb_ref, m_ref, l_ref, acc_ref,
               *, nq, top_k):
    i = pl.program_id(1)
    nh = SP_HEADS
    w = nh * HEAD_DIM
    lane = _lane_iota((QB, w))
    head_of_lane = lane // HEAD_DIM

    def norm256(xf, g):
        return jnp.concatenate([_head_rmsnorm128(xf[:, :BLK], g[:, :BLK]),
                                _head_rmsnorm128(xf[:, BLK:], g[:, BLK:])], axis=1)

    def dup(x):
        return jnp.concatenate([x, x], axis=1)

    @pl.when(i == 0)
    def _prep():
        def kbody(c, carry):
            r = pl.multiple_of(c * QB, QB)
            kn_ref[pl.ds(r, QB), :] = norm256(k_ref[0, pl.ds(r, QB), :].astype(F32), kg_ref[...]).astype(BF16)
            kx_ref[pl.ds(r, QB), :] = _dot(kwk_ref[0, pl.ds(r, QB), :], rep_ref[...]).astype(BF16)
            return carry

        lax.fori_loop(0, nq, kbody, 0)
        for h in range(nh):
            bias_ref[h] = _bias_tile(tab_ref, bkt_ref[...], h)

    causal = _lane_iota((QB, QB)) <= _row_iota((QB, QB))

    wb_ref[...] = _dot(kwq_ref[0], bcast_ref[...])
    qix = qix_ref[0].astype(F32)
    ix_head = lane // IDX_DIM
    qx = jnp.concatenate([jnp.where(ix_head == h, qix, 0.0) for h in range(IDX_HEADS)], axis=0).astype(BF16)

    def score_tile(j, diag):
        r = pl.multiple_of(j * QB, QB)
        d = _nt_dot(qx, kx_ref[pl.ds(r, QB), :])
        sc = jnp.zeros((QB, QB), F32)
        for h in range(IDX_HEADS):
            sc = sc + dup(wb_ref[:, h * BLK:(h + 1) * BLK]) * jnp.maximum(d[h * QB:(h + 1) * QB], 0.0)
        sc = jnp.where(sc == 0.0, 0.0, sc)
        bits = lax.bitcast_convert_type(sc, jnp.int32)
        key = jnp.where(bits < 0, bits ^ jnp.int32(0x7FFFFFFF), bits)
        if diag:
            key = jnp.where(causal, key, jnp.int32(INT_MIN))
        keys_ref[j] = key

    def score_body(j, c):
        score_tile(j, False)
        return c

    lax.fori_loop(0, i, score_body, 0)
    score_tile(i, True)

    nblk = i + 1

    def count(pred):
        parts = []
        for g in range(QB // ROW_GROUP):
            rows = slice(g * ROW_GROUP, (g + 1) * ROW_GROUP)

            def cbody(j, c, rows=rows):
                kj = keys_ref[j, rows, :]
                return c + jnp.where(pred(kj[:, :BLK], rows), 1.0, 0.0) + jnp.where(pred(kj[:, BLK:], rows), 1.0, 0.0)

            parts.append(lax.fori_loop(0, nblk, cbody, jnp.zeros((ROW_GROUP, BLK), F32)))
        return jnp.sum(jnp.concatenate(parts, axis=0), axis=-1, keepdims=True)

    def bit_body(b, carry):
        cur, cnt_cur = carry
        cand = cur + jnp.left_shift(jnp.int32(1), 31 - b)
        cnt = count(lambda k, rows: k >= cand[rows])
        ok = cnt >= float(top_k)
        return jnp.where(ok, cand, cur), jnp.where(ok, cnt, cnt_cur)

    cur0 = jnp.full((QB, BLK), INT_MIN, jnp.int32)
    cnt0 = jnp.zeros((QB, BLK), F32) + (nblk * QB).astype(F32)
    thr, cge = lax.fori_loop(0, 32, bit_body, (cur0, cnt0))
    thr_ref[...] = thr

    tie = jnp.where((cge > float(top_k)) & (thr > INT_MIN), 1, 0)

    @pl.when(jnp.max(tie) > 0)
    def _ties():
        need = float(top_k) - count(lambda k, rows: k > thr[rows])
        thr2 = dup(thr)

        def tbody(j, run):
            kj = keys_ref[j]
            eq = kj == thr2
            eqf = jnp.where(eq, 1.0, 0.0)
            rank = dup(run) + _dot(eqf.astype(BF16), before_ref[...])
            keep = jnp.where(kj > thr2, 1, jnp.where(eq & (rank < need), 1, -1))
            keys_ref[j] = keep.astype(jnp.int32)
            return run + jnp.sum(eqf, axis=-1, keepdims=True)

        lax.fori_loop(0, nblk, tbody, jnp.zeros((QB, BLK), F32))
        thr_ref[...] = jnp.zeros_like(thr_ref)

    qn = norm256(q_ref[0].astype(F32), qg_ref[...]) * (HEAD_DIM ** -0.5)
    qs = jnp.concatenate([jnp.where(head_of_lane == h, qn, 0.0) for h in range(nh)], axis=0).astype(BF16)
    m_ref[...] = jnp.full(m_ref.shape, M_INIT, F32)
    l_ref[...] = jnp.zeros_like(l_ref)
    acc_ref[...] = jnp.zeros_like(acc_ref)

    thr_sel = dup(thr_ref[...])

    def as_mask(sel):
        return lax.bitcast_convert_type(jnp.where(sel, 0.0, -jnp.inf), jnp.int32)

    def mask_body(j, c):
        keys_ref[j] = as_mask(keys_ref[j] >= thr_sel)
        return c

    lax.fori_loop(0, i, mask_body, 0)
    keys_ref[i] = as_mask((keys_ref[i] >= thr_sel) & causal)

    def selection(first, n):
        return jnp.concatenate([lax.bitcast_convert_type(keys_ref[first + c], F32) for c in range(n)], axis=1)

    def far_piece(start, hi):
        r = pl.multiple_of(start * QB, QB)
        s = _nt_dot(qs, kn_ref[pl.ds(r, SP_FAR * QB), :])
        madd = selection(start, SP_FAR) + _new_key_mask(start, hi, SP_FAR)
        s = jnp.concatenate([s[h * QB:(h + 1) * QB] + madd for h in range(nh)], axis=0)
        _softmax_piece(s, v_ref[0, pl.ds(r, SP_FAR * QB), :], m_ref, l_ref, acc_ref)

    def near_piece(first, n):
        r = pl.multiple_of(first * QB, QB)
        s = _nt_dot(qs, kn_ref[pl.ds(r, n * QB), :])
        madd = selection(first, n)
        s = jnp.concatenate([s[h * QB:(h + 1) * QB] + (madd + bias_ref[h][:, (2 - n) * QB:]) for h in range(nh)],
                            axis=0)
        _softmax_piece(s, v_ref[0, pl.ds(r, n * QB), :], m_ref, l_ref, acc_ref)

    _for_far_pieces(jnp.maximum(i - 1, 0), SP_FAR, far_piece)

    @pl.when(i >= 1)
    def _():
        near_piece(i - 1, 2)

    @pl.when(i == 0)
    def _():
        near_piece(0, 1)

    a = acc_ref[...] * dup(1.0 / l_ref[...])
    out = a[:QB]
    for h in range(1, nh):
        out = jnp.where(head_of_lane == h, a[h * QB:(h + 1) * QB], out)
    o_ref[0] = out.astype(o_ref.dtype)


def _sp_attn(z3, rel_bias, qg, kg, bkt, rep, bcast, before, top_k):
    b, tp, _ = z3.shape
    nq = tp // QB
    assert nq >= SP_FAR
    w = SP_HEADS * HEAD_DIM
    kern = functools.partial(_sp_kernel, nq=nq, top_k=top_k)
    c2 = lambda bb, i: (0, 0)
    return pl.pallas_call(
        kern,
        out_shape=jax.ShapeDtypeStruct((b, tp, w), BF16),
        grid=(b, nq),
        in_specs=[pl.BlockSpec(memory_space=pltpu.SMEM),
                  pl.BlockSpec((1, QB, w), lambda bb, i: (bb, i, Q_SP // w)),
                  pl.BlockSpec((1, tp, w), lambda bb, i: (bb, 0, K_SP // w)),
                  pl.BlockSpec((1, tp, w), lambda bb, i: (bb, 0, V_SP // w)),
                  pl.BlockSpec((1, QB, w), lambda bb, i: (bb, i, Q_IX // w)),
                  pl.BlockSpec((1, QB, w), lambda bb, i: (bb, i, KW_IX // w)),
                  pl.BlockSpec((1, tp, w), lambda bb, i: (bb, 0, KW_IX // w)),
                  pl.BlockSpec((1, w), c2),
                  pl.BlockSpec((1, w), c2),
                  pl.BlockSpec((QB, 2 * QB), c2),
                  pl.BlockSpec((w, w), c2),
                  pl.BlockSpec((w, IDX_HEADS * BLK), c2),
                  pl.BlockSpec((QB, QB), c2)],
        out_specs=pl.BlockSpec((1, QB, w), lambda bb, i: (bb, i, 0)),
        scratch_shapes=[pltpu.VMEM((tp, w), BF16),
                        pltpu.VMEM((tp, w), BF16),
                        pltpu.VMEM((SP_HEADS, QB, 2 * QB), F32),
                        pltpu.VMEM((nq, QB, QB), jnp.int32),
                        pltpu.VMEM((QB, BLK), jnp.int32),
                        pltpu.VMEM((QB, IDX_HEADS * BLK), F32),
                        pltpu.VMEM((SP_HEADS * QB, BLK), F32),
                        pltpu.VMEM((SP_HEADS * QB, BLK), F32),
                        pltpu.VMEM((SP_HEADS * QB, w), F32)],
        compiler_params=_params("parallel", "arbitrary"),
        name="sp_attn",
    )(rel_bias, z3, z3, z3, z3, z3, z3, qg, kg, bkt, rep, bcast, before)


def _mix_kernel(h_ref, gsb_ref, gsp_ref, gdf_ref, bg_ref, ysb_ref, ysp_ref, ydf_ref,
                wsb_ref, wsp_ref, wdf_ref, wo_ref, o_ref):
    def branch(g_ref, k, y_ref, w_ref):
        gate = jax.nn.sigmoid(g_ref[...].astype(F32) + bg_ref[:, k * D_MODEL:(k + 1) * D_MODEL])
        return gate * _dot(y_ref[...], w_ref[...])

    merged = (branch(gsb_ref, 0, ysb_ref, wsb_ref) + branch(gsp_ref, 1, ysp_ref, wsp_ref)
              + branch(gdf_ref, 2, ydf_ref, wdf_ref))
    o_ref[...] = h_ref[...] + _dot(merged.astype(BF16), wo_ref[...])


def _mix_out(h, z, b_gate, y_sb, y_sp, y_df, w_sb, w_sp, w_df, w_o):
    m, d = h.shape
    tm = _pick_rows(m, 512)
    row = lambda i: (i, 0)
    fixed = lambda i: (0, 0)
    return pl.pallas_call(
        _mix_kernel,
        out_shape=jax.ShapeDtypeStruct((m, d), F32),
        grid=(m // tm,),
        in_specs=[pl.BlockSpec((tm, d), row),
                  pl.BlockSpec((tm, d), lambda i: (i, G_SB // D_MODEL)),
                  pl.BlockSpec((tm, d), lambda i: (i, G_SP // D_MODEL)),
                  pl.BlockSpec((tm, d), lambda i: (i, G_DF // D_MODEL)),
                  pl.BlockSpec((1, 3 * d), fixed),
                  pl.BlockSpec((tm, y_sb.shape[1]), row),
                  pl.BlockSpec((tm, y_sp.shape[1]), row),
                  pl.BlockSpec((tm, y_df.shape[1]), row),
                  pl.BlockSpec(w_sb.shape, fixed),
                  pl.BlockSpec(w_sp.shape, fixed),
                  pl.BlockSpec(w_df.shape, fixed),
                  pl.BlockSpec(w_o.shape, fixed)],
        out_specs=pl.BlockSpec((tm, d), row),
        compiler_params=_params("parallel"),
        name="mix_out",
    )(h, z, z, z, b_gate.reshape(1, 3 * d), y_sb, y_sp, y_df, w_sb, w_sp, w_df, w_o)


def _ffn_kernel(h_ref, g_ref, wg_ref, wv_ref, cw_ref, cb_ref, wd_ref, o_ref,
                u_ref, gbuf_ref, carry_ref, acc_ref, *, tm, tp):
    r = pl.program_id(0)
    f = pl.program_id(1)

    @pl.when(f == 0)
    def _():
        x = h_ref[...]
        ms = jnp.mean(x * x, axis=-1, keepdims=True)
        u_ref[...] = (x * lax.rsqrt(ms + EPS) * g_ref[...]).astype(BF16)
        acc_ref[...] = jnp.zeros_like(acc_ref)

    u = u_ref[...]
    gate = _dot(u, wg_ref[...])
    val = _dot(u, wv_ref[...])
    @pl.when(r == 0)
    def _():
        carry_ref[f] = jnp.zeros((8, gate.shape[1]), F32)

    gbuf_ref[0:8] = carry_ref[f]
    gbuf_ref[8:8 + tm] = gate
    carry_ref[f] = gate[tm - 8:tm]
    seq_start = lax.rem(tp - lax.rem(r * tm, tp), tp)
    local = lax.broadcasted_iota(jnp.int32, (tm, 1), 0)
    g1 = jnp.where(local != seq_start, gbuf_ref[7:7 + tm], 0.0)
    g2 = jnp.where((local != seq_start) & (local != seq_start + 1), gbuf_ref[6:6 + tm], 0.0)
    cw = cw_ref[...]
    conv = cb_ref[...] + cw[0:1] * g2 + cw[1:2] * g1 + cw[2:3] * gate
    act = conv * jax.nn.sigmoid(conv) * val
    acc_ref[...] += _dot(act.astype(BF16), wd_ref[...])

    @pl.when(f == pl.num_programs(1) - 1)
    def _():
        o_ref[...] = h_ref[...] + acc_ref[...]


def _ffn(h, gain, w_up, conv_w, conv_b, w_down, tp):
    m, d = h.shape
    tm = _pick_rows(m, 512)
    assert tm <= tp
    tf = D_FF // 2
    nf = D_FF // tf
    kern = functools.partial(_ffn_kernel, tm=tm, tp=tp)
    return pl.pallas_call(
        kern,
        out_shape=jax.ShapeDtypeStruct((m, d), F32),
        grid=(m // tm, nf),
        in_specs=[pl.BlockSpec((tm, d), lambda r, f: (r, 0)),
                  pl.BlockSpec((1, d), lambda r, f: (0, 0)),
                  pl.BlockSpec((d, tf), lambda r, f: (0, f)),
                  pl.BlockSpec((d, tf), lambda r, f: (0, nf + f)),
                  pl.BlockSpec((8, tf), lambda r, f: (0, f)),
                  pl.BlockSpec((1, tf), lambda r, f: (0, f)),
                  pl.BlockSpec((tf, d), lambda r, f: (f, 0))],
        out_specs=pl.BlockSpec((tm, d), lambda r, f: (r, 0)),
        scratch_shapes=[pltpu.VMEM((tm, d), BF16),
                        pltpu.VMEM((tm + 8, tf), F32),
                        pltpu.VMEM((nf, 8, tf), F32),
                        pltpu.VMEM((tm, d), F32)],
        compiler_params=_params("arbitrary", "arbitrary"),
        name="conv_ffn",
    )(h, gain.reshape(1, d), w_up, w_up, conv_w, conv_b.reshape(1, D_FF), w_down)


def _permute_w_in(w):
    n_attn = KW_IX - Q_SB + IDX_DIM + IDX_HEADS
    n_gate = 3 * D_MODEL
    n_df = 3 * DF_HEADS * 2 * HEAD_DIM
    gates = w[:, n_attn + n_df:]
    attn = w[:, :n_attn]
    pad = jnp.zeros((w.shape[0], Q_DF - Q_SB - n_attn), w.dtype)
    df = w[:, n_attn:n_attn + n_df]
    out = jnp.concatenate([gates, attn, pad, df], axis=1)
    assert gates.shape[1] == n_gate and out.shape[1] == NZ
    return out


def kernel(x, meta_tokens, rel_bias, attn_norm, w_in, b_gate, q_norm_sp, k_norm_sp, q_norm_df, k_norm_df, lam_q1, lam_k1, lam_q2, lam_k2, subln_df, w_br_sb, w_br_sp, w_br_df, w_out, ffn_norm, w_up, conv_w, conv_b, w_down):
    b, s, d = x.shape
    depth = w_in.shape[0]
    t = N_META + s
    tp = -(-t // QB) * QB
    top_k = min(TOPK_MAX, t // 4)
    m = b * tp

    meta = jnp.broadcast_to(meta_tokens[None].astype(x.dtype), (b, N_META, d))
    h = jnp.concatenate([meta, x, jnp.zeros((b, tp - t, d), x.dtype)], axis=1).reshape(m, d)

    bkt = jnp.asarray(_bucket_tile())
    u_mat = jnp.asarray(_sb_prefix_matrix(), BF16)
    rep, bcast, before = (jnp.asarray(a, BF16) for a in _ix_select_matrices())
    rel_bias = rel_bias.astype(F32)

    for l in range(depth):
        lam_init = 0.8 - 0.6 * math.exp(-0.3 * l)
        z = _in_proj(h, attn_norm[l], _permute_w_in(w_in[l]).astype(BF16))
        z3 = z.reshape(b, tp, NZ)
        y_sb = _sb_attn(z3, u_mat)
        y_sp = _sp_attn(z3, rel_bias,
                        jnp.tile(q_norm_sp[l].astype(F32), SP_HEADS).reshape(1, -1),
                        jnp.tile(k_norm_sp[l].astype(F32), SP_HEADS).reshape(1, -1),
                        bkt, rep, bcast, before, top_k)
        lamv = jnp.zeros((8, BLK), F32).at[:4, :HEAD_DIM].set(
            jnp.stack([lam_q1[l], lam_k1[l], lam_q2[l], lam_k2[l]]).astype(F32))
        y_df = _df_attn(z3, rel_bias,
                        jnp.tile(q_norm_df[l].astype(F32), 2).reshape(1, -1),
                        jnp.tile(k_norm_df[l].astype(F32), 2).reshape(1, -1),
                        bkt, lamv, subln_df[l].astype(F32).reshape(1, -1), lam_init)
        h = _mix_out(h, z, b_gate[l], y_sb.reshape(m, -1), y_sp.reshape(m, -1), y_df.reshape(m, -1),
                     w_br_sb[l].astype(BF16), w_br_sp[l].astype(BF16), w_br_df[l].astype(BF16),
                     w_out[l].astype(BF16))
        cw = jnp.zeros((8, D_FF), F32).at[:conv_w.shape[1]].set(conv_w[l])
        h = _ffn(h, ffn_norm[l], w_up[l].astype(BF16), cw, conv_b[l], w_down[l].astype(BF16), tp)

    return h.reshape(b, tp, d)[:, N_META:t]
```

```python
import functools
import math

import numpy as np
import jax
import jax.numpy as jnp
from jax import lax
from jax.experimental import pallas as pl
from jax.experimental.pallas import tpu as pltpu

D_MODEL = 1024
HEAD_DIM = 64
N_META = 16
BLK = 128
QB = 256
SB_HEADS = 4
SP_HEADS = 4
IDX_HEADS = 8
IDX_DIM = 32
TOPK_MAX = 256
DF_HEADS = 4
N_BUCKETS = 32
MAX_DISTANCE = 128
D_FF = 2816
EPS = 1e-6
LOG2E = math.log2(math.e)
QK_SCALE = HEAD_DIM ** -0.5 * LOG2E
M_INIT = -1e30
INT_MIN = -2 ** 31
COUNT_ROWS = 32
SCORE_VREGS = 32
SB_FAR = 2
SP_FAR = 2
DF_FAR = 4

G_SB, G_SP, G_DF = 0, 1024, 2048
Q_SB, K_SB, V_SB = 3072, 3328, 3584
Q_SP, K_SP, V_SP = 3840, 4096, 4352
Q_IX, KW_IX = 4608, 4864
Q_DF, K_DF, V_DF = 5120, 5632, 6144
NZ = 6656
W_IX_LANE = IDX_DIM

VMEM_LIMIT = 56 * 1024 * 1024

F32 = jnp.float32
BF16 = jnp.bfloat16
NT_DIMS = (((1,), (1,)), ((), ()))


def _nt_dot(a, b):
    return lax.dot_general(a, b, NT_DIMS, preferred_element_type=F32)


def _dot(a, b):
    return jnp.dot(a, b, preferred_element_type=F32)


def _params(*sem):
    return pltpu.CompilerParams(dimension_semantics=sem, vmem_limit_bytes=VMEM_LIMIT)


def _pick_rows(m, cap):
    for c in (2048, 1024, 512, 256):
        if c <= cap and m % c == 0:
            return c
    raise ValueError(f"row count {m} is not a multiple of {QB}")


def _bucket_np(rel):
    n = np.maximum(rel, 0)
    max_exact = N_BUCKETS // 2
    nf = np.maximum(n, 1).astype(np.float32)
    large = max_exact + (np.log(nf / np.float32(max_exact)) / np.float32(math.log(MAX_DISTANCE / max_exact))
                         * np.float32(N_BUCKETS - max_exact)).astype(np.int32)
    return np.where(n < max_exact, n, np.minimum(large, N_BUCKETS - 1)).astype(np.int32)


def _bucket_tile():
    tq = np.arange(QB)[:, None]
    c = np.arange(2 * QB)[None, :]
    return _bucket_np(tq - c + QB)


def _sb_prefix_matrix():
    sp = np.arange(2 * BLK)[:, None] % BLK
    c = np.arange(2 * BLK)[None, :]
    return np.where(c < BLK, sp > c, True).astype(np.float32)


def _ix_select_matrices():
    c = np.arange(QB)[:, None]
    col = np.arange(QB)[None, :]
    rep = ((c < IDX_DIM) & (c == col % IDX_DIM)).astype(np.float32)
    col8 = np.arange(IDX_HEADS * BLK)[None, :]
    bcast = (c == W_IX_LANE + col8 // BLK).astype(np.float32)
    before = (col < c).astype(np.float32)
    return rep, bcast, before


def _lane_iota(shape):
    return lax.broadcasted_iota(jnp.int32, shape, len(shape) - 1)


def _row_iota(shape):
    return lax.broadcasted_iota(jnp.int32, shape, 0)


def _head_rmsnorm128(xf, gain):
    lo = _lane_iota((1, BLK)) < HEAD_DIM
    ss = xf * xf
    s_lo = jnp.sum(jnp.where(lo, ss, 0.0), axis=-1, keepdims=True)
    s_hi = jnp.sum(jnp.where(lo, 0.0, ss), axis=-1, keepdims=True)
    ms = jnp.where(lo, s_lo, s_hi) * (1.0 / HEAD_DIM)
    return xf * lax.rsqrt(ms + EPS) * gain


def _bias_tile(tab_ref, bk, head):
    far = tab_ref[N_BUCKETS - 1, head]
    acc = jnp.zeros(bk.shape, F32)
    for b in range(N_BUCKETS - 1):
        acc = jnp.where(bk == b, (tab_ref[b, head] - far) * LOG2E, acc)
    return acc


def _softmax_stats(s, m_ref, l_ref, adjust=None):
    n_rows, kc = s.shape
    rg = SCORE_VREGS * 8 * BLK // kc
    p_groups, alphas = [], []
    for g in range(n_rows // rg):
        rows = slice(g * rg, (g + 1) * rg)
        sg = s[rows] if adjust is None else adjust(s[rows], rows)
        parts = [sg[:, c * BLK:(c + 1) * BLK] for c in range(kc // BLK)]
        m_old = m_ref[rows]
        m_new = jnp.maximum(m_old, jnp.max(functools.reduce(jnp.maximum, parts), axis=-1, keepdims=True))
        alpha = jnp.exp2(m_old - m_new)
        ps = [jnp.exp2(t - m_new) for t in parts]
        l_ref[rows] = alpha * l_ref[rows] + jnp.sum(functools.reduce(jnp.add, ps), axis=-1, keepdims=True)
        m_ref[rows] = m_new
        p_groups.append(jnp.concatenate([t.astype(BF16) for t in ps], axis=1))
        alphas.append(alpha)
    return jnp.concatenate(p_groups, axis=0), jnp.concatenate(alphas, axis=0)


def _accumulate(p, alpha, v, acc_ref):
    w = acc_ref.shape[-1] // BLK
    a = alpha if w == 1 else jnp.concatenate([alpha] * w, axis=1)
    acc_ref[...] = a * acc_ref[...] + _dot(p, v)


def _softmax_piece(s, v, m_ref, l_ref, acc_ref, adjust=None):
    p, alpha = _softmax_stats(s, m_ref, l_ref, adjust)
    _accumulate(p, alpha, v, acc_ref)


def _far_pieces_pipelined(n_far, width, scores_fn, adjust_fn, values_fn, m_ref, l_ref, acc_ref, p_ref, a_ref):
    npieces = (n_far + width - 1) // width

    def window(p):
        hi = n_far - p * width
        return jnp.maximum(hi - width, 0), hi

    def stats(p, s):
        start, hi = window(p)
        p_ref[...], a_ref[...] = _softmax_stats(s, m_ref, l_ref, adjust_fn(start, hi))

    def flush(p):
        _accumulate(p_ref[...], a_ref[...], values_fn(window(p)[0]), acc_ref)

    @pl.when(npieces > 0)
    def _():
        stats(0, scores_fn(window(0)[0]))

        def body(p, carry):
            s = scores_fn(window(p)[0])
            flush(p - 1)
            stats(p, s)
            return carry

        lax.fori_loop(1, npieces, body, 0)
        flush(npieces - 1)


def _for_far_pieces(n_far, width, piece_fn):
    def body(p, carry):
        hi = n_far - p * width
        piece_fn(jnp.maximum(hi - width, 0), hi)
        return carry

    lax.fori_loop(0, (n_far + width - 1) // width, body, 0)


def _new_key_mask(start, hi, width):
    col = _lane_iota((1, width * QB))
    return jnp.where(col < (hi - start) * QB, 0.0, -jnp.inf)


def _in_proj_kernel(h_ref, g_ref, w_ref, o_ref, u_ref):
    @pl.when(pl.program_id(1) == 0)
    def _():
        x = h_ref[...]
        ms = jnp.mean(x * x, axis=-1, keepdims=True)
        u_ref[...] = (x * lax.rsqrt(ms + EPS) * g_ref[...]).astype(BF16)

    o_ref[...] = _dot(u_ref[...], w_ref[...]).astype(o_ref.dtype)


def _in_proj(h, gain, w):
    m, d = h.shape
    n = w.shape[1]
    tm = _pick_rows(m, 1024)
    tn = 512
    return pl.pallas_call(
        _in_proj_kernel,
        out_shape=jax.ShapeDtypeStruct((m, n), BF16),
        grid=(m // tm, n // tn),
        in_specs=[pl.BlockSpec((tm, d), lambda i, j: (i, 0)),
                  pl.BlockSpec((1, d), lambda i, j: (0, 0)),
                  pl.BlockSpec((d, tn), lambda i, j: (0, j))],
        out_specs=pl.BlockSpec((tm, tn), lambda i, j: (i, j)),
        scratch_shapes=[pltpu.VMEM((tm, d), BF16)],
        compiler_params=_params("parallel", "arbitrary"),
        name="in_proj",
    )(h, gain.reshape(1, d), w)


def _sb_kernel(q_ref, k_ref, v_ref, u_ref, o_ref, tot_ref, acc_ref):
    i = pl.program_id(1)
    nh = SB_HEADS
    head_of_lane = _lane_iota((QB, nh * HEAD_DIM)) // HEAD_DIM
    q = q_ref[0].astype(F32) * QK_SCALE
    qs = jnp.concatenate([jnp.where(head_of_lane == h, q, 0.0) for h in range(nh)], axis=0).astype(BF16)
    tot_ref[...] = jnp.zeros_like(tot_ref)
    acc_ref[...] = jnp.zeros_like(acc_ref)

    def piece(start, n_blocks, hi=None):
        r = pl.multiple_of(start * QB, QB)
        z_all = _nt_dot(qs, k_ref[0, pl.ds(r, n_blocks * QB), :])
        if hi is not None:
            z_all = z_all + _new_key_mask(start, hi, n_blocks)
        run = tot_ref[...]
        n_sub = n_blocks * QB // BLK
        ws = [None] * n_sub
        for c in reversed(range(n_sub)):
            z = z_all[:, c * BLK:(c + 1) * BLK]
            sp = jnp.maximum(z, 0.0) + jnp.log2(1.0 + jnp.exp2(-jnp.abs(z)))
            l1m = -sp
            if hi is None:
                mask = (_lane_iota(z.shape) + c * BLK) < (_row_iota(z.shape) & (QB - 1))
                l1m = jnp.where(mask, l1m, 0.0)
            l1m_hi = l1m.astype(BF16)
            l1m_lo = (l1m - l1m_hi.astype(F32)).astype(BF16)
            rs = _dot(jnp.concatenate([l1m_hi, l1m_lo], axis=1), u_ref[...])
            w = jnp.exp2((z - sp) + rs[:, :BLK] + run)
            if hi is None:
                w = jnp.where(mask, w, 0.0)
            ws[c] = w.astype(BF16)
            run = run + rs[:, BLK:]
        tot_ref[...] = run
        acc_ref[...] += _dot(jnp.concatenate(ws, axis=1), v_ref[0, pl.ds(r, n_blocks * QB), :])

    piece(i, 1)
    _for_far_pieces(i, SB_FAR, lambda start, hi: piece(start, SB_FAR, hi))

    a = acc_ref[...]
    out = a[:QB]
    for h in range(1, nh):
        out = jnp.where(head_of_lane == h, a[h * QB:(h + 1) * QB], out)
    o_ref[0] = out.astype(o_ref.dtype)


def _sb_attn(z3, u_mat):
    b, tp, _ = z3.shape
    nq = tp // QB
    assert nq >= SB_FAR
    w = SB_HEADS * HEAD_DIM
    return pl.pallas_call(
        _sb_kernel,
        out_shape=jax.ShapeDtypeStruct((b, tp, w), BF16),
        grid=(b, nq),
        in_specs=[pl.BlockSpec((1, QB, w), lambda bb, i: (bb, i, Q_SB // w)),
                  pl.BlockSpec((1, tp, w), lambda bb, i: (bb, 0, K_SB // w)),
                  pl.BlockSpec((1, tp, w), lambda bb, i: (bb, 0, V_SB // w)),
                  pl.BlockSpec((2 * BLK, 2 * BLK), lambda bb, i: (0, 0))],
        out_specs=pl.BlockSpec((1, QB, w), lambda bb, i: (bb, i, 0)),
        scratch_shapes=[pltpu.VMEM((SB_HEADS * QB, BLK), F32),
                        pltpu.VMEM((SB_HEADS * QB, w), F32)],
        compiler_params=_params("parallel", "arbitrary"),
        name="sb_attn",
    )(z3, z3, z3, u_mat)


def _df_kernel(tab_ref, q_ref, k_ref, v_ref, qg_ref, kg_ref, bkt_ref, lamv_ref, sub_ref, o_ref,
               kn_ref, bias_ref, m_ref, l_ref, acc_ref, p_ref, a_ref, *, nq, lam_init, head_off):
    h = pl.program_id(1)
    i = pl.program_id(2)
    lo = _lane_iota((1, BLK)) < HEAD_DIM

    @pl.when(i == 0)
    def _prep():
        def kbody(c, carry):
            r = pl.multiple_of(c * QB, QB)
            kf = k_ref[0, pl.ds(r, QB), :].astype(F32)
            kn_ref[pl.ds(r, QB), :] = _head_rmsnorm128(kf, kg_ref[...]).astype(BF16)
            return carry

        lax.fori_loop(0, nq, kbody, 0)
        bias_ref[...] = _bias_tile(tab_ref, bkt_ref[...], head_off + h)

    qn = _head_rmsnorm128(q_ref[0].astype(F32), qg_ref[...]) * QK_SCALE
    qs = jnp.concatenate([jnp.where(lo, qn, 0.0), jnp.where(lo, 0.0, qn)], axis=0).astype(BF16)
    m_ref[...] = jnp.full(m_ref.shape, M_INIT, F32)
    l_ref[...] = jnp.zeros_like(l_ref)
    acc_ref[...] = jnp.zeros_like(acc_ref)

    def far_scores(start):
        return _nt_dot(qs, kn_ref[pl.ds(pl.multiple_of(start * QB, QB), DF_FAR * QB), :])

    def far_adjust(start, hi):
        new_keys = _new_key_mask(start, hi, DF_FAR)
        return lambda t, rows: t + new_keys

    def far_values(start):
        return v_ref[0, pl.ds(pl.multiple_of(start * QB, QB), DF_FAR * QB), :]

    def near_piece(first, n):
        r = pl.multiple_of(first * QB, QB)
        s = _nt_dot(qs, kn_ref[pl.ds(r, n * QB), :])

        def adjust(t, rows):
            q_rows = slice(rows.start % QB, rows.start % QB + t.shape[0])
            t = t + bias_ref[q_rows, (2 - n) * QB:]
            causal = (_lane_iota(t.shape) - (n - 1) * QB) <= (_row_iota(t.shape) + q_rows.start)
            return jnp.where(causal, t, -jnp.inf)

        _softmax_piece(s, v_ref[0, pl.ds(r, n * QB), :], m_ref, l_ref, acc_ref, adjust=adjust)

    _far_pieces_pipelined(jnp.maximum(i - 1, 0), DF_FAR, far_scores, far_adjust, far_values,
                          m_ref, l_ref, acc_ref, p_ref, a_ref)

    @pl.when(i >= 1)
    def _():
        near_piece(i - 1, 2)

    @pl.when(i == 0)
    def _():
        near_piece(0, 1)

    a = acc_ref[...] / l_ref[...]
    lv = lamv_ref[...]
    lam = (jnp.exp(jnp.sum(lv[0:1] * lv[1:2], axis=-1, keepdims=True))
           - jnp.exp(jnp.sum(lv[2:3] * lv[3:4], axis=-1, keepdims=True)) + lam_init)
    y = a[:QB] - lam * a[QB:]
    y = y * lax.rsqrt(jnp.mean(y * y, axis=-1, keepdims=True) + EPS) * sub_ref[...]
    o_ref[0] = (y * (1.0 - lam_init)).astype(o_ref.dtype)


def _df_attn(z3, rel_bias, qg, kg, bkt, lamv, subln, lam_init):
    b, tp, _ = z3.shape
    nq = tp // QB
    assert nq >= DF_FAR
    kern = functools.partial(_df_kernel, nq=nq, lam_init=lam_init, head_off=SP_HEADS)
    vec = lambda bb, h, i: (0, 0)
    return pl.pallas_call(
        kern,
        out_shape=jax.ShapeDtypeStruct((b, tp, DF_HEADS * BLK), BF16),
        grid=(b, DF_HEADS, nq),
        in_specs=[pl.BlockSpec(memory_space=pltpu.SMEM),
                  pl.BlockSpec((1, QB, BLK), lambda bb, h, i: (bb, i, Q_DF // BLK + h)),
                  pl.BlockSpec((1, tp, BLK), lambda bb, h, i: (bb, 0, K_DF // BLK + h)),
                  pl.BlockSpec((1, tp, BLK), lambda bb, h, i: (bb, 0, V_DF // BLK + h)),
                  pl.BlockSpec((1, BLK), vec),
                  pl.BlockSpec((1, BLK), vec),
                  pl.BlockSpec((QB, 2 * QB), vec),
                  pl.BlockSpec((8, BLK), vec),
                  pl.BlockSpec((1, BLK), vec)],
        out_specs=pl.BlockSpec((1, QB, BLK), lambda bb, h, i: (bb, i, h)),
        scratch_shapes=[pltpu.VMEM((tp, BLK), BF16),
                        pltpu.VMEM((QB, 2 * QB), F32),
                        pltpu.VMEM((2 * QB, BLK), F32),
                        pltpu.VMEM((2 * QB, BLK), F32),
                        pltpu.VMEM((2 * QB, BLK), F32),
                        pltpu.VMEM((2 * QB, DF_FAR * QB), BF16),
                        pltpu.VMEM((2 * QB, BLK), F32)],
        compiler_params=_params("parallel", "parallel", "arbitrary"),
        name="df_attn",
    )(rel_bias, z3, z3, z3, qg, kg, bkt, lamv, subln)


def _sp_kernel(tab_ref, q_ref, k_ref, v_ref, qix_ref, kwq_ref, kwk_ref, qg_ref, kg_ref, bkt_ref,
               rep_ref, bcast_ref, before_ref, o_ref,
               kn_ref, kx_ref, bias_ref, keys_ref, thr_ref, wb_ref, m_ref, l_ref, acc_ref, p_ref, a_ref,
               *, nq, top_k):
    i = pl.program_id(1)
    nh = SP_HEADS
    w = nh * HEAD_DIM
    lane = _lane_iota((QB, w))
    head_of_lane = lane // HEAD_DIM

    def norm256(xf, g):
        return jnp.concatenate([_head_rmsnorm128(xf[:, :BLK], g[:, :BLK]),
                                _head_rmsnorm128(xf[:, BLK:], g[:, BLK:])], axis=1)

    def dup(x):
        return jnp.concatenate([x, x], axis=1)

    @pl.when(i == 0)
    def _prep():
        def kbody(c, carry):
            r = pl.multiple_of(c * QB, QB)
            kn_ref[pl.ds(r, QB), :] = norm256(k_ref[0, pl.ds(r, QB), :].astype(F32), kg_ref[...]).astype(BF16)
            kx_ref[pl.ds(r, QB), :] = _dot(kwk_ref[0, pl.ds(r, QB), :], rep_ref[...]).astype(BF16)
            return carry

        lax.fori_loop(0, nq, kbody, 0)
        for h in range(nh):
            bias_ref[h] = _bias_tile(tab_ref, bkt_ref[...], h)

    key_causal = _row_iota((QB, QB)) <= _lane_iota((QB, QB))

    wb_ref[...] = _dot(kwq_ref[0], bcast_ref[...])
    qix = qix_ref[0].astype(F32)
    ix_head = lane // IDX_DIM
    qx = jnp.concatenate([jnp.where(ix_head == h, qix, 0.0) for h in range(IDX_HEADS)], axis=0).astype(BF16)

    def score_tile(j, diag):
        r = pl.multiple_of(j * QB, QB)
        d = _nt_dot(qx, kx_ref[pl.ds(r, QB), :])
        sc = jnp.zeros((QB, QB), F32)
        for h in range(IDX_HEADS):
            sc = sc + dup(wb_ref[:, h * BLK:(h + 1) * BLK]) * jnp.maximum(d[h * QB:(h + 1) * QB], 0.0)
        sc = jnp.where(sc == 0.0, 0.0, sc).T
        bits = lax.bitcast_convert_type(sc, jnp.int32)
        key = jnp.where(bits < 0, bits ^ jnp.int32(0x7FFFFFFF), bits)
        if diag:
            key = jnp.where(key_causal, key, jnp.int32(INT_MIN))
        keys_ref[j] = key

    def score_body(j, c):
        score_tile(j, False)
        return c

    lax.fori_loop(0, i, score_body, 0)
    score_tile(i, True)

    nblk = i + 1

    def count(pred):
        def cbody(j, c):
            hit = jnp.where(pred(keys_ref[j]), 1.0, 0.0)
            return c + jnp.sum(hit.reshape(QB // COUNT_ROWS, COUNT_ROWS, QB), axis=0)

        c = lax.fori_loop(0, nblk, cbody, jnp.zeros((COUNT_ROWS, QB), F32))
        return jnp.sum(c, axis=0, keepdims=True)

    def bit_body(b, carry):
        cur, cnt_cur = carry
        cand = cur + jnp.left_shift(jnp.int32(1), 31 - b)
        cnt = count(lambda k: k >= cand)
        ok = cnt >= float(top_k)
        return jnp.where(ok, cand, cur), jnp.where(ok, cnt, cnt_cur)

    cur0 = jnp.full((1, QB), INT_MIN, jnp.int32)
    cnt0 = jnp.zeros((1, QB), F32) + (nblk * QB).astype(F32)
    thr, cge = lax.fori_loop(0, 32, bit_body, (cur0, cnt0))
    thr_ref[...] = jnp.broadcast_to(thr, thr_ref.shape)

    tie = jnp.where((cge > float(top_k)) & (thr > INT_MIN), 1, 0)

    @pl.when(jnp.max(tie) > 0)
    def _ties():
        need = float(top_k) - count(lambda k: k > thr)

        def tbody(j, run):
            kj = keys_ref[j]
            eq = kj == thr
            eqf = jnp.where(eq, 1.0, 0.0)
            rank = run + _dot(before_ref[...], eqf.astype(BF16))
            keep = jnp.where(kj > thr, 1, jnp.where(eq & (rank < need), 1, -1))
            keys_ref[j] = keep.astype(jnp.int32)
            return run + jnp.sum(eqf, axis=0, keepdims=True)

        lax.fori_loop(0, nblk, tbody, jnp.zeros((1, QB), F32))
        thr_ref[...] = jnp.zeros_like(thr_ref)

    qn = norm256(q_ref[0].astype(F32), qg_ref[...]) * QK_SCALE
    qs = jnp.concatenate([jnp.where(head_of_lane == h, qn, 0.0) for h in range(nh)], axis=0).astype(BF16)
    m_ref[...] = jnp.full(m_ref.shape, M_INIT, F32)
    l_ref[...] = jnp.zeros_like(l_ref)
    acc_ref[...] = jnp.zeros_like(acc_ref)

    thr_sel = thr_ref[0:1, :]

    def as_mask(sel):
        return lax.bitcast_convert_type(jnp.where(sel, 0.0, -jnp.inf).T, jnp.int32)

    def mask_body(j, c):
        keys_ref[j] = as_mask(keys_ref[j] >= thr_sel)
        return c

    lax.fori_loop(0, i, mask_body, 0)
    keys_ref[i] = as_mask((keys_ref[i] >= thr_sel) & key_causal)

    def selection(first, n, q_rows):
        return jnp.concatenate([lax.bitcast_convert_type(keys_ref[first + c, q_rows, :], F32) for c in range(n)],
                               axis=1)

    def far_scores(start):
        return _nt_dot(qs, kn_ref[pl.ds(pl.multiple_of(start * QB, QB), SP_FAR * QB), :])

    def far_adjust(start, hi):
        new_keys = _new_key_mask(start, hi, SP_FAR)

        def adjust(t, rows):
            q_rows = slice(rows.start % QB, rows.start % QB + t.shape[0])
            return t + (selection(start, SP_FAR, q_rows) + new_keys)

        return adjust

    def far_values(start):
        return v_ref[0, pl.ds(pl.multiple_of(start * QB, QB), SP_FAR * QB), :]

    def near_piece(first, n):
        r = pl.multiple_of(first * QB, QB)
        s = _nt_dot(qs, kn_ref[pl.ds(r, n * QB), :])

        def adjust(t, rows):
            q_rows = slice(rows.start % QB, rows.start % QB + t.shape[0])
            return t + (selection(first, n, q_rows) + bias_ref[rows.start // QB, q_rows, (2 - n) * QB:])

        _softmax_piece(s, v_ref[0, pl.ds(r, n * QB), :], m_ref, l_ref, acc_ref, adjust=adjust)

    _far_pieces_pipelined(jnp.maximum(i - 1, 0), SP_FAR, far_scores, far_adjust, far_values,
                          m_ref, l_ref, acc_ref, p_ref, a_ref)

    @pl.when(i >= 1)
    def _():
        near_piece(i - 1, 2)

    @pl.when(i == 0)
    def _():
        near_piece(0, 1)

    a = acc_ref[...] * dup(1.0 / l_ref[...])
    out = a[:QB]
    for h in range(1, nh):
        out = jnp.where(head_of_lane == h, a[h * QB:(h + 1) * QB], out)
    o_ref[0] = out.astype(o_ref.dtype)


def _sp_attn(z3, rel_bias, qg, kg, bkt, rep, bcast, before, top_k):
    b, tp, _ = z3.shape
    nq = tp // QB
    assert nq >= SP_FAR
    w = SP_HEADS * HEAD_DIM
    kern = functools.partial(_sp_kernel, nq=nq, top_k=top_k)
    c2 = lambda bb, i: (0, 0)
    return pl.pallas_call(
        kern,
        out_shape=jax.ShapeDtypeStruct((b, tp, w), BF16),
        grid=(b, nq),
        in_specs=[pl.BlockSpec(memory_space=pltpu.SMEM),
                  pl.BlockSpec((1, QB, w), lambda bb, i: (bb, i, Q_SP // w)),
                  pl.BlockSpec((1, tp, w), lambda bb, i: (bb, 0, K_SP // w)),
                  pl.BlockSpec((1, tp, w), lambda bb, i: (bb, 0, V_SP // w)),
                  pl.BlockSpec((1, QB, w), lambda bb, i: (bb, i, Q_IX // w)),
                  pl.BlockSpec((1, QB, w), lambda bb, i: (bb, i, KW_IX // w)),
                  pl.BlockSpec((1, tp, w), lambda bb, i: (bb, 0, KW_IX // w)),
                  pl.BlockSpec((1, w), c2),
                  pl.BlockSpec((1, w), c2),
                  pl.BlockSpec((QB, 2 * QB), c2),
                  pl.BlockSpec((w, w), c2),
                  pl.BlockSpec((w, IDX_HEADS * BLK), c2),
                  pl.BlockSpec((QB, QB), c2)],
        out_specs=pl.BlockSpec((1, QB, w), lambda bb, i: (bb, i, 0)),
        scratch_shapes=[pltpu.VMEM((tp, w), BF16),
                        pltpu.VMEM((tp, w), BF16),
                        pltpu.VMEM((SP_HEADS, QB, 2 * QB), F32),
                        pltpu.VMEM((nq, QB, QB), jnp.int32),
                        pltpu.VMEM((8, QB), jnp.int32),
                        pltpu.VMEM((QB, IDX_HEADS * BLK), F32),
                        pltpu.VMEM((SP_HEADS * QB, BLK), F32),
                        pltpu.VMEM((SP_HEADS * QB, BLK), F32),
                        pltpu.VMEM((SP_HEADS * QB, w), F32),
                        pltpu.VMEM((SP_HEADS * QB, SP_FAR * QB), BF16),
                        pltpu.VMEM((SP_HEADS * QB, BLK), F32)],
        compiler_params=_params("parallel", "arbitrary"),
        name="sp_attn",
    )(rel_bias, z3, z3, z3, z3, z3, z3, qg, kg, bkt, rep, bcast, before)


def _mix_kernel(h_ref, gsb_ref, gsp_ref, gdf_ref, bg_ref, ysb_ref, ysp_ref, ydf_ref,
                wsb_ref, wsp_ref, wdf_ref, wo_ref, o_ref):
    def branch(g_ref, k, y_ref, w_ref):
        gate = jax.nn.sigmoid(g_ref[...].astype(F32) + bg_ref[:, k * D_MODEL:(k + 1) * D_MODEL])
        return gate * _dot(y_ref[...], w_ref[...])

    merged = (branch(gsb_ref, 0, ysb_ref, wsb_ref) + branch(gsp_ref, 1, ysp_ref, wsp_ref)
              + branch(gdf_ref, 2, ydf_ref, wdf_ref))
    o_ref[...] = h_ref[...] + _dot(merged.astype(BF16), wo_ref[...])


def _mix_out(h, z, b_gate, y_sb, y_sp, y_df, w_sb, w_sp, w_df, w_o):
    m, d = h.shape
    tm = _pick_rows(m, 512)
    row = lambda i: (i, 0)
    fixed = lambda i: (0, 0)
    return pl.pallas_call(
        _mix_kernel,
        out_shape=jax.ShapeDtypeStruct((m, d), F32),
        grid=(m // tm,),
        in_specs=[pl.BlockSpec((tm, d), row),
                  pl.BlockSpec((tm, d), lambda i: (i, G_SB // D_MODEL)),
                  pl.BlockSpec((tm, d), lambda i: (i, G_SP // D_MODEL)),
                  pl.BlockSpec((tm, d), lambda i: (i, G_DF // D_MODEL)),
                  pl.BlockSpec((1, 3 * d), fixed),
                  pl.BlockSpec((tm, y_sb.shape[1]), row),
                  pl.BlockSpec((tm, y_sp.shape[1]), row),
                  pl.BlockSpec((tm, y_df.shape[1]), row),
                  pl.BlockSpec(w_sb.shape, fixed),
                  pl.BlockSpec(w_sp.shape, fixed),
                  pl.BlockSpec(w_df.shape, fixed),
                  pl.BlockSpec(w_o.shape, fixed)],
        out_specs=pl.BlockSpec((tm, d), row),
        compiler_params=_params("parallel"),
        name="mix_out",
    )(h, z, z, z, b_gate.reshape(1, 3 * d), y_sb, y_sp, y_df, w_sb, w_sp, w_df, w_o)


def _ffn_kernel(h_ref, g_ref, wg_ref, wv_ref, cw_ref, cb_ref, wd_ref, o_ref,
                u_ref, gbuf_ref, carry_ref, acc_ref, *, tm, tp):
    r = pl.program_id(0)
    f = pl.program_id(1)

    @pl.when(f == 0)
    def _():
        x = h_ref[...]
        ms = jnp.mean(x * x, axis=-1, keepdims=True)
        u_ref[...] = (x * lax.rsqrt(ms + EPS) * g_ref[...]).astype(BF16)
        acc_ref[...] = jnp.zeros_like(acc_ref)

    u = u_ref[...]
    gate = _dot(u, wg_ref[...])
    val = _dot(u, wv_ref[...])
    @pl.when(r == 0)
    def _():
        carry_ref[f] = jnp.zeros((8, gate.shape[1]), F32)

    gbuf_ref[0:8] = carry_ref[f]
    gbuf_ref[8:8 + tm] = gate
    carry_ref[f] = gate[tm - 8:tm]
    seq_start = lax.rem(tp - lax.rem(r * tm, tp), tp)
    local = lax.broadcasted_iota(jnp.int32, (tm, 1), 0)
    g1 = jnp.where(local != seq_start, gbuf_ref[7:7 + tm], 0.0)
    g2 = jnp.where((local != seq_start) & (local != seq_start + 1), gbuf_ref[6:6 + tm], 0.0)
    cw = cw_ref[...]
    conv = cb_ref[...] + cw[0:1] * g2 + cw[1:2] * g1 + cw[2:3] * gate
    act = conv * jax.nn.sigmoid(conv) * val
    acc_ref[...] += _dot(act.astype(BF16), wd_ref[...])

    @pl.when(f == pl.num_programs(1) - 1)
    def _():
        o_ref[...] = h_ref[...] + acc_ref[...]


def _ffn(h, gain, w_up, conv_w, conv_b, w_down, tp):
    m, d = h.shape
    tm = _pick_rows(m, 512)
    assert tm <= tp
    tf = D_FF // 2
    nf = D_FF // tf
    kern = functools.partial(_ffn_kernel, tm=tm, tp=tp)
    return pl.pallas_call(
        kern,
        out_shape=jax.ShapeDtypeStruct((m, d), F32),
        grid=(m // tm, nf),
        in_specs=[pl.BlockSpec((tm, d), lambda r, f: (r, 0)),
                  pl.BlockSpec((1, d), lambda r, f: (0, 0)),
                  pl.BlockSpec((d, tf), lambda r, f: (0, f)),
                  pl.BlockSpec((d, tf), lambda r, f: (0, nf + f)),
                  pl.BlockSpec((8, tf), lambda r, f: (0, f)),
                  pl.BlockSpec((1, tf), lambda r, f: (0, f)),
                  pl.BlockSpec((tf, d), lambda r, f: (f, 0))],
        out_specs=pl.BlockSpec((tm, d), lambda r, f: (r, 0)),
        scratch_shapes=[pltpu.VMEM((tm, d), BF16),
                        pltpu.VMEM((tm + 8, tf), F32),
                        pltpu.VMEM((nf, 8, tf), F32),
                        pltpu.VMEM((tm, d), F32)],
        compiler_params=_params("arbitrary", "arbitrary"),
        name="conv_ffn",
    )(h, gain.reshape(1, d), w_up, w_up, conv_w, conv_b.reshape(1, D_FF), w_down)


def _permute_w_in(w):
    n_attn = KW_IX - Q_SB + IDX_DIM + IDX_HEADS
    n_gate = 3 * D_MODEL
    n_df = 3 * DF_HEADS * 2 * HEAD_DIM
    gates = w[:, n_attn + n_df:]
    attn = w[:, :n_attn]
    pad = jnp.zeros((w.shape[0], Q_DF - Q_SB - n_attn), w.dtype)
    df = w[:, n_attn:n_attn + n_df]
    out = jnp.concatenate([gates, attn, pad, df], axis=1)
    assert gates.shape[1] == n_gate and out.shape[1] == NZ
    return out


def kernel(x, meta_tokens, rel_bias, attn_norm, w_in, b_gate, q_norm_sp, k_norm_sp, q_norm_df, k_norm_df, lam_q1, lam_k1, lam_q2, lam_k2, subln_df, w_br_sb, w_br_sp, w_br_df, w_out, ffn_norm, w_up, conv_w, conv_b, w_down):
    b, s, d = x.shape
    depth = w_in.shape[0]
    t = N_META + s
    tp = -(-t // QB) * QB
    top_k = min(TOPK_MAX, t // 4)
    m = b * tp

    meta = jnp.broadcast_to(meta_tokens[None].astype(x.dtype), (b, N_META, d))
    h = jnp.concatenate([meta, x, jnp.zeros((b, tp - t, d), x.dtype)], axis=1).reshape(m, d)

    bkt = jnp.asarray(_bucket_tile())
    u_mat = jnp.asarray(_sb_prefix_matrix(), BF16)
    rep, bcast, before = (jnp.asarray(a, BF16) for a in _ix_select_matrices())
    rel_bias = rel_bias.astype(F32)

    for l in range(depth):
        lam_init = 0.8 - 0.6 * math.exp(-0.3 * l)
        z = _in_proj(h, attn_norm[l], _permute_w_in(w_in[l]).astype(BF16))
        z3 = z.reshape(b, tp, NZ)
        y_sb = _sb_attn(z3, u_mat)
        y_sp = _sp_attn(z3, rel_bias,
                        jnp.tile(q_norm_sp[l].astype(F32), SP_HEADS).reshape(1, -1),
                        jnp.tile(k_norm_sp[l].astype(F32), SP_HEADS).reshape(1, -1),
                        bkt, rep, bcast, before, top_k)
        lamv = jnp.zeros((8, BLK), F32).at[:4, :HEAD_DIM].set(
            jnp.stack([lam_q1[l], lam_k1[l], lam_q2[l], lam_k2[l]]).astype(F32))
        y_df = _df_attn(z3, rel_bias,
                        jnp.tile(q_norm_df[l].astype(F32), 2).reshape(1, -1),
                        jnp.tile(k_norm_df[l].astype(F32), 2).reshape(1, -1),
                        bkt, lamv, subln_df[l].astype(F32).reshape(1, -1), lam_init)
        h = _mix_out(h, z, b_gate[l], y_sb.reshape(m, -1), y_sp.reshape(m, -1), y_df.reshape(m, -1),
                     w_br_sb[l].astype(BF16), w_br_sp[l].astype(BF16), w_br_df[l].astype(BF16),
                     w_out[l].astype(BF16))
        cw = jnp.zeros((8, D_FF), F32).at[:conv_w.shape[1]].set(conv_w[l])
        h = _ffn(h, ffn_norm[l], w_up[l].astype(BF16), cw, conv_b[l], w_down[l].astype(BF16), tp)

    return h.reshape(b, tp, d)[:, N_META:t]
```

```python
import functools
import math

import numpy as np
import jax
import jax.numpy as jnp
from jax import lax
from jax.experimental import pallas as pl
from jax.experimental.pallas import tpu as pltpu

D_MODEL = 1024
HEAD_DIM = 64
N_META = 16
BLK = 128
QB = 256
SB_HEADS = 4
SP_HEADS = 4
IDX_HEADS = 8
IDX_DIM = 32
TOPK_MAX = 256
DF_HEADS = 4
N_BUCKETS = 32
MAX_DISTANCE = 128
D_FF = 2816
EPS = 1e-6
LOG2E = math.log2(math.e)
QK_SCALE = HEAD_DIM ** -0.5 * LOG2E
M_INIT = -1e30
INT_MIN = -2 ** 31
COUNT_ROWS = 32
SCORE_VREGS = 32
SB_FAR = 2
SP_FAR = 2
DF_FAR = 4

G_SB, G_SP, G_DF = 0, 1024, 2048
Q_SB, K_SB, V_SB = 3072, 3328, 3584
Q_SP, K_SP, V_SP = 3840, 4096, 4352
Q_IX, KW_IX = 4608, 4864
Q_DF, K_DF, V_DF = 5120, 5632, 6144
NZ = 6656
W_IX_LANE = IDX_DIM

VMEM_LIMIT = 56 * 1024 * 1024

F32 = jnp.float32
BF16 = jnp.bfloat16
NT_DIMS = (((1,), (1,)), ((), ()))


def _nt_dot(a, b):
    return lax.dot_general(a, b, NT_DIMS, preferred_element_type=F32)


def _dot(a, b):
    return jnp.dot(a, b, preferred_element_type=F32)


def _params(*sem):
    return pltpu.CompilerParams(dimension_semantics=sem, vmem_limit_bytes=VMEM_LIMIT)


def _pick_rows(m, cap):
    for c in (2048, 1024, 512, 256):
        if c <= cap and m % c == 0:
            return c
    raise ValueError(f"row count {m} is not a multiple of {QB}")


def _bucket_np(rel):
    n = np.maximum(rel, 0)
    max_exact = N_BUCKETS // 2
    nf = np.maximum(n, 1).astype(np.float32)
    large = max_exact + (np.log(nf / np.float32(max_exact)) / np.float32(math.log(MAX_DISTANCE / max_exact))
                         * np.float32(N_BUCKETS - max_exact)).astype(np.int32)
    return np.where(n < max_exact, n, np.minimum(large, N_BUCKETS - 1)).astype(np.int32)


def _bucket_tile():
    tq = np.arange(QB)[:, None]
    c = np.arange(2 * QB)[None, :]
    return _bucket_np(tq - c + QB)


def _sb_prefix_matrix():
    sp = np.arange(2 * BLK)[:, None] % BLK
    c = np.arange(2 * BLK)[None, :]
    return np.where(c < BLK, sp > c, True).astype(np.float32)


def _ix_select_matrices():
    c = np.arange(QB)[:, None]
    col = np.arange(QB)[None, :]
    rep = ((c < IDX_DIM) & (c == col % IDX_DIM)).astype(np.float32)
    col8 = np.arange(IDX_HEADS * BLK)[None, :]
    bcast = (c == W_IX_LANE + col8 // BLK).astype(np.float32)
    before = (col < c).astype(np.float32)
    return rep, bcast, before


def _lane_iota(shape):
    return lax.broadcasted_iota(jnp.int32, shape, len(shape) - 1)


def _row_iota(shape):
    return lax.broadcasted_iota(jnp.int32, shape, 0)


def _head_rmsnorm128(xf, gain):
    lo = _lane_iota((1, BLK)) < HEAD_DIM
    ss = xf * xf
    s_lo = jnp.sum(jnp.where(lo, ss, 0.0), axis=-1, keepdims=True)
    s_hi = jnp.sum(jnp.where(lo, 0.0, ss), axis=-1, keepdims=True)
    ms = jnp.where(lo, s_lo, s_hi) * (1.0 / HEAD_DIM)
    return xf * lax.rsqrt(ms + EPS) * gain


def _bias_tile(tab_ref, bk, head):
    far = tab_ref[N_BUCKETS - 1, head]
    acc = jnp.zeros(bk.shape, F32)
    for b in range(N_BUCKETS - 1):
        acc = jnp.where(bk == b, (tab_ref[b, head] - far) * LOG2E, acc)
    return acc


def _softmax_stats(s, m_ref, l_ref, adjust=None):
    n_rows, kc = s.shape
    rg = SCORE_VREGS * 8 * BLK // kc
    p_groups, alphas = [], []
    for g in range(n_rows // rg):
        rows = slice(g * rg, (g + 1) * rg)
        sg = s[rows] if adjust is None else adjust(s[rows], rows)
        parts = [sg[:, c * BLK:(c + 1) * BLK] for c in range(kc // BLK)]
        m_old = m_ref[rows]
        m_new = jnp.maximum(m_old, jnp.max(functools.reduce(jnp.maximum, parts), axis=-1, keepdims=True))
        alpha = jnp.exp2(m_old - m_new)
        ps = [jnp.exp2(t - m_new) for t in parts]
        l_ref[rows] = alpha * l_ref[rows] + jnp.sum(functools.reduce(jnp.add, ps), axis=-1, keepdims=True)
        m_ref[rows] = m_new
        p_groups.append(jnp.concatenate([t.astype(BF16) for t in ps], axis=1))
        alphas.append(alpha)
    return jnp.concatenate(p_groups, axis=0), jnp.concatenate(alphas, axis=0)


def _accumulate(p, alpha, v, acc_ref):
    w = acc_ref.shape[-1] // BLK
    a = alpha if w == 1 else jnp.concatenate([alpha] * w, axis=1)
    acc_ref[...] = a * acc_ref[...] + _dot(p, v)


def _softmax_piece(s, v, m_ref, l_ref, acc_ref, adjust=None):
    p, alpha = _softmax_stats(s, m_ref, l_ref, adjust)
    _accumulate(p, alpha, v, acc_ref)


def _far_pieces_pipelined(n_far, width, scores_fn, adjust_fn, values_fn, m_ref, l_ref, acc_ref, p_ref, a_ref):
    npieces = (n_far + width - 1) // width

    def window(p):
        hi = n_far - p * width
        return jnp.maximum(hi - width, 0), hi

    def stats(p, s):
        start, hi = window(p)
        p_ref[...], a_ref[...] = _softmax_stats(s, m_ref, l_ref, adjust_fn(start, hi))

    def flush(p):
        _accumulate(p_ref[...], a_ref[...], values_fn(window(p)[0]), acc_ref)

    @pl.when(npieces > 0)
    def _():
        stats(0, scores_fn(window(0)[0]))

        def body(p, carry):
            s = scores_fn(window(p)[0])
            flush(p - 1)
            stats(p, s)
            return carry

        lax.fori_loop(1, npieces, body, 0)
        flush(npieces - 1)


def _for_far_pieces(n_far, width, piece_fn):
    def body(p, carry):
        hi = n_far - p * width
        piece_fn(jnp.maximum(hi - width, 0), hi)
        return carry

    lax.fori_loop(0, (n_far + width - 1) // width, body, 0)


def _new_key_mask(start, hi, width):
    col = _lane_iota((1, width * QB))
    return jnp.where(col < (hi - start) * QB, 0.0, -jnp.inf)


def _in_proj_kernel(h_ref, g_ref, w_ref, o_ref, *, tn):
    x = h_ref[...]
    ms = jnp.mean(x * x, axis=-1, keepdims=True)
    u = (x * lax.rsqrt(ms + EPS) * g_ref[...]).astype(BF16)
    for c in range(o_ref.shape[1] // tn):
        cols = slice(c * tn, (c + 1) * tn)
        o_ref[:, cols] = _dot(u, w_ref[:, cols]).astype(o_ref.dtype)


def _in_proj(h, gain, w):
    m, d = h.shape
    n = w.shape[1]
    tm = _pick_rows(m, 512)
    return pl.pallas_call(
        functools.partial(_in_proj_kernel, tn=512),
        out_shape=jax.ShapeDtypeStruct((m, n), BF16),
        grid=(m // tm,),
        in_specs=[pl.BlockSpec((tm, d), lambda i: (i, 0)),
                  pl.BlockSpec((1, d), lambda i: (0, 0)),
                  pl.BlockSpec((d, n), lambda i: (0, 0), pipeline_mode=pl.Buffered(1))],
        out_specs=pl.BlockSpec((tm, n), lambda i: (i, 0)),
        compiler_params=_params("parallel"),
        name="in_proj",
    )(h, gain.reshape(1, d), w)


def _sb_kernel(q_ref, k_ref, v_ref, u_ref, o_ref, tot_ref, acc_ref):
    i = pl.program_id(1)
    nh = SB_HEADS
    head_of_lane = _lane_iota((QB, nh * HEAD_DIM)) // HEAD_DIM
    q = q_ref[0].astype(F32) * QK_SCALE
    qs = jnp.concatenate([jnp.where(head_of_lane == h, q, 0.0) for h in range(nh)], axis=0).astype(BF16)
    tot_ref[...] = jnp.zeros_like(tot_ref)
    acc_ref[...] = jnp.zeros_like(acc_ref)

    def piece(start, n_blocks, hi=None):
        r = pl.multiple_of(start * QB, QB)
        z_all = _nt_dot(qs, k_ref[0, pl.ds(r, n_blocks * QB), :])
        if hi is not None:
            z_all = z_all + _new_key_mask(start, hi, n_blocks)
        run = tot_ref[...]
        n_sub = n_blocks * QB // BLK
        ws = [None] * n_sub
        for c in reversed(range(n_sub)):
            z = z_all[:, c * BLK:(c + 1) * BLK]
            sp = jnp.maximum(z, 0.0) + jnp.log2(1.0 + jnp.exp2(-jnp.abs(z)))
            l1m = -sp
            if hi is None:
                mask = (_lane_iota(z.shape) + c * BLK) < (_row_iota(z.shape) & (QB - 1))
                l1m = jnp.where(mask, l1m, 0.0)
            l1m_hi = l1m.astype(BF16)
            l1m_lo = (l1m - l1m_hi.astype(F32)).astype(BF16)
            rs = _dot(jnp.concatenate([l1m_hi, l1m_lo], axis=1), u_ref[...])
            w = jnp.exp2((z - sp) + rs[:, :BLK] + run)
            if hi is None:
                w = jnp.where(mask, w, 0.0)
            ws[c] = w.astype(BF16)
            run = run + rs[:, BLK:]
        tot_ref[...] = run
        acc_ref[...] += _dot(jnp.concatenate(ws, axis=1), v_ref[0, pl.ds(r, n_blocks * QB), :])

    piece(i, 1)
    _for_far_pieces(i, SB_FAR, lambda start, hi: piece(start, SB_FAR, hi))

    a = acc_ref[...]
    out = a[:QB]
    for h in range(1, nh):
        out = jnp.where(head_of_lane == h, a[h * QB:(h + 1) * QB], out)
    o_ref[0] = out.astype(o_ref.dtype)


def _sb_attn(z3, u_mat):
    b, tp, _ = z3.shape
    nq = tp // QB
    assert nq >= SB_FAR
    w = SB_HEADS * HEAD_DIM
    return pl.pallas_call(
        _sb_kernel,
        out_shape=jax.ShapeDtypeStruct((b, tp, w), BF16),
        grid=(b, nq),
        in_specs=[pl.BlockSpec((1, QB, w), lambda bb, i: (bb, i, Q_SB // w)),
                  pl.BlockSpec((1, tp, w), lambda bb, i: (bb, 0, K_SB // w)),
                  pl.BlockSpec((1, tp, w), lambda bb, i: (bb, 0, V_SB // w)),
                  pl.BlockSpec((2 * BLK, 2 * BLK), lambda bb, i: (0, 0))],
        out_specs=pl.BlockSpec((1, QB, w), lambda bb, i: (bb, i, 0)),
        scratch_shapes=[pltpu.VMEM((SB_HEADS * QB, BLK), F32),
                        pltpu.VMEM((SB_HEADS * QB, w), F32)],
        compiler_params=_params("parallel", "arbitrary"),
        name="sb_attn",
    )(z3, z3, z3, u_mat)


def _df_kernel(tab_ref, q_ref, k_ref, v_ref, qg_ref, kg_ref, bkt_ref, lamv_ref, sub_ref, o_ref,
               kn_ref, bias_ref, m_ref, l_ref, acc_ref, p_ref, a_ref, *, nq, lam_init, head_off):
    h = pl.program_id(1)
    i = pl.program_id(2)
    lo = _lane_iota((1, BLK)) < HEAD_DIM

    @pl.when(i == 0)
    def _prep():
        def kbody(c, carry):
            r = pl.multiple_of(c * QB, QB)
            kf = k_ref[0, pl.ds(r, QB), :].astype(F32)
            kn_ref[pl.ds(r, QB), :] = _head_rmsnorm128(kf, kg_ref[...]).astype(BF16)
            return carry

        lax.fori_loop(0, nq, kbody, 0)
        bias_ref[...] = _bias_tile(tab_ref, bkt_ref[...], head_off + h)

    qn = _head_rmsnorm128(q_ref[0].astype(F32), qg_ref[...]) * QK_SCALE
    qs = jnp.concatenate([jnp.where(lo, qn, 0.0), jnp.where(lo, 0.0, qn)], axis=0).astype(BF16)
    m_ref[...] = jnp.full(m_ref.shape, M_INIT, F32)
    l_ref[...] = jnp.zeros_like(l_ref)
    acc_ref[...] = jnp.zeros_like(acc_ref)

    def far_scores(start):
        return _nt_dot(qs, kn_ref[pl.ds(pl.multiple_of(start * QB, QB), DF_FAR * QB), :])

    def far_adjust(start, hi):
        new_keys = _new_key_mask(start, hi, DF_FAR)
        return lambda t, rows: t + new_keys

    def far_values(start):
        return v_ref[0, pl.ds(pl.multiple_of(start * QB, QB), DF_FAR * QB), :]

    def near_piece(first, n):
        r = pl.multiple_of(first * QB, QB)
        s = _nt_dot(qs, kn_ref[pl.ds(r, n * QB), :])

        def adjust(t, rows):
            q_rows = slice(rows.start % QB, rows.start % QB + t.shape[0])
            t = t + bias_ref[q_rows, (2 - n) * QB:]
            causal = (_lane_iota(t.shape) - (n - 1) * QB) <= (_row_iota(t.shape) + q_rows.start)
            return jnp.where(causal, t, -jnp.inf)

        _softmax_piece(s, v_ref[0, pl.ds(r, n * QB), :], m_ref, l_ref, acc_ref, adjust=adjust)

    _far_pieces_pipelined(jnp.maximum(i - 1, 0), DF_FAR, far_scores, far_adjust, far_values,
                          m_ref, l_ref, acc_ref, p_ref, a_ref)

    @pl.when(i >= 1)
    def _():
        near_piece(i - 1, 2)

    @pl.when(i == 0)
    def _():
        near_piece(0, 1)

    a = acc_ref[...] / l_ref[...]
    lv = lamv_ref[...]
    lam = (jnp.exp(jnp.sum(lv[0:1] * lv[1:2], axis=-1, keepdims=True))
           - jnp.exp(jnp.sum(lv[2:3] * lv[3:4], axis=-1, keepdims=True)) + lam_init)
    y = a[:QB] - lam * a[QB:]
    y = y * lax.rsqrt(jnp.mean(y * y, axis=-1, keepdims=True) + EPS) * sub_ref[...]
    o_ref[0] = (y * (1.0 - lam_init)).astype(o_ref.dtype)


def _df_attn(z3, rel_bias, qg, kg, bkt, lamv, subln, lam_init):
    b, tp, _ = z3.shape
    nq = tp // QB
    assert nq >= DF_FAR
    kern = functools.partial(_df_kernel, nq=nq, lam_init=lam_init, head_off=SP_HEADS)
    vec = lambda bb, h, i: (0, 0)
    return pl.pallas_call(
        kern,
        out_shape=jax.ShapeDtypeStruct((b, tp, DF_HEADS * BLK), BF16),
        grid=(b, DF_HEADS, nq),
        in_specs=[pl.BlockSpec(memory_space=pltpu.SMEM),
                  pl.BlockSpec((1, QB, BLK), lambda bb, h, i: (bb, i, Q_DF // BLK + h)),
                  pl.BlockSpec((1, tp, BLK), lambda bb, h, i: (bb, 0, K_DF // BLK + h)),
                  pl.BlockSpec((1, tp, BLK), lambda bb, h, i: (bb, 0, V_DF // BLK + h)),
                  pl.BlockSpec((1, BLK), vec),
                  pl.BlockSpec((1, BLK), vec),
                  pl.BlockSpec((QB, 2 * QB), vec),
                  pl.BlockSpec((8, BLK), vec),
                  pl.BlockSpec((1, BLK), vec)],
        out_specs=pl.BlockSpec((1, QB, BLK), lambda bb, h, i: (bb, i, h)),
        scratch_shapes=[pltpu.VMEM((tp, BLK), BF16),
                        pltpu.VMEM((QB, 2 * QB), F32),
                        pltpu.VMEM((2 * QB, BLK), F32),
                        pltpu.VMEM((2 * QB, BLK), F32),
                        pltpu.VMEM((2 * QB, BLK), F32),
                        pltpu.VMEM((2 * QB, DF_FAR * QB), BF16),
                        pltpu.VMEM((2 * QB, BLK), F32)],
        compiler_params=_params("parallel", "parallel", "arbitrary"),
        name="df_attn",
    )(rel_bias, z3, z3, z3, qg, kg, bkt, lamv, subln)


def _sp_kernel(tab_ref, q_ref, k_ref, v_ref, qix_ref, kwq_ref, kwk_ref, qg_ref, kg_ref, bkt_ref,
               rep_ref, bcast_ref, before_ref, o_ref,
               kn_ref, kx_ref, bias_ref, keys_ref, thr_ref, wb_ref, m_ref, l_ref, acc_ref, p_ref, a_ref,
               *, nq, top_k):
    i = pl.program_id(1)
    nh = SP_HEADS
    w = nh * HEAD_DIM
    lane = _lane_iota((QB, w))
    head_of_lane = lane // HEAD_DIM

    def norm256(xf, g):
        return jnp.concatenate([_head_rmsnorm128(xf[:, :BLK], g[:, :BLK]),
                                _head_rmsnorm128(xf[:, BLK:], g[:, BLK:])], axis=1)

    def dup(x):
        return jnp.concatenate([x, x], axis=1)

    @pl.when(i == 0)
    def _prep():
        def kbody(c, carry):
            r = pl.multiple_of(c * QB, QB)
            kn_ref[pl.ds(r, QB), :] = norm256(k_ref[0, pl.ds(r, QB), :].astype(F32), kg_ref[...]).astype(BF16)
            kx_ref[pl.ds(r, QB), :] = _dot(kwk_ref[0, pl.ds(r, QB), :], rep_ref[...]).astype(BF16)
            return carry

        lax.fori_loop(0, nq, kbody, 0)
        for h in range(nh):
            bias_ref[h] = _bias_tile(tab_ref, bkt_ref[...], h)

    key_causal = _row_iota((QB, QB)) <= _lane_iota((QB, QB))

    wb_ref[...] = _dot(kwq_ref[0], bcast_ref[...])
    qix = qix_ref[0].astype(F32)
    ix_head = lane // IDX_DIM
    qx = jnp.concatenate([jnp.where(ix_head == h, qix, 0.0) for h in range(IDX_HEADS)], axis=0).astype(BF16)

    def score_tile(j, diag):
        r = pl.multiple_of(j * QB, QB)
        d = _nt_dot(qx, kx_ref[pl.ds(r, QB), :])
        sc = jnp.zeros((QB, QB), F32)
        for h in range(IDX_HEADS):
            sc = sc + dup(wb_ref[:, h * BLK:(h + 1) * BLK]) * jnp.maximum(d[h * QB:(h + 1) * QB], 0.0)
        sc = jnp.where(sc == 0.0, 0.0, sc).T
        bits = lax.bitcast_convert_type(sc, jnp.int32)
        key = jnp.where(bits < 0, bits ^ jnp.int32(0x7FFFFFFF), bits)
        if diag:
            key = jnp.where(key_causal, key, jnp.int32(INT_MIN))
        keys_ref[j] = key

    def score_body(j, c):
        score_tile(j, False)
        return c

    lax.fori_loop(0, i, score_body, 0)
    score_tile(i, True)

    nblk = i + 1

    def count(pred):
        def cbody(j, c):
            hit = jnp.where(pred(keys_ref[j]), 1.0, 0.0)
            return c + jnp.sum(hit.reshape(QB // COUNT_ROWS, COUNT_ROWS, QB), axis=0)

        c = lax.fori_loop(0, nblk, cbody, jnp.zeros((COUNT_ROWS, QB), F32))
        return jnp.sum(c, axis=0, keepdims=True)

    def bit_body(b, carry):
        cur, cnt_cur = carry
        cand = cur + jnp.left_shift(jnp.int32(1), 31 - b)
        cnt = count(lambda k: k >= cand)
        ok = cnt >= float(top_k)
        return jnp.where(ok, cand, cur), jnp.where(ok, cnt, cnt_cur)

    cur0 = jnp.full((1, QB), INT_MIN, jnp.int32)
    cnt0 = jnp.zeros((1, QB), F32) + (nblk * QB).astype(F32)
    thr, cge = lax.fori_loop(0, 32, bit_body, (cur0, cnt0))
    thr_ref[...] = jnp.broadcast_to(thr, thr_ref.shape)

    tie = jnp.where((cge > float(top_k)) & (thr > INT_MIN), 1, 0)

    @pl.when(jnp.max(tie) > 0)
    def _ties():
        need = float(top_k) - count(lambda k: k > thr)

        def tbody(j, run):
            kj = keys_ref[j]
            eq = kj == thr
            eqf = jnp.where(eq, 1.0, 0.0)
            rank = run + _dot(before_ref[...], eqf.astype(BF16))
            keep = jnp.where(kj > thr, 1, jnp.where(eq & (rank < need), 1, -1))
            keys_ref[j] = keep.astype(jnp.int32)
            return run + jnp.sum(eqf, axis=0, keepdims=True)

        lax.fori_loop(0, nblk, tbody, jnp.zeros((1, QB), F32))
        thr_ref[...] = jnp.zeros_like(thr_ref)

    qn = norm256(q_ref[0].astype(F32), qg_ref[...]) * QK_SCALE
    qs = jnp.concatenate([jnp.where(head_of_lane == h, qn, 0.0) for h in range(nh)], axis=0).astype(BF16)
    m_ref[...] = jnp.full(m_ref.shape, M_INIT, F32)
    l_ref[...] = jnp.zeros_like(l_ref)
    acc_ref[...] = jnp.zeros_like(acc_ref)

    thr_sel = thr_ref[0:1, :]

    def as_mask(sel):
        return lax.bitcast_convert_type(jnp.where(sel, 0.0, -jnp.inf).T, jnp.int32)

    def mask_body(j, c):
        keys_ref[j] = as_mask(keys_ref[j] >= thr_sel)
        return c

    lax.fori_loop(0, i, mask_body, 0)
    keys_ref[i] = as_mask((keys_ref[i] >= thr_sel) & key_causal)

    def selection(first, n, q_rows):
        return jnp.concatenate([lax.bitcast_convert_type(keys_ref[first + c, q_rows, :], F32) for c in range(n)],
                               axis=1)

    def far_scores(start):
        return _nt_dot(qs, kn_ref[pl.ds(pl.multiple_of(start * QB, QB), SP_FAR * QB), :])

    def far_adjust(start, hi):
        new_keys = _new_key_mask(start, hi, SP_FAR)

        def adjust(t, rows):
            q_rows = slice(rows.start % QB, rows.start % QB + t.shape[0])
            return t + (selection(start, SP_FAR, q_rows) + new_keys)

        return adjust

    def far_values(start):
        return v_ref[0, pl.ds(pl.multiple_of(start * QB, QB), SP_FAR * QB), :]

    def near_piece(first, n):
        r = pl.multiple_of(first * QB, QB)
        s = _nt_dot(qs, kn_ref[pl.ds(r, n * QB), :])

        def adjust(t, rows):
            q_rows = slice(rows.start % QB, rows.start % QB + t.shape[0])
            return t + (selection(first, n, q_rows) + bias_ref[rows.start // QB, q_rows, (2 - n) * QB:])

        _softmax_piece(s, v_ref[0, pl.ds(r, n * QB), :], m_ref, l_ref, acc_ref, adjust=adjust)

    _far_pieces_pipelined(jnp.maximum(i - 1, 0), SP_FAR, far_scores, far_adjust, far_values,
                          m_ref, l_ref, acc_ref, p_ref, a_ref)

    @pl.when(i >= 1)
    def _():
        near_piece(i - 1, 2)

    @pl.when(i == 0)
    def _():
        near_piece(0, 1)

    a = acc_ref[...] * dup(1.0 / l_ref[...])
    out = a[:QB]
    for h in range(1, nh):
        out = jnp.where(head_of_lane == h, a[h * QB:(h + 1) * QB], out)
    o_ref[0] = out.astype(o_ref.dtype)


def _sp_attn(z3, rel_bias, qg, kg, bkt, rep, bcast, before, top_k):
    b, tp, _ = z3.shape
    nq = tp // QB
    assert nq >= SP_FAR
    w = SP_HEADS * HEAD_DIM
    kern = functools.partial(_sp_kernel, nq=nq, top_k=top_k)
    c2 = lambda bb, i: (0, 0)
    return pl.pallas_call(
        kern,
        out_shape=jax.ShapeDtypeStruct((b, tp, w), BF16),
        grid=(b, nq),
        in_specs=[pl.BlockSpec(memory_space=pltpu.SMEM),
                  pl.BlockSpec((1, QB, w), lambda bb, i: (bb, i, Q_SP // w)),
                  pl.BlockSpec((1, tp, w), lambda bb, i: (bb, 0, K_SP // w)),
                  pl.BlockSpec((1, tp, w), lambda bb, i: (bb, 0, V_SP // w)),
                  pl.BlockSpec((1, QB, w), lambda bb, i: (bb, i, Q_IX // w)),
                  pl.BlockSpec((1, QB, w), lambda bb, i: (bb, i, KW_IX // w)),
                  pl.BlockSpec((1, tp, w), lambda bb, i: (bb, 0, KW_IX // w)),
                  pl.BlockSpec((1, w), c2),
                  pl.BlockSpec((1, w), c2),
                  pl.BlockSpec((QB, 2 * QB), c2),
                  pl.BlockSpec((w, w), c2),
                  pl.BlockSpec((w, IDX_HEADS * BLK), c2),
                  pl.BlockSpec((QB, QB), c2)],
        out_specs=pl.BlockSpec((1, QB, w), lambda bb, i: (bb, i, 0)),
        scratch_shapes=[pltpu.VMEM((tp, w), BF16),
                        pltpu.VMEM((tp, w), BF16),
                        pltpu.VMEM((SP_HEADS, QB, 2 * QB), F32),
                        pltpu.VMEM((nq, QB, QB), jnp.int32),
                        pltpu.VMEM((8, QB), jnp.int32),
                        pltpu.VMEM((QB, IDX_HEADS * BLK), F32),
                        pltpu.VMEM((SP_HEADS * QB, BLK), F32),
                        pltpu.VMEM((SP_HEADS * QB, BLK), F32),
                        pltpu.VMEM((SP_HEADS * QB, w), F32),
                        pltpu.VMEM((SP_HEADS * QB, SP_FAR * QB), BF16),
                        pltpu.VMEM((SP_HEADS * QB, BLK), F32)],
        compiler_params=_params("parallel", "arbitrary"),
        name="sp_attn",
    )(rel_bias, z3, z3, z3, z3, z3, z3, qg, kg, bkt, rep, bcast, before)


def _mix_kernel(h_ref, gsb_ref, gsp_ref, gdf_ref, bg_ref, ysb_ref, ysp_ref, ydf_ref,
                wsb_ref, wsp_ref, wdf_ref, wo_ref, o_ref):
    def branch(g_ref, k, y_ref, w_ref):
        gate = jax.nn.sigmoid(g_ref[...].astype(F32) + bg_ref[:, k * D_MODEL:(k + 1) * D_MODEL])
        return gate * _dot(y_ref[...], w_ref[...])

    merged = (branch(gsb_ref, 0, ysb_ref, wsb_ref) + branch(gsp_ref, 1, ysp_ref, wsp_ref)
              + branch(gdf_ref, 2, ydf_ref, wdf_ref))
    o_ref[...] = h_ref[...] + _dot(merged.astype(BF16), wo_ref[...])


def _mix_out(h, z, b_gate, y_sb, y_sp, y_df, w_sb, w_sp, w_df, w_o):
    m, d = h.shape
    tm = _pick_rows(m, 512)
    row = lambda i: (i, 0)
    fixed = lambda i: (0, 0)
    return pl.pallas_call(
        _mix_kernel,
        out_shape=jax.ShapeDtypeStruct((m, d), F32),
        grid=(m // tm,),
        in_specs=[pl.BlockSpec((tm, d), row),
                  pl.BlockSpec((tm, d), lambda i: (i, G_SB // D_MODEL)),
                  pl.BlockSpec((tm, d), lambda i: (i, G_SP // D_MODEL)),
                  pl.BlockSpec((tm, d), lambda i: (i, G_DF // D_MODEL)),
                  pl.BlockSpec((1, 3 * d), fixed),
                  pl.BlockSpec((tm, y_sb.shape[1]), row),
                  pl.BlockSpec((tm, y_sp.shape[1]), row),
                  pl.BlockSpec((tm, y_df.shape[1]), row),
                  pl.BlockSpec(w_sb.shape, fixed),
                  pl.BlockSpec(w_sp.shape, fixed),
                  pl.BlockSpec(w_df.shape, fixed),
                  pl.BlockSpec(w_o.shape, fixed)],
        out_specs=pl.BlockSpec((tm, d), row),
        compiler_params=_params("parallel"),
        name="mix_out",
    )(h, z, z, z, b_gate.reshape(1, 3 * d), y_sb, y_sp, y_df, w_sb, w_sp, w_df, w_o)


def _ffn_kernel(h_ref, g_ref, wu_ref, cw_ref, cb_ref, wd_ref, o_ref, gbuf_ref, carry_ref, *, tm, tp, tf):
    r = pl.program_id(0)
    x = h_ref[...]
    ms = jnp.mean(x * x, axis=-1, keepdims=True)
    u = (x * lax.rsqrt(ms + EPS) * g_ref[...]).astype(BF16)

    @pl.when(r == 0)
    def _():
        carry_ref[...] = jnp.zeros_like(carry_ref)

    seq_start = lax.rem(tp - lax.rem(r * tm, tp), tp)
    local = lax.broadcasted_iota(jnp.int32, (tm, 1), 0)
    tap1 = local != seq_start
    tap2 = tap1 & (local != seq_start + 1)
    out = x
    for f in range(D_FF // tf):
        cols = slice(f * tf, (f + 1) * tf)
        gate = _dot(u, wu_ref[:, cols])
        val = _dot(u, wu_ref[:, D_FF + f * tf:D_FF + (f + 1) * tf])
        gbuf_ref[0:8] = carry_ref[f]
        gbuf_ref[8:8 + tm] = gate
        carry_ref[f] = gate[tm - 8:tm]
        g1 = jnp.where(tap1, gbuf_ref[7:7 + tm], 0.0)
        g2 = jnp.where(tap2, gbuf_ref[6:6 + tm], 0.0)
        conv = cb_ref[:, cols] + cw_ref[0:1, cols] * g2 + cw_ref[1:2, cols] * g1 + cw_ref[2:3, cols] * gate
        act = conv * jax.nn.sigmoid(conv) * val
        out = out + _dot(act.astype(BF16), wd_ref[cols, :])
    o_ref[...] = out


def _ffn(h, gain, w_up, conv_w, conv_b, w_down, tp):
    m, d = h.shape
    tm = _pick_rows(m, 512)
    assert tm <= tp
    tf = D_FF // 2
    nf = D_FF // tf
    kern = functools.partial(_ffn_kernel, tm=tm, tp=tp, tf=tf)
    fixed = lambda r: (0, 0)
    resident = pl.Buffered(1)
    return pl.pallas_call(
        kern,
        out_shape=jax.ShapeDtypeStruct((m, d), F32),
        grid=(m // tm,),
        in_specs=[pl.BlockSpec((tm, d), lambda r: (r, 0)),
                  pl.BlockSpec((1, d), fixed),
                  pl.BlockSpec((d, 2 * D_FF), fixed, pipeline_mode=resident),
                  pl.BlockSpec((8, D_FF), fixed),
                  pl.BlockSpec((1, D_FF), fixed),
                  pl.BlockSpec((D_FF, d), fixed, pipeline_mode=resident)],
        out_specs=pl.BlockSpec((tm, d), lambda r: (r, 0)),
        scratch_shapes=[pltpu.VMEM((tm + 8, tf), F32),
                        pltpu.VMEM((nf, 8, tf), F32)],
        compiler_params=_params("arbitrary"),
        name="conv_ffn",
    )(h, gain.reshape(1, d), w_up, conv_w, conv_b.reshape(1, D_FF), w_down)


def _permute_w_in(w):
    n_attn = KW_IX - Q_SB + IDX_DIM + IDX_HEADS
    n_gate = 3 * D_MODEL
    n_df = 3 * DF_HEADS * 2 * HEAD_DIM
    gates = w[:, n_attn + n_df:]
    attn = w[:, :n_attn]
    pad = jnp.zeros((w.shape[0], Q_DF - Q_SB - n_attn), w.dtype)
    df = w[:, n_attn:n_attn + n_df]
    out = jnp.concatenate([gates, attn, pad, df], axis=1)
    assert gates.shape[1] == n_gate and out.shape[1] == NZ
    return out


def kernel(x, meta_tokens, rel_bias, attn_norm, w_in, b_gate, q_norm_sp, k_norm_sp, q_norm_df, k_norm_df, lam_q1, lam_k1, lam_q2, lam_k2, subln_df, w_br_sb, w_br_sp, w_br_df, w_out, ffn_norm, w_up, conv_w, conv_b, w_down):
    b, s, d = x.shape
    depth = w_in.shape[0]
    t = N_META + s
    tp = -(-t // QB) * QB
    top_k = min(TOPK_MAX, t // 4)
    m = b * tp

    meta = jnp.broadcast_to(meta_tokens[None].astype(x.dtype), (b, N_META, d))
    h = jnp.concatenate([meta, x, jnp.zeros((b, tp - t, d), x.dtype)], axis=1).reshape(m, d)

    bkt = jnp.asarray(_bucket_tile())
    u_mat = jnp.asarray(_sb_prefix_matrix(), BF16)
    rep, bcast, before = (jnp.asarray(a, BF16) for a in _ix_select_matrices())
    rel_bias = rel_bias.astype(F32)

    for l in range(depth):
        lam_init = 0.8 - 0.6 * math.exp(-0.3 * l)
        z = _in_proj(h, attn_norm[l], _permute_w_in(w_in[l]).astype(BF16))
        z3 = z.reshape(b, tp, NZ)
        y_sb = _sb_attn(z3, u_mat)
        y_sp = _sp_attn(z3, rel_bias,
                        jnp.tile(q_norm_sp[l].astype(F32), SP_HEADS).reshape(1, -1),
                        jnp.tile(k_norm_sp[l].astype(F32), SP_HEADS).reshape(1, -1),
                        bkt, rep, bcast, before, top_k)
        lamv = jnp.zeros((8, BLK), F32).at[:4, :HEAD_DIM].set(
            jnp.stack([lam_q1[l], lam_k1[l], lam_q2[l], lam_k2[l]]).astype(F32))
        y_df = _df_attn(z3, rel_bias,
                        jnp.tile(q_norm_df[l].astype(F32), 2).reshape(1, -1),
                        jnp.tile(k_norm_df[l].astype(F32), 2).reshape(1, -1),
                        bkt, lamv, subln_df[l].astype(F32).reshape(1, -1), lam_init)
        h = _mix_out(h, z, b_gate[l], y_sb.reshape(m, -1), y_sp.reshape(m, -1), y_df.reshape(m, -1),
                     w_br_sb[l].astype(BF16), w_br_sp[l].astype(BF16), w_br_df[l].astype(BF16),
                     w_out[l].astype(BF16))
        cw = jnp.zeros((8, D_FF), F32).at[:conv_w.shape[1]].set(conv_w[l])
        h = _ffn(h, ffn_norm[l], w_up[l].astype(BF16), cw, conv_b[l], w_down[l].astype(BF16), tp)

    return h.reshape(b, tp, d)[:, N_META:t]
```

```python
import functools
import math

import numpy as np
import jax
import jax.numpy as jnp
from jax import lax
from jax.experimental import pallas as pl
from jax.experimental.pallas import tpu as pltpu

D_MODEL = 1024
HEAD_DIM = 64
N_META = 16
BLK = 128
QB = 256
SB_HEADS = 4
SP_HEADS = 4
IDX_HEADS = 8
IDX_DIM = 32
TOPK_MAX = 256
DF_HEADS = 4
N_BUCKETS = 32
MAX_DISTANCE = 128
D_FF = 2816
EPS = 1e-6
LOG2E = math.log2(math.e)
QK_SCALE = HEAD_DIM ** -0.5 * LOG2E
M_INIT = -1e30
INT_MIN = -2 ** 31
I16_MIN = -2 ** 15
COUNT_ROWS = 32
DIGIT_ROWS = 64
SCORE_VREGS = 32
SB_FAR = 2
SP_FAR = 2
DF_FAR = 4
DF_GROUP = 1

G_SB, G_SP, G_DF = 0, 1024, 2048
Q_SB, K_SB, V_SB = 3072, 3328, 3584
Q_SP, K_SP, V_SP = 3840, 4096, 4352
Q_IX, KW_IX = 4608, 4864
Q_DF, K_DF, V_DF = 5120, 5632, 6144
NZ = 6656
W_IX_LANE = IDX_DIM

VMEM_LIMIT = 56 * 1024 * 1024

F32 = jnp.float32
BF16 = jnp.bfloat16
NT_DIMS = (((1,), (1,)), ((), ()))


def _nt_dot(a, b):
    return lax.dot_general(a, b, NT_DIMS, preferred_element_type=F32)


def _dot(a, b):
    return jnp.dot(a, b, preferred_element_type=F32)


def _params(*sem):
    return pltpu.CompilerParams(dimension_semantics=sem, vmem_limit_bytes=VMEM_LIMIT)


def _pick_rows(m, cap):
    for c in (2048, 1024, 512, 256):
        if c <= cap and m % c == 0:
            return c
    raise ValueError(f"row count {m} is not a multiple of {QB}")


def _bucket_np(rel):
    n = np.maximum(rel, 0)
    max_exact = N_BUCKETS // 2
    nf = np.maximum(n, 1).astype(np.float32)
    large = max_exact + (np.log(nf / np.float32(max_exact)) / np.float32(math.log(MAX_DISTANCE / max_exact))
                         * np.float32(N_BUCKETS - max_exact)).astype(np.int32)
    return np.where(n < max_exact, n, np.minimum(large, N_BUCKETS - 1)).astype(np.int32)


def _bucket_tile():
    tq = np.arange(QB)[:, None]
    c = np.arange(2 * QB)[None, :]
    return _bucket_np(tq - c + QB)


def _sb_prefix_matrix():
    sp = np.arange(2 * BLK)[:, None] % BLK
    c = np.arange(2 * BLK)[None, :]
    return np.where(c < BLK, sp > c, True).astype(np.float32)


def _ix_select_matrices():
    c = np.arange(QB)[:, None]
    col = np.arange(QB)[None, :]
    rep = ((c < IDX_DIM) & (c == col % IDX_DIM)).astype(np.float32)
    col8 = np.arange(IDX_HEADS * BLK)[None, :]
    bcast = (c == W_IX_LANE + col8 // BLK).astype(np.float32)
    before = (col < c).astype(np.float32)
    return rep, bcast, before


def _lane_iota(shape):
    return lax.broadcasted_iota(jnp.int32, shape, len(shape) - 1)


def _row_iota(shape):
    return lax.broadcasted_iota(jnp.int32, shape, 0)


def _head_rmsnorm128(xf, gain):
    lo = _lane_iota((1, BLK)) < HEAD_DIM
    ss = xf * xf
    s_lo = jnp.sum(jnp.where(lo, ss, 0.0), axis=-1, keepdims=True)
    s_hi = jnp.sum(jnp.where(lo, 0.0, ss), axis=-1, keepdims=True)
    ms = jnp.where(lo, s_lo, s_hi) * (1.0 / HEAD_DIM)
    return xf * lax.rsqrt(ms + EPS) * gain


def _bias_tile(tab_ref, bk, head):
    far = tab_ref[N_BUCKETS - 1, head]
    acc = jnp.zeros(bk.shape, F32)
    for b in range(N_BUCKETS - 1):
        acc = jnp.where(bk == b, (tab_ref[b, head] - far) * LOG2E, acc)
    return acc


def _softmax_stats(s, m_ref, l_ref, adjust=None):
    n_rows, kc = s.shape
    rg = SCORE_VREGS * 8 * BLK // kc
    p_groups, alphas = [], []
    for g in range(n_rows // rg):
        rows = slice(g * rg, (g + 1) * rg)
        sg = s[rows] if adjust is None else adjust(s[rows], rows)
        parts = [sg[:, c * BLK:(c + 1) * BLK] for c in range(kc // BLK)]
        m_old = m_ref[rows]
        m_new = jnp.maximum(m_old, jnp.max(functools.reduce(jnp.maximum, parts), axis=-1, keepdims=True))
        alpha = jnp.exp2(m_old - m_new)
        ps = [jnp.exp2(t - m_new) for t in parts]
        l_ref[rows] = alpha * l_ref[rows] + jnp.sum(functools.reduce(jnp.add, ps), axis=-1, keepdims=True)
        m_ref[rows] = m_new
        p_groups.append(jnp.concatenate([t.astype(BF16) for t in ps], axis=1))
        alphas.append(alpha)
    return jnp.concatenate(p_groups, axis=0), jnp.concatenate(alphas, axis=0)


def _accumulate(p, alpha, v, acc_ref):
    w = acc_ref.shape[-1] // BLK
    a = alpha if w == 1 else jnp.concatenate([alpha] * w, axis=1)
    acc_ref[...] = a * acc_ref[...] + _dot(p, v)


def _softmax_piece(s, v, m_ref, l_ref, acc_ref, adjust=None):
    p, alpha = _softmax_stats(s, m_ref, l_ref, adjust)
    _accumulate(p, alpha, v, acc_ref)


def _far_pieces_pipelined(n_far, width, chains):
    npieces = (n_far + width - 1) // width

    def window(p):
        hi = n_far - p * width
        return jnp.maximum(hi - width, 0), hi

    def scores(p):
        return [c[0](window(p)[0]) for c in chains]

    def stats(p, ss):
        start, hi = window(p)
        for (_, adjust_fn, _, m_ref, l_ref, _, p_ref, a_ref), s in zip(chains, ss):
            p_ref[...], a_ref[...] = _softmax_stats(s, m_ref, l_ref, adjust_fn(start, hi))

    def flush(p):
        for _, _, values_fn, _, _, acc_ref, p_ref, a_ref in chains:
            _accumulate(p_ref[...], a_ref[...], values_fn(window(p)[0]), acc_ref)

    @pl.when(npieces > 0)
    def _():
        stats(0, scores(0))

        def body(p, carry):
            ss = scores(p)
            flush(p - 1)
            stats(p, ss)
            return carry

        lax.fori_loop(1, npieces, body, 0)
        flush(npieces - 1)


def _for_far_pieces(n_far, width, piece_fn):
    def body(p, carry):
        hi = n_far - p * width
        piece_fn(jnp.maximum(hi - width, 0), hi)
        return carry

    lax.fori_loop(0, (n_far + width - 1) // width, body, 0)


def _new_key_mask(start, hi, width):
    col = _lane_iota((1, width * QB))
    return jnp.where(col < (hi - start) * QB, 0.0, -jnp.inf)


def _in_proj_kernel(h_ref, g_ref, w_ref, o_ref, *, tn):
    x = h_ref[...]
    ms = jnp.mean(x * x, axis=-1, keepdims=True)
    u = (x * lax.rsqrt(ms + EPS) * g_ref[...]).astype(BF16)
    for c in range(o_ref.shape[1] // tn):
        cols = slice(c * tn, (c + 1) * tn)
        o_ref[:, cols] = _dot(u, w_ref[:, cols]).astype(o_ref.dtype)


def _in_proj(h, gain, w):
    m, d = h.shape
    n = w.shape[1]
    tm = _pick_rows(m, 512)
    return pl.pallas_call(
        functools.partial(_in_proj_kernel, tn=512),
        out_shape=jax.ShapeDtypeStruct((m, n), BF16),
        grid=(m // tm,),
        in_specs=[pl.BlockSpec((tm, d), lambda i: (i, 0)),
                  pl.BlockSpec((1, d), lambda i: (0, 0)),
                  pl.BlockSpec((d, n), lambda i: (0, 0), pipeline_mode=pl.Buffered(1))],
        out_specs=pl.BlockSpec((tm, n), lambda i: (i, 0)),
        compiler_params=_params("parallel"),
        name="in_proj",
    )(h, gain.reshape(1, d), w)


def _sb_kernel(q_ref, k_ref, v_ref, u_ref, o_ref, tot_ref, acc_ref):
    i = pl.program_id(1)
    nh = SB_HEADS
    head_of_lane = _lane_iota((QB, nh * HEAD_DIM)) // HEAD_DIM
    q = q_ref[0].astype(F32) * QK_SCALE
    qs = jnp.concatenate([jnp.where(head_of_lane == h, q, 0.0) for h in range(nh)], axis=0).astype(BF16)
    tot_ref[...] = jnp.zeros_like(tot_ref)
    acc_ref[...] = jnp.zeros_like(acc_ref)

    def piece(start, n_blocks, hi=None):
        r = pl.multiple_of(start * QB, QB)
        z_all = _nt_dot(qs, k_ref[0, pl.ds(r, n_blocks * QB), :])
        if hi is not None:
            z_all = z_all + _new_key_mask(start, hi, n_blocks)
        run = tot_ref[...]
        n_sub = n_blocks * QB // BLK
        ws = [None] * n_sub
        for c in reversed(range(n_sub)):
            z = z_all[:, c * BLK:(c + 1) * BLK]
            sp = jnp.maximum(z, 0.0) + jnp.log2(1.0 + jnp.exp2(-jnp.abs(z)))
            l1m = -sp
            if hi is None:
                mask = (_lane_iota(z.shape) + c * BLK) < (_row_iota(z.shape) & (QB - 1))
                l1m = jnp.where(mask, l1m, 0.0)
            l1m_hi = l1m.astype(BF16)
            l1m_lo = (l1m - l1m_hi.astype(F32)).astype(BF16)
            rs = _dot(jnp.concatenate([l1m_hi, l1m_lo], axis=1), u_ref[...])
            w = jnp.exp2((z - sp) + rs[:, :BLK] + run)
            if hi is None:
                w = jnp.where(mask, w, 0.0)
            ws[c] = w.astype(BF16)
            run = run + rs[:, BLK:]
        tot_ref[...] = run
        acc_ref[...] += _dot(jnp.concatenate(ws, axis=1), v_ref[0, pl.ds(r, n_blocks * QB), :])

    piece(i, 1)
    _for_far_pieces(i, SB_FAR, lambda start, hi: piece(start, SB_FAR, hi))

    a = acc_ref[...]
    out = a[:QB]
    for h in range(1, nh):
        out = jnp.where(head_of_lane == h, a[h * QB:(h + 1) * QB], out)
    o_ref[0] = out.astype(o_ref.dtype)


def _sb_attn(z3, u_mat):
    b, tp, _ = z3.shape
    nq = tp // QB
    assert nq >= SB_FAR
    w = SB_HEADS * HEAD_DIM
    return pl.pallas_call(
        _sb_kernel,
        out_shape=jax.ShapeDtypeStruct((b, tp, w), BF16),
        grid=(b, nq),
        in_specs=[pl.BlockSpec((1, QB, w), lambda bb, i: (bb, i, Q_SB // w)),
                  pl.BlockSpec((1, tp, w), lambda bb, i: (bb, 0, K_SB // w)),
                  pl.BlockSpec((1, tp, w), lambda bb, i: (bb, 0, V_SB // w)),
                  pl.BlockSpec((2 * BLK, 2 * BLK), lambda bb, i: (0, 0))],
        out_specs=pl.BlockSpec((1, QB, w), lambda bb, i: (bb, i, 0)),
        scratch_shapes=[pltpu.VMEM((SB_HEADS * QB, BLK), F32),
                        pltpu.VMEM((SB_HEADS * QB, w), F32)],
        compiler_params=_params("parallel", "arbitrary"),
        name="sb_attn",
    )(z3, z3, z3, u_mat)


def _df_kernel(tab_ref, q_ref, k_ref, v_ref, qg_ref, kg_ref, bkt_ref, lamv_ref, sub_ref, o_ref,
               kn_ref, bias_ref, m_ref, l_ref, acc_ref, p_ref, a_ref, *, nq, lam_init, head_off):
    g = pl.program_id(1)
    i = pl.program_id(2)
    lo = _lane_iota((1, BLK)) < HEAD_DIM
    lanes = [slice(c * BLK, (c + 1) * BLK) for c in range(DF_GROUP)]

    @pl.when(i == 0)
    def _prep():
        def kbody(j, carry):
            r = pl.multiple_of(j * QB, QB)
            for c in range(DF_GROUP):
                kf = k_ref[0, pl.ds(r, QB), lanes[c]].astype(F32)
                kn_ref[pl.ds(r, QB), lanes[c]] = _head_rmsnorm128(kf, kg_ref[...]).astype(BF16)
            return carry

        lax.fori_loop(0, nq, kbody, 0)
        for c in range(DF_GROUP):
            bias_ref[c] = _bias_tile(tab_ref, bkt_ref[...], head_off + g * DF_GROUP + c)

    m_ref[...] = jnp.full(m_ref.shape, M_INIT, F32)
    l_ref[...] = jnp.zeros_like(l_ref)
    acc_ref[...] = jnp.zeros_like(acc_ref)

    def chain(c):
        qn = _head_rmsnorm128(q_ref[0, :, lanes[c]].astype(F32), qg_ref[...]) * QK_SCALE
        qs = jnp.concatenate([jnp.where(lo, qn, 0.0), jnp.where(lo, 0.0, qn)], axis=0).astype(BF16)

        def scores(first, n):
            return _nt_dot(qs, kn_ref[pl.ds(pl.multiple_of(first * QB, QB), n * QB), lanes[c]])

        def values(first, n):
            return v_ref[0, pl.ds(pl.multiple_of(first * QB, QB), n * QB), lanes[c]]

        def far_adjust(start, hi):
            new_keys = _new_key_mask(start, hi, DF_FAR)
            return lambda t, rows: t + new_keys

        def near_adjust(n):
            def adjust(t, rows):
                q_rows = slice(rows.start % QB, rows.start % QB + t.shape[0])
                t = t + bias_ref[c, q_rows, (2 - n) * QB:]
                causal = (_lane_iota(t.shape) - (n - 1) * QB) <= (_row_iota(t.shape) + q_rows.start)
                return jnp.where(causal, t, -jnp.inf)

            return adjust

        return dict(scores=scores, values=values, near_adjust=near_adjust,
                    far=(lambda start: scores(start, DF_FAR), far_adjust, lambda start: values(start, DF_FAR),
                         m_ref.at[c], l_ref.at[c], acc_ref.at[c], p_ref.at[c], a_ref.at[c]))

    chains = [chain(c) for c in range(DF_GROUP)]
    _far_pieces_pipelined(jnp.maximum(i - 1, 0), DF_FAR, [ch["far"] for ch in chains])

    def near_piece(first, n):
        ss = [ch["scores"](first, n) for ch in chains]
        for c, (ch, s) in enumerate(zip(chains, ss)):
            _softmax_piece(s, ch["values"](first, n), m_ref.at[c], l_ref.at[c], acc_ref.at[c],
                           adjust=ch["near_adjust"](n))

    @pl.when(i >= 1)
    def _():
        near_piece(i - 1, 2)

    @pl.when(i == 0)
    def _():
        near_piece(0, 1)

    lv = lamv_ref[...]
    lam = (jnp.exp(jnp.sum(lv[0:1] * lv[1:2], axis=-1, keepdims=True))
           - jnp.exp(jnp.sum(lv[2:3] * lv[3:4], axis=-1, keepdims=True)) + lam_init)
    for c in range(DF_GROUP):
        a = acc_ref[c] / l_ref[c]
        y = a[:QB] - lam * a[QB:]
        y = y * lax.rsqrt(jnp.mean(y * y, axis=-1, keepdims=True) + EPS) * sub_ref[...]
        o_ref[0, :, lanes[c]] = (y * (1.0 - lam_init)).astype(o_ref.dtype)


def _df_attn(z3, rel_bias, qg, kg, bkt, lamv, subln, lam_init):
    b, tp, _ = z3.shape
    nq = tp // QB
    assert nq >= DF_FAR
    kern = functools.partial(_df_kernel, nq=nq, lam_init=lam_init, head_off=SP_HEADS)
    vec = lambda bb, g, i: (0, 0)
    w = DF_GROUP * BLK
    state = lambda cols, dt: pltpu.VMEM((DF_GROUP, 2 * QB, cols), dt)
    return pl.pallas_call(
        kern,
        out_shape=jax.ShapeDtypeStruct((b, tp, DF_HEADS * BLK), BF16),
        grid=(b, DF_HEADS // DF_GROUP, nq),
        in_specs=[pl.BlockSpec(memory_space=pltpu.SMEM),
                  pl.BlockSpec((1, QB, w), lambda bb, g, i: (bb, i, Q_DF // w + g)),
                  pl.BlockSpec((1, tp, w), lambda bb, g, i: (bb, 0, K_DF // w + g)),
                  pl.BlockSpec((1, tp, w), lambda bb, g, i: (bb, 0, V_DF // w + g)),
                  pl.BlockSpec((1, BLK), vec),
                  pl.BlockSpec((1, BLK), vec),
                  pl.BlockSpec((QB, 2 * QB), vec),
                  pl.BlockSpec((8, BLK), vec),
                  pl.BlockSpec((1, BLK), vec)],
        out_specs=pl.BlockSpec((1, QB, w), lambda bb, g, i: (bb, i, g)),
        scratch_shapes=[pltpu.VMEM((tp, w), BF16),
                        pltpu.VMEM((DF_GROUP, QB, 2 * QB), F32),
                        state(BLK, F32), state(BLK, F32), state(BLK, F32),
                        state(DF_FAR * QB, BF16),
                        state(BLK, F32)],
        compiler_params=_params("parallel", "parallel", "arbitrary"),
        name="df_attn",
    )(rel_bias, z3, z3, z3, qg, kg, bkt, lamv, subln)


def _sp_kernel(tab_ref, q_ref, k_ref, v_ref, qix_ref, kwq_ref, kwk_ref, qg_ref, kg_ref, bkt_ref,
               rep_ref, bcast_ref, before_ref, o_ref,
               kn_ref, kx_ref, bias_ref, keys_ref, dig_ref, thr_ref, wb_ref, m_ref, l_ref, acc_ref, p_ref, a_ref,
               *, nq, top_k):
    i = pl.program_id(1)
    nh = SP_HEADS
    w = nh * HEAD_DIM
    lane = _lane_iota((QB, w))
    head_of_lane = lane // HEAD_DIM

    def norm256(xf, g):
        return jnp.concatenate([_head_rmsnorm128(xf[:, :BLK], g[:, :BLK]),
                                _head_rmsnorm128(xf[:, BLK:], g[:, BLK:])], axis=1)

    def dup(x):
        return jnp.concatenate([x, x], axis=1)

    @pl.when(i == 0)
    def _prep():
        def kbody(c, carry):
            r = pl.multiple_of(c * QB, QB)
            kn_ref[pl.ds(r, QB), :] = norm256(k_ref[0, pl.ds(r, QB), :].astype(F32), kg_ref[...]).astype(BF16)
            kx_ref[pl.ds(r, QB), :] = _dot(kwk_ref[0, pl.ds(r, QB), :], rep_ref[...]).astype(BF16)
            return carry

        lax.fori_loop(0, nq, kbody, 0)
        for h in range(nh):
            bias_ref[h] = _bias_tile(tab_ref, bkt_ref[...], h)

    key_causal = _row_iota((QB, QB)) <= _lane_iota((QB, QB))

    wb_ref[...] = _dot(kwq_ref[0], bcast_ref[...])
    qix = qix_ref[0].astype(F32)
    ix_head = lane // IDX_DIM
    qx = jnp.concatenate([jnp.where(ix_head == h, qix, 0.0) for h in range(IDX_HEADS)], axis=0).astype(BF16)

    def score_tile(j, diag):
        r = pl.multiple_of(j * QB, QB)
        d = _nt_dot(qx, kx_ref[pl.ds(r, QB), :])
        sc = jnp.zeros((QB, QB), F32)
        for h in range(IDX_HEADS):
            sc = sc + dup(wb_ref[:, h * BLK:(h + 1) * BLK]) * jnp.maximum(d[h * QB:(h + 1) * QB], 0.0)
        sc = jnp.where(sc == 0.0, 0.0, sc).T
        bits = lax.bitcast_convert_type(sc, jnp.int32)
        key = jnp.where(bits < 0, bits ^ jnp.int32(0x7FFFFFFF), bits)
        if diag:
            key = jnp.where(key_causal, key, jnp.int32(INT_MIN))
        keys_ref[j] = key
        dig_ref[j] = jnp.right_shift(key, 16).astype(jnp.int16)

    def score_body(j, c):
        score_tile(j, False)
        return c

    lax.fori_loop(0, i, score_body, 0)
    score_tile(i, True)

    nblk = i + 1

    def count(pred):
        def cbody(j, c):
            hit = jnp.where(pred(keys_ref[j]), 1.0, 0.0)
            return c + jnp.sum(hit.reshape(QB // COUNT_ROWS, COUNT_ROWS, QB), axis=0)

        c = lax.fori_loop(0, nblk, cbody, jnp.zeros((COUNT_ROWS, QB), F32))
        return jnp.sum(c, axis=0, keepdims=True)

    def count16(pred):
        def cbody(j, c):
            hit = jnp.where(pred(dig_ref[j]), jnp.int16(1), jnp.int16(0)).reshape(QB // DIGIT_ROWS, DIGIT_ROWS, QB)
            return c + functools.reduce(jnp.add, [hit[r] for r in range(QB // DIGIT_ROWS)])

        c = lax.fori_loop(0, nblk, cbody, jnp.zeros((DIGIT_ROWS, QB), jnp.int16))
        return jnp.sum(c.astype(F32), axis=0, keepdims=True)

    n_all = jnp.zeros((1, QB), F32) + (nblk * QB).astype(F32)

    def bisect16(need):
        def bit_body(b, carry):
            cur, cnt_cur = carry
            cand = cur + jnp.left_shift(jnp.int32(1), 15 - b)
            cand16 = cand.astype(jnp.int16)
            cnt = count16(lambda d: d >= cand16)
            ok = cnt >= need
            return jnp.where(ok, cand, cur), jnp.where(ok, cnt, cnt_cur)

        return lax.fori_loop(0, 16, bit_body, (jnp.full((1, QB), I16_MIN, jnp.int32), n_all))

    t_hi, c_ge_hi = bisect16(float(top_k))
    t_hi16 = t_hi.astype(jnp.int16)
    c_gt_hi = count16(lambda d: d > t_hi16)
    base = jnp.left_shift(t_hi, 16)

    def low_digits(j, c):
        y = keys_ref[j] - base
        dig_ref[j] = jnp.where(jnp.right_shift(y, 16) == 0, y + I16_MIN, I16_MIN).astype(jnp.int16)
        return c

    lax.fori_loop(0, nblk, low_digits, 0)
    t_lo, c_ge_lo = bisect16(float(top_k) - c_gt_hi)
    thr = base + (t_lo - I16_MIN)
    cge = c_gt_hi + jnp.where(t_lo > I16_MIN, c_ge_lo, c_ge_hi - c_gt_hi)
    thr_ref[...] = jnp.broadcast_to(thr, thr_ref.shape)

    tie = jnp.where((cge > float(top_k)) & (thr > INT_MIN), 1, 0)

    @pl.when(jnp.max(tie) > 0)
    def _ties():
        need = float(top_k) - count(lambda k: k > thr)

        def tbody(j, run):
            kj = keys_ref[j]
            eq = kj == thr
            eqf = jnp.where(eq, 1.0, 0.0)
            rank = run + _dot(before_ref[...], eqf.astype(BF16))
            keep = jnp.where(kj > thr, 1, jnp.where(eq & (rank < need), 1, -1))
            keys_ref[j] = keep.astype(jnp.int32)
            return run + jnp.sum(eqf, axis=0, keepdims=True)

        lax.fori_loop(0, nblk, tbody, jnp.zeros((1, QB), F32))
        thr_ref[...] = jnp.zeros_like(thr_ref)

    qn = norm256(q_ref[0].astype(F32), qg_ref[...]) * QK_SCALE
    qs = jnp.concatenate([jnp.where(head_of_lane == h, qn, 0.0) for h in range(nh)], axis=0).astype(BF16)
    m_ref[...] = jnp.full(m_ref.shape, M_INIT, F32)
    l_ref[...] = jnp.zeros_like(l_ref)
    acc_ref[...] = jnp.zeros_like(acc_ref)

    thr_sel = thr_ref[0:1, :]

    def as_mask(sel):
        return lax.bitcast_convert_type(jnp.where(sel, 0.0, -jnp.inf).T, jnp.int32)

    def mask_body(j, c):
        keys_ref[j] = as_mask(keys_ref[j] >= thr_sel)
        return c

    lax.fori_loop(0, i, mask_body, 0)
    keys_ref[i] = as_mask((keys_ref[i] >= thr_sel) & key_causal)

    def selection(first, n, q_rows):
        return jnp.concatenate([lax.bitcast_convert_type(keys_ref[first + c, q_rows, :], F32) for c in range(n)],
                               axis=1)

    def far_scores(start):
        return _nt_dot(qs, kn_ref[pl.ds(pl.multiple_of(start * QB, QB), SP_FAR * QB), :])

    def far_adjust(start, hi):
        new_keys = _new_key_mask(start, hi, SP_FAR)

        def adjust(t, rows):
            q_rows = slice(rows.start % QB, rows.start % QB + t.shape[0])
            return t + (selection(start, SP_FAR, q_rows) + new_keys)

        return adjust

    def far_values(start):
        return v_ref[0, pl.ds(pl.multiple_of(start * QB, QB), SP_FAR * QB), :]

    def near_piece(first, n):
        r = pl.multiple_of(first * QB, QB)
        s = _nt_dot(qs, kn_ref[pl.ds(r, n * QB), :])

        def adjust(t, rows):
            q_rows = slice(rows.start % QB, rows.start % QB + t.shape[0])
            return t + (selection(first, n, q_rows) + bias_ref[rows.start // QB, q_rows, (2 - n) * QB:])

        _softmax_piece(s, v_ref[0, pl.ds(r, n * QB), :], m_ref, l_ref, acc_ref, adjust=adjust)

    _far_pieces_pipelined(jnp.maximum(i - 1, 0), SP_FAR,
                          [(far_scores, far_adjust, far_values, m_ref, l_ref, acc_ref, p_ref, a_ref)])

    @pl.when(i >= 1)
    def _():
        near_piece(i - 1, 2)

    @pl.when(i == 0)
    def _():
        near_piece(0, 1)

    a = acc_ref[...] * dup(1.0 / l_ref[...])
    out = a[:QB]
    for h in range(1, nh):
        out = jnp.where(head_of_lane == h, a[h * QB:(h + 1) * QB], out)
    o_ref[0] = out.astype(o_ref.dtype)


def _sp_attn(z3, rel_bias, qg, kg, bkt, rep, bcast, before, top_k):
    b, tp, _ = z3.shape
    nq = tp // QB
    assert nq >= SP_FAR
    w = SP_HEADS * HEAD_DIM
    kern = functools.partial(_sp_kernel, nq=nq, top_k=top_k)
    c2 = lambda bb, i: (0, 0)
    return pl.pallas_call(
        kern,
        out_shape=jax.ShapeDtypeStruct((b, tp, w), BF16),
        grid=(b, nq),
        in_specs=[pl.BlockSpec(memory_space=pltpu.SMEM),
                  pl.BlockSpec((1, QB, w), lambda bb, i: (bb, i, Q_SP // w)),
                  pl.BlockSpec((1, tp, w), lambda bb, i: (bb, 0, K_SP // w)),
                  pl.BlockSpec((1, tp, w), lambda bb, i: (bb, 0, V_SP // w)),
                  pl.BlockSpec((1, QB, w), lambda bb, i: (bb, i, Q_IX // w)),
                  pl.BlockSpec((1, QB, w), lambda bb, i: (bb, i, KW_IX // w)),
                  pl.BlockSpec((1, tp, w), lambda bb, i: (bb, 0, KW_IX // w)),
                  pl.BlockSpec((1, w), c2),
                  pl.BlockSpec((1, w), c2),
                  pl.BlockSpec((QB, 2 * QB), c2),
                  pl.BlockSpec((w, w), c2),
                  pl.BlockSpec((w, IDX_HEADS * BLK), c2),
                  pl.BlockSpec((QB, QB), c2)],
        out_specs=pl.BlockSpec((1, QB, w), lambda bb, i: (bb, i, 0)),
        scratch_shapes=[pltpu.VMEM((tp, w), BF16),
                        pltpu.VMEM((tp, w), BF16),
                        pltpu.VMEM((SP_HEADS, QB, 2 * QB), F32),
                        pltpu.VMEM((nq, QB, QB), jnp.int32),
                        pltpu.VMEM((nq, QB, QB), jnp.int16),
                        pltpu.VMEM((8, QB), jnp.int32),
                        pltpu.VMEM((QB, IDX_HEADS * BLK), F32),
                        pltpu.VMEM((SP_HEADS * QB, BLK), F32),
                        pltpu.VMEM((SP_HEADS * QB, BLK), F32),
                        pltpu.VMEM((SP_HEADS * QB, w), F32),
                        pltpu.VMEM((SP_HEADS * QB, SP_FAR * QB), BF16),
                        pltpu.VMEM((SP_HEADS * QB, BLK), F32)],
        compiler_params=_params("parallel", "arbitrary"),
        name="sp_attn",
    )(rel_bias, z3, z3, z3, z3, z3, z3, qg, kg, bkt, rep, bcast, before)


def _mix_kernel(h_ref, gsb_ref, gsp_ref, gdf_ref, bg_ref, ysb_ref, ysp_ref, ydf_ref,
                wsb_ref, wsp_ref, wdf_ref, wo_ref, o_ref):
    def branch(g_ref, k, y_ref, w_ref):
        gate = jax.nn.sigmoid(g_ref[...].astype(F32) + bg_ref[:, k * D_MODEL:(k + 1) * D_MODEL])
        return gate * _dot(y_ref[...], w_ref[...])

    merged = (branch(gsb_ref, 0, ysb_ref, wsb_ref) + branch(gsp_ref, 1, ysp_ref, wsp_ref)
              + branch(gdf_ref, 2, ydf_ref, wdf_ref))
    o_ref[...] = h_ref[...] + _dot(merged.astype(BF16), wo_ref[...])


def _mix_out(h, z, b_gate, y_sb, y_sp, y_df, w_sb, w_sp, w_df, w_o):
    m, d = h.shape
    tm = _pick_rows(m, 512)
    row = lambda i: (i, 0)
    fixed = lambda i: (0, 0)
    return pl.pallas_call(
        _mix_kernel,
        out_shape=jax.ShapeDtypeStruct((m, d), F32),
        grid=(m // tm,),
        in_specs=[pl.BlockSpec((tm, d), row),
                  pl.BlockSpec((tm, d), lambda i: (i, G_SB // D_MODEL)),
                  pl.BlockSpec((tm, d), lambda i: (i, G_SP // D_MODEL)),
                  pl.BlockSpec((tm, d), lambda i: (i, G_DF // D_MODEL)),
                  pl.BlockSpec((1, 3 * d), fixed),
                  pl.BlockSpec((tm, y_sb.shape[1]), row),
                  pl.BlockSpec((tm, y_sp.shape[1]), row),
                  pl.BlockSpec((tm, y_df.shape[1]), row),
                  pl.BlockSpec(w_sb.shape, fixed),
                  pl.BlockSpec(w_sp.shape, fixed),
                  pl.BlockSpec(w_df.shape, fixed),
                  pl.BlockSpec(w_o.shape, fixed)],
        out_specs=pl.BlockSpec((tm, d), row),
        compiler_params=_params("parallel"),
        name="mix_out",
    )(h, z, z, z, b_gate.reshape(1, 3 * d), y_sb, y_sp, y_df, w_sb, w_sp, w_df, w_o)


def _ffn_kernel(h_ref, g_ref, wu_ref, cw_ref, cb_ref, wd_ref, o_ref, gbuf_ref, carry_ref, *, tm, tp, tf):
    r = pl.program_id(0)
    x = h_ref[...]
    ms = jnp.mean(x * x, axis=-1, keepdims=True)
    u = (x * lax.rsqrt(ms + EPS) * g_ref[...]).astype(BF16)

    @pl.when(r == 0)
    def _():
        carry_ref[...] = jnp.zeros_like(carry_ref)

    seq_start = lax.rem(tp - lax.rem(r * tm, tp), tp)
    local = lax.broadcasted_iota(jnp.int32, (tm, 1), 0)
    tap1 = local != seq_start
    tap2 = tap1 & (local != seq_start + 1)
    out = x
    for f in range(D_FF // tf):
        cols = slice(f * tf, (f + 1) * tf)
        gate = _dot(u, wu_ref[:, cols])
        val = _dot(u, wu_ref[:, D_FF + f * tf:D_FF + (f + 1) * tf])
        gbuf_ref[0:8] = carry_ref[f]
        gbuf_ref[8:8 + tm] = gate
        carry_ref[f] = gate[tm - 8:tm]
        g1 = jnp.where(tap1, gbuf_ref[7:7 + tm], 0.0)
        g2 = jnp.where(tap2, gbuf_ref[6:6 + tm], 0.0)
        conv = cb_ref[:, cols] + cw_ref[0:1, cols] * g2 + cw_ref[1:2, cols] * g1 + cw_ref[2:3, cols] * gate
        act = conv * jax.nn.sigmoid(conv) * val
        out = out + _dot(act.astype(BF16), wd_ref[cols, :])
    o_ref[...] = out


def _ffn(h, gain, w_up, conv_w, conv_b, w_down, tp):
    m, d = h.shape
    tm = _pick_rows(m, 512)
    assert tm <= tp
    tf = D_FF // 2
    nf = D_FF // tf
    kern = functools.partial(_ffn_kernel, tm=tm, tp=tp, tf=tf)
    fixed = lambda r: (0, 0)
    resident = pl.Buffered(1)
    return pl.pallas_call(
        kern,
        out_shape=jax.ShapeDtypeStruct((m, d), F32),
        grid=(m // tm,),
        in_specs=[pl.BlockSpec((tm, d), lambda r: (r, 0)),
                  pl.BlockSpec((1, d), fixed),
                  pl.BlockSpec((d, 2 * D_FF), fixed, pipeline_mode=resident),
                  pl.BlockSpec((8, D_FF), fixed),
                  pl.BlockSpec((1, D_FF), fixed),
                  pl.BlockSpec((D_FF, d), fixed, pipeline_mode=resident)],
        out_specs=pl.BlockSpec((tm, d), lambda r: (r, 0)),
        scratch_shapes=[pltpu.VMEM((tm + 8, tf), F32),
                        pltpu.VMEM((nf, 8, tf), F32)],
        compiler_params=_params("arbitrary"),
        name="conv_ffn",
    )(h, gain.reshape(1, d), w_up, conv_w, conv_b.reshape(1, D_FF), w_down)


def _permute_w_in(w):
    n_attn = KW_IX - Q_SB + IDX_DIM + IDX_HEADS
    n_gate = 3 * D_MODEL
    n_df = 3 * DF_HEADS * 2 * HEAD_DIM
    gates = w[:, n_attn + n_df:]
    attn = w[:, :n_attn]
    pad = jnp.zeros((w.shape[0], Q_DF - Q_SB - n_attn), w.dtype)
    df = w[:, n_attn:n_attn + n_df]
    out = jnp.concatenate([gates, attn, pad, df], axis=1)
    assert gates.shape[1] == n_gate and out.shape[1] == NZ
    return out


def kernel(x, meta_tokens, rel_bias, attn_norm, w_in, b_gate, q_norm_sp, k_norm_sp, q_norm_df, k_norm_df, lam_q1, lam_k1, lam_q2, lam_k2, subln_df, w_br_sb, w_br_sp, w_br_df, w_out, ffn_norm, w_up, conv_w, conv_b, w_down):
    b, s, d = x.shape
    depth = w_in.shape[0]
    t = N_META + s
    tp = -(-t // QB) * QB
    top_k = min(TOPK_MAX, t // 4)
    m = b * tp

    meta = jnp.broadcast_to(meta_tokens[None].astype(x.dtype), (b, N_META, d))
    h = jnp.concatenate([meta, x, jnp.zeros((b, tp - t, d), x.dtype)], axis=1).reshape(m, d)

    bkt = jnp.asarray(_bucket_tile())
    u_mat = jnp.asarray(_sb_prefix_matrix(), BF16)
    rep, bcast, before = (jnp.asarray(a, BF16) for a in _ix_select_matrices())
    rel_bias = rel_bias.astype(F32)

    for l in range(depth):
        lam_init = 0.8 - 0.6 * math.exp(-0.3 * l)
        z = _in_proj(h, attn_norm[l], _permute_w_in(w_in[l]).astype(BF16))
        z3 = z.reshape(b, tp, NZ)
        y_sb = _sb_attn(z3, u_mat)
        y_sp = _sp_attn(z3, rel_bias,
                        jnp.tile(q_norm_sp[l].astype(F32), SP_HEADS).reshape(1, -1),
                        jnp.tile(k_norm_sp[l].astype(F32), SP_HEADS).reshape(1, -1),
                        bkt, rep, bcast, before, top_k)
        lamv = jnp.zeros((8, BLK), F32).at[:4, :HEAD_DIM].set(
            jnp.stack([lam_q1[l], lam_k1[l], lam_q2[l], lam_k2[l]]).astype(F32))
        y_df = _df_attn(z3, rel_bias,
                        jnp.tile(q_norm_df[l].astype(F32), 2).reshape(1, -1),
                        jnp.tile(k_norm_df[l].astype(F32), 2).reshape(1, -1),
                        bkt, lamv, subln_df[l].astype(F32).reshape(1, -1), lam_init)
        h = _mix_out(h, z, b_gate[l], y_sb.reshape(m, -1), y_sp.reshape(m, -1), y_df.reshape(m, -1),
                     w_br_sb[l].astype(BF16), w_br_sp[l].astype(BF16), w_br_df[l].astype(BF16),
                     w_out[l].astype(BF16))
        cw = jnp.zeros((8, D_FF), F32).at[:conv_w.shape[1]].set(conv_w[l])
        h = _ffn(h, ffn_norm[l], w_up[l].astype(BF16), cw, conv_b[l], w_down[l].astype(BF16), tp)

    return h.reshape(b, tp, d)[:, N_META:t]
```

```python
import functools
import math

import numpy as np
import jax
import jax.numpy as jnp
from jax import lax
from jax.experimental import pallas as pl
from jax.experimental.pallas import tpu as pltpu

D_MODEL = 1024
HEAD_DIM = 64
N_META = 16
BLK = 128
QB = 256
SB_HEADS = 4
SP_HEADS = 4
IDX_HEADS = 8
IDX_DIM = 32
TOPK_MAX = 256
DF_HEADS = 4
N_BUCKETS = 32
MAX_DISTANCE = 128
D_FF = 2816
EPS = 1e-6
LOG2E = math.log2(math.e)
QK_SCALE = HEAD_DIM ** -0.5 * LOG2E
M_INIT = -1e30
BOUND_SLACK = 1.02
UNDERFLOW_GUARD = 2.0 ** -100
INT_MIN = -2 ** 31
I16_MIN = -2 ** 15
COUNT_ROWS = 32
DIGIT_ROWS = 64
SCORE_VREGS = 32
SB_FAR = 2
SP_FAR = 2
DF_FAR = 4
DF_GROUP = 1

G_SB, G_SP, G_DF = 0, 1024, 2048
Q_SB, K_SB, V_SB = 3072, 3328, 3584
Q_SP, K_SP, V_SP = 3840, 4096, 4352
Q_IX, KW_IX = 4608, 4864
Q_DF, K_DF, V_DF = 5120, 5632, 6144
NZ = 6656
W_IX_LANE = IDX_DIM

VMEM_LIMIT = 56 * 1024 * 1024

F32 = jnp.float32
BF16 = jnp.bfloat16
NT_DIMS = (((1,), (1,)), ((), ()))


def _nt_dot(a, b):
    return lax.dot_general(a, b, NT_DIMS, preferred_element_type=F32)


def _dot(a, b):
    return jnp.dot(a, b, preferred_element_type=F32)


def _params(*sem):
    return pltpu.CompilerParams(dimension_semantics=sem, vmem_limit_bytes=VMEM_LIMIT)


def _pick_rows(m, cap):
    for c in (2048, 1024, 512, 256):
        if c <= cap and m % c == 0:
            return c
    raise ValueError(f"row count {m} is not a multiple of {QB}")


def _bucket_np(rel):
    n = np.maximum(rel, 0)
    max_exact = N_BUCKETS // 2
    nf = np.maximum(n, 1).astype(np.float32)
    large = max_exact + (np.log(nf / np.float32(max_exact)) / np.float32(math.log(MAX_DISTANCE / max_exact))
                         * np.float32(N_BUCKETS - max_exact)).astype(np.int32)
    return np.where(n < max_exact, n, np.minimum(large, N_BUCKETS - 1)).astype(np.int32)


def _bucket_tile():
    tq = np.arange(QB)[:, None]
    c = np.arange(2 * QB)[None, :]
    return _bucket_np(tq - c + QB)


def _sb_prefix_matrix():
    sp = np.arange(2 * BLK)[:, None] % BLK
    c = np.arange(2 * BLK)[None, :]
    return np.where(c < BLK, sp > c, True).astype(np.float32)


def _ix_select_matrices():
    c = np.arange(QB)[:, None]
    col = np.arange(QB)[None, :]
    rep = ((c < IDX_DIM) & (c == col % IDX_DIM)).astype(np.float32)
    col8 = np.arange(IDX_HEADS * BLK)[None, :]
    bcast = (c == W_IX_LANE + col8 // BLK).astype(np.float32)
    before = (col < c).astype(np.float32)
    return rep, bcast, before


def _lane_iota(shape):
    return lax.broadcasted_iota(jnp.int32, shape, len(shape) - 1)


def _row_iota(shape):
    return lax.broadcasted_iota(jnp.int32, shape, 0)


def _head_rmsnorm128(xf, gain):
    lo = _lane_iota((1, BLK)) < HEAD_DIM
    ss = xf * xf
    s_lo = jnp.sum(jnp.where(lo, ss, 0.0), axis=-1, keepdims=True)
    s_hi = jnp.sum(jnp.where(lo, 0.0, ss), axis=-1, keepdims=True)
    ms = jnp.where(lo, s_lo, s_hi) * (1.0 / HEAD_DIM)
    return xf * lax.rsqrt(ms + EPS) * gain


def _bias_tile(tab_ref, bk, head):
    far = tab_ref[N_BUCKETS - 1, head]
    acc = jnp.zeros(bk.shape, F32)
    for b in range(N_BUCKETS - 1):
        acc = jnp.where(bk == b, (tab_ref[b, head] - far) * LOG2E, acc)
    return acc


def _softmax_stats(s, m_ref, l_ref, adjust=None):
    n_rows, kc = s.shape
    rg = SCORE_VREGS * 8 * BLK // kc
    p_groups, alphas = [], []
    for g in range(n_rows // rg):
        rows = slice(g * rg, (g + 1) * rg)
        sg = s[rows] if adjust is None else adjust(s[rows], rows)
        parts = [sg[:, c * BLK:(c + 1) * BLK] for c in range(kc // BLK)]
        m_old = m_ref[rows]
        m_new = jnp.maximum(m_old, jnp.max(functools.reduce(jnp.maximum, parts), axis=-1, keepdims=True))
        alpha = jnp.exp2(m_old - m_new)
        ps = [jnp.exp2(t - m_new) for t in parts]
        l_ref[rows] = alpha * l_ref[rows] + jnp.sum(functools.reduce(jnp.add, ps), axis=-1, keepdims=True)
        m_ref[rows] = m_new
        p_groups.append(jnp.concatenate([t.astype(BF16) for t in ps], axis=1))
        alphas.append(alpha)
    return jnp.concatenate(p_groups, axis=0), jnp.concatenate(alphas, axis=0)


def _accumulate(p, alpha, v, acc_ref):
    w = acc_ref.shape[-1] // BLK
    a = alpha if w == 1 else jnp.concatenate([alpha] * w, axis=1)
    acc_ref[...] = a * acc_ref[...] + _dot(p, v)


def _softmax_piece(s, v, m_ref, l_ref, acc_ref, adjust=None):
    p, alpha = _softmax_stats(s, m_ref, l_ref, adjust)
    _accumulate(p, alpha, v, acc_ref)


def _far_pieces_pipelined(n_far, width, chains):
    npieces = (n_far + width - 1) // width

    def window(p):
        hi = n_far - p * width
        return jnp.maximum(hi - width, 0), hi

    def scores(p):
        return [c[0](window(p)[0]) for c in chains]

    def stats(p, ss):
        start, hi = window(p)
        for (_, adjust_fn, _, m_ref, l_ref, _, p_ref, a_ref), s in zip(chains, ss):
            p_ref[...], a_ref[...] = _softmax_stats(s, m_ref, l_ref, adjust_fn(start, hi))

    def flush(p):
        for _, _, values_fn, _, _, acc_ref, p_ref, a_ref in chains:
            _accumulate(p_ref[...], a_ref[...], values_fn(window(p)[0]), acc_ref)

    @pl.when(npieces > 0)
    def _():
        stats(0, scores(0))

        def body(p, carry):
            ss = scores(p)
            flush(p - 1)
            stats(p, ss)
            return carry

        lax.fori_loop(1, npieces, body, 0)
        flush(npieces - 1)


def _for_far_pieces(n_far, width, piece_fn):
    def body(p, carry):
        hi = n_far - p * width
        piece_fn(jnp.maximum(hi - width, 0), hi)
        return carry

    lax.fori_loop(0, (n_far + width - 1) // width, body, 0)


def _new_key_mask(start, hi, width):
    col = _lane_iota((1, width * QB))
    return jnp.where(col < (hi - start) * QB, 0.0, -jnp.inf)


def _in_proj_kernel(h_ref, g_ref, w_ref, o_ref, *, tn):
    x = h_ref[...]
    ms = jnp.mean(x * x, axis=-1, keepdims=True)
    u = (x * lax.rsqrt(ms + EPS) * g_ref[...]).astype(BF16)
    for c in range(o_ref.shape[1] // tn):
        cols = slice(c * tn, (c + 1) * tn)
        o_ref[:, cols] = _dot(u, w_ref[:, cols]).astype(o_ref.dtype)


def _in_proj(h, gain, w):
    m, d = h.shape
    n = w.shape[1]
    tm = _pick_rows(m, 512)
    return pl.pallas_call(
        functools.partial(_in_proj_kernel, tn=512),
        out_shape=jax.ShapeDtypeStruct((m, n), BF16),
        grid=(m // tm,),
        in_specs=[pl.BlockSpec((tm, d), lambda i: (i, 0)),
                  pl.BlockSpec((1, d), lambda i: (0, 0)),
                  pl.BlockSpec((d, n), lambda i: (0, 0), pipeline_mode=pl.Buffered(1))],
        out_specs=pl.BlockSpec((tm, n), lambda i: (i, 0)),
        compiler_params=_params("parallel"),
        name="in_proj",
    )(h, gain.reshape(1, d), w)


def _sb_kernel(q_ref, k_ref, v_ref, u_ref, o_ref, tot_ref, acc_ref):
    i = pl.program_id(1)
    nh = SB_HEADS
    head_of_lane = _lane_iota((QB, nh * HEAD_DIM)) // HEAD_DIM
    q = q_ref[0].astype(F32) * QK_SCALE
    qs = jnp.concatenate([jnp.where(head_of_lane == h, q, 0.0) for h in range(nh)], axis=0).astype(BF16)
    tot_ref[...] = jnp.zeros_like(tot_ref)
    acc_ref[...] = jnp.zeros_like(acc_ref)

    def piece(start, n_blocks, hi=None):
        r = pl.multiple_of(start * QB, QB)
        z_all = _nt_dot(qs, k_ref[0, pl.ds(r, n_blocks * QB), :])
        if hi is not None:
            z_all = z_all + _new_key_mask(start, hi, n_blocks)
        run = tot_ref[...]
        n_sub = n_blocks * QB // BLK
        ws = [None] * n_sub
        for c in reversed(range(n_sub)):
            z = z_all[:, c * BLK:(c + 1) * BLK]
            sp = jnp.maximum(z, 0.0) + jnp.log2(1.0 + jnp.exp2(-jnp.abs(z)))
            l1m = -sp
            if hi is None:
                mask = (_lane_iota(z.shape) + c * BLK) < (_row_iota(z.shape) & (QB - 1))
                l1m = jnp.where(mask, l1m, 0.0)
            l1m_hi = l1m.astype(BF16)
            l1m_lo = (l1m - l1m_hi.astype(F32)).astype(BF16)
            rs = _dot(jnp.concatenate([l1m_hi, l1m_lo], axis=1), u_ref[...])
            w = jnp.exp2((z - sp) + rs[:, :BLK] + run)
            if hi is None:
                w = jnp.where(mask, w, 0.0)
            ws[c] = w.astype(BF16)
            run = run + rs[:, BLK:]
        tot_ref[...] = run
        acc_ref[...] += _dot(jnp.concatenate(ws, axis=1), v_ref[0, pl.ds(r, n_blocks * QB), :])

    piece(i, 1)
    _for_far_pieces(i, SB_FAR, lambda start, hi: piece(start, SB_FAR, hi))

    a = acc_ref[...]
    out = a[:QB]
    for h in range(1, nh):
        out = jnp.where(head_of_lane == h, a[h * QB:(h + 1) * QB], out)
    o_ref[0] = out.astype(o_ref.dtype)


def _sb_attn(z3, u_mat):
    b, tp, _ = z3.shape
    nq = tp // QB
    assert nq >= SB_FAR
    w = SB_HEADS * HEAD_DIM
    return pl.pallas_call(
        _sb_kernel,
        out_shape=jax.ShapeDtypeStruct((b, tp, w), BF16),
        grid=(b, nq),
        in_specs=[pl.BlockSpec((1, QB, w), lambda bb, i: (bb, i, Q_SB // w)),
                  pl.BlockSpec((1, tp, w), lambda bb, i: (bb, 0, K_SB // w)),
                  pl.BlockSpec((1, tp, w), lambda bb, i: (bb, 0, V_SB // w)),
                  pl.BlockSpec((2 * BLK, 2 * BLK), lambda bb, i: (0, 0))],
        out_specs=pl.BlockSpec((1, QB, w), lambda bb, i: (bb, i, 0)),
        scratch_shapes=[pltpu.VMEM((SB_HEADS * QB, BLK), F32),
                        pltpu.VMEM((SB_HEADS * QB, w), F32)],
        compiler_params=_params("parallel", "arbitrary"),
        name="sb_attn",
    )(z3, z3, z3, u_mat)


def _df_kernel(tab_ref, q_ref, k_ref, v_ref, qg_ref, kg_ref, bkt_ref, lamv_ref, sub_ref, o_ref,
               kn_ref, bias_ref, m_ref, l_ref, acc_ref, p_ref, a_ref, *, nq, lam_init, head_off):
    g = pl.program_id(1)
    i = pl.program_id(2)
    lo = _lane_iota((1, BLK)) < HEAD_DIM
    lanes = [slice(c * BLK, (c + 1) * BLK) for c in range(DF_GROUP)]

    @pl.when(i == 0)
    def _prep():
        def kbody(j, carry):
            r = pl.multiple_of(j * QB, QB)
            for c in range(DF_GROUP):
                kf = k_ref[0, pl.ds(r, QB), lanes[c]].astype(F32)
                kn_ref[pl.ds(r, QB), lanes[c]] = _head_rmsnorm128(kf, kg_ref[...]).astype(BF16)
            return carry

        lax.fori_loop(0, nq, kbody, 0)
        for c in range(DF_GROUP):
            bias_ref[c] = _bias_tile(tab_ref, bkt_ref[...], head_off + g * DF_GROUP + c)

    m_ref[...] = jnp.full(m_ref.shape, M_INIT, F32)
    l_ref[...] = jnp.zeros_like(l_ref)
    acc_ref[...] = jnp.zeros_like(acc_ref)

    def chain(c):
        qn = _head_rmsnorm128(q_ref[0, :, lanes[c]].astype(F32), qg_ref[...]) * QK_SCALE
        qs = jnp.concatenate([jnp.where(lo, qn, 0.0), jnp.where(lo, 0.0, qn)], axis=0).astype(BF16)

        def scores(first, n):
            return _nt_dot(qs, kn_ref[pl.ds(pl.multiple_of(first * QB, QB), n * QB), lanes[c]])

        def values(first, n):
            return v_ref[0, pl.ds(pl.multiple_of(first * QB, QB), n * QB), lanes[c]]

        def far_adjust(start, hi):
            new_keys = _new_key_mask(start, hi, DF_FAR)
            return lambda t, rows: t + new_keys

        def near_adjust(n):
            def adjust(t, rows):
                q_rows = slice(rows.start % QB, rows.start % QB + t.shape[0])
                t = t + bias_ref[c, q_rows, (2 - n) * QB:]
                causal = (_lane_iota(t.shape) - (n - 1) * QB) <= (_row_iota(t.shape) + q_rows.start)
                return jnp.where(causal, t, -jnp.inf)

            return adjust

        return dict(scores=scores, values=values, near_adjust=near_adjust,
                    far=(lambda start: scores(start, DF_FAR), far_adjust, lambda start: values(start, DF_FAR),
                         m_ref.at[c], l_ref.at[c], acc_ref.at[c], p_ref.at[c], a_ref.at[c]))

    chains = [chain(c) for c in range(DF_GROUP)]
    _far_pieces_pipelined(jnp.maximum(i - 1, 0), DF_FAR, [ch["far"] for ch in chains])

    def near_piece(first, n):
        ss = [ch["scores"](first, n) for ch in chains]
        for c, (ch, s) in enumerate(zip(chains, ss)):
            _softmax_piece(s, ch["values"](first, n), m_ref.at[c], l_ref.at[c], acc_ref.at[c],
                           adjust=ch["near_adjust"](n))

    @pl.when(i >= 1)
    def _():
        near_piece(i - 1, 2)

    @pl.when(i == 0)
    def _():
        near_piece(0, 1)

    lv = lamv_ref[...]
    lam = (jnp.exp(jnp.sum(lv[0:1] * lv[1:2], axis=-1, keepdims=True))
           - jnp.exp(jnp.sum(lv[2:3] * lv[3:4], axis=-1, keepdims=True)) + lam_init)
    for c in range(DF_GROUP):
        a = acc_ref[c] / l_ref[c]
        y = a[:QB] - lam * a[QB:]
        y = y * lax.rsqrt(jnp.mean(y * y, axis=-1, keepdims=True) + EPS) * sub_ref[...]
        o_ref[0, :, lanes[c]] = (y * (1.0 - lam_init)).astype(o_ref.dtype)


def _df_attn(z3, rel_bias, qg, kg, bkt, lamv, subln, lam_init):
    b, tp, _ = z3.shape
    nq = tp // QB
    assert nq >= DF_FAR
    kern = functools.partial(_df_kernel, nq=nq, lam_init=lam_init, head_off=SP_HEADS)
    vec = lambda bb, g, i: (0, 0)
    w = DF_GROUP * BLK
    state = lambda cols, dt: pltpu.VMEM((DF_GROUP, 2 * QB, cols), dt)
    return pl.pallas_call(
        kern,
        out_shape=jax.ShapeDtypeStruct((b, tp, DF_HEADS * BLK), BF16),
        grid=(b, DF_HEADS // DF_GROUP, nq),
        in_specs=[pl.BlockSpec(memory_space=pltpu.SMEM),
                  pl.BlockSpec((1, QB, w), lambda bb, g, i: (bb, i, Q_DF // w + g)),
                  pl.BlockSpec((1, tp, w), lambda bb, g, i: (bb, 0, K_DF // w + g)),
                  pl.BlockSpec((1, tp, w), lambda bb, g, i: (bb, 0, V_DF // w + g)),
                  pl.BlockSpec((1, BLK), vec),
                  pl.BlockSpec((1, BLK), vec),
                  pl.BlockSpec((QB, 2 * QB), vec),
                  pl.BlockSpec((8, BLK), vec),
                  pl.BlockSpec((1, BLK), vec)],
        out_specs=pl.BlockSpec((1, QB, w), lambda bb, g, i: (bb, i, g)),
        scratch_shapes=[pltpu.VMEM((tp, w), BF16),
                        pltpu.VMEM((DF_GROUP, QB, 2 * QB), F32),
                        state(BLK, F32), state(BLK, F32), state(BLK, F32),
                        state(DF_FAR * QB, BF16),
                        state(BLK, F32)],
        compiler_params=_params("parallel", "parallel", "arbitrary"),
        name="df_attn",
    )(rel_bias, z3, z3, z3, qg, kg, bkt, lamv, subln)


def _softmax_piece_t(st, vt, m_ref, l_ref, acc_ref):
    m_old = m_ref[0:1]
    m_new = jnp.maximum(m_old, jnp.max(st, axis=0, keepdims=True))
    alpha = jnp.exp2(m_old - m_new)
    p = jnp.exp2(st - m_new)
    l_ref[0:1] = alpha * l_ref[0:1] + jnp.sum(p, axis=0, keepdims=True)
    m_ref[0:1] = m_new
    acc_ref[...] = alpha * acc_ref[...] + _dot(vt, p.astype(BF16))


def _bounded_piece_t(st, vt, bound, l_ref, acc_ref):
    p = jnp.exp2(st - bound)
    l_ref[...] += jnp.sum(p.reshape(p.shape[0] // 8, 8, p.shape[1]), axis=0)
    acc_ref[...] += _dot(vt, p.astype(BF16))


def _dft_kernel(tab_ref, q_ref, k_ref, v_ref, qg_ref, kg_ref, bkt_ref, lamv_ref, sub_ref, o_ref,
                kn_ref, vt_ref, bias_ref, kmax_ref, m_ref, l_ref, lsum_ref, acc_ref, *, nq, lam_init, head_off):
    h = pl.program_id(1)
    i = pl.program_id(2)
    lo = _lane_iota((1, BLK)) < HEAD_DIM

    @pl.when(i == 0)
    def _prep():
        kmax_ref[...] = jnp.zeros_like(kmax_ref)

        def kbody(j, carry):
            r = pl.multiple_of(j * QB, QB)
            kf = k_ref[0, pl.ds(r, QB), :].astype(F32)
            kn = _head_rmsnorm128(kf, kg_ref[...]).astype(BF16)
            kn_ref[pl.ds(r, QB), :] = kn
            ksq = kn.astype(F32) ** 2
            half_norms = jnp.maximum(jnp.sum(jnp.where(lo, ksq, 0.0), axis=-1, keepdims=True),
                                     jnp.sum(jnp.where(lo, 0.0, ksq), axis=-1, keepdims=True))
            kmax_ref[...] = jnp.maximum(kmax_ref[...], jnp.max(half_norms))
            vt_ref[j] = v_ref[0, pl.ds(r, QB), :].astype(F32).T.astype(BF16)
            return carry

        lax.fori_loop(0, nq, kbody, 0)
        bias_ref[...] = _bias_tile(tab_ref, bkt_ref[...], head_off + h)

    qn = _head_rmsnorm128(q_ref[0].astype(F32), qg_ref[...]) * QK_SCALE
    qs = jnp.concatenate([jnp.where(lo, qn, 0.0), jnp.where(lo, 0.0, qn)], axis=0).astype(BF16)

    q_sq = _nt_dot(jnp.ones((8, BLK), BF16), (qs.astype(F32) ** 2).astype(BF16))[0:1]
    far_b = tab_ref[N_BUCKETS - 1, head_off + h]
    bias_max = jnp.float32(0.0)
    for b in range(N_BUCKETS - 1):
        bias_max = jnp.maximum(bias_max, (tab_ref[b, head_off + h] - far_b) * LOG2E)
    bound = jnp.sqrt(q_sq * kmax_ref[0:1, 0:1]) * BOUND_SLACK + (bias_max + BOUND_SLACK)

    def scores(first, n):
        return _nt_dot(kn_ref[pl.ds(pl.multiple_of(first * QB, QB), n * QB), :], qs)

    def values(first, n):
        return jnp.concatenate([vt_ref[first + c] for c in range(n)], axis=1)

    def attend(step):
        def far_piece(start, hi):
            st = scores(start, DF_FAR)
            new_keys = _row_iota(st.shape) < (hi - start) * QB
            step(jnp.where(new_keys, st, -jnp.inf), values(start, DF_FAR))

        def near_piece(first, n):
            b = bias_ref[(2 - n) * QB:, :]
            st = scores(first, n) + jnp.concatenate([b, b], axis=1)
            causal = (_row_iota(st.shape) - (n - 1) * QB) <= (_lane_iota(st.shape) & (QB - 1))
            step(jnp.where(causal, st, -jnp.inf), values(first, n))

        _for_far_pieces(jnp.maximum(i - 1, 0), DF_FAR, far_piece)

        @pl.when(i >= 1)
        def _():
            near_piece(i - 1, 2)

        @pl.when(i == 0)
        def _():
            near_piece(0, 1)

    l_ref[...] = jnp.zeros_like(l_ref)
    acc_ref[...] = jnp.zeros_like(acc_ref)
    attend(lambda st, vt: _bounded_piece_t(st, vt, bound, l_ref, acc_ref))
    lsum_ref[0:1] = jnp.sum(l_ref[...], axis=0, keepdims=True)

    @pl.when(jnp.min(lsum_ref[0:1]) < UNDERFLOW_GUARD)
    def _():
        m_ref[...] = jnp.full(m_ref.shape, M_INIT, F32)
        l_ref[...] = jnp.zeros_like(l_ref)
        acc_ref[...] = jnp.zeros_like(acc_ref)
        attend(lambda st, vt: _softmax_piece_t(st, vt, m_ref, l_ref, acc_ref))
        lsum_ref[0:1] = l_ref[0:1]

    a = acc_ref[...] / lsum_ref[0:1]
    lv = lamv_ref[...]
    lam = (jnp.exp(jnp.sum(lv[0:1] * lv[1:2], axis=-1, keepdims=True))
           - jnp.exp(jnp.sum(lv[2:3] * lv[3:4], axis=-1, keepdims=True)) + lam_init)
    y = (a[:, :QB] - lam * a[:, QB:]).T
    y = y * lax.rsqrt(jnp.mean(y * y, axis=-1, keepdims=True) + EPS) * sub_ref[...]
    o_ref[0] = (y * (1.0 - lam_init)).astype(o_ref.dtype)


def _dft_attn(z3, rel_bias, qg, kg, bkt_t, lamv, subln, lam_init):
    b, tp, _ = z3.shape
    nq = tp // QB
    assert nq >= DF_FAR
    kern = functools.partial(_dft_kernel, nq=nq, lam_init=lam_init, head_off=SP_HEADS)
    vec = lambda bb, h, i: (0, 0)
    return pl.pallas_call(
        kern,
        out_shape=jax.ShapeDtypeStruct((b, tp, DF_HEADS * BLK), BF16),
        grid=(b, DF_HEADS, nq),
        in_specs=[pl.BlockSpec(memory_space=pltpu.SMEM),
                  pl.BlockSpec((1, QB, BLK), lambda bb, h, i: (bb, i, Q_DF // BLK + h)),
                  pl.BlockSpec((1, tp, BLK), lambda bb, h, i: (bb, 0, K_DF // BLK + h)),
                  pl.BlockSpec((1, tp, BLK), lambda bb, h, i: (bb, 0, V_DF // BLK + h)),
                  pl.BlockSpec((1, BLK), vec),
                  pl.BlockSpec((1, BLK), vec),
                  pl.BlockSpec((2 * QB, QB), vec),
                  pl.BlockSpec((8, BLK), vec),
                  pl.BlockSpec((1, BLK), vec)],
        out_specs=pl.BlockSpec((1, QB, BLK), lambda bb, h, i: (bb, i, h)),
        scratch_shapes=[pltpu.VMEM((tp, BLK), BF16),
                        pltpu.VMEM((nq, BLK, QB), BF16),
                        pltpu.VMEM((2 * QB, QB), F32),
                        pltpu.VMEM((8, BLK), F32),
                        pltpu.VMEM((8, 2 * QB), F32),
                        pltpu.VMEM((8, 2 * QB), F32),
                        pltpu.VMEM((8, 2 * QB), F32),
                        pltpu.VMEM((BLK, 2 * QB), F32)],
        compiler_params=_params("parallel", "parallel", "arbitrary"),
        name="df_attn",
    )(rel_bias, z3, z3, z3, qg, kg, bkt_t, lamv, subln)


def _sp_kernel(tab_ref, q_ref, k_ref, v_ref, qix_ref, kwq_ref, kwk_ref, qg_ref, kg_ref, bkt_ref,
               rep_ref, bcast_ref, before_ref, o_ref,
               kn_ref, kx_ref, bias_ref, keys_ref, dig_ref, thr_ref, wb_ref, m_ref, l_ref, acc_ref, p_ref, a_ref,
               *, nq, top_k):
    i = pl.program_id(1)
    nh = SP_HEADS
    w = nh * HEAD_DIM
    lane = _lane_iota((QB, w))
    head_of_lane = lane // HEAD_DIM

    def norm256(xf, g):
        return jnp.concatenate([_head_rmsnorm128(xf[:, :BLK], g[:, :BLK]),
                                _head_rmsnorm128(xf[:, BLK:], g[:, BLK:])], axis=1)

    def dup(x):
        return jnp.concatenate([x, x], axis=1)

    @pl.when(i == 0)
    def _prep():
        def kbody(c, carry):
            r = pl.multiple_of(c * QB, QB)
            kn_ref[pl.ds(r, QB), :] = norm256(k_ref[0, pl.ds(r, QB), :].astype(F32), kg_ref[...]).astype(BF16)
            kx_ref[pl.ds(r, QB), :] = _dot(kwk_ref[0, pl.ds(r, QB), :], rep_ref[...]).astype(BF16)
            return carry

        lax.fori_loop(0, nq, kbody, 0)
        for h in range(nh):
            bias_ref[h] = _bias_tile(tab_ref, bkt_ref[...], h)

    key_causal = _row_iota((QB, QB)) <= _lane_iota((QB, QB))

    wb_ref[...] = _dot(kwq_ref[0], bcast_ref[...])
    qix = qix_ref[0].astype(F32)
    ix_head = lane // IDX_DIM
    qx = jnp.concatenate([jnp.where(ix_head == h, qix, 0.0) for h in range(IDX_HEADS)], axis=0).astype(BF16)

    def score_tile(j, diag):
        r = pl.multiple_of(j * QB, QB)
        d = _nt_dot(qx, kx_ref[pl.ds(r, QB), :])
        sc = jnp.zeros((QB, QB), F32)
        for h in range(IDX_HEADS):
            sc = sc + dup(wb_ref[:, h * BLK:(h + 1) * BLK]) * jnp.maximum(d[h * QB:(h + 1) * QB], 0.0)
        sc = jnp.where(sc == 0.0, 0.0, sc).T
        bits = lax.bitcast_convert_type(sc, jnp.int32)
        key = jnp.where(bits < 0, bits ^ jnp.int32(0x7FFFFFFF), bits)
        if diag:
            key = jnp.where(key_causal, key, jnp.int32(INT_MIN))
        keys_ref[j] = key
        dig_ref[j] = jnp.right_shift(key, 16).astype(jnp.int16)

    def score_body(j, c):
        score_tile(j, False)
        return c

    lax.fori_loop(0, i, score_body, 0)
    score_tile(i, True)

    nblk = i + 1

    def count(pred):
        def cbody(j, c):
            hit = jnp.where(pred(keys_ref[j]), 1.0, 0.0)
            return c + jnp.sum(hit.reshape(QB // COUNT_ROWS, COUNT_ROWS, QB), axis=0)

        c = lax.fori_loop(0, nblk, cbody, jnp.zeros((COUNT_ROWS, QB), F32))
        return jnp.sum(c, axis=0, keepdims=True)

    def count16(pred):
        def cbody(j, c):
            hit = jnp.where(pred(dig_ref[j]), jnp.int16(1), jnp.int16(0)).reshape(QB // DIGIT_ROWS, DIGIT_ROWS, QB)
            return c + functools.reduce(jnp.add, [hit[r] for r in range(QB // DIGIT_ROWS)])

        c = lax.fori_loop(0, nblk, cbody, jnp.zeros((DIGIT_ROWS, QB), jnp.int16))
        return jnp.sum(c.astype(F32), axis=0, keepdims=True)

    n_all = jnp.zeros((1, QB), F32) + (nblk * QB).astype(F32)

    def bisect16(need):
        def bit_body(b, carry):
            cur, cnt_cur = carry
            cand = cur + jnp.left_shift(jnp.int32(1), 15 - b)
            cand16 = cand.astype(jnp.int16)
            cnt = count16(lambda d: d >= cand16)
            ok = cnt >= need
            return jnp.where(ok, cand, cur), jnp.where(ok, cnt, cnt_cur)

        return lax.fori_loop(0, 16, bit_body, (jnp.full((1, QB), I16_MIN, jnp.int32), n_all))

    t_hi, c_ge_hi = bisect16(float(top_k))
    t_hi16 = t_hi.astype(jnp.int16)
    c_gt_hi = count16(lambda d: d > t_hi16)
    base = jnp.left_shift(t_hi, 16)

    def low_digits(j, c):
        y = keys_ref[j] - base
        dig_ref[j] = jnp.where(jnp.right_shift(y, 16) == 0, y + I16_MIN, I16_MIN).astype(jnp.int16)
        return c

    lax.fori_loop(0, nblk, low_digits, 0)
    t_lo, c_ge_lo = bisect16(float(top_k) - c_gt_hi)
    thr = base + (t_lo - I16_MIN)
    cge = c_gt_hi + jnp.where(t_lo > I16_MIN, c_ge_lo, c_ge_hi - c_gt_hi)
    thr_ref[...] = jnp.broadcast_to(thr, thr_ref.shape)

    tie = jnp.where((cge > float(top_k)) & (thr > INT_MIN), 1, 0)

    @pl.when(jnp.max(tie) > 0)
    def _ties():
        need = float(top_k) - count(lambda k: k > thr)

        def tbody(j, run):
            kj = keys_ref[j]
            eq = kj == thr
            eqf = jnp.where(eq, 1.0, 0.0)
            rank = run + _dot(before_ref[...], eqf.astype(BF16))
            keep = jnp.where(kj > thr, 1, jnp.where(eq & (rank < need), 1, -1))
            keys_ref[j] = keep.astype(jnp.int32)
            return run + jnp.sum(eqf, axis=0, keepdims=True)

        lax.fori_loop(0, nblk, tbody, jnp.zeros((1, QB), F32))
        thr_ref[...] = jnp.zeros_like(thr_ref)

    qn = norm256(q_ref[0].astype(F32), qg_ref[...]) * QK_SCALE
    qs = jnp.concatenate([jnp.where(head_of_lane == h, qn, 0.0) for h in range(nh)], axis=0).astype(BF16)
    m_ref[...] = jnp.full(m_ref.shape, M_INIT, F32)
    l_ref[...] = jnp.zeros_like(l_ref)
    acc_ref[...] = jnp.zeros_like(acc_ref)

    thr_sel = thr_ref[0:1, :]

    def as_mask(sel):
        return lax.bitcast_convert_type(jnp.where(sel, 0.0, -jnp.inf).T, jnp.int32)

    def mask_body(j, c):
        keys_ref[j] = as_mask(keys_ref[j] >= thr_sel)
        return c

    lax.fori_loop(0, i, mask_body, 0)
    keys_ref[i] = as_mask((keys_ref[i] >= thr_sel) & key_causal)

    def selection(first, n, q_rows):
        return jnp.concatenate([lax.bitcast_convert_type(keys_ref[first + c, q_rows, :], F32) for c in range(n)],
                               axis=1)

    def far_scores(start):
        return _nt_dot(qs, kn_ref[pl.ds(pl.multiple_of(start * QB, QB), SP_FAR * QB), :])

    def far_adjust(start, hi):
        new_keys = _new_key_mask(start, hi, SP_FAR)

        def adjust(t, rows):
            q_rows = slice(rows.start % QB, rows.start % QB + t.shape[0])
            return t + (selection(start, SP_FAR, q_rows) + new_keys)

        return adjust

    def far_values(start):
        return v_ref[0, pl.ds(pl.multiple_of(start * QB, QB), SP_FAR * QB), :]

    def near_piece(first, n):
        r = pl.multiple_of(first * QB, QB)
        s = _nt_dot(qs, kn_ref[pl.ds(r, n * QB), :])

        def adjust(t, rows):
            q_rows = slice(rows.start % QB, rows.start % QB + t.shape[0])
            return t + (selection(first, n, q_rows) + bias_ref[rows.start // QB, q_rows, (2 - n) * QB:])

        _softmax_piece(s, v_ref[0, pl.ds(r, n * QB), :], m_ref, l_ref, acc_ref, adjust=adjust)

    _far_pieces_pipelined(jnp.maximum(i - 1, 0), SP_FAR,
                          [(far_scores, far_adjust, far_values, m_ref, l_ref, acc_ref, p_ref, a_ref)])

    @pl.when(i >= 1)
    def _():
        near_piece(i - 1, 2)

    @pl.when(i == 0)
    def _():
        near_piece(0, 1)

    a = acc_ref[...] * dup(1.0 / l_ref[...])
    out = a[:QB]
    for h in range(1, nh):
        out = jnp.where(head_of_lane == h, a[h * QB:(h + 1) * QB], out)
    o_ref[0] = out.astype(o_ref.dtype)


def _sp_attn(z3, rel_bias, qg, kg, bkt, rep, bcast, before, top_k):
    b, tp, _ = z3.shape
    nq = tp // QB
    assert nq >= SP_FAR
    w = SP_HEADS * HEAD_DIM
    kern = functools.partial(_sp_kernel, nq=nq, top_k=top_k)
    c2 = lambda bb, i: (0, 0)
    return pl.pallas_call(
        kern,
        out_shape=jax.ShapeDtypeStruct((b, tp, w), BF16),
        grid=(b, nq),
        in_specs=[pl.BlockSpec(memory_space=pltpu.SMEM),
                  pl.BlockSpec((1, QB, w), lambda bb, i: (bb, i, Q_SP // w)),
                  pl.BlockSpec((1, tp, w), lambda bb, i: (bb, 0, K_SP // w)),
                  pl.BlockSpec((1, tp, w), lambda bb, i: (bb, 0, V_SP // w)),
                  pl.BlockSpec((1, QB, w), lambda bb, i: (bb, i, Q_IX // w)),
                  pl.BlockSpec((1, QB, w), lambda bb, i: (bb, i, KW_IX // w)),
                  pl.BlockSpec((1, tp, w), lambda bb, i: (bb, 0, KW_IX // w)),
                  pl.BlockSpec((1, w), c2),
                  pl.BlockSpec((1, w), c2),
                  pl.BlockSpec((QB, 2 * QB), c2),
                  pl.BlockSpec((w, w), c2),
                  pl.BlockSpec((w, IDX_HEADS * BLK), c2),
                  pl.BlockSpec((QB, QB), c2)],
        out_specs=pl.BlockSpec((1, QB, w), lambda bb, i: (bb, i, 0)),
        scratch_shapes=[pltpu.VMEM((tp, w), BF16),
                        pltpu.VMEM((tp, w), BF16),
                        pltpu.VMEM((SP_HEADS, QB, 2 * QB), F32),
                        pltpu.VMEM((nq, QB, QB), jnp.int32),
                        pltpu.VMEM((nq, QB, QB), jnp.int16),
                        pltpu.VMEM((8, QB), jnp.int32),
                        pltpu.VMEM((QB, IDX_HEADS * BLK), F32),
                        pltpu.VMEM((SP_HEADS * QB, BLK), F32),
                        pltpu.VMEM((SP_HEADS * QB, BLK), F32),
                        pltpu.VMEM((SP_HEADS * QB, w), F32),
                        pltpu.VMEM((SP_HEADS * QB, SP_FAR * QB), BF16),
                        pltpu.VMEM((SP_HEADS * QB, BLK), F32)],
        compiler_params=_params("parallel", "arbitrary"),
        name="sp_attn",
    )(rel_bias, z3, z3, z3, z3, z3, z3, qg, kg, bkt, rep, bcast, before)


def _mix_kernel(h_ref, gsb_ref, gsp_ref, gdf_ref, bg_ref, ysb_ref, ysp_ref, ydf_ref,
                wsb_ref, wsp_ref, wdf_ref, wo_ref, o_ref):
    def branch(g_ref, k, y_ref, w_ref):
        gate = jax.nn.sigmoid(g_ref[...].astype(F32) + bg_ref[:, k * D_MODEL:(k + 1) * D_MODEL])
        return gate * _dot(y_ref[...], w_ref[...])

    merged = (branch(gsb_ref, 0, ysb_ref, wsb_ref) + branch(gsp_ref, 1, ysp_ref, wsp_ref)
              + branch(gdf_ref, 2, ydf_ref, wdf_ref))
    o_ref[...] = h_ref[...] + _dot(merged.astype(BF16), wo_ref[...])


def _mix_out(h, z, b_gate, y_sb, y_sp, y_df, w_sb, w_sp, w_df, w_o):
    m, d = h.shape
    tm = _pick_rows(m, 512)
    row = lambda i: (i, 0)
    fixed = lambda i: (0, 0)
    return pl.pallas_call(
        _mix_kernel,
        out_shape=jax.ShapeDtypeStruct((m, d), F32),
        grid=(m // tm,),
        in_specs=[pl.BlockSpec((tm, d), row),
                  pl.BlockSpec((tm, d), lambda i: (i, G_SB // D_MODEL)),
                  pl.BlockSpec((tm, d), lambda i: (i, G_SP // D_MODEL)),
                  pl.BlockSpec((tm, d), lambda i: (i, G_DF // D_MODEL)),
                  pl.BlockSpec((1, 3 * d), fixed),
                  pl.BlockSpec((tm, y_sb.shape[1]), row),
                  pl.BlockSpec((tm, y_sp.shape[1]), row),
                  pl.BlockSpec((tm, y_df.shape[1]), row),
                  pl.BlockSpec(w_sb.shape, fixed),
                  pl.BlockSpec(w_sp.shape, fixed),
                  pl.BlockSpec(w_df.shape, fixed),
                  pl.BlockSpec(w_o.shape, fixed)],
        out_specs=pl.BlockSpec((tm, d), row),
        compiler_params=_params("parallel"),
        name="mix_out",
    )(h, z, z, z, b_gate.reshape(1, 3 * d), y_sb, y_sp, y_df, w_sb, w_sp, w_df, w_o)


def _ffn_kernel(h_ref, g_ref, wu_ref, cw_ref, cb_ref, wd_ref, o_ref, gbuf_ref, carry_ref, *, tm, tp, tf):
    r = pl.program_id(0)
    x = h_ref[...]
    ms = jnp.mean(x * x, axis=-1, keepdims=True)
    u = (x * lax.rsqrt(ms + EPS) * g_ref[...]).astype(BF16)

    @pl.when(r == 0)
    def _():
        carry_ref[...] = jnp.zeros_like(carry_ref)

    seq_start = lax.rem(tp - lax.rem(r * tm, tp), tp)
    local = lax.broadcasted_iota(jnp.int32, (tm, 1), 0)
    tap1 = local != seq_start
    tap2 = tap1 & (local != seq_start + 1)
    out = x
    for f in range(D_FF // tf):
        cols = slice(f * tf, (f + 1) * tf)
        gate = _dot(u, wu_ref[:, cols])
        val = _dot(u, wu_ref[:, D_FF + f * tf:D_FF + (f + 1) * tf])
        gbuf_ref[0:8] = carry_ref[f]
        gbuf_ref[8:8 + tm] = gate
        carry_ref[f] = gate[tm - 8:tm]
        g1 = jnp.where(tap1, gbuf_ref[7:7 + tm], 0.0)
        g2 = jnp.where(tap2, gbuf_ref[6:6 + tm], 0.0)
        conv = cb_ref[:, cols] + cw_ref[0:1, cols] * g2 + cw_ref[1:2, cols] * g1 + cw_ref[2:3, cols] * gate
        act = conv * jax.nn.sigmoid(conv) * val
        out = out + _dot(act.astype(BF16), wd_ref[cols, :])
    o_ref[...] = out


def _ffn(h, gain, w_up, conv_w, conv_b, w_down, tp):
    m, d = h.shape
    tm = _pick_rows(m, 512)
    assert tm <= tp
    tf = D_FF // 2
    nf = D_FF // tf
    kern = functools.partial(_ffn_kernel, tm=tm, tp=tp, tf=tf)
    fixed = lambda r: (0, 0)
    resident = pl.Buffered(1)
    return pl.pallas_call(
        kern,
        out_shape=jax.ShapeDtypeStruct((m, d), F32),
        grid=(m // tm,),
        in_specs=[pl.BlockSpec((tm, d), lambda r: (r, 0)),
                  pl.BlockSpec((1, d), fixed),
                  pl.BlockSpec((d, 2 * D_FF), fixed, pipeline_mode=resident),
                  pl.BlockSpec((8, D_FF), fixed),
                  pl.BlockSpec((1, D_FF), fixed),
                  pl.BlockSpec((D_FF, d), fixed, pipeline_mode=resident)],
        out_specs=pl.BlockSpec((tm, d), lambda r: (r, 0)),
        scratch_shapes=[pltpu.VMEM((tm + 8, tf), F32),
                        pltpu.VMEM((nf, 8, tf), F32)],
        compiler_params=_params("arbitrary"),
        name="conv_ffn",
    )(h, gain.reshape(1, d), w_up, conv_w, conv_b.reshape(1, D_FF), w_down)


def _permute_w_in(w):
    n_attn = KW_IX - Q_SB + IDX_DIM + IDX_HEADS
    n_gate = 3 * D_MODEL
    n_df = 3 * DF_HEADS * 2 * HEAD_DIM
    gates = w[:, n_attn + n_df:]
    attn = w[:, :n_attn]
    pad = jnp.zeros((w.shape[0], Q_DF - Q_SB - n_attn), w.dtype)
    df = w[:, n_attn:n_attn + n_df]
    out = jnp.concatenate([gates, attn, pad, df], axis=1)
    assert gates.shape[1] == n_gate and out.shape[1] == NZ
    return out


def kernel(x, meta_tokens, rel_bias, attn_norm, w_in, b_gate, q_norm_sp, k_norm_sp, q_norm_df, k_norm_df, lam_q1, lam_k1, lam_q2, lam_k2, subln_df, w_br_sb, w_br_sp, w_br_df, w_out, ffn_norm, w_up, conv_w, conv_b, w_down):
    b, s, d = x.shape
    depth = w_in.shape[0]
    t = N_META + s
    tp = -(-t // QB) * QB
    top_k = min(TOPK_MAX, t // 4)
    m = b * tp

    meta = jnp.broadcast_to(meta_tokens[None].astype(x.dtype), (b, N_META, d))
    h = jnp.concatenate([meta, x, jnp.zeros((b, tp - t, d), x.dtype)], axis=1).reshape(m, d)

    bkt = jnp.asarray(_bucket_tile())
    bkt_t = jnp.asarray(np.ascontiguousarray(_bucket_tile().T))
    u_mat = jnp.asarray(_sb_prefix_matrix(), BF16)
    rep, bcast, before = (jnp.asarray(a, BF16) for a in _ix_select_matrices())
    rel_bias = rel_bias.astype(F32)

    for l in range(depth):
        lam_init = 0.8 - 0.6 * math.exp(-0.3 * l)
        z = _in_proj(h, attn_norm[l], _permute_w_in(w_in[l]).astype(BF16))
        z3 = z.reshape(b, tp, NZ)
        y_sb = _sb_attn(z3, u_mat)
        y_sp = _sp_attn(z3, rel_bias,
                        jnp.tile(q_norm_sp[l].astype(F32), SP_HEADS).reshape(1, -1),
                        jnp.tile(k_norm_sp[l].astype(F32), SP_HEADS).reshape(1, -1),
                        bkt, rep, bcast, before, top_k)
        lamv = jnp.zeros((8, BLK), F32).at[:4, :HEAD_DIM].set(
            jnp.stack([lam_q1[l], lam_k1[l], lam_q2[l], lam_k2[l]]).astype(F32))
        y_df = _dft_attn(z3, rel_bias,
                         jnp.tile(q_norm_df[l].astype(F32), 2).reshape(1, -1),
                         jnp.tile(k_norm_df[l].astype(F32), 2).reshape(1, -1),
                         bkt_t, lamv, subln_df[l].astype(F32).reshape(1, -1), lam_init)
        h = _mix_out(h, z, b_gate[l], y_sb.reshape(m, -1), y_sp.reshape(m, -1), y_df.reshape(m, -1),
                     w_br_sb[l].astype(BF16), w_br_sp[l].astype(BF16), w_br_df[l].astype(BF16),
                     w_out[l].astype(BF16))
        cw = jnp.zeros((8, D_FF), F32).at[:conv_w.shape[1]].set(conv_w[l])
        h = _ffn(h, ffn_norm[l], w_up[l].astype(BF16), cw, conv_b[l], w_down[l].astype(BF16), tp)

    return h.reshape(b, tp, d)[:, N_META:t]
```

```python
import functools
import math

import numpy as np
import jax
import jax.numpy as jnp
from jax import lax
from jax.experimental import pallas as pl
from jax.experimental.pallas import tpu as pltpu

D_MODEL = 1024
HEAD_DIM = 64
N_META = 16
BLK = 128
QB = 256
SB_HEADS = 4
SP_HEADS = 4
IDX_HEADS = 8
IDX_DIM = 32
TOPK_MAX = 256
DF_HEADS = 4
N_BUCKETS = 32
MAX_DISTANCE = 128
D_FF = 2816
EPS = 1e-6
LOG2E = math.log2(math.e)
QK_SCALE = HEAD_DIM ** -0.5 * LOG2E
M_INIT = -1e30
BOUND_SLACK = 1.02
UNDERFLOW_GUARD = 2.0 ** -100
INT_MIN = -2 ** 31
I16_MIN = -2 ** 15
COUNT_ROWS = 32
DIGIT_ROWS = 64
SCORE_VREGS = 32
SB_FAR = 2
SP_FAR = 2
DF_FAR = 4
DF_GROUP = 1

G_SB, G_SP, G_DF = 0, 1024, 2048
Q_SB, K_SB, V_SB = 3072, 3328, 3584
Q_SP, K_SP, V_SP = 3840, 4096, 4352
Q_IX, KW_IX = 4608, 4864
Q_DF, K_DF, V_DF = 5120, 5632, 6144
NZ = 6656
W_IX_LANE = IDX_DIM

VMEM_LIMIT = 56 * 1024 * 1024

F32 = jnp.float32
BF16 = jnp.bfloat16
NT_DIMS = (((1,), (1,)), ((), ()))


def _nt_dot(a, b):
    return lax.dot_general(a, b, NT_DIMS, preferred_element_type=F32)


def _dot(a, b):
    return jnp.dot(a, b, preferred_element_type=F32)


def _params(*sem):
    return pltpu.CompilerParams(dimension_semantics=sem, vmem_limit_bytes=VMEM_LIMIT)


def _pick_rows(m, cap):
    for c in (2048, 1024, 512, 256):
        if c <= cap and m % c == 0:
            return c
    raise ValueError(f"row count {m} is not a multiple of {QB}")


def _bucket_np(rel):
    n = np.maximum(rel, 0)
    max_exact = N_BUCKETS // 2
    nf = np.maximum(n, 1).astype(np.float32)
    large = max_exact + (np.log(nf / np.float32(max_exact)) / np.float32(math.log(MAX_DISTANCE / max_exact))
                         * np.float32(N_BUCKETS - max_exact)).astype(np.int32)
    return np.where(n < max_exact, n, np.minimum(large, N_BUCKETS - 1)).astype(np.int32)


def _bucket_tile():
    tq = np.arange(QB)[:, None]
    c = np.arange(2 * QB)[None, :]
    return _bucket_np(tq - c + QB)


def _sb_prefix_matrix():
    sp = np.arange(2 * BLK)[:, None] % BLK
    c = np.arange(2 * BLK)[None, :]
    return np.where(c < BLK, sp > c, True).astype(np.float32)


def _ix_select_matrices():
    c = np.arange(QB)[:, None]
    col = np.arange(QB)[None, :]
    rep = ((c < IDX_DIM) & (c == col % IDX_DIM)).astype(np.float32)
    col8 = np.arange(IDX_HEADS * BLK)[None, :]
    bcast = (c == W_IX_LANE + col8 // BLK).astype(np.float32)
    before = (col < c).astype(np.float32)
    return rep, bcast, before


def _lane_iota(shape):
    return lax.broadcasted_iota(jnp.int32, shape, len(shape) - 1)


def _row_iota(shape):
    return lax.broadcasted_iota(jnp.int32, shape, 0)


def _head_rmsnorm128(xf, gain):
    lo = _lane_iota((1, BLK)) < HEAD_DIM
    ss = xf * xf
    s_lo = jnp.sum(jnp.where(lo, ss, 0.0), axis=-1, keepdims=True)
    s_hi = jnp.sum(jnp.where(lo, 0.0, ss), axis=-1, keepdims=True)
    ms = jnp.where(lo, s_lo, s_hi) * (1.0 / HEAD_DIM)
    return xf * lax.rsqrt(ms + EPS) * gain


def _bias_tile(tab_ref, bk, head):
    far = tab_ref[N_BUCKETS - 1, head]
    acc = jnp.zeros(bk.shape, F32)
    for b in range(N_BUCKETS - 1):
        acc = jnp.where(bk == b, (tab_ref[b, head] - far) * LOG2E, acc)
    return acc


def _softmax_stats(s, m_ref, l_ref, adjust=None):
    n_rows, kc = s.shape
    rg = SCORE_VREGS * 8 * BLK // kc
    p_groups, alphas = [], []
    for g in range(n_rows // rg):
        rows = slice(g * rg, (g + 1) * rg)
        sg = s[rows] if adjust is None else adjust(s[rows], rows)
        parts = [sg[:, c * BLK:(c + 1) * BLK] for c in range(kc // BLK)]
        m_old = m_ref[rows]
        m_new = jnp.maximum(m_old, jnp.max(functools.reduce(jnp.maximum, parts), axis=-1, keepdims=True))
        alpha = jnp.exp2(m_old - m_new)
        ps = [jnp.exp2(t - m_new) for t in parts]
        l_ref[rows] = alpha * l_ref[rows] + jnp.sum(functools.reduce(jnp.add, ps), axis=-1, keepdims=True)
        m_ref[rows] = m_new
        p_groups.append(jnp.concatenate([t.astype(BF16) for t in ps], axis=1))
        alphas.append(alpha)
    return jnp.concatenate(p_groups, axis=0), jnp.concatenate(alphas, axis=0)


def _accumulate(p, alpha, v, acc_ref):
    w = acc_ref.shape[-1] // BLK
    a = alpha if w == 1 else jnp.concatenate([alpha] * w, axis=1)
    acc_ref[...] = a * acc_ref[...] + _dot(p, v)


def _softmax_piece(s, v, m_ref, l_ref, acc_ref, adjust=None):
    p, alpha = _softmax_stats(s, m_ref, l_ref, adjust)
    _accumulate(p, alpha, v, acc_ref)


def _far_pieces_pipelined(n_far, width, chains):
    npieces = (n_far + width - 1) // width

    def window(p):
        hi = n_far - p * width
        return jnp.maximum(hi - width, 0), hi

    def scores(p):
        return [c[0](window(p)[0]) for c in chains]

    def stats(p, ss):
        start, hi = window(p)
        for (_, adjust_fn, _, m_ref, l_ref, _, p_ref, a_ref), s in zip(chains, ss):
            p_ref[...], a_ref[...] = _softmax_stats(s, m_ref, l_ref, adjust_fn(start, hi))

    def flush(p):
        for _, _, values_fn, _, _, acc_ref, p_ref, a_ref in chains:
            _accumulate(p_ref[...], a_ref[...], values_fn(window(p)[0]), acc_ref)

    @pl.when(npieces > 0)
    def _():
        stats(0, scores(0))

        def body(p, carry):
            ss = scores(p)
            flush(p - 1)
            stats(p, ss)
            return carry

        lax.fori_loop(1, npieces, body, 0)
        flush(npieces - 1)


def _for_far_pieces(n_far, width, piece_fn):
    def body(p, carry):
        hi = n_far - p * width
        piece_fn(jnp.maximum(hi - width, 0), hi)
        return carry

    lax.fori_loop(0, (n_far + width - 1) // width, body, 0)


def _new_key_mask(start, hi, width):
    col = _lane_iota((1, width * QB))
    return jnp.where(col < (hi - start) * QB, 0.0, -jnp.inf)


def _in_proj_kernel(h_ref, g_ref, w_ref, o_ref, *, tn):
    x = h_ref[...]
    ms = jnp.mean(x * x, axis=-1, keepdims=True)
    u = (x * lax.rsqrt(ms + EPS) * g_ref[...]).astype(BF16)
    for c in range(o_ref.shape[1] // tn):
        cols = slice(c * tn, (c + 1) * tn)
        o_ref[:, cols] = _dot(u, w_ref[:, cols]).astype(o_ref.dtype)


def _in_proj(h, gain, w):
    m, d = h.shape
    n = w.shape[1]
    tm = _pick_rows(m, 512)
    return pl.pallas_call(
        functools.partial(_in_proj_kernel, tn=512),
        out_shape=jax.ShapeDtypeStruct((m, n), BF16),
        grid=(m // tm,),
        in_specs=[pl.BlockSpec((tm, d), lambda i: (i, 0)),
                  pl.BlockSpec((1, d), lambda i: (0, 0)),
                  pl.BlockSpec((d, n), lambda i: (0, 0), pipeline_mode=pl.Buffered(1))],
        out_specs=pl.BlockSpec((tm, n), lambda i: (i, 0)),
        compiler_params=_params("parallel"),
        name="in_proj",
    )(h, gain.reshape(1, d), w)


def _sb_kernel(q_ref, k_ref, v_ref, u_ref, o_ref, tot_ref, acc_ref):
    i = pl.program_id(1)
    nh = SB_HEADS
    head_of_lane = _lane_iota((QB, nh * HEAD_DIM)) // HEAD_DIM
    q = q_ref[0].astype(F32) * QK_SCALE
    qs = jnp.concatenate([jnp.where(head_of_lane == h, q, 0.0) for h in range(nh)], axis=0).astype(BF16)
    tot_ref[...] = jnp.zeros_like(tot_ref)
    acc_ref[...] = jnp.zeros_like(acc_ref)

    def piece(start, n_blocks, hi=None):
        r = pl.multiple_of(start * QB, QB)
        z_all = _nt_dot(qs, k_ref[0, pl.ds(r, n_blocks * QB), :])
        if hi is not None:
            z_all = z_all + _new_key_mask(start, hi, n_blocks)
        run = tot_ref[...]
        n_sub = n_blocks * QB // BLK
        ws = [None] * n_sub
        for c in reversed(range(n_sub)):
            z = z_all[:, c * BLK:(c + 1) * BLK]
            sp = jnp.maximum(z, 0.0) + jnp.log2(1.0 + jnp.exp2(-jnp.abs(z)))
            l1m = -sp
            if hi is None:
                mask = (_lane_iota(z.shape) + c * BLK) < (_row_iota(z.shape) & (QB - 1))
                l1m = jnp.where(mask, l1m, 0.0)
            l1m_hi = l1m.astype(BF16)
            l1m_lo = (l1m - l1m_hi.astype(F32)).astype(BF16)
            rs = _dot(jnp.concatenate([l1m_hi, l1m_lo], axis=1), u_ref[...])
            w = jnp.exp2((z - sp) + rs[:, :BLK] + run)
            if hi is None:
                w = jnp.where(mask, w, 0.0)
            ws[c] = w.astype(BF16)
            run = run + rs[:, BLK:]
        tot_ref[...] = run
        acc_ref[...] += _dot(jnp.concatenate(ws, axis=1), v_ref[0, pl.ds(r, n_blocks * QB), :])

    piece(i, 1)
    _for_far_pieces(i, SB_FAR, lambda start, hi: piece(start, SB_FAR, hi))

    a = acc_ref[...]
    out = a[:QB]
    for h in range(1, nh):
        out = jnp.where(head_of_lane == h, a[h * QB:(h + 1) * QB], out)
    o_ref[0] = out.astype(o_ref.dtype)


def _sb_attn(z3, u_mat):
    b, tp, _ = z3.shape
    nq = tp // QB
    assert nq >= SB_FAR
    w = SB_HEADS * HEAD_DIM
    return pl.pallas_call(
        _sb_kernel,
        out_shape=jax.ShapeDtypeStruct((b, tp, w), BF16),
        grid=(b, nq),
        in_specs=[pl.BlockSpec((1, QB, w), lambda bb, i: (bb, i, Q_SB // w)),
                  pl.BlockSpec((1, tp, w), lambda bb, i: (bb, 0, K_SB // w)),
                  pl.BlockSpec((1, tp, w), lambda bb, i: (bb, 0, V_SB // w)),
                  pl.BlockSpec((2 * BLK, 2 * BLK), lambda bb, i: (0, 0))],
        out_specs=pl.BlockSpec((1, QB, w), lambda bb, i: (bb, i, 0)),
        scratch_shapes=[pltpu.VMEM((SB_HEADS * QB, BLK), F32),
                        pltpu.VMEM((SB_HEADS * QB, w), F32)],
        compiler_params=_params("parallel", "arbitrary"),
        name="sb_attn",
    )(z3, z3, z3, u_mat)


def _df_kernel(tab_ref, q_ref, k_ref, v_ref, qg_ref, kg_ref, bkt_ref, lamv_ref, sub_ref, o_ref,
               kn_ref, bias_ref, m_ref, l_ref, acc_ref, p_ref, a_ref, *, nq, lam_init, head_off):
    g = pl.program_id(1)
    i = pl.program_id(2)
    lo = _lane_iota((1, BLK)) < HEAD_DIM
    lanes = [slice(c * BLK, (c + 1) * BLK) for c in range(DF_GROUP)]

    @pl.when(i == 0)
    def _prep():
        def kbody(j, carry):
            r = pl.multiple_of(j * QB, QB)
            for c in range(DF_GROUP):
                kf = k_ref[0, pl.ds(r, QB), lanes[c]].astype(F32)
                kn_ref[pl.ds(r, QB), lanes[c]] = _head_rmsnorm128(kf, kg_ref[...]).astype(BF16)
            return carry

        lax.fori_loop(0, nq, kbody, 0)
        for c in range(DF_GROUP):
            bias_ref[c] = _bias_tile(tab_ref, bkt_ref[...], head_off + g * DF_GROUP + c)

    m_ref[...] = jnp.full(m_ref.shape, M_INIT, F32)
    l_ref[...] = jnp.zeros_like(l_ref)
    acc_ref[...] = jnp.zeros_like(acc_ref)

    def chain(c):
        qn = _head_rmsnorm128(q_ref[0, :, lanes[c]].astype(F32), qg_ref[...]) * QK_SCALE
        qs = jnp.concatenate([jnp.where(lo, qn, 0.0), jnp.where(lo, 0.0, qn)], axis=0).astype(BF16)

        def scores(first, n):
            return _nt_dot(qs, kn_ref[pl.ds(pl.multiple_of(first * QB, QB), n * QB), lanes[c]])

        def values(first, n):
            return v_ref[0, pl.ds(pl.multiple_of(first * QB, QB), n * QB), lanes[c]]

        def far_adjust(start, hi):
            new_keys = _new_key_mask(start, hi, DF_FAR)
            return lambda t, rows: t + new_keys

        def near_adjust(n):
            def adjust(t, rows):
                q_rows = slice(rows.start % QB, rows.start % QB + t.shape[0])
                t = t + bias_ref[c, q_rows, (2 - n) * QB:]
                causal = (_lane_iota(t.shape) - (n - 1) * QB) <= (_row_iota(t.shape) + q_rows.start)
                return jnp.where(causal, t, -jnp.inf)

            return adjust

        return dict(scores=scores, values=values, near_adjust=near_adjust,
                    far=(lambda start: scores(start, DF_FAR), far_adjust, lambda start: values(start, DF_FAR),
                         m_ref.at[c], l_ref.at[c], acc_ref.at[c], p_ref.at[c], a_ref.at[c]))

    chains = [chain(c) for c in range(DF_GROUP)]
    _far_pieces_pipelined(jnp.maximum(i - 1, 0), DF_FAR, [ch["far"] for ch in chains])

    def near_piece(first, n):
        ss = [ch["scores"](first, n) for ch in chains]
        for c, (ch, s) in enumerate(zip(chains, ss)):
            _softmax_piece(s, ch["values"](first, n), m_ref.at[c], l_ref.at[c], acc_ref.at[c],
                           adjust=ch["near_adjust"](n))

    @pl.when(i >= 1)
    def _():
        near_piece(i - 1, 2)

    @pl.when(i == 0)
    def _():
        near_piece(0, 1)

    lv = lamv_ref[...]
    lam = (jnp.exp(jnp.sum(lv[0:1] * lv[1:2], axis=-1, keepdims=True))
           - jnp.exp(jnp.sum(lv[2:3] * lv[3:4], axis=-1, keepdims=True)) + lam_init)
    for c in range(DF_GROUP):
        a = acc_ref[c] / l_ref[c]
        y = a[:QB] - lam * a[QB:]
        y = y * lax.rsqrt(jnp.mean(y * y, axis=-1, keepdims=True) + EPS) * sub_ref[...]
        o_ref[0, :, lanes[c]] = (y * (1.0 - lam_init)).astype(o_ref.dtype)


def _df_attn(z3, rel_bias, qg, kg, bkt, lamv, subln, lam_init):
    b, tp, _ = z3.shape
    nq = tp // QB
    assert nq >= DF_FAR
    kern = functools.partial(_df_kernel, nq=nq, lam_init=lam_init, head_off=SP_HEADS)
    vec = lambda bb, g, i: (0, 0)
    w = DF_GROUP * BLK
    state = lambda cols, dt: pltpu.VMEM((DF_GROUP, 2 * QB, cols), dt)
    return pl.pallas_call(
        kern,
        out_shape=jax.ShapeDtypeStruct((b, tp, DF_HEADS * BLK), BF16),
        grid=(b, DF_HEADS // DF_GROUP, nq),
        in_specs=[pl.BlockSpec(memory_space=pltpu.SMEM),
                  pl.BlockSpec((1, QB, w), lambda bb, g, i: (bb, i, Q_DF // w + g)),
                  pl.BlockSpec((1, tp, w), lambda bb, g, i: (bb, 0, K_DF // w + g)),
                  pl.BlockSpec((1, tp, w), lambda bb, g, i: (bb, 0, V_DF // w + g)),
                  pl.BlockSpec((1, BLK), vec),
                  pl.BlockSpec((1, BLK), vec),
                  pl.BlockSpec((QB, 2 * QB), vec),
                  pl.BlockSpec((8, BLK), vec),
                  pl.BlockSpec((1, BLK), vec)],
        out_specs=pl.BlockSpec((1, QB, w), lambda bb, g, i: (bb, i, g)),
        scratch_shapes=[pltpu.VMEM((tp, w), BF16),
                        pltpu.VMEM((DF_GROUP, QB, 2 * QB), F32),
                        state(BLK, F32), state(BLK, F32), state(BLK, F32),
                        state(DF_FAR * QB, BF16),
                        state(BLK, F32)],
        compiler_params=_params("parallel", "parallel", "arbitrary"),
        name="df_attn",
    )(rel_bias, z3, z3, z3, qg, kg, bkt, lamv, subln)


def _softmax_piece_t(st, vt, m_ref, l_ref, acc_ref):
    m_old = m_ref[0:1]
    m_new = jnp.maximum(m_old, jnp.max(st, axis=0, keepdims=True))
    alpha = jnp.exp2(m_old - m_new)
    p = jnp.exp2(st - m_new)
    l_ref[0:1] = alpha * l_ref[0:1] + jnp.sum(p, axis=0, keepdims=True)
    m_ref[0:1] = m_new
    acc_ref[...] = alpha * acc_ref[...] + _dot(vt, p.astype(BF16))


def _bounded_piece_t(st, vt, bound, l_ref, acc_ref):
    p = jnp.exp2(st - bound)
    l_ref[...] += jnp.sum(p.reshape(p.shape[0] // 8, 8, p.shape[1]), axis=0)
    acc_ref[...] += _dot(vt, p.astype(BF16))


def _dft_kernel(tab_ref, q_ref, k_ref, v_ref, qg_ref, kg_ref, bkt_ref, lamv_ref, sub_ref, o_ref,
                kn_ref, vt_ref, bias_ref, kmax_ref, m_ref, l_ref, lsum_ref, acc_ref, *, nq, lam_init, head_off):
    h = pl.program_id(1)
    i = pl.program_id(2)
    lo = _lane_iota((1, BLK)) < HEAD_DIM

    @pl.when(i == 0)
    def _prep():
        kmax_ref[...] = jnp.zeros_like(kmax_ref)

        def kbody(j, carry):
            r = pl.multiple_of(j * QB, QB)
            kf = k_ref[0, pl.ds(r, QB), :].astype(F32)
            kn = _head_rmsnorm128(kf, kg_ref[...]).astype(BF16)
            kn_ref[pl.ds(r, QB), :] = kn
            ksq = kn.astype(F32) ** 2
            half_norms = jnp.maximum(jnp.sum(jnp.where(lo, ksq, 0.0), axis=-1, keepdims=True),
                                     jnp.sum(jnp.where(lo, 0.0, ksq), axis=-1, keepdims=True))
            kmax_ref[...] = jnp.maximum(kmax_ref[...], jnp.max(half_norms))
            vt_ref[j] = v_ref[0, pl.ds(r, QB), :].astype(F32).T.astype(BF16)
            return carry

        lax.fori_loop(0, nq, kbody, 0)
        bias_ref[...] = _bias_tile(tab_ref, bkt_ref[...], head_off + h)

    qn = _head_rmsnorm128(q_ref[0].astype(F32), qg_ref[...]) * QK_SCALE
    qs = jnp.concatenate([jnp.where(lo, qn, 0.0), jnp.where(lo, 0.0, qn)], axis=0).astype(BF16)

    q_sq = _nt_dot(jnp.ones((8, BLK), BF16), (qs.astype(F32) ** 2).astype(BF16))[0:1]
    far_b = tab_ref[N_BUCKETS - 1, head_off + h]
    bias_max = jnp.float32(0.0)
    for b in range(N_BUCKETS - 1):
        bias_max = jnp.maximum(bias_max, (tab_ref[b, head_off + h] - far_b) * LOG2E)
    bound = jnp.sqrt(q_sq * kmax_ref[0:1, 0:1]) * BOUND_SLACK + (bias_max + BOUND_SLACK)

    def scores(first, n):
        return _nt_dot(kn_ref[pl.ds(pl.multiple_of(first * QB, QB), n * QB), :], qs)

    def values(first, n):
        return jnp.concatenate([vt_ref[first + c] for c in range(n)], axis=1)

    def attend(step):
        def far_piece(start, hi):
            st = scores(start, DF_FAR)
            new_keys = _row_iota(st.shape) < (hi - start) * QB
            step(jnp.where(new_keys, st, -jnp.inf), values(start, DF_FAR))

        def near_piece(first, n):
            b = bias_ref[(2 - n) * QB:, :]
            st = scores(first, n) + jnp.concatenate([b, b], axis=1)
            causal = (_row_iota(st.shape) - (n - 1) * QB) <= (_lane_iota(st.shape) & (QB - 1))
            step(jnp.where(causal, st, -jnp.inf), values(first, n))

        _for_far_pieces(jnp.maximum(i - 1, 0), DF_FAR, far_piece)

        @pl.when(i >= 1)
        def _():
            near_piece(i - 1, 2)

        @pl.when(i == 0)
        def _():
            near_piece(0, 1)

    l_ref[...] = jnp.zeros_like(l_ref)
    acc_ref[...] = jnp.zeros_like(acc_ref)
    attend(lambda st, vt: _bounded_piece_t(st, vt, bound, l_ref, acc_ref))
    lsum_ref[0:1] = jnp.sum(l_ref[...], axis=0, keepdims=True)

    @pl.when(jnp.min(lsum_ref[0:1]) < UNDERFLOW_GUARD)
    def _():
        m_ref[...] = jnp.full(m_ref.shape, M_INIT, F32)
        l_ref[...] = jnp.zeros_like(l_ref)
        acc_ref[...] = jnp.zeros_like(acc_ref)
        attend(lambda st, vt: _softmax_piece_t(st, vt, m_ref, l_ref, acc_ref))
        lsum_ref[0:1] = l_ref[0:1]

    a = acc_ref[...] / lsum_ref[0:1]
    lv = lamv_ref[...]
    lam = (jnp.exp(jnp.sum(lv[0:1] * lv[1:2], axis=-1, keepdims=True))
           - jnp.exp(jnp.sum(lv[2:3] * lv[3:4], axis=-1, keepdims=True)) + lam_init)
    y = (a[:, :QB] - lam * a[:, QB:]).T
    y = y * lax.rsqrt(jnp.mean(y * y, axis=-1, keepdims=True) + EPS) * sub_ref[...]
    o_ref[0] = (y * (1.0 - lam_init)).astype(o_ref.dtype)


def _dft_attn(z3, rel_bias, qg, kg, bkt_t, lamv, subln, lam_init):
    b, tp, _ = z3.shape
    nq = tp // QB
    assert nq >= DF_FAR
    kern = functools.partial(_dft_kernel, nq=nq, lam_init=lam_init, head_off=SP_HEADS)
    vec = lambda bb, h, i: (0, 0)
    return pl.pallas_call(
        kern,
        out_shape=jax.ShapeDtypeStruct((b, tp, DF_HEADS * BLK), BF16),
        grid=(b, DF_HEADS, nq),
        in_specs=[pl.BlockSpec(memory_space=pltpu.SMEM),
                  pl.BlockSpec((1, QB, BLK), lambda bb, h, i: (bb, i, Q_DF // BLK + h)),
                  pl.BlockSpec((1, tp, BLK), lambda bb, h, i: (bb, 0, K_DF // BLK + h)),
                  pl.BlockSpec((1, tp, BLK), lambda bb, h, i: (bb, 0, V_DF // BLK + h)),
                  pl.BlockSpec((1, BLK), vec),
                  pl.BlockSpec((1, BLK), vec),
                  pl.BlockSpec((2 * QB, QB), vec),
                  pl.BlockSpec((8, BLK), vec),
                  pl.BlockSpec((1, BLK), vec)],
        out_specs=pl.BlockSpec((1, QB, BLK), lambda bb, h, i: (bb, i, h)),
        scratch_shapes=[pltpu.VMEM((tp, BLK), BF16),
                        pltpu.VMEM((nq, BLK, QB), BF16),
                        pltpu.VMEM((2 * QB, QB), F32),
                        pltpu.VMEM((8, BLK), F32),
                        pltpu.VMEM((8, 2 * QB), F32),
                        pltpu.VMEM((8, 2 * QB), F32),
                        pltpu.VMEM((8, 2 * QB), F32),
                        pltpu.VMEM((BLK, 2 * QB), F32)],
        compiler_params=_params("parallel", "parallel", "arbitrary"),
        name="df_attn",
    )(rel_bias, z3, z3, z3, qg, kg, bkt_t, lamv, subln)


def _sp_kernel(tab_ref, q_ref, k_ref, v_ref, qix_ref, kwq_ref, kwk_ref, qg_ref, kg_ref, bkt_ref,
               rep_ref, bcast_ref, before_ref, o_ref,
               kn_ref, kx_ref, vt_ref, bias_ref, kmax_ref, keys_ref, dig_ref, thr_ref, wb_ref,
               m_ref, l_ref, lsum_ref, acc_ref, *, nq, top_k):
    i = pl.program_id(1)
    nh = SP_HEADS
    w = nh * HEAD_DIM
    lane = _lane_iota((QB, w))
    head_of_lane = lane // HEAD_DIM

    def norm256(xf, g):
        return jnp.concatenate([_head_rmsnorm128(xf[:, :BLK], g[:, :BLK]),
                                _head_rmsnorm128(xf[:, BLK:], g[:, BLK:])], axis=1)

    def dup(x):
        return jnp.concatenate([x, x], axis=1)

    @pl.when(i == 0)
    def _prep():
        kmax_ref[...] = jnp.zeros_like(kmax_ref)

        def kbody(c, carry):
            r = pl.multiple_of(c * QB, QB)
            kn = norm256(k_ref[0, pl.ds(r, QB), :].astype(F32), kg_ref[...]).astype(BF16)
            kn_ref[pl.ds(r, QB), :] = kn
            ksq = kn.astype(F32) ** 2
            head_norms = functools.reduce(jnp.maximum, [
                jnp.sum(jnp.where(head_of_lane == h, ksq, 0.0), axis=-1, keepdims=True) for h in range(nh)])
            kmax_ref[0:1] = jnp.maximum(kmax_ref[0:1], jnp.max(head_norms))
            kx_ref[pl.ds(r, QB), :] = _dot(kwk_ref[0, pl.ds(r, QB), :], rep_ref[...]).astype(BF16)
            vt_ref[c] = v_ref[0, pl.ds(r, QB), :].astype(F32).T.astype(BF16)
            return carry

        lax.fori_loop(0, nq, kbody, 0)
        bias_max = jnp.float32(0.0)
        for h in range(nh):
            bias_ref[:, h * QB:(h + 1) * QB] = _bias_tile(tab_ref, bkt_ref[...], h)
            for b in range(N_BUCKETS - 1):
                bias_max = jnp.maximum(bias_max, (tab_ref[b, h] - tab_ref[N_BUCKETS - 1, h]) * LOG2E)
        kmax_ref[1:2] = jnp.full((1, BLK), bias_max, F32)

    key_causal = _row_iota((QB, QB)) <= _lane_iota((QB, QB))

    wb_ref[...] = _dot(kwq_ref[0], bcast_ref[...])
    qix = qix_ref[0].astype(F32)
    ix_head = lane // IDX_DIM
    qx = jnp.concatenate([jnp.where(ix_head == h, qix, 0.0) for h in range(IDX_HEADS)], axis=0).astype(BF16)

    def score_tile(j, diag):
        r = pl.multiple_of(j * QB, QB)
        d = _nt_dot(qx, kx_ref[pl.ds(r, QB), :])
        sc = jnp.zeros((QB, QB), F32)
        for h in range(IDX_HEADS):
            sc = sc + dup(wb_ref[:, h * BLK:(h + 1) * BLK]) * jnp.maximum(d[h * QB:(h + 1) * QB], 0.0)
        sc = jnp.where(sc == 0.0, 0.0, sc).T
        bits = lax.bitcast_convert_type(sc, jnp.int32)
        key = jnp.where(bits < 0, bits ^ jnp.int32(0x7FFFFFFF), bits)
        if diag:
            key = jnp.where(key_causal, key, jnp.int32(INT_MIN))
        keys_ref[j] = key
        dig_ref[j] = jnp.right_shift(key, 16).astype(jnp.int16)

    def score_body(j, c):
        score_tile(j, False)
        return c

    lax.fori_loop(0, i, score_body, 0)
    score_tile(i, True)

    nblk = i + 1

    def count(pred):
        def cbody(j, c):
            hit = jnp.where(pred(keys_ref[j]), 1.0, 0.0)
            return c + jnp.sum(hit.reshape(QB // COUNT_ROWS, COUNT_ROWS, QB), axis=0)

        c = lax.fori_loop(0, nblk, cbody, jnp.zeros((COUNT_ROWS, QB), F32))
        return jnp.sum(c, axis=0, keepdims=True)

    def count16(pred):
        def cbody(j, c):
            hit = jnp.where(pred(dig_ref[j]), jnp.int16(1), jnp.int16(0)).reshape(QB // DIGIT_ROWS, DIGIT_ROWS, QB)
            return c + functools.reduce(jnp.add, [hit[r] for r in range(QB // DIGIT_ROWS)])

        c = lax.fori_loop(0, nblk, cbody, jnp.zeros((DIGIT_ROWS, QB), jnp.int16))
        return jnp.sum(c.astype(F32), axis=0, keepdims=True)

    n_all = jnp.zeros((1, QB), F32) + (nblk * QB).astype(F32)

    def bisect16(need):
        def bit_body(b, carry):
            cur, cnt_cur = carry
            cand = cur + jnp.left_shift(jnp.int32(1), 15 - b)
            cand16 = cand.astype(jnp.int16)
            cnt = count16(lambda d: d >= cand16)
            ok = cnt >= need
            return jnp.where(ok, cand, cur), jnp.where(ok, cnt, cnt_cur)

        return lax.fori_loop(0, 16, bit_body, (jnp.full((1, QB), I16_MIN, jnp.int32), n_all))

    t_hi, c_ge_hi = bisect16(float(top_k))
    t_hi16 = t_hi.astype(jnp.int16)
    c_gt_hi = count16(lambda d: d > t_hi16)
    base = jnp.left_shift(t_hi, 16)

    def low_digits(j, c):
        y = keys_ref[j] - base
        dig_ref[j] = jnp.where(jnp.right_shift(y, 16) == 0, y + I16_MIN, I16_MIN).astype(jnp.int16)
        return c

    lax.fori_loop(0, nblk, low_digits, 0)
    t_lo, c_ge_lo = bisect16(float(top_k) - c_gt_hi)
    thr = base + (t_lo - I16_MIN)
    cge = c_gt_hi + jnp.where(t_lo > I16_MIN, c_ge_lo, c_ge_hi - c_gt_hi)
    thr_ref[...] = jnp.broadcast_to(thr, thr_ref.shape)

    tie = jnp.where((cge > float(top_k)) & (thr > INT_MIN), 1, 0)

    @pl.when(jnp.max(tie) > 0)
    def _ties():
        need = float(top_k) - count(lambda k: k > thr)

        def tbody(j, run):
            kj = keys_ref[j]
            eq = kj == thr
            eqf = jnp.where(eq, 1.0, 0.0)
            rank = run + _dot(before_ref[...], eqf.astype(BF16))
            keep = jnp.where(kj > thr, 1, jnp.where(eq & (rank < need), 1, -1))
            keys_ref[j] = keep.astype(jnp.int32)
            return run + jnp.sum(eqf, axis=0, keepdims=True)

        lax.fori_loop(0, nblk, tbody, jnp.zeros((1, QB), F32))
        thr_ref[...] = jnp.zeros_like(thr_ref)

    qn = norm256(q_ref[0].astype(F32), qg_ref[...]) * QK_SCALE
    qs = jnp.concatenate([jnp.where(head_of_lane == h, qn, 0.0) for h in range(nh)], axis=0).astype(BF16)
    q_sq = _nt_dot(jnp.ones((8, w), BF16), (qs.astype(F32) ** 2).astype(BF16))[0:1]
    bound = jnp.sqrt(q_sq * kmax_ref[0:1, 0:1]) * BOUND_SLACK + (kmax_ref[1:2, 0:1] + BOUND_SLACK)

    thr_sel = thr_ref[0:1, :]

    def as_mask(sel):
        return lax.bitcast_convert_type(jnp.where(sel, 0.0, -jnp.inf), jnp.int32)

    def mask_body(j, c):
        keys_ref[j] = as_mask(keys_ref[j] >= thr_sel)
        return c

    lax.fori_loop(0, i, mask_body, 0)
    keys_ref[i] = as_mask((keys_ref[i] >= thr_sel) & key_causal)

    def scores(first, n):
        return _nt_dot(kn_ref[pl.ds(pl.multiple_of(first * QB, QB), n * QB), :], qs)

    def values(first, n):
        return jnp.concatenate([vt_ref[first + c] for c in range(n)], axis=1)

    def selection(first, n):
        return jnp.concatenate([lax.bitcast_convert_type(keys_ref[first + c], F32) for c in range(n)],
                               axis=0)

    def attend(step):
        def far_piece(start, hi):
            m = selection(start, SP_FAR)
            m = jnp.where(_row_iota(m.shape) < (hi - start) * QB, m, -jnp.inf)
            step(scores(start, SP_FAR) + jnp.concatenate([m] * nh, axis=1), values(start, SP_FAR))

        def near_piece(first, n):
            m = selection(first, n)
            step(scores(first, n) + (jnp.concatenate([m] * nh, axis=1) + bias_ref[(2 - n) * QB:, :]),
                 values(first, n))

        _for_far_pieces(jnp.maximum(i - 1, 0), SP_FAR, far_piece)

        @pl.when(i >= 1)
        def _():
            near_piece(i - 1, 2)

        @pl.when(i == 0)
        def _():
            near_piece(0, 1)

    l_ref[...] = jnp.zeros_like(l_ref)
    acc_ref[...] = jnp.zeros_like(acc_ref)
    attend(lambda st, vt: _bounded_piece_t(st, vt, bound, l_ref, acc_ref))
    lsum_ref[0:1] = jnp.sum(l_ref[...], axis=0, keepdims=True)

    @pl.when(jnp.min(lsum_ref[0:1]) < UNDERFLOW_GUARD)
    def _():
        m_ref[...] = jnp.full(m_ref.shape, M_INIT, F32)
        l_ref[...] = jnp.zeros_like(l_ref)
        acc_ref[...] = jnp.zeros_like(acc_ref)
        attend(lambda st, vt: _softmax_piece_t(st, vt, m_ref, l_ref, acc_ref))
        lsum_ref[0:1] = l_ref[0:1]

    a = acc_ref[...] / lsum_ref[0:1]
    out_t = jnp.concatenate([a[h * HEAD_DIM:(h + 1) * HEAD_DIM, h * QB:(h + 1) * QB] for h in range(nh)], axis=0)
    o_ref[0] = out_t.T.astype(o_ref.dtype)


def _sp_attn(z3, rel_bias, qg, kg, bkt_t, rep, bcast, before, top_k):
    b, tp, _ = z3.shape
    nq = tp // QB
    assert nq >= SP_FAR
    w = SP_HEADS * HEAD_DIM
    kern = functools.partial(_sp_kernel, nq=nq, top_k=top_k)
    c2 = lambda bb, i: (0, 0)
    return pl.pallas_call(
        kern,
        out_shape=jax.ShapeDtypeStruct((b, tp, w), BF16),
        grid=(b, nq),
        in_specs=[pl.BlockSpec(memory_space=pltpu.SMEM),
                  pl.BlockSpec((1, QB, w), lambda bb, i: (bb, i, Q_SP // w)),
                  pl.BlockSpec((1, tp, w), lambda bb, i: (bb, 0, K_SP // w)),
                  pl.BlockSpec((1, tp, w), lambda bb, i: (bb, 0, V_SP // w)),
                  pl.BlockSpec((1, QB, w), lambda bb, i: (bb, i, Q_IX // w)),
                  pl.BlockSpec((1, QB, w), lambda bb, i: (bb, i, KW_IX // w)),
                  pl.BlockSpec((1, tp, w), lambda bb, i: (bb, 0, KW_IX // w)),
                  pl.BlockSpec((1, w), c2),
                  pl.BlockSpec((1, w), c2),
                  pl.BlockSpec((2 * QB, QB), c2),
                  pl.BlockSpec((w, w), c2),
                  pl.BlockSpec((w, IDX_HEADS * BLK), c2),
                  pl.BlockSpec((QB, QB), c2)],
        out_specs=pl.BlockSpec((1, QB, w), lambda bb, i: (bb, i, 0)),
        scratch_shapes=[pltpu.VMEM((tp, w), BF16),
                        pltpu.VMEM((tp, w), BF16),
                        pltpu.VMEM((nq, w, QB), BF16),
                        pltpu.VMEM((2 * QB, SP_HEADS * QB), F32),
                        pltpu.VMEM((8, BLK), F32),
                        pltpu.VMEM((nq, QB, QB), jnp.int32),
                        pltpu.VMEM((nq, QB, QB), jnp.int16),
                        pltpu.VMEM((8, QB), jnp.int32),
                        pltpu.VMEM((QB, IDX_HEADS * BLK), F32),
                        pltpu.VMEM((8, SP_HEADS * QB), F32),
                        pltpu.VMEM((8, SP_HEADS * QB), F32),
                        pltpu.VMEM((8, SP_HEADS * QB), F32),
                        pltpu.VMEM((w, SP_HEADS * QB), F32)],
        compiler_params=_params("parallel", "arbitrary"),
        name="sp_attn",
    )(rel_bias, z3, z3, z3, z3, z3, z3, qg, kg, bkt_t, rep, bcast, before)


def _mix_kernel(h_ref, gsb_ref, gsp_ref, gdf_ref, bg_ref, ysb_ref, ysp_ref, ydf_ref,
                wsb_ref, wsp_ref, wdf_ref, wo_ref, o_ref):
    def branch(g_ref, k, y_ref, w_ref):
        gate = jax.nn.sigmoid(g_ref[...].astype(F32) + bg_ref[:, k * D_MODEL:(k + 1) * D_MODEL])
        return gate * _dot(y_ref[...], w_ref[...])

    merged = (branch(gsb_ref, 0, ysb_ref, wsb_ref) + branch(gsp_ref, 1, ysp_ref, wsp_ref)
              + branch(gdf_ref, 2, ydf_ref, wdf_ref))
    o_ref[...] = h_ref[...] + _dot(merged.astype(BF16), wo_ref[...])


def _mix_out(h, z, b_gate, y_sb, y_sp, y_df, w_sb, w_sp, w_df, w_o):
    m, d = h.shape
    tm = _pick_rows(m, 512)
    row = lambda i: (i, 0)
    fixed = lambda i: (0, 0)
    return pl.pallas_call(
        _mix_kernel,
        out_shape=jax.ShapeDtypeStruct((m, d), F32),
        grid=(m // tm,),
        in_specs=[pl.BlockSpec((tm, d), row),
                  pl.BlockSpec((tm, d), lambda i: (i, G_SB // D_MODEL)),
                  pl.BlockSpec((tm, d), lambda i: (i, G_SP // D_MODEL)),
                  pl.BlockSpec((tm, d), lambda i: (i, G_DF // D_MODEL)),
                  pl.BlockSpec((1, 3 * d), fixed),
                  pl.BlockSpec((tm, y_sb.shape[1]), row),
                  pl.BlockSpec((tm, y_sp.shape[1]), row),
                  pl.BlockSpec((tm, y_df.shape[1]), row),
                  pl.BlockSpec(w_sb.shape, fixed),
                  pl.BlockSpec(w_sp.shape, fixed),
                  pl.BlockSpec(w_df.shape, fixed),
                  pl.BlockSpec(w_o.shape, fixed)],
        out_specs=pl.BlockSpec((tm, d), row),
        compiler_params=_params("parallel"),
        name="mix_out",
    )(h, z, z, z, b_gate.reshape(1, 3 * d), y_sb, y_sp, y_df, w_sb, w_sp, w_df, w_o)


def _ffn_kernel(h_ref, g_ref, wu_ref, cw_ref, cb_ref, wd_ref, o_ref, gbuf_ref, carry_ref, *, tm, tp, tf):
    r = pl.program_id(0)
    x = h_ref[...]
    ms = jnp.mean(x * x, axis=-1, keepdims=True)
    u = (x * lax.rsqrt(ms + EPS) * g_ref[...]).astype(BF16)

    @pl.when(r == 0)
    def _():
        carry_ref[...] = jnp.zeros_like(carry_ref)

    seq_start = lax.rem(tp - lax.rem(r * tm, tp), tp)
    local = lax.broadcasted_iota(jnp.int32, (tm, 1), 0)
    tap1 = local != seq_start
    tap2 = tap1 & (local != seq_start + 1)
    out = x
    for f in range(D_FF // tf):
        cols = slice(f * tf, (f + 1) * tf)
        gate = _dot(u, wu_ref[:, cols])
        val = _dot(u, wu_ref[:, D_FF + f * tf:D_FF + (f + 1) * tf])
        gbuf_ref[0:8] = carry_ref[f]
        gbuf_ref[8:8 + tm] = gate
        carry_ref[f] = gate[tm - 8:tm]
        g1 = jnp.where(tap1, gbuf_ref[7:7 + tm], 0.0)
        g2 = jnp.where(tap2, gbuf_ref[6:6 + tm], 0.0)
        conv = cb_ref[:, cols] + cw_ref[0:1, cols] * g2 + cw_ref[1:2, cols] * g1 + cw_ref[2:3, cols] * gate
        act = conv * jax.nn.sigmoid(conv) * val
        out = out + _dot(act.astype(BF16), wd_ref[cols, :])
    o_ref[...] = out


def _ffn(h, gain, w_up, conv_w, conv_b, w_down, tp):
    m, d = h.shape
    tm = _pick_rows(m, 512)
    assert tm <= tp
    tf = D_FF // 2
    nf = D_FF // tf
    kern = functools.partial(_ffn_kernel, tm=tm, tp=tp, tf=tf)
    fixed = lambda r: (0, 0)
    resident = pl.Buffered(1)
    return pl.pallas_call(
        kern,
        out_shape=jax.ShapeDtypeStruct((m, d), F32),
        grid=(m // tm,),
        in_specs=[pl.BlockSpec((tm, d), lambda r: (r, 0)),
                  pl.BlockSpec((1, d), fixed),
                  pl.BlockSpec((d, 2 * D_FF), fixed, pipeline_mode=resident),
                  pl.BlockSpec((8, D_FF), fixed),
                  pl.BlockSpec((1, D_FF), fixed),
                  pl.BlockSpec((D_FF, d), fixed, pipeline_mode=resident)],
        out_specs=pl.BlockSpec((tm, d), lambda r: (r, 0)),
        scratch_shapes=[pltpu.VMEM((tm + 8, tf), F32),
                        pltpu.VMEM((nf, 8, tf), F32)],
        compiler_params=_params("arbitrary"),
        name="conv_ffn",
    )(h, gain.reshape(1, d), w_up, conv_w, conv_b.reshape(1, D_FF), w_down)


def _permute_w_in(w):
    n_attn = KW_IX - Q_SB + IDX_DIM + IDX_HEADS
    n_gate = 3 * D_MODEL
    n_df = 3 * DF_HEADS * 2 * HEAD_DIM
    gates = w[:, n_attn + n_df:]
    attn = w[:, :n_attn]
    pad = jnp.zeros((w.shape[0], Q_DF - Q_SB - n_attn), w.dtype)
    df = w[:, n_attn:n_attn + n_df]
    out = jnp.concatenate([gates, attn, pad, df], axis=1)
    assert gates.shape[1] == n_gate and out.shape[1] == NZ
    return out


def kernel(x, meta_tokens, rel_bias, attn_norm, w_in, b_gate, q_norm_sp, k_norm_sp, q_norm_df, k_norm_df, lam_q1, lam_k1, lam_q2, lam_k2, subln_df, w_br_sb, w_br_sp, w_br_df, w_out, ffn_norm, w_up, conv_w, conv_b, w_down):
    b, s, d = x.shape
    depth = w_in.shape[0]
    t = N_META + s
    tp = -(-t // QB) * QB
    top_k = min(TOPK_MAX, t // 4)
    m = b * tp

    meta = jnp.broadcast_to(meta_tokens[None].astype(x.dtype), (b, N_META, d))
    h = jnp.concatenate([meta, x, jnp.zeros((b, tp - t, d), x.dtype)], axis=1).reshape(m, d)

    bkt_t = jnp.asarray(np.ascontiguousarray(_bucket_tile().T))
    u_mat = jnp.asarray(_sb_prefix_matrix(), BF16)
    rep, bcast, before = (jnp.asarray(a, BF16) for a in _ix_select_matrices())
    rel_bias = rel_bias.astype(F32)

    for l in range(depth):
        lam_init = 0.8 - 0.6 * math.exp(-0.3 * l)
        z = _in_proj(h, attn_norm[l], _permute_w_in(w_in[l]).astype(BF16))
        z3 = z.reshape(b, tp, NZ)
        y_sb = _sb_attn(z3, u_mat)
        y_sp = _sp_attn(z3, rel_bias,
                        jnp.tile(q_norm_sp[l].astype(F32), SP_HEADS).reshape(1, -1),
                        jnp.tile(k_norm_sp[l].astype(F32), SP_HEADS).reshape(1, -1),
                        bkt_t, rep, bcast, before, top_k)
        lamv = jnp.zeros((8, BLK), F32).at[:4, :HEAD_DIM].set(
            jnp.stack([lam_q1[l], lam_k1[l], lam_q2[l], lam_k2[l]]).astype(F32))
        y_df = _dft_attn(z3, rel_bias,
                         jnp.tile(q_norm_df[l].astype(F32), 2).reshape(1, -1),
                         jnp.tile(k_norm_df[l].astype(F32), 2).reshape(1, -1),
                         bkt_t, lamv, subln_df[l].astype(F32).reshape(1, -1), lam_init)
        h = _mix_out(h, z, b_gate[l], y_sb.reshape(m, -1), y_sp.reshape(m, -1), y_df.reshape(m, -1),
                     w_br_sb[l].astype(BF16), w_br_sp[l].astype(BF16), w_br_df[l].astype(BF16),
                     w_out[l].astype(BF16))
        cw = jnp.zeros((8, D_FF), F32).at[:conv_w.shape[1]].set(conv_w[l])
        h = _ffn(h, ffn_norm[l], w_up[l].astype(BF16), cw, conv_b[l], w_down[l].astype(BF16), tp)

    return h.reshape(b, tp, d)[:, N_META:t]
```

```python
import functools
import math

import numpy as np
import jax
import jax.numpy as jnp
from jax import lax
from jax.experimental import pallas as pl
from jax.experimental.pallas import tpu as pltpu

D_MODEL = 1024
HEAD_DIM = 64
N_META = 16
BLK = 128
QB = 256
SB_HEADS = 4
SP_HEADS = 4
IDX_HEADS = 8
IDX_DIM = 32
TOPK_MAX = 256
DF_HEADS = 4
N_BUCKETS = 32
MAX_DISTANCE = 128
D_FF = 2816
EPS = 1e-6
LOG2E = math.log2(math.e)
QK_SCALE = HEAD_DIM ** -0.5 * LOG2E
M_INIT = -1e30
BOUND_SLACK = 1.02
UNDERFLOW_GUARD = 2.0 ** -100
INT_MIN = -2 ** 31
I16_MIN = -2 ** 15
COUNT_ROWS = 32
DIGIT_ROWS = 64
SB_FAR = 2
SP_FAR = 2
DF_FAR = 4

G_SB, G_SP, G_DF = 0, 1024, 2048
Q_SB, K_SB, V_SB = 3072, 3328, 3584
Q_SP, K_SP, V_SP = 3840, 4096, 4352
Q_IX, KW_IX = 4608, 4864
Q_DF, K_DF, V_DF = 5120, 5632, 6144
NZ = 6656
W_IX_LANE = IDX_DIM

VMEM_LIMIT = 56 * 1024 * 1024

F32 = jnp.float32
BF16 = jnp.bfloat16
NT_DIMS = (((1,), (1,)), ((), ()))


def _nt_dot(a, b):
    return lax.dot_general(a, b, NT_DIMS, preferred_element_type=F32)


def _dot(a, b):
    return jnp.dot(a, b, preferred_element_type=F32)


def _params(*sem):
    return pltpu.CompilerParams(dimension_semantics=sem, vmem_limit_bytes=VMEM_LIMIT)


def _pick_rows(m, cap):
    for c in (2048, 1024, 512, 256):
        if c <= cap and m % c == 0:
            return c
    raise ValueError(f"row count {m} is not a multiple of {QB}")


def _bucket_np(rel):
    n = np.maximum(rel, 0)
    max_exact = N_BUCKETS // 2
    nf = np.maximum(n, 1).astype(np.float32)
    large = max_exact + (np.log(nf / np.float32(max_exact)) / np.float32(math.log(MAX_DISTANCE / max_exact))
                         * np.float32(N_BUCKETS - max_exact)).astype(np.int32)
    return np.where(n < max_exact, n, np.minimum(large, N_BUCKETS - 1)).astype(np.int32)


def _bucket_tile():
    tq = np.arange(QB)[:, None]
    c = np.arange(2 * QB)[None, :]
    return _bucket_np(tq - c + QB)


def _sb_prefix_matrix():
    sp = np.arange(2 * BLK)[:, None] % BLK
    c = np.arange(2 * BLK)[None, :]
    return np.where(c < BLK, sp > c, True).astype(np.float32)


def _ix_select_matrices():
    c = np.arange(QB)[:, None]
    col = np.arange(QB)[None, :]
    rep = ((c < IDX_DIM) & (c == col % IDX_DIM)).astype(np.float32)
    col8 = np.arange(IDX_HEADS * BLK)[None, :]
    bcast = (c == W_IX_LANE + col8 // BLK).astype(np.float32)
    before = (col < c).astype(np.float32)
    return rep, bcast, before


def _lane_iota(shape):
    return lax.broadcasted_iota(jnp.int32, shape, len(shape) - 1)


def _row_iota(shape):
    return lax.broadcasted_iota(jnp.int32, shape, 0)


def _head_rmsnorm128(xf, gain):
    lo = _lane_iota((1, BLK)) < HEAD_DIM
    ss = xf * xf
    s_lo = jnp.sum(jnp.where(lo, ss, 0.0), axis=-1, keepdims=True)
    s_hi = jnp.sum(jnp.where(lo, 0.0, ss), axis=-1, keepdims=True)
    ms = jnp.where(lo, s_lo, s_hi) * (1.0 / HEAD_DIM)
    return xf * lax.rsqrt(ms + EPS) * gain


def _bias_tile(tab_ref, bk, head):
    far = tab_ref[N_BUCKETS - 1, head]
    acc = jnp.zeros(bk.shape, F32)
    for b in range(N_BUCKETS - 1):
        acc = jnp.where(bk == b, (tab_ref[b, head] - far) * LOG2E, acc)
    return acc


def _softmax_piece_t(st, vt, m_ref, l_ref, acc_ref):
    m_old = m_ref[0:1]
    m_new = jnp.maximum(m_old, jnp.max(st, axis=0, keepdims=True))
    alpha = jnp.exp2(m_old - m_new)
    p = jnp.exp2(st - m_new)
    l_ref[0:1] = alpha * l_ref[0:1] + jnp.sum(p, axis=0, keepdims=True)
    m_ref[0:1] = m_new
    acc_ref[...] = alpha * acc_ref[...] + _dot(vt, p.astype(BF16))


def _online_pieces_t(pieces, m_ref, l_ref, acc_ref):
    for st, vt in pieces:
        _softmax_piece_t(st, vt, m_ref, l_ref, acc_ref)


def _bounded_pieces_t(pieces, bound, l_ref, acc_ref):
    ps = [jnp.exp2(st - bound) for st, _ in pieces]
    l_ref[...] += functools.reduce(jnp.add, [jnp.sum(p.reshape(p.shape[0] // 8, 8, p.shape[1]), axis=0) for p in ps])
    acc_ref[...] += functools.reduce(jnp.add, [_dot(vt, p.astype(BF16)) for (_, vt), p in zip(pieces, ps)])


def _attend_windows(i, width, far_fn, near_fn, step):
    n_far = jnp.maximum(i - 1, 0)
    n_win = (n_far + width - 1) // width

    def far(p):
        hi = n_far - p * width
        return far_fn(jnp.maximum(hi - width, 0), hi)

    def body(pp, carry):
        step([far(2 * pp), far(2 * pp + 1)])
        return carry

    lax.fori_loop(0, n_win // 2, body, 0)
    odd = n_win % 2 == 1

    @pl.when(odd)
    def _():
        step([far(n_win - 1), near_fn(i - 1, 2)])

    @pl.when(jnp.logical_not(odd) & (i >= 1))
    def _():
        step([near_fn(i - 1, 2)])

    @pl.when(i == 0)
    def _():
        step([near_fn(0, 1)])


def _for_far_pieces(n_far, width, piece_fn):
    def body(p, carry):
        hi = n_far - p * width
        piece_fn(jnp.maximum(hi - width, 0), hi)
        return carry

    lax.fori_loop(0, (n_far + width - 1) // width, body, 0)


def _new_key_mask(start, hi, width):
    col = _lane_iota((1, width * QB))
    return jnp.where(col < (hi - start) * QB, 0.0, -jnp.inf)


def _in_proj_kernel(h_ref, g_ref, w_ref, o_ref, *, tn):
    x = h_ref[...]
    ms = jnp.mean(x * x, axis=-1, keepdims=True)
    u = (x * lax.rsqrt(ms + EPS) * g_ref[...]).astype(BF16)
    for c in range(o_ref.shape[1] // tn):
        cols = slice(c * tn, (c + 1) * tn)
        o_ref[:, cols] = _dot(u, w_ref[:, cols]).astype(o_ref.dtype)


def _in_proj(h, gain, w):
    m, d = h.shape
    n = w.shape[1]
    tm = _pick_rows(m, 512)
    return pl.pallas_call(
        functools.partial(_in_proj_kernel, tn=512),
        out_shape=jax.ShapeDtypeStruct((m, n), BF16),
        grid=(m // tm,),
        in_specs=[pl.BlockSpec((tm, d), lambda i: (i, 0)),
                  pl.BlockSpec((1, d), lambda i: (0, 0)),
                  pl.BlockSpec((d, n), lambda i: (0, 0), pipeline_mode=pl.Buffered(1))],
        out_specs=pl.BlockSpec((tm, n), lambda i: (i, 0)),
        compiler_params=_params("parallel"),
        name="in_proj",
    )(h, gain.reshape(1, d), w)


def _sb_kernel(q_ref, k_ref, v_ref, u_ref, o_ref, tot_ref, acc_ref):
    i = pl.program_id(1)
    nh = SB_HEADS
    head_of_lane = _lane_iota((QB, nh * HEAD_DIM)) // HEAD_DIM
    q = q_ref[0].astype(F32) * QK_SCALE
    qs = jnp.concatenate([jnp.where(head_of_lane == h, q, 0.0) for h in range(nh)], axis=0).astype(BF16)
    tot_ref[...] = jnp.zeros_like(tot_ref)
    acc_ref[...] = jnp.zeros_like(acc_ref)

    def piece(start, n_blocks, hi=None):
        r = pl.multiple_of(start * QB, QB)
        z_all = _nt_dot(qs, k_ref[0, pl.ds(r, n_blocks * QB), :])
        if hi is not None:
            z_all = z_all + _new_key_mask(start, hi, n_blocks)
        run = tot_ref[...]
        n_sub = n_blocks * QB // BLK
        ws = [None] * n_sub
        for c in reversed(range(n_sub)):
            z = z_all[:, c * BLK:(c + 1) * BLK]
            sp = jnp.maximum(z, 0.0) + jnp.log2(1.0 + jnp.exp2(-jnp.abs(z)))
            l1m = -sp
            if hi is None:
                mask = (_lane_iota(z.shape) + c * BLK) < (_row_iota(z.shape) & (QB - 1))
                l1m = jnp.where(mask, l1m, 0.0)
            l1m_hi = l1m.astype(BF16)
            l1m_lo = (l1m - l1m_hi.astype(F32)).astype(BF16)
            rs = _dot(jnp.concatenate([l1m_hi, l1m_lo], axis=1), u_ref[...])
            w = jnp.exp2((z - sp) + rs[:, :BLK] + run)
            if hi is None:
                w = jnp.where(mask, w, 0.0)
            ws[c] = w.astype(BF16)
            run = run + rs[:, BLK:]
        tot_ref[...] = run
        acc_ref[...] += _dot(jnp.concatenate(ws, axis=1), v_ref[0, pl.ds(r, n_blocks * QB), :])

    piece(i, 1)
    _for_far_pieces(i, SB_FAR, lambda start, hi: piece(start, SB_FAR, hi))

    a = acc_ref[...]
    out = a[:QB]
    for h in range(1, nh):
        out = jnp.where(head_of_lane == h, a[h * QB:(h + 1) * QB], out)
    o_ref[0] = out.astype(o_ref.dtype)


def _sb_attn(z3, u_mat):
    b, tp, _ = z3.shape
    nq = tp // QB
    assert nq >= SB_FAR
    w = SB_HEADS * HEAD_DIM
    return pl.pallas_call(
        _sb_kernel,
        out_shape=jax.ShapeDtypeStruct((b, tp, w), BF16),
        grid=(b, nq),
        in_specs=[pl.BlockSpec((1, QB, w), lambda bb, i: (bb, i, Q_SB // w)),
                  pl.BlockSpec((1, tp, w), lambda bb, i: (bb, 0, K_SB // w)),
                  pl.BlockSpec((1, tp, w), lambda bb, i: (bb, 0, V_SB // w)),
                  pl.BlockSpec((2 * BLK, 2 * BLK), lambda bb, i: (0, 0))],
        out_specs=pl.BlockSpec((1, QB, w), lambda bb, i: (bb, i, 0)),
        scratch_shapes=[pltpu.VMEM((SB_HEADS * QB, BLK), F32),
                        pltpu.VMEM((SB_HEADS * QB, w), F32)],
        compiler_params=_params("parallel", "arbitrary"),
        name="sb_attn",
    )(z3, z3, z3, u_mat)


def _dft_kernel(tab_ref, q_ref, k_ref, v_ref, qg_ref, kg_ref, bkt_ref, lamv_ref, sub_ref, o_ref,
                kn_ref, vt_ref, bias_ref, kmax_ref, m_ref, l_ref, lsum_ref, acc_ref, *, nq, lam_init, head_off):
    h = pl.program_id(1)
    i = pl.program_id(2)
    lo = _lane_iota((1, BLK)) < HEAD_DIM

    @pl.when(i == 0)
    def _prep():
        kmax_ref[...] = jnp.zeros_like(kmax_ref)

        def kbody(j, carry):
            r = pl.multiple_of(j * QB, QB)
            kf = k_ref[0, pl.ds(r, QB), :].astype(F32)
            kn = _head_rmsnorm128(kf, kg_ref[...]).astype(BF16)
            kn_ref[pl.ds(r, QB), :] = kn
            ksq = kn.astype(F32) ** 2
            half_norms = jnp.maximum(jnp.sum(jnp.where(lo, ksq, 0.0), axis=-1, keepdims=True),
                                     jnp.sum(jnp.where(lo, 0.0, ksq), axis=-1, keepdims=True))
            kmax_ref[...] = jnp.maximum(kmax_ref[...], jnp.max(half_norms))
            vt_ref[j] = v_ref[0, pl.ds(r, QB), :].astype(F32).T.astype(BF16)
            return carry

        lax.fori_loop(0, nq, kbody, 0)
        bias_ref[...] = _bias_tile(tab_ref, bkt_ref[...], head_off + h)

    qn = _head_rmsnorm128(q_ref[0].astype(F32), qg_ref[...]) * QK_SCALE
    qs = jnp.concatenate([jnp.where(lo, qn, 0.0), jnp.where(lo, 0.0, qn)], axis=0).astype(BF16)

    q_sq = _nt_dot(jnp.ones((8, BLK), BF16), (qs.astype(F32) ** 2).astype(BF16))[0:1]
    far_b = tab_ref[N_BUCKETS - 1, head_off + h]
    bias_max = jnp.float32(0.0)
    for b in range(N_BUCKETS - 1):
        bias_max = jnp.maximum(bias_max, (tab_ref[b, head_off + h] - far_b) * LOG2E)
    bound = jnp.sqrt(q_sq * kmax_ref[0:1, 0:1]) * BOUND_SLACK + (bias_max + BOUND_SLACK)

    def scores(first, n):
        return _nt_dot(kn_ref[pl.ds(pl.multiple_of(first * QB, QB), n * QB), :], qs)

    def values(first, n):
        return jnp.concatenate([vt_ref[first + c] for c in range(n)], axis=1)

    def far_piece(start, hi):
        st = scores(start, DF_FAR)
        new_keys = _row_iota(st.shape) < (hi - start) * QB
        return jnp.where(new_keys, st, -jnp.inf), values(start, DF_FAR)

    def near_piece(first, n):
        b = bias_ref[(2 - n) * QB:, :]
        st = scores(first, n) + jnp.concatenate([b, b], axis=1)
        causal = (_row_iota(st.shape) - (n - 1) * QB) <= (_lane_iota(st.shape) & (QB - 1))
        return jnp.where(causal, st, -jnp.inf), values(first, n)

    def attend(step):
        _attend_windows(i, DF_FAR, far_piece, near_piece, step)

    l_ref[...] = jnp.zeros_like(l_ref)
    acc_ref[...] = jnp.zeros_like(acc_ref)
    attend(lambda pieces: _bounded_pieces_t(pieces, bound, l_ref, acc_ref))
    lsum_ref[0:1] = jnp.sum(l_ref[...], axis=0, keepdims=True)

    @pl.when(jnp.min(lsum_ref[0:1]) < UNDERFLOW_GUARD)
    def _():
        m_ref[...] = jnp.full(m_ref.shape, M_INIT, F32)
        l_ref[...] = jnp.zeros_like(l_ref)
        acc_ref[...] = jnp.zeros_like(acc_ref)
        attend(lambda pieces: _online_pieces_t(pieces, m_ref, l_ref, acc_ref))
        lsum_ref[0:1] = l_ref[0:1]

    a = acc_ref[...] / lsum_ref[0:1]
    lv = lamv_ref[...]
    lam = (jnp.exp(jnp.sum(lv[0:1] * lv[1:2], axis=-1, keepdims=True))
           - jnp.exp(jnp.sum(lv[2:3] * lv[3:4], axis=-1, keepdims=True)) + lam_init)
    y = (a[:, :QB] - lam * a[:, QB:]).T
    y = y * lax.rsqrt(jnp.mean(y * y, axis=-1, keepdims=True) + EPS) * sub_ref[...]
    o_ref[0] = (y * (1.0 - lam_init)).astype(o_ref.dtype)


def _dft_attn(z3, rel_bias, qg, kg, bkt_t, lamv, subln, lam_init):
    b, tp, _ = z3.shape
    nq = tp // QB
    assert nq >= DF_FAR
    kern = functools.partial(_dft_kernel, nq=nq, lam_init=lam_init, head_off=SP_HEADS)
    vec = lambda bb, h, i: (0, 0)
    return pl.pallas_call(
        kern,
        out_shape=jax.ShapeDtypeStruct((b, tp, DF_HEADS * BLK), BF16),
        grid=(b, DF_HEADS, nq),
        in_specs=[pl.BlockSpec(memory_space=pltpu.SMEM),
                  pl.BlockSpec((1, QB, BLK), lambda bb, h, i: (bb, i, Q_DF // BLK + h)),
                  pl.BlockSpec((1, tp, BLK), lambda bb, h, i: (bb, 0, K_DF // BLK + h)),
                  pl.BlockSpec((1, tp, BLK), lambda bb, h, i: (bb, 0, V_DF // BLK + h)),
                  pl.BlockSpec((1, BLK), vec),
                  pl.BlockSpec((1, BLK), vec),
                  pl.BlockSpec((2 * QB, QB), vec),
                  pl.BlockSpec((8, BLK), vec),
                  pl.BlockSpec((1, BLK), vec)],
        out_specs=pl.BlockSpec((1, QB, BLK), lambda bb, h, i: (bb, i, h)),
        scratch_shapes=[pltpu.VMEM((tp, BLK), BF16),
                        pltpu.VMEM((nq, BLK, QB), BF16),
                        pltpu.VMEM((2 * QB, QB), F32),
                        pltpu.VMEM((8, BLK), F32),
                        pltpu.VMEM((8, 2 * QB), F32),
                        pltpu.VMEM((8, 2 * QB), F32),
                        pltpu.VMEM((8, 2 * QB), F32),
                        pltpu.VMEM((BLK, 2 * QB), F32)],
        compiler_params=_params("parallel", "parallel", "arbitrary"),
        name="df_attn",
    )(rel_bias, z3, z3, z3, qg, kg, bkt_t, lamv, subln)


def _sp_kernel(tab_ref, q_ref, k_ref, v_ref, qix_ref, kwq_ref, kwk_ref, qg_ref, kg_ref, bkt_ref,
               rep_ref, bcast_ref, before_ref, o_ref,
               kn_ref, kx_ref, vt_ref, bias_ref, kmax_ref, keys_ref, dig_ref, thr_ref, wb_ref,
               m_ref, l_ref, lsum_ref, acc_ref, *, nq, top_k):
    i = pl.program_id(1)
    nh = SP_HEADS
    w = nh * HEAD_DIM
    lane = _lane_iota((QB, w))
    head_of_lane = lane // HEAD_DIM

    def norm256(xf, g):
        return jnp.concatenate([_head_rmsnorm128(xf[:, :BLK], g[:, :BLK]),
                                _head_rmsnorm128(xf[:, BLK:], g[:, BLK:])], axis=1)

    def dup(x):
        return jnp.concatenate([x, x], axis=1)

    @pl.when(i == 0)
    def _prep():
        kmax_ref[...] = jnp.zeros_like(kmax_ref)

        def kbody(c, carry):
            r = pl.multiple_of(c * QB, QB)
            kn = norm256(k_ref[0, pl.ds(r, QB), :].astype(F32), kg_ref[...]).astype(BF16)
            kn_ref[pl.ds(r, QB), :] = kn
            ksq = kn.astype(F32) ** 2
            head_norms = functools.reduce(jnp.maximum, [
                jnp.sum(jnp.where(head_of_lane == h, ksq, 0.0), axis=-1, keepdims=True) for h in range(nh)])
            kmax_ref[0:1] = jnp.maximum(kmax_ref[0:1], jnp.max(head_norms))
            kx_ref[pl.ds(r, QB), :] = _dot(kwk_ref[0, pl.ds(r, QB), :], rep_ref[...]).astype(BF16)
            vt_ref[c] = v_ref[0, pl.ds(r, QB), :].astype(F32).T.astype(BF16)
            return carry

        lax.fori_loop(0, nq, kbody, 0)
        bias_max = jnp.float32(0.0)
        for h in range(nh):
            bias_ref[:, h * QB:(h + 1) * QB] = _bias_tile(tab_ref, bkt_ref[...], h)
            for b in range(N_BUCKETS - 1):
                bias_max = jnp.maximum(bias_max, (tab_ref[b, h] - tab_ref[N_BUCKETS - 1, h]) * LOG2E)
        kmax_ref[1:2] = jnp.full((1, BLK), bias_max, F32)

    key_causal = _row_iota((QB, QB)) <= _lane_iota((QB, QB))

    wb_ref[...] = _dot(kwq_ref[0], bcast_ref[...])
    qix = qix_ref[0].astype(F32)
    ix_head = lane // IDX_DIM
    qx = jnp.concatenate([jnp.where(ix_head == h, qix, 0.0) for h in range(IDX_HEADS)], axis=0).astype(BF16)

    def score_tiles(blocks, last_is_diagonal):
        dots = [_nt_dot(qx, kx_ref[pl.ds(pl.multiple_of(j * QB, QB), QB), :]) for j in blocks]
        for n, (j, d) in enumerate(zip(blocks, dots)):
            sc = jnp.zeros((QB, QB), F32)
            for h in range(IDX_HEADS):
                sc = sc + dup(wb_ref[:, h * BLK:(h + 1) * BLK]) * jnp.maximum(d[h * QB:(h + 1) * QB], 0.0)
            sc = jnp.where(sc == 0.0, 0.0, sc).T
            bits = lax.bitcast_convert_type(sc, jnp.int32)
            key = jnp.where(bits < 0, bits ^ jnp.int32(0x7FFFFFFF), bits)
            if last_is_diagonal and n == len(blocks) - 1:
                key = jnp.where(key_causal, key, jnp.int32(INT_MIN))
            keys_ref[j] = key
            dig_ref[j] = jnp.right_shift(key, 16).astype(jnp.int16)

    def pair_body(p, c):
        score_tiles([2 * p, 2 * p + 1], False)
        return c

    lax.fori_loop(0, i // 2, pair_body, 0)

    @pl.when(i % 2 == 1)
    def _():
        score_tiles([i - 1, i], True)

    @pl.when(i % 2 == 0)
    def _():
        score_tiles([i], True)

    nblk = i + 1

    def count(pred):
        def cbody(j, c):
            hit = jnp.where(pred(keys_ref[j]), 1.0, 0.0)
            return c + jnp.sum(hit.reshape(QB // COUNT_ROWS, COUNT_ROWS, QB), axis=0)

        c = lax.fori_loop(0, nblk, cbody, jnp.zeros((COUNT_ROWS, QB), F32))
        return jnp.sum(c, axis=0, keepdims=True)

    def count16(pred):
        def cbody(j, c):
            hit = jnp.where(pred(dig_ref[j]), jnp.int16(1), jnp.int16(0)).reshape(QB // DIGIT_ROWS, DIGIT_ROWS, QB)
            return c + functools.reduce(jnp.add, [hit[r] for r in range(QB // DIGIT_ROWS)])

        c = lax.fori_loop(0, nblk, cbody, jnp.zeros((DIGIT_ROWS, QB), jnp.int16))
        return jnp.sum(c.astype(F32), axis=0, keepdims=True)

    n_all = jnp.zeros((1, QB), F32) + (nblk * QB).astype(F32)

    def bisect16(need):
        def bit_body(b, carry):
            cur, cnt_cur = carry
            cand = cur + jnp.left_shift(jnp.int32(1), 15 - b)
            cand16 = cand.astype(jnp.int16)
            cnt = count16(lambda d: d >= cand16)
            ok = cnt >= need
            return jnp.where(ok, cand, cur), jnp.where(ok, cnt, cnt_cur)

        return lax.fori_loop(0, 16, bit_body, (jnp.full((1, QB), I16_MIN, jnp.int32), n_all))

    t_hi, c_ge_hi = bisect16(float(top_k))
    t_hi16 = t_hi.astype(jnp.int16)
    c_gt_hi = count16(lambda d: d > t_hi16)
    base = jnp.left_shift(t_hi, 16)

    def low_digits(j, c):
        y = keys_ref[j] - base
        dig_ref[j] = jnp.where(jnp.right_shift(y, 16) == 0, y + I16_MIN, I16_MIN).astype(jnp.int16)
        return c

    lax.fori_loop(0, nblk, low_digits, 0)
    t_lo, c_ge_lo = bisect16(float(top_k) - c_gt_hi)
    thr = base + (t_lo - I16_MIN)
    cge = c_gt_hi + jnp.where(t_lo > I16_MIN, c_ge_lo, c_ge_hi - c_gt_hi)
    thr_ref[...] = jnp.broadcast_to(thr, thr_ref.shape)

    tie = jnp.where((cge > float(top_k)) & (thr > INT_MIN), 1, 0)

    @pl.when(jnp.max(tie) > 0)
    def _ties():
        need = float(top_k) - count(lambda k: k > thr)

        def tbody(j, run):
            kj = keys_ref[j]
            eq = kj == thr
            eqf = jnp.where(eq, 1.0, 0.0)
            rank = run + _dot(before_ref[...], eqf.astype(BF16))
            keep = jnp.where(kj > thr, 1, jnp.where(eq & (rank < need), 1, -1))
            keys_ref[j] = keep.astype(jnp.int32)
            return run + jnp.sum(eqf, axis=0, keepdims=True)

        lax.fori_loop(0, nblk, tbody, jnp.zeros((1, QB), F32))
        thr_ref[...] = jnp.zeros_like(thr_ref)

    qn = norm256(q_ref[0].astype(F32), qg_ref[...]) * QK_SCALE
    qs = jnp.concatenate([jnp.where(head_of_lane == h, qn, 0.0) for h in range(nh)], axis=0).astype(BF16)
    q_sq = _nt_dot(jnp.ones((8, w), BF16), (qs.astype(F32) ** 2).astype(BF16))[0:1]
    bound = jnp.sqrt(q_sq * kmax_ref[0:1, 0:1]) * BOUND_SLACK + (kmax_ref[1:2, 0:1] + BOUND_SLACK)

    thr_sel = thr_ref[0:1, :]

    def as_mask(sel):
        return lax.bitcast_convert_type(jnp.where(sel, 0.0, -jnp.inf), jnp.int32)

    def mask_body(j, c):
        keys_ref[j] = as_mask(keys_ref[j] >= thr_sel)
        return c

    lax.fori_loop(0, i, mask_body, 0)
    keys_ref[i] = as_mask((keys_ref[i] >= thr_sel) & key_causal)

    def scores(first, n):
        return _nt_dot(kn_ref[pl.ds(pl.multiple_of(first * QB, QB), n * QB), :], qs)

    def values(first, n):
        return jnp.concatenate([vt_ref[first + c] for c in range(n)], axis=1)

    def selection(first, n):
        return jnp.concatenate([lax.bitcast_convert_type(keys_ref[first + c], F32) for c in range(n)],
                               axis=0)

    def far_piece(start, hi):
        m = selection(start, SP_FAR)
        m = jnp.where(_row_iota(m.shape) < (hi - start) * QB, m, -jnp.inf)
        return scores(start, SP_FAR) + jnp.concatenate([m] * nh, axis=1), values(start, SP_FAR)

    def near_piece(first, n):
        m = selection(first, n)
        return (scores(first, n) + (jnp.concatenate([m] * nh, axis=1) + bias_ref[(2 - n) * QB:, :]),
                values(first, n))

    def attend(step):
        _attend_windows(i, SP_FAR, far_piece, near_piece, step)

    l_ref[...] = jnp.zeros_like(l_ref)
    acc_ref[...] = jnp.zeros_like(acc_ref)
    attend(lambda pieces: _bounded_pieces_t(pieces, bound, l_ref, acc_ref))
    lsum_ref[0:1] = jnp.sum(l_ref[...], axis=0, keepdims=True)

    @pl.when(jnp.min(lsum_ref[0:1]) < UNDERFLOW_GUARD)
    def _():
        m_ref[...] = jnp.full(m_ref.shape, M_INIT, F32)
        l_ref[...] = jnp.zeros_like(l_ref)
        acc_ref[...] = jnp.zeros_like(acc_ref)
        attend(lambda pieces: _online_pieces_t(pieces, m_ref, l_ref, acc_ref))
        lsum_ref[0:1] = l_ref[0:1]

    a = acc_ref[...] / lsum_ref[0:1]
    out_t = jnp.concatenate([a[h * HEAD_DIM:(h + 1) * HEAD_DIM, h * QB:(h + 1) * QB] for h in range(nh)], axis=0)
    o_ref[0] = out_t.T.astype(o_ref.dtype)


def _sp_attn(z3, rel_bias, qg, kg, bkt_t, rep, bcast, before, top_k):
    b, tp, _ = z3.shape
    nq = tp // QB
    assert nq >= SP_FAR
    w = SP_HEADS * HEAD_DIM
    kern = functools.partial(_sp_kernel, nq=nq, top_k=top_k)
    c2 = lambda bb, i: (0, 0)
    return pl.pallas_call(
        kern,
        out_shape=jax.ShapeDtypeStruct((b, tp, w), BF16),
        grid=(b, nq),
        in_specs=[pl.BlockSpec(memory_space=pltpu.SMEM),
                  pl.BlockSpec((1, QB, w), lambda bb, i: (bb, i, Q_SP // w)),
                  pl.BlockSpec((1, tp, w), lambda bb, i: (bb, 0, K_SP // w)),
                  pl.BlockSpec((1, tp, w), lambda bb, i: (bb, 0, V_SP // w)),
                  pl.BlockSpec((1, QB, w), lambda bb, i: (bb, i, Q_IX // w)),
                  pl.BlockSpec((1, QB, w), lambda bb, i: (bb, i, KW_IX // w)),
                  pl.BlockSpec((1, tp, w), lambda bb, i: (bb, 0, KW_IX // w)),
                  pl.BlockSpec((1, w), c2),
                  pl.BlockSpec((1, w), c2),
                  pl.BlockSpec((2 * QB, QB), c2),
                  pl.BlockSpec((w, w), c2),
                  pl.BlockSpec((w, IDX_HEADS * BLK), c2),
                  pl.BlockSpec((QB, QB), c2)],
        out_specs=pl.BlockSpec((1, QB, w), lambda bb, i: (bb, i, 0)),
        scratch_shapes=[pltpu.VMEM((tp, w), BF16),
                        pltpu.VMEM((tp, w), BF16),
                        pltpu.VMEM((nq, w, QB), BF16),
                        pltpu.VMEM((2 * QB, SP_HEADS * QB), F32),
                        pltpu.VMEM((8, BLK), F32),
                        pltpu.VMEM((nq, QB, QB), jnp.int32),
                        pltpu.VMEM((nq, QB, QB), jnp.int16),
                        pltpu.VMEM((8, QB), jnp.int32),
                        pltpu.VMEM((QB, IDX_HEADS * BLK), F32),
                        pltpu.VMEM((8, SP_HEADS * QB), F32),
                        pltpu.VMEM((8, SP_HEADS * QB), F32),
                        pltpu.VMEM((8, SP_HEADS * QB), F32),
                        pltpu.VMEM((w, SP_HEADS * QB), F32)],
        compiler_params=_params("parallel", "arbitrary"),
        name="sp_attn",
    )(rel_bias, z3, z3, z3, z3, z3, z3, qg, kg, bkt_t, rep, bcast, before)


def _mix_kernel(h_ref, gsb_ref, gsp_ref, gdf_ref, bg_ref, ysb_ref, ysp_ref, ydf_ref,
                wsb_ref, wsp_ref, wdf_ref, wo_ref, o_ref):
    def branch(g_ref, k, y_ref, w_ref):
        gate = jax.nn.sigmoid(g_ref[...].astype(F32) + bg_ref[:, k * D_MODEL:(k + 1) * D_MODEL])
        return gate * _dot(y_ref[...], w_ref[...])

    merged = (branch(gsb_ref, 0, ysb_ref, wsb_ref) + branch(gsp_ref, 1, ysp_ref, wsp_ref)
              + branch(gdf_ref, 2, ydf_ref, wdf_ref))
    o_ref[...] = h_ref[...] + _dot(merged.astype(BF16), wo_ref[...])


def _mix_out(h, z, b_gate, y_sb, y_sp, y_df, w_sb, w_sp, w_df, w_o):
    m, d = h.shape
    tm = _pick_rows(m, 512)
    row = lambda i: (i, 0)
    fixed = lambda i: (0, 0)
    return pl.pallas_call(
        _mix_kernel,
        out_shape=jax.ShapeDtypeStruct((m, d), F32),
        grid=(m // tm,),
        in_specs=[pl.BlockSpec((tm, d), row),
                  pl.BlockSpec((tm, d), lambda i: (i, G_SB // D_MODEL)),
                  pl.BlockSpec((tm, d), lambda i: (i, G_SP // D_MODEL)),
                  pl.BlockSpec((tm, d), lambda i: (i, G_DF // D_MODEL)),
                  pl.BlockSpec((1, 3 * d), fixed),
                  pl.BlockSpec((tm, y_sb.shape[1]), row),
                  pl.BlockSpec((tm, y_sp.shape[1]), row),
                  pl.BlockSpec((tm, y_df.shape[1]), row),
                  pl.BlockSpec(w_sb.shape, fixed),
                  pl.BlockSpec(w_sp.shape, fixed),
                  pl.BlockSpec(w_df.shape, fixed),
                  pl.BlockSpec(w_o.shape, fixed)],
        out_specs=pl.BlockSpec((tm, d), row),
        compiler_params=_params("parallel"),
        name="mix_out",
    )(h, z, z, z, b_gate.reshape(1, 3 * d), y_sb, y_sp, y_df, w_sb, w_sp, w_df, w_o)


def _ffn_kernel(h_ref, g_ref, wu_ref, cw_ref, cb_ref, wd_ref, o_ref, gbuf_ref, carry_ref, *, tm, tp, tf):
    r = pl.program_id(0)
    x = h_ref[...]
    ms = jnp.mean(x * x, axis=-1, keepdims=True)
    u = (x * lax.rsqrt(ms + EPS) * g_ref[...]).astype(BF16)

    @pl.when(r == 0)
    def _():
        carry_ref[...] = jnp.zeros_like(carry_ref)

    seq_start = lax.rem(tp - lax.rem(r * tm, tp), tp)
    local = lax.broadcasted_iota(jnp.int32, (tm, 1), 0)
    tap1 = local != seq_start
    tap2 = tap1 & (local != seq_start + 1)
    out = x
    for f in range(D_FF // tf):
        cols = slice(f * tf, (f + 1) * tf)
        gate = _dot(u, wu_ref[:, cols])
        val = _dot(u, wu_ref[:, D_FF + f * tf:D_FF + (f + 1) * tf])
        gbuf_ref[0:8] = carry_ref[f]
        gbuf_ref[8:8 + tm] = gate
        carry_ref[f] = gate[tm - 8:tm]
        g1 = jnp.where(tap1, gbuf_ref[7:7 + tm], 0.0)
        g2 = jnp.where(tap2, gbuf_ref[6:6 + tm], 0.0)
        conv = cb_ref[:, cols] + cw_ref[0:1, cols] * g2 + cw_ref[1:2, cols] * g1 + cw_ref[2:3, cols] * gate
        act = conv * jax.nn.sigmoid(conv) * val
        out = out + _dot(act.astype(BF16), wd_ref[cols, :])
    o_ref[...] = out


def _ffn(h, gain, w_up, conv_w, conv_b, w_down, tp):
    m, d = h.shape
    tm = _pick_rows(m, 512)
    assert tm <= tp
    tf = D_FF // 2
    nf = D_FF // tf
    kern = functools.partial(_ffn_kernel, tm=tm, tp=tp, tf=tf)
    fixed = lambda r: (0, 0)
    resident = pl.Buffered(1)
    return pl.pallas_call(
        kern,
        out_shape=jax.ShapeDtypeStruct((m, d), F32),
        grid=(m // tm,),
        in_specs=[pl.BlockSpec((tm, d), lambda r: (r, 0)),
                  pl.BlockSpec((1, d), fixed),
                  pl.BlockSpec((d, 2 * D_FF), fixed, pipeline_mode=resident),
                  pl.BlockSpec((8, D_FF), fixed),
                  pl.BlockSpec((1, D_FF), fixed),
                  pl.BlockSpec((D_FF, d), fixed, pipeline_mode=resident)],
        out_specs=pl.BlockSpec((tm, d), lambda r: (r, 0)),
        scratch_shapes=[pltpu.VMEM((tm + 8, tf), F32),
                        pltpu.VMEM((nf, 8, tf), F32)],
        compiler_params=_params("arbitrary"),
        name="conv_ffn",
    )(h, gain.reshape(1, d), w_up, conv_w, conv_b.reshape(1, D_FF), w_down)


def _permute_w_in(w):
    n_attn = KW_IX - Q_SB + IDX_DIM + IDX_HEADS
    n_gate = 3 * D_MODEL
    n_df = 3 * DF_HEADS * 2 * HEAD_DIM
    gates = w[:, n_attn + n_df:]
    attn = w[:, :n_attn]
    pad = jnp.zeros((w.shape[0], Q_DF - Q_SB - n_attn), w.dtype)
    df = w[:, n_attn:n_attn + n_df]
    out = jnp.concatenate([gates, attn, pad, df], axis=1)
    assert gates.shape[1] == n_gate and out.shape[1] == NZ
    return out


def kernel(x, meta_tokens, rel_bias, attn_norm, w_in, b_gate, q_norm_sp, k_norm_sp, q_norm_df, k_norm_df, lam_q1, lam_k1, lam_q2, lam_k2, subln_df, w_br_sb, w_br_sp, w_br_df, w_out, ffn_norm, w_up, conv_w, conv_b, w_down):
    b, s, d = x.shape
    depth = w_in.shape[0]
    t = N_META + s
    tp = -(-t // QB) * QB
    top_k = min(TOPK_MAX, t // 4)
    m = b * tp

    meta = jnp.broadcast_to(meta_tokens[None].astype(x.dtype), (b, N_META, d))
    h = jnp.concatenate([meta, x, jnp.zeros((b, tp - t, d), x.dtype)], axis=1).reshape(m, d)

    bkt_t = jnp.asarray(np.ascontiguousarray(_bucket_tile().T))
    u_mat = jnp.asarray(_sb_prefix_matrix(), BF16)
    rep, bcast, before = (jnp.asarray(a, BF16) for a in _ix_select_matrices())
    rel_bias = rel_bias.astype(F32)

    for l in range(depth):
        lam_init = 0.8 - 0.6 * math.exp(-0.3 * l)
        z = _in_proj(h, attn_norm[l], _permute_w_in(w_in[l]).astype(BF16))
        z3 = z.reshape(b, tp, NZ)
        y_sb = _sb_attn(z3, u_mat)
        y_sp = _sp_attn(z3, rel_bias,
                        jnp.tile(q_norm_sp[l].astype(F32), SP_HEADS).reshape(1, -1),
                        jnp.tile(k_norm_sp[l].astype(F32), SP_HEADS).reshape(1, -1),
                        bkt_t, rep, bcast, before, top_k)
        lamv = jnp.zeros((8, BLK), F32).at[:4, :HEAD_DIM].set(
            jnp.stack([lam_q1[l], lam_k1[l], lam_q2[l], lam_k2[l]]).astype(F32))
        y_df = _dft_attn(z3, rel_bias,
                         jnp.tile(q_norm_df[l].astype(F32), 2).reshape(1, -1),
                         jnp.tile(k_norm_df[l].astype(F32), 2).reshape(1, -1),
                         bkt_t, lamv, subln_df[l].astype(F32).reshape(1, -1), lam_init)
        h = _mix_out(h, z, b_gate[l], y_sb.reshape(m, -1), y_sp.reshape(m, -1), y_df.reshape(m, -1),
                     w_br_sb[l].astype(BF16), w_br_sp[l].astype(BF16), w_br_df[l].astype(BF16),
                     w_out[l].astype(BF16))
        cw = jnp.zeros((8, D_FF), F32).at[:conv_w.shape[1]].set(conv_w[l])
        h = _ffn(h, ffn_norm[l], w_up[l].astype(BF16), cw, conv_b[l], w_down[l].astype(BF16), tp)

    return h.reshape(b, tp, d)[:, N_META:t]
```

```python
import functools
import math

import numpy as np
import jax
import jax.numpy as jnp
from jax import lax
from jax.experimental import pallas as pl
from jax.experimental.pallas import tpu as pltpu

D_MODEL = 1024
HEAD_DIM = 64
N_META = 16
BLK = 128
QB = 256
SB_HEADS = 4
SP_HEADS = 4
IDX_HEADS = 8
IDX_DIM = 32
TOPK_MAX = 256
DF_HEADS = 4
N_BUCKETS = 32
MAX_DISTANCE = 128
D_FF = 2816
EPS = 1e-6
LOG2E = math.log2(math.e)
QK_SCALE = HEAD_DIM ** -0.5 * LOG2E
M_INIT = -1e30
BOUND_SLACK = 1.02
UNDERFLOW_GUARD = 2.0 ** -100
INT_MIN = -2 ** 31
I16_MIN = -2 ** 15
COUNT_ROWS = 32
DIGIT_ROWS = 64
SCORE_GROUP = 4
WINDOW_GROUP = 3
SB_FAR = 2
SP_FAR = 2
DF_FAR = 4

G_SB, G_SP, G_DF = 0, 1024, 2048
Q_SB, K_SB, V_SB = 3072, 3328, 3584
Q_SP, K_SP, V_SP = 3840, 4096, 4352
Q_IX, KW_IX = 4608, 4864
Q_DF, K_DF, V_DF = 5120, 5632, 6144
NZ = 6656
W_IX_LANE = IDX_DIM

VMEM_LIMIT = 56 * 1024 * 1024

F32 = jnp.float32
BF16 = jnp.bfloat16
NT_DIMS = (((1,), (1,)), ((), ()))


def _nt_dot(a, b):
    return lax.dot_general(a, b, NT_DIMS, preferred_element_type=F32)


def _dot(a, b):
    return jnp.dot(a, b, preferred_element_type=F32)


def _params(*sem):
    return pltpu.CompilerParams(dimension_semantics=sem, vmem_limit_bytes=VMEM_LIMIT)


def _pick_rows(m, cap):
    for c in (2048, 1024, 512, 256):
        if c <= cap and m % c == 0:
            return c
    raise ValueError(f"row count {m} is not a multiple of {QB}")


def _bucket_np(rel):
    n = np.maximum(rel, 0)
    max_exact = N_BUCKETS // 2
    nf = np.maximum(n, 1).astype(np.float32)
    large = max_exact + (np.log(nf / np.float32(max_exact)) / np.float32(math.log(MAX_DISTANCE / max_exact))
                         * np.float32(N_BUCKETS - max_exact)).astype(np.int32)
    return np.where(n < max_exact, n, np.minimum(large, N_BUCKETS - 1)).astype(np.int32)


def _bucket_tile():
    tq = np.arange(QB)[:, None]
    c = np.arange(2 * QB)[None, :]
    return _bucket_np(tq - c + QB)


def _sb_prefix_matrix():
    sp = np.arange(2 * BLK)[:, None] % BLK
    c = np.arange(2 * BLK)[None, :]
    return np.where(c < BLK, sp > c, True).astype(np.float32)


def _ix_select_matrices():
    c = np.arange(QB)[:, None]
    col = np.arange(QB)[None, :]
    rep = ((c < IDX_DIM) & (c == col % IDX_DIM)).astype(np.float32)
    col8 = np.arange(IDX_HEADS * BLK)[None, :]
    bcast = (c == W_IX_LANE + col8 // BLK).astype(np.float32)
    before = (col < c).astype(np.float32)
    return rep, bcast, before


def _lane_iota(shape):
    return lax.broadcasted_iota(jnp.int32, shape, len(shape) - 1)


def _row_iota(shape):
    return lax.broadcasted_iota(jnp.int32, shape, 0)


def _head_rmsnorm128(xf, gain):
    lo = _lane_iota((1, BLK)) < HEAD_DIM
    ss = xf * xf
    s_lo = jnp.sum(jnp.where(lo, ss, 0.0), axis=-1, keepdims=True)
    s_hi = jnp.sum(jnp.where(lo, 0.0, ss), axis=-1, keepdims=True)
    ms = jnp.where(lo, s_lo, s_hi) * (1.0 / HEAD_DIM)
    return xf * lax.rsqrt(ms + EPS) * gain


def _bias_tile(tab_ref, bk, head):
    far = tab_ref[N_BUCKETS - 1, head]
    acc = jnp.zeros(bk.shape, F32)
    for b in range(N_BUCKETS - 1):
        acc = jnp.where(bk == b, (tab_ref[b, head] - far) * LOG2E, acc)
    return acc


def _softmax_piece_t(st, vt, m_ref, l_ref, acc_ref):
    m_old = m_ref[0:1]
    m_new = jnp.maximum(m_old, jnp.max(st, axis=0, keepdims=True))
    alpha = jnp.exp2(m_old - m_new)
    p = jnp.exp2(st - m_new)
    l_ref[0:1] = alpha * l_ref[0:1] + jnp.sum(p, axis=0, keepdims=True)
    m_ref[0:1] = m_new
    acc_ref[...] = alpha * acc_ref[...] + _dot(vt, p.astype(BF16))


def _online_pieces_t(pieces, m_ref, l_ref, acc_ref):
    for st, vt in pieces:
        _softmax_piece_t(st, vt, m_ref, l_ref, acc_ref)


def _bounded_pieces_t(pieces, bound, l_ref, acc_ref):
    ps = [jnp.exp2(st - bound) for st, _ in pieces]
    l_ref[...] += functools.reduce(jnp.add, [jnp.sum(p.reshape(p.shape[0] // 8, 8, p.shape[1]), axis=0) for p in ps])
    acc_ref[...] += functools.reduce(jnp.add, [_dot(vt, p.astype(BF16)) for (_, vt), p in zip(pieces, ps)])


def _attend_windows(i, width, far_fn, near_fn, step):
    n_far = jnp.maximum(i - 1, 0)
    n_win = (n_far + width - 1) // width

    def far(p):
        hi = n_far - p * width
        return far_fn(jnp.maximum(hi - width, 0), hi)

    def body(g, carry):
        step([far(WINDOW_GROUP * g + n) for n in range(WINDOW_GROUP)])
        return carry

    lax.fori_loop(0, n_win // WINDOW_GROUP, body, 0)
    first_left = WINDOW_GROUP * (n_win // WINDOW_GROUP)
    for n_left in range(WINDOW_GROUP):
        @pl.when((n_win - first_left == n_left) & (i >= 1))
        def _(n_left=n_left):
            step([far(first_left + n) for n in range(n_left)] + [near_fn(i - 1, 2)])

    @pl.when(i == 0)
    def _():
        step([near_fn(0, 1)])


def _for_far_pieces(n_far, width, piece_fn):
    def body(p, carry):
        hi = n_far - p * width
        piece_fn(jnp.maximum(hi - width, 0), hi)
        return carry

    lax.fori_loop(0, (n_far + width - 1) // width, body, 0)


def _new_key_mask(start, hi, width):
    col = _lane_iota((1, width * QB))
    return jnp.where(col < (hi - start) * QB, 0.0, -jnp.inf)


def _in_proj_kernel(h_ref, g_ref, w_ref, o_ref, *, tn):
    x = h_ref[...]
    ms = jnp.mean(x * x, axis=-1, keepdims=True)
    u = (x * lax.rsqrt(ms + EPS) * g_ref[...]).astype(BF16)
    for c in range(o_ref.shape[1] // tn):
        cols = slice(c * tn, (c + 1) * tn)
        o_ref[:, cols] = _dot(u, w_ref[:, cols]).astype(o_ref.dtype)


def _in_proj(h, gain, w):
    m, d = h.shape
    n = w.shape[1]
    tm = _pick_rows(m, 512)
    return pl.pallas_call(
        functools.partial(_in_proj_kernel, tn=512),
        out_shape=jax.ShapeDtypeStruct((m, n), BF16),
        grid=(m // tm,),
        in_specs=[pl.BlockSpec((tm, d), lambda i: (i, 0)),
                  pl.BlockSpec((1, d), lambda i: (0, 0)),
                  pl.BlockSpec((d, n), lambda i: (0, 0), pipeline_mode=pl.Buffered(1))],
        out_specs=pl.BlockSpec((tm, n), lambda i: (i, 0)),
        compiler_params=_params("parallel"),
        name="in_proj",
    )(h, gain.reshape(1, d), w)


def _sb_kernel(q_ref, k_ref, v_ref, u_ref, o_ref, tot_ref, acc_ref):
    i = pl.program_id(1)
    nh = SB_HEADS
    head_of_lane = _lane_iota((QB, nh * HEAD_DIM)) // HEAD_DIM
    q = q_ref[0].astype(F32) * QK_SCALE
    qs = jnp.concatenate([jnp.where(head_of_lane == h, q, 0.0) for h in range(nh)], axis=0).astype(BF16)
    tot_ref[...] = jnp.zeros_like(tot_ref)
    acc_ref[...] = jnp.zeros_like(acc_ref)

    def piece(start, n_blocks, hi=None):
        r = pl.multiple_of(start * QB, QB)
        z_all = _nt_dot(qs, k_ref[0, pl.ds(r, n_blocks * QB), :])
        if hi is not None:
            z_all = z_all + _new_key_mask(start, hi, n_blocks)
        run = tot_ref[...]
        n_sub = n_blocks * QB // BLK
        ws = [None] * n_sub
        for c in reversed(range(n_sub)):
            z = z_all[:, c * BLK:(c + 1) * BLK]
            sp = jnp.maximum(z, 0.0) + jnp.log2(1.0 + jnp.exp2(-jnp.abs(z)))
            l1m = -sp
            if hi is None:
                mask = (_lane_iota(z.shape) + c * BLK) < (_row_iota(z.shape) & (QB - 1))
                l1m = jnp.where(mask, l1m, 0.0)
            l1m_hi = l1m.astype(BF16)
            l1m_lo = (l1m - l1m_hi.astype(F32)).astype(BF16)
            rs = _dot(jnp.concatenate([l1m_hi, l1m_lo], axis=1), u_ref[...])
            w = jnp.exp2((z - sp) + rs[:, :BLK] + run)
            if hi is None:
                w = jnp.where(mask, w, 0.0)
            ws[c] = w.astype(BF16)
            run = run + rs[:, BLK:]
        tot_ref[...] = run
        acc_ref[...] += _dot(jnp.concatenate(ws, axis=1), v_ref[0, pl.ds(r, n_blocks * QB), :])

    piece(i, 1)
    _for_far_pieces(i, SB_FAR, lambda start, hi: piece(start, SB_FAR, hi))

    a = acc_ref[...]
    out = a[:QB]
    for h in range(1, nh):
        out = jnp.where(head_of_lane == h, a[h * QB:(h + 1) * QB], out)
    o_ref[0] = out.astype(o_ref.dtype)


def _sb_attn(z3, u_mat):
    b, tp, _ = z3.shape
    nq = tp // QB
    assert nq >= SB_FAR
    w = SB_HEADS * HEAD_DIM
    return pl.pallas_call(
        _sb_kernel,
        out_shape=jax.ShapeDtypeStruct((b, tp, w), BF16),
        grid=(b, nq),
        in_specs=[pl.BlockSpec((1, QB, w), lambda bb, i: (bb, i, Q_SB // w)),
                  pl.BlockSpec((1, tp, w), lambda bb, i: (bb, 0, K_SB // w)),
                  pl.BlockSpec((1, tp, w), lambda bb, i: (bb, 0, V_SB // w)),
                  pl.BlockSpec((2 * BLK, 2 * BLK), lambda bb, i: (0, 0))],
        out_specs=pl.BlockSpec((1, QB, w), lambda bb, i: (bb, i, 0)),
        scratch_shapes=[pltpu.VMEM((SB_HEADS * QB, BLK), F32),
                        pltpu.VMEM((SB_HEADS * QB, w), F32)],
        compiler_params=_params("parallel", "arbitrary"),
        name="sb_attn",
    )(z3, z3, z3, u_mat)


def _dft_kernel(tab_ref, q_ref, k_ref, v_ref, qg_ref, kg_ref, bkt_ref, lamv_ref, sub_ref, o_ref,
                kn_ref, vt_ref, bias_ref, kmax_ref, m_ref, l_ref, lsum_ref, acc_ref, *, nq, lam_init, head_off):
    h = pl.program_id(1)
    i = pl.program_id(2)
    lo = _lane_iota((1, BLK)) < HEAD_DIM

    @pl.when(i == 0)
    def _prep():
        kmax_ref[...] = jnp.zeros_like(kmax_ref)

        def kbody(j, carry):
            r = pl.multiple_of(j * QB, QB)
            kf = k_ref[0, pl.ds(r, QB), :].astype(F32)
            kn = _head_rmsnorm128(kf, kg_ref[...]).astype(BF16)
            kn_ref[pl.ds(r, QB), :] = kn
            ksq = kn.astype(F32) ** 2
            half_norms = jnp.maximum(jnp.sum(jnp.where(lo, ksq, 0.0), axis=-1, keepdims=True),
                                     jnp.sum(jnp.where(lo, 0.0, ksq), axis=-1, keepdims=True))
            kmax_ref[...] = jnp.maximum(kmax_ref[...], jnp.max(half_norms))
            vt_ref[j] = v_ref[0, pl.ds(r, QB), :].astype(F32).T.astype(BF16)
            return carry

        lax.fori_loop(0, nq, kbody, 0)
        bias_ref[...] = _bias_tile(tab_ref, bkt_ref[...], head_off + h)

    qn = _head_rmsnorm128(q_ref[0].astype(F32), qg_ref[...]) * QK_SCALE
    qs = jnp.concatenate([jnp.where(lo, qn, 0.0), jnp.where(lo, 0.0, qn)], axis=0).astype(BF16)

    q_sq = _nt_dot(jnp.ones((8, BLK), BF16), (qs.astype(F32) ** 2).astype(BF16))[0:1]
    far_b = tab_ref[N_BUCKETS - 1, head_off + h]
    bias_max = jnp.float32(0.0)
    for b in range(N_BUCKETS - 1):
        bias_max = jnp.maximum(bias_max, (tab_ref[b, head_off + h] - far_b) * LOG2E)
    bound = jnp.sqrt(q_sq * kmax_ref[0:1, 0:1]) * BOUND_SLACK + (bias_max + BOUND_SLACK)

    def scores(first, n):
        return _nt_dot(kn_ref[pl.ds(pl.multiple_of(first * QB, QB), n * QB), :], qs)

    def values(first, n):
        return jnp.concatenate([vt_ref[first + c] for c in range(n)], axis=1)

    def far_piece(start, hi):
        st = scores(start, DF_FAR)
        new_keys = _row_iota(st.shape) < (hi - start) * QB
        return jnp.where(new_keys, st, -jnp.inf), values(start, DF_FAR)

    def near_piece(first, n):
        b = bias_ref[(2 - n) * QB:, :]
        st = scores(first, n) + jnp.concatenate([b, b], axis=1)
        causal = (_row_iota(st.shape) - (n - 1) * QB) <= (_lane_iota(st.shape) & (QB - 1))
        return jnp.where(causal, st, -jnp.inf), values(first, n)

    def attend(step):
        _attend_windows(i, DF_FAR, far_piece, near_piece, step)

    l_ref[...] = jnp.zeros_like(l_ref)
    acc_ref[...] = jnp.zeros_like(acc_ref)
    attend(lambda pieces: _bounded_pieces_t(pieces, bound, l_ref, acc_ref))
    lsum_ref[0:1] = jnp.sum(l_ref[...], axis=0, keepdims=True)

    @pl.when(jnp.min(lsum_ref[0:1]) < UNDERFLOW_GUARD)
    def _():
        m_ref[...] = jnp.full(m_ref.shape, M_INIT, F32)
        l_ref[...] = jnp.zeros_like(l_ref)
        acc_ref[...] = jnp.zeros_like(acc_ref)
        attend(lambda pieces: _online_pieces_t(pieces, m_ref, l_ref, acc_ref))
        lsum_ref[0:1] = l_ref[0:1]

    a = acc_ref[...] / lsum_ref[0:1]
    lv = lamv_ref[...]
    lam = (jnp.exp(jnp.sum(lv[0:1] * lv[1:2], axis=-1, keepdims=True))
           - jnp.exp(jnp.sum(lv[2:3] * lv[3:4], axis=-1, keepdims=True)) + lam_init)
    y = (a[:, :QB] - lam * a[:, QB:]).T
    y = y * lax.rsqrt(jnp.mean(y * y, axis=-1, keepdims=True) + EPS) * sub_ref[...]
    o_ref[0] = (y * (1.0 - lam_init)).astype(o_ref.dtype)


def _dft_attn(z3, rel_bias, qg, kg, bkt_t, lamv, subln, lam_init):
    b, tp, _ = z3.shape
    nq = tp // QB
    assert nq >= DF_FAR
    kern = functools.partial(_dft_kernel, nq=nq, lam_init=lam_init, head_off=SP_HEADS)
    vec = lambda bb, h, i: (0, 0)
    return pl.pallas_call(
        kern,
        out_shape=jax.ShapeDtypeStruct((b, tp, DF_HEADS * BLK), BF16),
        grid=(b, DF_HEADS, nq),
        in_specs=[pl.BlockSpec(memory_space=pltpu.SMEM),
                  pl.BlockSpec((1, QB, BLK), lambda bb, h, i: (bb, i, Q_DF // BLK + h)),
                  pl.BlockSpec((1, tp, BLK), lambda bb, h, i: (bb, 0, K_DF // BLK + h)),
                  pl.BlockSpec((1, tp, BLK), lambda bb, h, i: (bb, 0, V_DF // BLK + h)),
                  pl.BlockSpec((1, BLK), vec),
                  pl.BlockSpec((1, BLK), vec),
                  pl.BlockSpec((2 * QB, QB), vec),
                  pl.BlockSpec((8, BLK), vec),
                  pl.BlockSpec((1, BLK), vec)],
        out_specs=pl.BlockSpec((1, QB, BLK), lambda bb, h, i: (bb, i, h)),
        scratch_shapes=[pltpu.VMEM((tp, BLK), BF16),
                        pltpu.VMEM((nq, BLK, QB), BF16),
                        pltpu.VMEM((2 * QB, QB), F32),
                        pltpu.VMEM((8, BLK), F32),
                        pltpu.VMEM((8, 2 * QB), F32),
                        pltpu.VMEM((8, 2 * QB), F32),
                        pltpu.VMEM((8, 2 * QB), F32),
                        pltpu.VMEM((BLK, 2 * QB), F32)],
        compiler_params=_params("parallel", "parallel", "arbitrary"),
        name="df_attn",
    )(rel_bias, z3, z3, z3, qg, kg, bkt_t, lamv, subln)


def _sp_kernel(tab_ref, q_ref, k_ref, v_ref, qix_ref, kwq_ref, kwk_ref, qg_ref, kg_ref, bkt_ref,
               rep_ref, bcast_ref, before_ref, o_ref,
               kn_ref, kx_ref, vt_ref, bias_ref, kmax_ref, keys_ref, dig_ref, thr_ref, wb_ref,
               m_ref, l_ref, lsum_ref, acc_ref, *, nq, top_k):
    i = pl.program_id(1)
    nh = SP_HEADS
    w = nh * HEAD_DIM
    lane = _lane_iota((QB, w))
    head_of_lane = lane // HEAD_DIM

    def norm256(xf, g):
        return jnp.concatenate([_head_rmsnorm128(xf[:, :BLK], g[:, :BLK]),
                                _head_rmsnorm128(xf[:, BLK:], g[:, BLK:])], axis=1)

    def dup(x):
        return jnp.concatenate([x, x], axis=1)

    @pl.when(i == 0)
    def _prep():
        kmax_ref[...] = jnp.zeros_like(kmax_ref)

        def kbody(c, carry):
            r = pl.multiple_of(c * QB, QB)
            kn = norm256(k_ref[0, pl.ds(r, QB), :].astype(F32), kg_ref[...]).astype(BF16)
            kn_ref[pl.ds(r, QB), :] = kn
            ksq = kn.astype(F32) ** 2
            head_norms = functools.reduce(jnp.maximum, [
                jnp.sum(jnp.where(head_of_lane == h, ksq, 0.0), axis=-1, keepdims=True) for h in range(nh)])
            kmax_ref[0:1] = jnp.maximum(kmax_ref[0:1], jnp.max(head_norms))
            kx_ref[pl.ds(r, QB), :] = _dot(kwk_ref[0, pl.ds(r, QB), :], rep_ref[...]).astype(BF16)
            vt_ref[c] = v_ref[0, pl.ds(r, QB), :].astype(F32).T.astype(BF16)
            return carry

        lax.fori_loop(0, nq, kbody, 0)
        bias_max = jnp.float32(0.0)
        for h in range(nh):
            bias_ref[:, h * QB:(h + 1) * QB] = _bias_tile(tab_ref, bkt_ref[...], h)
            for b in range(N_BUCKETS - 1):
                bias_max = jnp.maximum(bias_max, (tab_ref[b, h] - tab_ref[N_BUCKETS - 1, h]) * LOG2E)
        kmax_ref[1:2] = jnp.full((1, BLK), bias_max, F32)

    key_causal = _row_iota((QB, QB)) <= _lane_iota((QB, QB))

    wb_ref[...] = _dot(kwq_ref[0], bcast_ref[...])
    qix = qix_ref[0].astype(F32)
    ix_head = lane // IDX_DIM
    qx = jnp.concatenate([jnp.where(ix_head == h, qix, 0.0) for h in range(IDX_HEADS)], axis=0).astype(BF16)

    def score_tiles(blocks, last_is_diagonal):
        dots = [_nt_dot(qx, kx_ref[pl.ds(pl.multiple_of(j * QB, QB), QB), :]) for j in blocks]
        for n, (j, d) in enumerate(zip(blocks, dots)):
            sc = jnp.zeros((QB, QB), F32)
            for h in range(IDX_HEADS):
                sc = sc + dup(wb_ref[:, h * BLK:(h + 1) * BLK]) * jnp.maximum(d[h * QB:(h + 1) * QB], 0.0)
            sc = jnp.where(sc == 0.0, 0.0, sc).T
            bits = lax.bitcast_convert_type(sc, jnp.int32)
            key = jnp.where(bits < 0, bits ^ jnp.int32(0x7FFFFFFF), bits)
            if last_is_diagonal and n == len(blocks) - 1:
                key = jnp.where(key_causal, key, jnp.int32(INT_MIN))
            keys_ref[j] = key
            dig_ref[j] = jnp.right_shift(key, 16).astype(jnp.int16)

    def group_body(g, c):
        score_tiles([SCORE_GROUP * g + n for n in range(SCORE_GROUP)], False)
        return c

    lax.fori_loop(0, i // SCORE_GROUP, group_body, 0)
    first_left = SCORE_GROUP * (i // SCORE_GROUP)
    for n_left in range(1, SCORE_GROUP + 1):
        @pl.when(i - first_left == n_left - 1)
        def _(n_left=n_left):
            score_tiles([first_left + n for n in range(n_left)], True)

    nblk = i + 1

    def count(pred):
        def cbody(j, c):
            hit = jnp.where(pred(keys_ref[j]), 1.0, 0.0)
            return c + jnp.sum(hit.reshape(QB // COUNT_ROWS, COUNT_ROWS, QB), axis=0)

        c = lax.fori_loop(0, nblk, cbody, jnp.zeros((COUNT_ROWS, QB), F32))
        return jnp.sum(c, axis=0, keepdims=True)

    def count16(pred):
        def cbody(j, c):
            hit = jnp.where(pred(dig_ref[j]), jnp.int16(1), jnp.int16(0)).reshape(QB // DIGIT_ROWS, DIGIT_ROWS, QB)
            return c + functools.reduce(jnp.add, [hit[r] for r in range(QB // DIGIT_ROWS)])

        c = lax.fori_loop(0, nblk, cbody, jnp.zeros((DIGIT_ROWS, QB), jnp.int16))
        return jnp.sum(c.astype(F32), axis=0, keepdims=True)

    n_all = jnp.zeros((1, QB), F32) + (nblk * QB).astype(F32)

    def bisect16(need):
        def bit_body(b, carry):
            cur, cnt_cur = carry
            cand = cur + jnp.left_shift(jnp.int32(1), 15 - b)
            cand16 = cand.astype(jnp.int16)
            cnt = count16(lambda d: d >= cand16)
            ok = cnt >= need
            return jnp.where(ok, cand, cur), jnp.where(ok, cnt, cnt_cur)

        return lax.fori_loop(0, 16, bit_body, (jnp.full((1, QB), I16_MIN, jnp.int32), n_all))

    t_hi, c_ge_hi = bisect16(float(top_k))
    t_hi16 = t_hi.astype(jnp.int16)
    c_gt_hi = count16(lambda d: d > t_hi16)
    base = jnp.left_shift(t_hi, 16)

    def low_digits(j, c):
        y = keys_ref[j] - base
        dig_ref[j] = jnp.where(jnp.right_shift(y, 16) == 0, y + I16_MIN, I16_MIN).astype(jnp.int16)
        return c

    lax.fori_loop(0, nblk, low_digits, 0)
    t_lo, c_ge_lo = bisect16(float(top_k) - c_gt_hi)
    thr = base + (t_lo - I16_MIN)
    cge = c_gt_hi + jnp.where(t_lo > I16_MIN, c_ge_lo, c_ge_hi - c_gt_hi)
    thr_ref[...] = jnp.broadcast_to(thr, thr_ref.shape)

    tie = jnp.where((cge > float(top_k)) & (thr > INT_MIN), 1, 0)

    @pl.when(jnp.max(tie) > 0)
    def _ties():
        need = float(top_k) - count(lambda k: k > thr)

        def tbody(j, run):
            kj = keys_ref[j]
            eq = kj == thr
            eqf = jnp.where(eq, 1.0, 0.0)
            rank = run + _dot(before_ref[...], eqf.astype(BF16))
            keep = jnp.where(kj > thr, 1, jnp.where(eq & (rank < need), 1, -1))
            keys_ref[j] = keep.astype(jnp.int32)
            return run + jnp.sum(eqf, axis=0, keepdims=True)

        lax.fori_loop(0, nblk, tbody, jnp.zeros((1, QB), F32))
        thr_ref[...] = jnp.zeros_like(thr_ref)

    qn = norm256(q_ref[0].astype(F32), qg_ref[...]) * QK_SCALE
    qs = jnp.concatenate([jnp.where(head_of_lane == h, qn, 0.0) for h in range(nh)], axis=0).astype(BF16)
    q_sq = _nt_dot(jnp.ones((8, w), BF16), (qs.astype(F32) ** 2).astype(BF16))[0:1]
    bound = jnp.sqrt(q_sq * kmax_ref[0:1, 0:1]) * BOUND_SLACK + (kmax_ref[1:2, 0:1] + BOUND_SLACK)

    thr_sel = thr_ref[0:1, :]

    def as_mask(sel):
        return lax.bitcast_convert_type(jnp.where(sel, 0.0, -jnp.inf), jnp.int32)

    def mask_body(j, c):
        keys_ref[j] = as_mask(keys_ref[j] >= thr_sel)
        return c

    lax.fori_loop(0, i, mask_body, 0)
    keys_ref[i] = as_mask((keys_ref[i] >= thr_sel) & key_causal)

    def scores(first, n):
        return _nt_dot(kn_ref[pl.ds(pl.multiple_of(first * QB, QB), n * QB), :], qs)

    def values(first, n):
        return jnp.concatenate([vt_ref[first + c] for c in range(n)], axis=1)

    def selection(first, n):
        return jnp.concatenate([lax.bitcast_convert_type(keys_ref[first + c], F32) for c in range(n)],
                               axis=0)

    def far_piece(start, hi):
        m = selection(start, SP_FAR)
        m = jnp.where(_row_iota(m.shape) < (hi - start) * QB, m, -jnp.inf)
        return scores(start, SP_FAR) + jnp.concatenate([m] * nh, axis=1), values(start, SP_FAR)

    def near_piece(first, n):
        m = selection(first, n)
        return (scores(first, n) + (jnp.concatenate([m] * nh, axis=1) + bias_ref[(2 - n) * QB:, :]),
                values(first, n))

    def attend(step):
        _attend_windows(i, SP_FAR, far_piece, near_piece, step)

    l_ref[...] = jnp.zeros_like(l_ref)
    acc_ref[...] = jnp.zeros_like(acc_ref)
    attend(lambda pieces: _bounded_pieces_t(pieces, bound, l_ref, acc_ref))
    lsum_ref[0:1] = jnp.sum(l_ref[...], axis=0, keepdims=True)

    @pl.when(jnp.min(lsum_ref[0:1]) < UNDERFLOW_GUARD)
    def _():
        m_ref[...] = jnp.full(m_ref.shape, M_INIT, F32)
        l_ref[...] = jnp.zeros_like(l_ref)
        acc_ref[...] = jnp.zeros_like(acc_ref)
        attend(lambda pieces: _online_pieces_t(pieces, m_ref, l_ref, acc_ref))
        lsum_ref[0:1] = l_ref[0:1]

    a = acc_ref[...] / lsum_ref[0:1]
    out_t = jnp.concatenate([a[h * HEAD_DIM:(h + 1) * HEAD_DIM, h * QB:(h + 1) * QB] for h in range(nh)], axis=0)
    o_ref[0] = out_t.T.astype(o_ref.dtype)


def _sp_attn(z3, rel_bias, qg, kg, bkt_t, rep, bcast, before, top_k):
    b, tp, _ = z3.shape
    nq = tp // QB
    assert nq >= SP_FAR
    w = SP_HEADS * HEAD_DIM
    kern = functools.partial(_sp_kernel, nq=nq, top_k=top_k)
    c2 = lambda bb, i: (0, 0)
    return pl.pallas_call(
        kern,
        out_shape=jax.ShapeDtypeStruct((b, tp, w), BF16),
        grid=(b, nq),
        in_specs=[pl.BlockSpec(memory_space=pltpu.SMEM),
                  pl.BlockSpec((1, QB, w), lambda bb, i: (bb, i, Q_SP // w)),
                  pl.BlockSpec((1, tp, w), lambda bb, i: (bb, 0, K_SP // w)),
                  pl.BlockSpec((1, tp, w), lambda bb, i: (bb, 0, V_SP // w)),
                  pl.BlockSpec((1, QB, w), lambda bb, i: (bb, i, Q_IX // w)),
                  pl.BlockSpec((1, QB, w), lambda bb, i: (bb, i, KW_IX // w)),
                  pl.BlockSpec((1, tp, w), lambda bb, i: (bb, 0, KW_IX // w)),
                  pl.BlockSpec((1, w), c2),
                  pl.BlockSpec((1, w), c2),
                  pl.BlockSpec((2 * QB, QB), c2),
                  pl.BlockSpec((w, w), c2),
                  pl.BlockSpec((w, IDX_HEADS * BLK), c2),
                  pl.BlockSpec((QB, QB), c2)],
        out_specs=pl.BlockSpec((1, QB, w), lambda bb, i: (bb, i, 0)),
        scratch_shapes=[pltpu.VMEM((tp, w), BF16),
                        pltpu.VMEM((tp, w), BF16),
                        pltpu.VMEM((nq, w, QB), BF16),
                        pltpu.VMEM((2 * QB, SP_HEADS * QB), F32),
                        pltpu.VMEM((8, BLK), F32),
                        pltpu.VMEM((nq, QB, QB), jnp.int32),
                        pltpu.VMEM((nq, QB, QB), jnp.int16),
                        pltpu.VMEM((8, QB), jnp.int32),
                        pltpu.VMEM((QB, IDX_HEADS * BLK), F32),
                        pltpu.VMEM((8, SP_HEADS * QB), F32),
                        pltpu.VMEM((8, SP_HEADS * QB), F32),
                        pltpu.VMEM((8, SP_HEADS * QB), F32),
                        pltpu.VMEM((w, SP_HEADS * QB), F32)],
        compiler_params=_params("parallel", "arbitrary"),
        name="sp_attn",
    )(rel_bias, z3, z3, z3, z3, z3, z3, qg, kg, bkt_t, rep, bcast, before)


def _mix_kernel(h_ref, gsb_ref, gsp_ref, gdf_ref, bg_ref, ysb_ref, ysp_ref, ydf_ref,
                wsb_ref, wsp_ref, wdf_ref, wo_ref, o_ref):
    def branch(g_ref, k, y_ref, w_ref):
        gate = jax.nn.sigmoid(g_ref[...].astype(F32) + bg_ref[:, k * D_MODEL:(k + 1) * D_MODEL])
        return gate * _dot(y_ref[...], w_ref[...])

    merged = (branch(gsb_ref, 0, ysb_ref, wsb_ref) + branch(gsp_ref, 1, ysp_ref, wsp_ref)
              + branch(gdf_ref, 2, ydf_ref, wdf_ref))
    o_ref[...] = h_ref[...] + _dot(merged.astype(BF16), wo_ref[...])


def _mix_out(h, z, b_gate, y_sb, y_sp, y_df, w_sb, w_sp, w_df, w_o):
    m, d = h.shape
    tm = _pick_rows(m, 512)
    row = lambda i: (i, 0)
    fixed = lambda i: (0, 0)
    return pl.pallas_call(
        _mix_kernel,
        out_shape=jax.ShapeDtypeStruct((m, d), F32),
        grid=(m // tm,),
        in_specs=[pl.BlockSpec((tm, d), row),
                  pl.BlockSpec((tm, d), lambda i: (i, G_SB // D_MODEL)),
                  pl.BlockSpec((tm, d), lambda i: (i, G_SP // D_MODEL)),
                  pl.BlockSpec((tm, d), lambda i: (i, G_DF // D_MODEL)),
                  pl.BlockSpec((1, 3 * d), fixed),
                  pl.BlockSpec((tm, y_sb.shape[1]), row),
                  pl.BlockSpec((tm, y_sp.shape[1]), row),
                  pl.BlockSpec((tm, y_df.shape[1]), row),
                  pl.BlockSpec(w_sb.shape, fixed),
                  pl.BlockSpec(w_sp.shape, fixed),
                  pl.BlockSpec(w_df.shape, fixed),
                  pl.BlockSpec(w_o.shape, fixed)],
        out_specs=pl.BlockSpec((tm, d), row),
        compiler_params=_params("parallel"),
        name="mix_out",
    )(h, z, z, z, b_gate.reshape(1, 3 * d), y_sb, y_sp, y_df, w_sb, w_sp, w_df, w_o)


def _ffn_kernel(h_ref, g_ref, wu_ref, cw_ref, cb_ref, wd_ref, o_ref, gbuf_ref, carry_ref, *, tm, tp, tf):
    r = pl.program_id(0)
    x = h_ref[...]
    ms = jnp.mean(x * x, axis=-1, keepdims=True)
    u = (x * lax.rsqrt(ms + EPS) * g_ref[...]).astype(BF16)

    @pl.when(r == 0)
    def _():
        carry_ref[...] = jnp.zeros_like(carry_ref)

    seq_start = lax.rem(tp - lax.rem(r * tm, tp), tp)
    local = lax.broadcasted_iota(jnp.int32, (tm, 1), 0)
    tap1 = local != seq_start
    tap2 = tap1 & (local != seq_start + 1)
    out = x
    for f in range(D_FF // tf):
        cols = slice(f * tf, (f + 1) * tf)
        gate = _dot(u, wu_ref[:, cols])
        val = _dot(u, wu_ref[:, D_FF + f * tf:D_FF + (f + 1) * tf])
        gbuf_ref[0:8] = carry_ref[f]
        gbuf_ref[8:8 + tm] = gate
        carry_ref[f] = gate[tm - 8:tm]
        g1 = jnp.where(tap1, gbuf_ref[7:7 + tm], 0.0)
        g2 = jnp.where(tap2, gbuf_ref[6:6 + tm], 0.0)
        conv = cb_ref[:, cols] + cw_ref[0:1, cols] * g2 + cw_ref[1:2, cols] * g1 + cw_ref[2:3, cols] * gate
        act = conv * jax.nn.sigmoid(conv) * val
        out = out + _dot(act.astype(BF16), wd_ref[cols, :])
    o_ref[...] = out


def _ffn(h, gain, w_up, conv_w, conv_b, w_down, tp):
    m, d = h.shape
    tm = _pick_rows(m, 512)
    assert tm <= tp
    tf = D_FF // 2
    nf = D_FF // tf
    kern = functools.partial(_ffn_kernel, tm=tm, tp=tp, tf=tf)
    fixed = lambda r: (0, 0)
    resident = pl.Buffered(1)
    return pl.pallas_call(
        kern,
        out_shape=jax.ShapeDtypeStruct((m, d), F32),
        grid=(m // tm,),
        in_specs=[pl.BlockSpec((tm, d), lambda r: (r, 0)),
                  pl.BlockSpec((1, d), fixed),
                  pl.BlockSpec((d, 2 * D_FF), fixed, pipeline_mode=resident),
                  pl.BlockSpec((8, D_FF), fixed),
                  pl.BlockSpec((1, D_FF), fixed),
                  pl.BlockSpec((D_FF, d), fixed, pipeline_mode=resident)],
        out_specs=pl.BlockSpec((tm, d), lambda r: (r, 0)),
        scratch_shapes=[pltpu.VMEM((tm + 8, tf), F32),
                        pltpu.VMEM((nf, 8, tf), F32)],
        compiler_params=_params("arbitrary"),
        name="conv_ffn",
    )(h, gain.reshape(1, d), w_up, conv_w, conv_b.reshape(1, D_FF), w_down)


def _permute_w_in(w):
    n_attn = KW_IX - Q_SB + IDX_DIM + IDX_HEADS
    n_gate = 3 * D_MODEL
    n_df = 3 * DF_HEADS * 2 * HEAD_DIM
    gates = w[:, n_attn + n_df:]
    attn = w[:, :n_attn]
    pad = jnp.zeros((w.shape[0], Q_DF - Q_SB - n_attn), w.dtype)
    df = w[:, n_attn:n_attn + n_df]
    out = jnp.concatenate([gates, attn, pad, df], axis=1)
    assert gates.shape[1] == n_gate and out.shape[1] == NZ
    return out


def kernel(x, meta_tokens, rel_bias, attn_norm, w_in, b_gate, q_norm_sp, k_norm_sp, q_norm_df, k_norm_df, lam_q1, lam_k1, lam_q2, lam_k2, subln_df, w_br_sb, w_br_sp, w_br_df, w_out, ffn_norm, w_up, conv_w, conv_b, w_down):
    b, s, d = x.shape
    depth = w_in.shape[0]
    t = N_META + s
    tp = -(-t // QB) * QB
    top_k = min(TOPK_MAX, t // 4)
    m = b * tp

    meta = jnp.broadcast_to(meta_tokens[None].astype(x.dtype), (b, N_META, d))
    h = jnp.concatenate([meta, x, jnp.zeros((b, tp - t, d), x.dtype)], axis=1).reshape(m, d)

    bkt_t = jnp.asarray(np.ascontiguousarray(_bucket_tile().T))
    u_mat = jnp.asarray(_sb_prefix_matrix(), BF16)
    rep, bcast, before = (jnp.asarray(a, BF16) for a in _ix_select_matrices())
    rel_bias = rel_bias.astype(F32)

    for l in range(depth):
        lam_init = 0.8 - 0.6 * math.exp(-0.3 * l)
        z = _in_proj(h, attn_norm[l], _permute_w_in(w_in[l]).astype(BF16))
        z3 = z.reshape(b, tp, NZ)
        y_sb = _sb_attn(z3, u_mat)
        y_sp = _sp_attn(z3, rel_bias,
                        jnp.tile(q_norm_sp[l].astype(F32), SP_HEADS).reshape(1, -1),
                        jnp.tile(k_norm_sp[l].astype(F32), SP_HEADS).reshape(1, -1),
                        bkt_t, rep, bcast, before, top_k)
        lamv = jnp.zeros((8, BLK), F32).at[:4, :HEAD_DIM].set(
            jnp.stack([lam_q1[l], lam_k1[l], lam_q2[l], lam_k2[l]]).astype(F32))
        y_df = _dft_attn(z3, rel_bias,
                         jnp.tile(q_norm_df[l].astype(F32), 2).reshape(1, -1),
                         jnp.tile(k_norm_df[l].astype(F32), 2).reshape(1, -1),
                         bkt_t, lamv, subln_df[l].astype(F32).reshape(1, -1), lam_init)
        h = _mix_out(h, z, b_gate[l], y_sb.reshape(m, -1), y_sp.reshape(m, -1), y_df.reshape(m, -1),
                     w_br_sb[l].astype(BF16), w_br_sp[l].astype(BF16), w_br_df[l].astype(BF16),
                     w_out[l].astype(BF16))
        cw = jnp.zeros((8, D_FF), F32).at[:conv_w.shape[1]].set(conv_w[l])
        h = _ffn(h, ffn_norm[l], w_up[l].astype(BF16), cw, conv_b[l], w_down[l].astype(BF16), tp)

    return h.reshape(b, tp, d)[:, N_META:t]
```

```python
import functools
import math

import numpy as np
import jax
import jax.numpy as jnp
from jax import lax
from jax.experimental import pallas as pl
from jax.experimental.pallas import tpu as pltpu

D_MODEL = 1024
HEAD_DIM = 64
N_META = 16
BLK = 128
QB = 256
SB_HEADS = 4
SP_HEADS = 4
IDX_HEADS = 8
IDX_DIM = 32
TOPK_MAX = 256
DF_HEADS = 4
N_BUCKETS = 32
MAX_DISTANCE = 128
D_FF = 2816
EPS = 1e-6
LOG2E = math.log2(math.e)
QK_SCALE = HEAD_DIM ** -0.5 * LOG2E
M_INIT = -1e30
BOUND_SLACK = 1.02
UNDERFLOW_GUARD = 2.0 ** -100
INT_MIN = -2 ** 31
I16_MIN = -2 ** 15
COUNT_ROWS = 32
DIGIT_ROWS = 64
SCORE_GROUP = 4
WINDOW_GROUP = 3
SB_FAR = 2
SP_FAR = 2
DF_FAR = 4

G_SB, G_SP, G_DF = 0, 1024, 2048
Q_SB, K_SB, V_SB = 3072, 3328, 3584
Q_SP, K_SP, V_SP = 3840, 4096, 4352
Q_IX, KW_IX = 4608, 4864
Q_DF, K_DF, V_DF = 5120, 5632, 6144
NZ = 6656
W_IX_LANE = IDX_DIM

VMEM_LIMIT = 56 * 1024 * 1024

F32 = jnp.float32
BF16 = jnp.bfloat16
NT_DIMS = (((1,), (1,)), ((), ()))


def _nt_dot(a, b):
    return lax.dot_general(a, b, NT_DIMS, preferred_element_type=F32)


def _dot(a, b):
    return jnp.dot(a, b, preferred_element_type=F32)


def _params(*sem):
    return pltpu.CompilerParams(dimension_semantics=sem, vmem_limit_bytes=VMEM_LIMIT)


def _pick_rows(m, cap):
    for c in (2048, 1024, 512, 256):
        if c <= cap and m % c == 0:
            return c
    raise ValueError(f"row count {m} is not a multiple of {QB}")


def _bucket_np(rel):
    n = np.maximum(rel, 0)
    max_exact = N_BUCKETS // 2
    nf = np.maximum(n, 1).astype(np.float32)
    large = max_exact + (np.log(nf / np.float32(max_exact)) / np.float32(math.log(MAX_DISTANCE / max_exact))
                         * np.float32(N_BUCKETS - max_exact)).astype(np.int32)
    return np.where(n < max_exact, n, np.minimum(large, N_BUCKETS - 1)).astype(np.int32)


def _bucket_tile():
    tq = np.arange(QB)[:, None]
    c = np.arange(2 * QB)[None, :]
    return _bucket_np(tq - c + QB)


def _sb_suffix_matrix():
    r = np.arange(2 * BLK)[:, None]
    key = np.arange(2 * BLK)[None, :] % BLK
    return np.where(r < BLK, key > r, True).astype(np.float32)


def _ix_select_matrices():
    c = np.arange(QB)[:, None]
    col = np.arange(QB)[None, :]
    rep = ((c < IDX_DIM) & (c == col % IDX_DIM)).astype(np.float32)
    col8 = np.arange(IDX_HEADS * BLK)[None, :]
    bcast = (c == W_IX_LANE + col8 // BLK).astype(np.float32)
    before = (col < c).astype(np.float32)
    return rep, bcast, before


def _lane_iota(shape):
    return lax.broadcasted_iota(jnp.int32, shape, len(shape) - 1)


def _row_iota(shape):
    return lax.broadcasted_iota(jnp.int32, shape, 0)


def _head_rmsnorm128(xf, gain):
    lo = _lane_iota((1, BLK)) < HEAD_DIM
    ss = xf * xf
    s_lo = jnp.sum(jnp.where(lo, ss, 0.0), axis=-1, keepdims=True)
    s_hi = jnp.sum(jnp.where(lo, 0.0, ss), axis=-1, keepdims=True)
    ms = jnp.where(lo, s_lo, s_hi) * (1.0 / HEAD_DIM)
    return xf * lax.rsqrt(ms + EPS) * gain


def _bias_tile(tab_ref, bk, head):
    far = tab_ref[N_BUCKETS - 1, head]
    acc = jnp.zeros(bk.shape, F32)
    for b in range(N_BUCKETS - 1):
        acc = jnp.where(bk == b, (tab_ref[b, head] - far) * LOG2E, acc)
    return acc


def _softmax_piece_t(st, vt, m_ref, l_ref, acc_ref):
    m_old = m_ref[0:1]
    m_new = jnp.maximum(m_old, jnp.max(st, axis=0, keepdims=True))
    alpha = jnp.exp2(m_old - m_new)
    p = jnp.exp2(st - m_new)
    l_ref[0:1] = alpha * l_ref[0:1] + jnp.sum(p, axis=0, keepdims=True)
    m_ref[0:1] = m_new
    acc_ref[...] = alpha * acc_ref[...] + _dot(vt, p.astype(BF16))


def _online_pieces_t(pieces, m_ref, l_ref, acc_ref):
    for st, vt in pieces:
        _softmax_piece_t(st, vt, m_ref, l_ref, acc_ref)


def _bounded_pieces_t(pieces, bound, l_ref, acc_ref):
    ps = [jnp.exp2(st - bound) for st, _ in pieces]
    l_ref[...] += functools.reduce(jnp.add, [jnp.sum(p.reshape(p.shape[0] // 8, 8, p.shape[1]), axis=0) for p in ps])
    acc_ref[...] += functools.reduce(jnp.add, [_dot(vt, p.astype(BF16)) for (_, vt), p in zip(pieces, ps)])


def _attend_windows(i, width, far_fn, near_fn, step):
    n_far = jnp.maximum(i - 1, 0)
    n_win = (n_far + width - 1) // width

    def far(p):
        hi = n_far - p * width
        return far_fn(jnp.maximum(hi - width, 0), hi)

    def body(g, carry):
        step([far(WINDOW_GROUP * g + n) for n in range(WINDOW_GROUP)])
        return carry

    lax.fori_loop(0, n_win // WINDOW_GROUP, body, 0)
    first_left = WINDOW_GROUP * (n_win // WINDOW_GROUP)
    for n_left in range(WINDOW_GROUP):
        @pl.when((n_win - first_left == n_left) & (i >= 1))
        def _(n_left=n_left):
            step([far(first_left + n) for n in range(n_left)] + [near_fn(i - 1, 2)])

    @pl.when(i == 0)
    def _():
        step([near_fn(0, 1)])


def _for_far_pieces(n_far, width, piece_fn):
    def body(p, carry):
        hi = n_far - p * width
        piece_fn(jnp.maximum(hi - width, 0), hi)
        return carry

    lax.fori_loop(0, (n_far + width - 1) // width, body, 0)


def _in_proj_kernel(h_ref, g_ref, w_ref, o_ref, *, tn):
    x = h_ref[...]
    ms = jnp.mean(x * x, axis=-1, keepdims=True)
    u = (x * lax.rsqrt(ms + EPS) * g_ref[...]).astype(BF16)
    for c in range(o_ref.shape[1] // tn):
        cols = slice(c * tn, (c + 1) * tn)
        o_ref[:, cols] = _dot(u, w_ref[:, cols]).astype(o_ref.dtype)


def _in_proj(h, gain, w):
    m, d = h.shape
    n = w.shape[1]
    tm = _pick_rows(m, 512)
    return pl.pallas_call(
        functools.partial(_in_proj_kernel, tn=512),
        out_shape=jax.ShapeDtypeStruct((m, n), BF16),
        grid=(m // tm,),
        in_specs=[pl.BlockSpec((tm, d), lambda i: (i, 0)),
                  pl.BlockSpec((1, d), lambda i: (0, 0)),
                  pl.BlockSpec((d, n), lambda i: (0, 0), pipeline_mode=pl.Buffered(1))],
        out_specs=pl.BlockSpec((tm, n), lambda i: (i, 0)),
        compiler_params=_params("parallel"),
        name="in_proj",
    )(h, gain.reshape(1, d), w)


def _sb_kernel(q_ref, k_ref, v_ref, u_ref, o_ref, vt_ref, run_ref, acc_ref, *, nq):
    i = pl.program_id(1)
    nh = SB_HEADS
    head_of_lane = _lane_iota((QB, nh * HEAD_DIM)) // HEAD_DIM

    @pl.when(i == 0)
    def _prep():
        def vbody(j, carry):
            vt_ref[j] = v_ref[0, pl.ds(pl.multiple_of(j * QB, QB), QB), :].astype(F32).T.astype(BF16)
            return carry

        lax.fori_loop(0, nq, vbody, 0)

    q = q_ref[0].astype(F32) * QK_SCALE
    qs = jnp.concatenate([jnp.where(head_of_lane == h, q, 0.0) for h in range(nh)], axis=0).astype(BF16)
    run_ref[...] = jnp.zeros_like(run_ref)
    acc_ref[...] = jnp.zeros_like(acc_ref)

    def piece(start, n_blocks, hi=None):
        r = pl.multiple_of(start * QB, QB)
        z_all = _nt_dot(k_ref[0, pl.ds(r, n_blocks * QB), :], qs)
        if hi is not None:
            covered = _row_iota(z_all.shape) >= (hi - start) * QB
            z_all = jnp.where(covered, -jnp.inf, z_all)
        run = run_ref[0:1]
        n_sub = n_blocks * QB // BLK
        ws = [None] * n_sub
        for c in reversed(range(n_sub)):
            z = z_all[c * BLK:(c + 1) * BLK]
            sp = jnp.maximum(z, 0.0) + jnp.log2(1.0 + jnp.exp2(-jnp.abs(z)))
            if hi is None:
                mask = (_row_iota(z.shape) + c * BLK) < (_lane_iota(z.shape) & (QB - 1))
                sp = jnp.where(mask, sp, 0.0)
            sp_hi = sp.astype(BF16)
            sp_lo = (sp - sp_hi.astype(F32)).astype(BF16)
            rs = _dot(u_ref[...], jnp.concatenate([sp_hi, sp_lo], axis=0))
            w = jnp.exp2((z - sp) - rs[:BLK] - run)
            if hi is None:
                w = jnp.where(mask, w, 0.0)
            ws[c] = w.astype(BF16)
            run = run + rs[BLK:BLK + 1]
        run_ref[0:1] = run
        vt = jnp.concatenate([vt_ref[start + c] for c in range(n_blocks)], axis=1)
        acc_ref[...] += _dot(vt, jnp.concatenate(ws, axis=0))

    piece(i, 1)
    _for_far_pieces(i, SB_FAR, lambda start, hi: piece(start, SB_FAR, hi))

    a = acc_ref[...]
    out_t = jnp.concatenate([a[h * HEAD_DIM:(h + 1) * HEAD_DIM, h * QB:(h + 1) * QB] for h in range(nh)], axis=0)
    o_ref[0] = out_t.T.astype(o_ref.dtype)


def _sb_attn(z3, u_mat):
    b, tp, _ = z3.shape
    nq = tp // QB
    assert nq >= SB_FAR
    w = SB_HEADS * HEAD_DIM
    return pl.pallas_call(
        functools.partial(_sb_kernel, nq=nq),
        out_shape=jax.ShapeDtypeStruct((b, tp, w), BF16),
        grid=(b, nq),
        in_specs=[pl.BlockSpec((1, QB, w), lambda bb, i: (bb, i, Q_SB // w)),
                  pl.BlockSpec((1, tp, w), lambda bb, i: (bb, 0, K_SB // w)),
                  pl.BlockSpec((1, tp, w), lambda bb, i: (bb, 0, V_SB // w)),
                  pl.BlockSpec((2 * BLK, 2 * BLK), lambda bb, i: (0, 0))],
        out_specs=pl.BlockSpec((1, QB, w), lambda bb, i: (bb, i, 0)),
        scratch_shapes=[pltpu.VMEM((nq, w, QB), BF16),
                        pltpu.VMEM((8, SB_HEADS * QB), F32),
                        pltpu.VMEM((w, SB_HEADS * QB), F32)],
        compiler_params=_params("parallel", "arbitrary"),
        name="sb_attn",
    )(z3, z3, z3, u_mat)


def _dft_kernel(tab_ref, q_ref, k_ref, v_ref, qg_ref, kg_ref, bkt_ref, lamv_ref, sub_ref, o_ref,
                kn_ref, vt_ref, bias_ref, kmax_ref, m_ref, l_ref, lsum_ref, acc_ref, *, nq, lam_init, head_off):
    h = pl.program_id(1)
    i = pl.program_id(2)
    lo = _lane_iota((1, BLK)) < HEAD_DIM

    @pl.when(i == 0)
    def _prep():
        kmax_ref[...] = jnp.zeros_like(kmax_ref)

        def kbody(j, carry):
            r = pl.multiple_of(j * QB, QB)
            kf = k_ref[0, pl.ds(r, QB), :].astype(F32)
            kn = _head_rmsnorm128(kf, kg_ref[...]).astype(BF16)
            kn_ref[pl.ds(r, QB), :] = kn
            ksq = kn.astype(F32) ** 2
            half_norms = jnp.maximum(jnp.sum(jnp.where(lo, ksq, 0.0), axis=-1, keepdims=True),
                                     jnp.sum(jnp.where(lo, 0.0, ksq), axis=-1, keepdims=True))
            kmax_ref[...] = jnp.maximum(kmax_ref[...], jnp.max(half_norms))
            vt_ref[j] = v_ref[0, pl.ds(r, QB), :].astype(F32).T.astype(BF16)
            return carry

        lax.fori_loop(0, nq, kbody, 0)
        bias_ref[...] = _bias_tile(tab_ref, bkt_ref[...], head_off + h)

    qn = _head_rmsnorm128(q_ref[0].astype(F32), qg_ref[...]) * QK_SCALE
    qs = jnp.concatenate([jnp.where(lo, qn, 0.0), jnp.where(lo, 0.0, qn)], axis=0).astype(BF16)

    q_sq = _nt_dot(jnp.ones((8, BLK), BF16), (qs.astype(F32) ** 2).astype(BF16))[0:1]
    far_b = tab_ref[N_BUCKETS - 1, head_off + h]
    bias_max = jnp.float32(0.0)
    for b in range(N_BUCKETS - 1):
        bias_max = jnp.maximum(bias_max, (tab_ref[b, head_off + h] - far_b) * LOG2E)
    bound = jnp.sqrt(q_sq * kmax_ref[0:1, 0:1]) * BOUND_SLACK + (bias_max + BOUND_SLACK)

    def scores(first, n):
        return _nt_dot(kn_ref[pl.ds(pl.multiple_of(first * QB, QB), n * QB), :], qs)

    def values(first, n):
        return jnp.concatenate([vt_ref[first + c] for c in range(n)], axis=1)

    def far_piece(start, hi):
        st = scores(start, DF_FAR)
        new_keys = _row_iota(st.shape) < (hi - start) * QB
        return jnp.where(new_keys, st, -jnp.inf), values(start, DF_FAR)

    def near_piece(first, n):
        b = bias_ref[(2 - n) * QB:, :]
        st = scores(first, n) + jnp.concatenate([b, b], axis=1)
        causal = (_row_iota(st.shape) - (n - 1) * QB) <= (_lane_iota(st.shape) & (QB - 1))
        return jnp.where(causal, st, -jnp.inf), values(first, n)

    def attend(step):
        _attend_windows(i, DF_FAR, far_piece, near_piece, step)

    l_ref[...] = jnp.zeros_like(l_ref)
    acc_ref[...] = jnp.zeros_like(acc_ref)
    attend(lambda pieces: _bounded_pieces_t(pieces, bound, l_ref, acc_ref))
    lsum_ref[0:1] = jnp.sum(l_ref[...], axis=0, keepdims=True)

    @pl.when(jnp.min(lsum_ref[0:1]) < UNDERFLOW_GUARD)
    def _():
        m_ref[...] = jnp.full(m_ref.shape, M_INIT, F32)
        l_ref[...] = jnp.zeros_like(l_ref)
        acc_ref[...] = jnp.zeros_like(acc_ref)
        attend(lambda pieces: _online_pieces_t(pieces, m_ref, l_ref, acc_ref))
        lsum_ref[0:1] = l_ref[0:1]

    a = acc_ref[...] / lsum_ref[0:1]
    lv = lamv_ref[...]
    lam = (jnp.exp(jnp.sum(lv[0:1] * lv[1:2], axis=-1, keepdims=True))
           - jnp.exp(jnp.sum(lv[2:3] * lv[3:4], axis=-1, keepdims=True)) + lam_init)
    y = (a[:, :QB] - lam * a[:, QB:]).T
    y = y * lax.rsqrt(jnp.mean(y * y, axis=-1, keepdims=True) + EPS) * sub_ref[...]
    o_ref[0] = (y * (1.0 - lam_init)).astype(o_ref.dtype)


def _dft_attn(z3, rel_bias, qg, kg, bkt_t, lamv, subln, lam_init):
    b, tp, _ = z3.shape
    nq = tp // QB
    assert nq >= DF_FAR
    kern = functools.partial(_dft_kernel, nq=nq, lam_init=lam_init, head_off=SP_HEADS)
    vec = lambda bb, h, i: (0, 0)
    return pl.pallas_call(
        kern,
        out_shape=jax.ShapeDtypeStruct((b, tp, DF_HEADS * BLK), BF16),
        grid=(b, DF_HEADS, nq),
        in_specs=[pl.BlockSpec(memory_space=pltpu.SMEM),
                  pl.BlockSpec((1, QB, BLK), lambda bb, h, i: (bb, i, Q_DF // BLK + h)),
                  pl.BlockSpec((1, tp, BLK), lambda bb, h, i: (bb, 0, K_DF // BLK + h)),
                  pl.BlockSpec((1, tp, BLK), lambda bb, h, i: (bb, 0, V_DF // BLK + h)),
                  pl.BlockSpec((1, BLK), vec),
                  pl.BlockSpec((1, BLK), vec),
                  pl.BlockSpec((2 * QB, QB), vec),
                  pl.BlockSpec((8, BLK), vec),
                  pl.BlockSpec((1, BLK), vec)],
        out_specs=pl.BlockSpec((1, QB, BLK), lambda bb, h, i: (bb, i, h)),
        scratch_shapes=[pltpu.VMEM((tp, BLK), BF16),
                        pltpu.VMEM((nq, BLK, QB), BF16),
                        pltpu.VMEM((2 * QB, QB), F32),
                        pltpu.VMEM((8, BLK), F32),
                        pltpu.VMEM((8, 2 * QB), F32),
                        pltpu.VMEM((8, 2 * QB), F32),
                        pltpu.VMEM((8, 2 * QB), F32),
                        pltpu.VMEM((BLK, 2 * QB), F32)],
        compiler_params=_params("parallel", "parallel", "arbitrary"),
        name="df_attn",
    )(rel_bias, z3, z3, z3, qg, kg, bkt_t, lamv, subln)


def _sp_kernel(tab_ref, q_ref, k_ref, v_ref, qix_ref, kwq_ref, kwk_ref, qg_ref, kg_ref, bkt_ref,
               rep_ref, bcast_ref, before_ref, o_ref,
               kn_ref, kx_ref, vt_ref, bias_ref, kmax_ref, keys_ref, dig_ref, thr_ref, wb_ref,
               m_ref, l_ref, lsum_ref, acc_ref, *, nq, top_k):
    i = pl.program_id(1)
    nh = SP_HEADS
    w = nh * HEAD_DIM
    lane = _lane_iota((QB, w))
    head_of_lane = lane // HEAD_DIM

    def norm256(xf, g):
        return jnp.concatenate([_head_rmsnorm128(xf[:, :BLK], g[:, :BLK]),
                                _head_rmsnorm128(xf[:, BLK:], g[:, BLK:])], axis=1)

    def dup(x):
        return jnp.concatenate([x, x], axis=1)

    @pl.when(i == 0)
    def _prep():
        kmax_ref[...] = jnp.zeros_like(kmax_ref)

        def kbody(c, carry):
            r = pl.multiple_of(c * QB, QB)
            kn = norm256(k_ref[0, pl.ds(r, QB), :].astype(F32), kg_ref[...]).astype(BF16)
            kn_ref[pl.ds(r, QB), :] = kn
            ksq = kn.astype(F32) ** 2
            head_norms = functools.reduce(jnp.maximum, [
                jnp.sum(jnp.where(head_of_lane == h, ksq, 0.0), axis=-1, keepdims=True) for h in range(nh)])
            kmax_ref[0:1] = jnp.maximum(kmax_ref[0:1], jnp.max(head_norms))
            kx_ref[pl.ds(r, QB), :] = _dot(kwk_ref[0, pl.ds(r, QB), :], rep_ref[...]).astype(BF16)
            vt_ref[c] = v_ref[0, pl.ds(r, QB), :].astype(F32).T.astype(BF16)
            return carry

        lax.fori_loop(0, nq, kbody, 0)
        bias_max = jnp.float32(0.0)
        for h in range(nh):
            bias_ref[:, h * QB:(h + 1) * QB] = _bias_tile(tab_ref, bkt_ref[...], h)
            for b in range(N_BUCKETS - 1):
                bias_max = jnp.maximum(bias_max, (tab_ref[b, h] - tab_ref[N_BUCKETS - 1, h]) * LOG2E)
        kmax_ref[1:2] = jnp.full((1, BLK), bias_max, F32)

    key_causal = _row_iota((QB, QB)) <= _lane_iota((QB, QB))

    wb_ref[...] = _dot(kwq_ref[0], bcast_ref[...])
    qix = qix_ref[0].astype(F32)
    ix_head = lane // IDX_DIM
    qx = jnp.concatenate([jnp.where(ix_head == h, qix, 0.0) for h in range(IDX_HEADS)], axis=0).astype(BF16)

    def score_tiles(blocks, last_is_diagonal):
        dots = [_nt_dot(qx, kx_ref[pl.ds(pl.multiple_of(j * QB, QB), QB), :]) for j in blocks]
        for n, (j, d) in enumerate(zip(blocks, dots)):
            sc = jnp.zeros((QB, QB), F32)
            for h in range(IDX_HEADS):
                sc = sc + dup(wb_ref[:, h * BLK:(h + 1) * BLK]) * jnp.maximum(d[h * QB:(h + 1) * QB], 0.0)
            sc = jnp.where(sc == 0.0, 0.0, sc).T
            bits = lax.bitcast_convert_type(sc, jnp.int32)
            key = jnp.where(bits < 0, bits ^ jnp.int32(0x7FFFFFFF), bits)
            if last_is_diagonal and n == len(blocks) - 1:
                key = jnp.where(key_causal, key, jnp.int32(INT_MIN))
            keys_ref[j] = key
            dig_ref[j] = jnp.right_shift(key, 16).astype(jnp.int16)

    def group_body(g, c):
        score_tiles([SCORE_GROUP * g + n for n in range(SCORE_GROUP)], False)
        return c

    lax.fori_loop(0, i // SCORE_GROUP, group_body, 0)
    first_left = SCORE_GROUP * (i // SCORE_GROUP)
    for n_left in range(1, SCORE_GROUP + 1):
        @pl.when(i - first_left == n_left - 1)
        def _(n_left=n_left):
            score_tiles([first_left + n for n in range(n_left)], True)

    nblk = i + 1

    def count(pred):
        def cbody(j, c):
            hit = jnp.where(pred(keys_ref[j]), 1.0, 0.0)
            return c + jnp.sum(hit.reshape(QB // COUNT_ROWS, COUNT_ROWS, QB), axis=0)

        c = lax.fori_loop(0, nblk, cbody, jnp.zeros((COUNT_ROWS, QB), F32))
        return jnp.sum(c, axis=0, keepdims=True)

    def count16(pred):
        def cbody(j, c):
            hit = jnp.where(pred(dig_ref[j]), jnp.int16(1), jnp.int16(0)).reshape(QB // DIGIT_ROWS, DIGIT_ROWS, QB)
            return c + functools.reduce(jnp.add, [hit[r] for r in range(QB // DIGIT_ROWS)])

        c = lax.fori_loop(0, nblk, cbody, jnp.zeros((DIGIT_ROWS, QB), jnp.int16))
        return jnp.sum(c.astype(F32), axis=0, keepdims=True)

    n_all = jnp.zeros((1, QB), F32) + (nblk * QB).astype(F32)

    def bisect16(need):
        def bit_body(b, carry):
            cur, cnt_cur = carry
            cand = cur + jnp.left_shift(jnp.int32(1), 15 - b)
            cand16 = cand.astype(jnp.int16)
            cnt = count16(lambda d: d >= cand16)
            ok = cnt >= need
            return jnp.where(ok, cand, cur), jnp.where(ok, cnt, cnt_cur)

        return lax.fori_loop(0, 16, bit_body, (jnp.full((1, QB), I16_MIN, jnp.int32), n_all))

    t_hi, c_ge_hi = bisect16(float(top_k))
    t_hi16 = t_hi.astype(jnp.int16)
    c_gt_hi = count16(lambda d: d > t_hi16)
    base = jnp.left_shift(t_hi, 16)

    def low_digits(j, c):
        y = keys_ref[j] - base
        dig_ref[j] = jnp.where(jnp.right_shift(y, 16) == 0, y + I16_MIN, I16_MIN).astype(jnp.int16)
        return c

    lax.fori_loop(0, nblk, low_digits, 0)
    t_lo, c_ge_lo = bisect16(float(top_k) - c_gt_hi)
    thr = base + (t_lo - I16_MIN)
    cge = c_gt_hi + jnp.where(t_lo > I16_MIN, c_ge_lo, c_ge_hi - c_gt_hi)
    thr_ref[...] = jnp.broadcast_to(thr, thr_ref.shape)

    tie = jnp.where((cge > float(top_k)) & (thr > INT_MIN), 1, 0)

    @pl.when(jnp.max(tie) > 0)
    def _ties():
        need = float(top_k) - count(lambda k: k > thr)

        def tbody(j, run):
            kj = keys_ref[j]
            eq = kj == thr
            eqf = jnp.where(eq, 1.0, 0.0)
            rank = run + _dot(before_ref[...], eqf.astype(BF16))
            keep = jnp.where(kj > thr, 1, jnp.where(eq & (rank < need), 1, -1))
            keys_ref[j] = keep.astype(jnp.int32)
            return run + jnp.sum(eqf, axis=0, keepdims=True)

        lax.fori_loop(0, nblk, tbody, jnp.zeros((1, QB), F32))
        thr_ref[...] = jnp.zeros_like(thr_ref)

    qn = norm256(q_ref[0].astype(F32), qg_ref[...]) * QK_SCALE
    qs = jnp.concatenate([jnp.where(head_of_lane == h, qn, 0.0) for h in range(nh)], axis=0).astype(BF16)
    q_sq = _nt_dot(jnp.ones((8, w), BF16), (qs.astype(F32) ** 2).astype(BF16))[0:1]
    bound = jnp.sqrt(q_sq * kmax_ref[0:1, 0:1]) * BOUND_SLACK + (kmax_ref[1:2, 0:1] + BOUND_SLACK)

    thr_sel = thr_ref[0:1, :]

    def as_mask(sel):
        return lax.bitcast_convert_type(jnp.where(sel, 0.0, -jnp.inf), jnp.int32)

    def mask_body(j, c):
        keys_ref[j] = as_mask(keys_ref[j] >= thr_sel)
        return c

    lax.fori_loop(0, i, mask_body, 0)
    keys_ref[i] = as_mask((keys_ref[i] >= thr_sel) & key_causal)

    def scores(first, n):
        return _nt_dot(kn_ref[pl.ds(pl.multiple_of(first * QB, QB), n * QB), :], qs)

    def values(first, n):
        return jnp.concatenate([vt_ref[first + c] for c in range(n)], axis=1)

    def selection(first, n):
        return jnp.concatenate([lax.bitcast_convert_type(keys_ref[first + c], F32) for c in range(n)],
                               axis=0)

    def far_piece(start, hi):
        m = selection(start, SP_FAR)
        m = jnp.where(_row_iota(m.shape) < (hi - start) * QB, m, -jnp.inf)
        return scores(start, SP_FAR) + jnp.concatenate([m] * nh, axis=1), values(start, SP_FAR)

    def near_piece(first, n):
        m = selection(first, n)
        return (scores(first, n) + (jnp.concatenate([m] * nh, axis=1) + bias_ref[(2 - n) * QB:, :]),
                values(first, n))

    def attend(step):
        _attend_windows(i, SP_FAR, far_piece, near_piece, step)

    l_ref[...] = jnp.zeros_like(l_ref)
    acc_ref[...] = jnp.zeros_like(acc_ref)
    attend(lambda pieces: _bounded_pieces_t(pieces, bound, l_ref, acc_ref))
    lsum_ref[0:1] = jnp.sum(l_ref[...], axis=0, keepdims=True)

    @pl.when(jnp.min(lsum_ref[0:1]) < UNDERFLOW_GUARD)
    def _():
        m_ref[...] = jnp.full(m_ref.shape, M_INIT, F32)
        l_ref[...] = jnp.zeros_like(l_ref)
        acc_ref[...] = jnp.zeros_like(acc_ref)
        attend(lambda pieces: _online_pieces_t(pieces, m_ref, l_ref, acc_ref))
        lsum_ref[0:1] = l_ref[0:1]

    a = acc_ref[...] / lsum_ref[0:1]
    out_t = jnp.concatenate([a[h * HEAD_DIM:(h + 1) * HEAD_DIM, h * QB:(h + 1) * QB] for h in range(nh)], axis=0)
    o_ref[0] = out_t.T.astype(o_ref.dtype)


def _sp_attn(z3, rel_bias, qg, kg, bkt_t, rep, bcast, before, top_k):
    b, tp, _ = z3.shape
    nq = tp // QB
    assert nq >= SP_FAR
    w = SP_HEADS * HEAD_DIM
    kern = functools.partial(_sp_kernel, nq=nq, top_k=top_k)
    c2 = lambda bb, i: (0, 0)
    return pl.pallas_call(
        kern,
        out_shape=jax.ShapeDtypeStruct((b, tp, w), BF16),
        grid=(b, nq),
        in_specs=[pl.BlockSpec(memory_space=pltpu.SMEM),
                  pl.BlockSpec((1, QB, w), lambda bb, i: (bb, i, Q_SP // w)),
                  pl.BlockSpec((1, tp, w), lambda bb, i: (bb, 0, K_SP // w)),
                  pl.BlockSpec((1, tp, w), lambda bb, i: (bb, 0, V_SP // w)),
                  pl.BlockSpec((1, QB, w), lambda bb, i: (bb, i, Q_IX // w)),
                  pl.BlockSpec((1, QB, w), lambda bb, i: (bb, i, KW_IX // w)),
                  pl.BlockSpec((1, tp, w), lambda bb, i: (bb, 0, KW_IX // w)),
                  pl.BlockSpec((1, w), c2),
                  pl.BlockSpec((1, w), c2),
                  pl.BlockSpec((2 * QB, QB), c2),
                  pl.BlockSpec((w, w), c2),
                  pl.BlockSpec((w, IDX_HEADS * BLK), c2),
                  pl.BlockSpec((QB, QB), c2)],
        out_specs=pl.BlockSpec((1, QB, w), lambda bb, i: (bb, i, 0)),
        scratch_shapes=[pltpu.VMEM((tp, w), BF16),
                        pltpu.VMEM((tp, w), BF16),
                        pltpu.VMEM((nq, w, QB), BF16),
                        pltpu.VMEM((2 * QB, SP_HEADS * QB), F32),
                        pltpu.VMEM((8, BLK), F32),
                        pltpu.VMEM((nq, QB, QB), jnp.int32),
                        pltpu.VMEM((nq, QB, QB), jnp.int16),
                        pltpu.VMEM((8, QB), jnp.int32),
                        pltpu.VMEM((QB, IDX_HEADS * BLK), F32),
                        pltpu.VMEM((8, SP_HEADS * QB), F32),
                        pltpu.VMEM((8, SP_HEADS * QB), F32),
                        pltpu.VMEM((8, SP_HEADS * QB), F32),
                        pltpu.VMEM((w, SP_HEADS * QB), F32)],
        compiler_params=_params("parallel", "arbitrary"),
        name="sp_attn",
    )(rel_bias, z3, z3, z3, z3, z3, z3, qg, kg, bkt_t, rep, bcast, before)


def _mix_kernel(h_ref, gsb_ref, gsp_ref, gdf_ref, bg_ref, ysb_ref, ysp_ref, ydf_ref,
                wsb_ref, wsp_ref, wdf_ref, wo_ref, o_ref):
    def branch(g_ref, k, y_ref, w_ref):
        gate = jax.nn.sigmoid(g_ref[...].astype(F32) + bg_ref[:, k * D_MODEL:(k + 1) * D_MODEL])
        return gate * _dot(y_ref[...], w_ref[...])

    merged = (branch(gsb_ref, 0, ysb_ref, wsb_ref) + branch(gsp_ref, 1, ysp_ref, wsp_ref)
              + branch(gdf_ref, 2, ydf_ref, wdf_ref))
    o_ref[...] = h_ref[...] + _dot(merged.astype(BF16), wo_ref[...])


def _mix_out(h, z, b_gate, y_sb, y_sp, y_df, w_sb, w_sp, w_df, w_o):
    m, d = h.shape
    tm = _pick_rows(m, 512)
    row = lambda i: (i, 0)
    fixed = lambda i: (0, 0)
    return pl.pallas_call(
        _mix_kernel,
        out_shape=jax.ShapeDtypeStruct((m, d), F32),
        grid=(m // tm,),
        in_specs=[pl.BlockSpec((tm, d), row),
                  pl.BlockSpec((tm, d), lambda i: (i, G_SB // D_MODEL)),
                  pl.BlockSpec((tm, d), lambda i: (i, G_SP // D_MODEL)),
                  pl.BlockSpec((tm, d), lambda i: (i, G_DF // D_MODEL)),
                  pl.BlockSpec((1, 3 * d), fixed),
                  pl.BlockSpec((tm, y_sb.shape[1]), row),
                  pl.BlockSpec((tm, y_sp.shape[1]), row),
                  pl.BlockSpec((tm, y_df.shape[1]), row),
                  pl.BlockSpec(w_sb.shape, fixed),
                  pl.BlockSpec(w_sp.shape, fixed),
                  pl.BlockSpec(w_df.shape, fixed),
                  pl.BlockSpec(w_o.shape, fixed)],
        out_specs=pl.BlockSpec((tm, d), row),
        compiler_params=_params("parallel"),
        name="mix_out",
    )(h, z, z, z, b_gate.reshape(1, 3 * d), y_sb, y_sp, y_df, w_sb, w_sp, w_df, w_o)


def _ffn_kernel(h_ref, g_ref, wu_ref, cw_ref, cb_ref, wd_ref, o_ref, gbuf_ref, carry_ref, *, tm, tp, tf):
    r = pl.program_id(0)
    x = h_ref[...]
    ms = jnp.mean(x * x, axis=-1, keepdims=True)
    u = (x * lax.rsqrt(ms + EPS) * g_ref[...]).astype(BF16)

    @pl.when(r == 0)
    def _():
        carry_ref[...] = jnp.zeros_like(carry_ref)

    seq_start = lax.rem(tp - lax.rem(r * tm, tp), tp)
    local = lax.broadcasted_iota(jnp.int32, (tm, 1), 0)
    tap1 = local != seq_start
    tap2 = tap1 & (local != seq_start + 1)
    out = x
    for f in range(D_FF // tf):
        cols = slice(f * tf, (f + 1) * tf)
        gate = _dot(u, wu_ref[:, cols])
        val = _dot(u, wu_ref[:, D_FF + f * tf:D_FF + (f + 1) * tf])
        gbuf_ref[0:8] = carry_ref[f]
        gbuf_ref[8:8 + tm] = gate
        carry_ref[f] = gate[tm - 8:tm]
        g1 = jnp.where(tap1, gbuf_ref[7:7 + tm], 0.0)
        g2 = jnp.where(tap2, gbuf_ref[6:6 + tm], 0.0)
        conv = cb_ref[:, cols] + cw_ref[0:1, cols] * g2 + cw_ref[1:2, cols] * g1 + cw_ref[2:3, cols] * gate
        act = conv * jax.nn.sigmoid(conv) * val
        out = out + _dot(act.astype(BF16), wd_ref[cols, :])
    o_ref[...] = out


def _ffn(h, gain, w_up, conv_w, conv_b, w_down, tp):
    m, d = h.shape
    tm = _pick_rows(m, 512)
    assert tm <= tp
    tf = D_FF // 2
    nf = D_FF // tf
    kern = functools.partial(_ffn_kernel, tm=tm, tp=tp, tf=tf)
    fixed = lambda r: (0, 0)
    resident = pl.Buffered(1)
    return pl.pallas_call(
        kern,
        out_shape=jax.ShapeDtypeStruct((m, d), F32),
        grid=(m // tm,),
        in_specs=[pl.BlockSpec((tm, d), lambda r: (r, 0)),
                  pl.BlockSpec((1, d), fixed),
                  pl.BlockSpec((d, 2 * D_FF), fixed, pipeline_mode=resident),
                  pl.BlockSpec((8, D_FF), fixed),
                  pl.BlockSpec((1, D_FF), fixed),
                  pl.BlockSpec((D_FF, d), fixed, pipeline_mode=resident)],
        out_specs=pl.BlockSpec((tm, d), lambda r: (r, 0)),
        scratch_shapes=[pltpu.VMEM((tm + 8, tf), F32),
                        pltpu.VMEM((nf, 8, tf), F32)],
        compiler_params=_params("arbitrary"),
        name="conv_ffn",
    )(h, gain.reshape(1, d), w_up, conv_w, conv_b.reshape(1, D_FF), w_down)


def _permute_w_in(w):
    n_attn = KW_IX - Q_SB + IDX_DIM + IDX_HEADS
    n_gate = 3 * D_MODEL
    n_df = 3 * DF_HEADS * 2 * HEAD_DIM
    gates = w[:, n_attn + n_df:]
    attn = w[:, :n_attn]
    pad = jnp.zeros((w.shape[0], Q_DF - Q_SB - n_attn), w.dtype)
    df = w[:, n_attn:n_attn + n_df]
    out = jnp.concatenate([gates, attn, pad, df], axis=1)
    assert gates.shape[1] == n_gate and out.shape[1] == NZ
    return out


def kernel(x, meta_tokens, rel_bias, attn_norm, w_in, b_gate, q_norm_sp, k_norm_sp, q_norm_df, k_norm_df, lam_q1, lam_k1, lam_q2, lam_k2, subln_df, w_br_sb, w_br_sp, w_br_df, w_out, ffn_norm, w_up, conv_w, conv_b, w_down):
    b, s, d = x.shape
    depth = w_in.shape[0]
    t = N_META + s
    tp = -(-t // QB) * QB
    top_k = min(TOPK_MAX, t // 4)
    m = b * tp

    meta = jnp.broadcast_to(meta_tokens[None].astype(x.dtype), (b, N_META, d))
    h = jnp.concatenate([meta, x, jnp.zeros((b, tp - t, d), x.dtype)], axis=1).reshape(m, d)

    bkt_t = jnp.asarray(np.ascontiguousarray(_bucket_tile().T))
    u_mat = jnp.asarray(_sb_suffix_matrix(), BF16)
    rep, bcast, before = (jnp.asarray(a, BF16) for a in _ix_select_matrices())
    rel_bias = rel_bias.astype(F32)

    for l in range(depth):
        lam_init = 0.8 - 0.6 * math.exp(-0.3 * l)
        z = _in_proj(h, attn_norm[l], _permute_w_in(w_in[l]).astype(BF16))
        z3 = z.reshape(b, tp, NZ)
        y_sb = _sb_attn(z3, u_mat)
        y_sp = _sp_attn(z3, rel_bias,
                        jnp.tile(q_norm_sp[l].astype(F32), SP_HEADS).reshape(1, -1),
                        jnp.tile(k_norm_sp[l].astype(F32), SP_HEADS).reshape(1, -1),
                        bkt_t, rep, bcast, before, top_k)
        lamv = jnp.zeros((8, BLK), F32).at[:4, :HEAD_DIM].set(
            jnp.stack([lam_q1[l], lam_k1[l], lam_q2[l], lam_k2[l]]).astype(F32))
        y_df = _dft_attn(z3, rel_bias,
                         jnp.tile(q_norm_df[l].astype(F32), 2).reshape(1, -1),
                         jnp.tile(k_norm_df[l].astype(F32), 2).reshape(1, -1),
                         bkt_t, lamv, subln_df[l].astype(F32).reshape(1, -1), lam_init)
        h = _mix_out(h, z, b_gate[l], y_sb.reshape(m, -1), y_sp.reshape(m, -1), y_df.reshape(m, -1),
                     w_br_sb[l].astype(BF16), w_br_sp[l].astype(BF16), w_br_df[l].astype(BF16),
                     w_out[l].astype(BF16))
        cw = jnp.zeros((8, D_FF), F32).at[:conv_w.shape[1]].set(conv_w[l])
        h = _ffn(h, ffn_norm[l], w_up[l].astype(BF16), cw, conv_b[l], w_down[l].astype(BF16), tp)

    return h.reshape(b, tp, d)[:, N_META:t]
```

```python
import functools
import math

import numpy as np
import jax
import jax.numpy as jnp
from jax import lax
from jax.experimental import pallas as pl
from jax.experimental.pallas import tpu as pltpu

D_MODEL = 1024
HEAD_DIM = 64
N_META = 16
BLK = 128
QB = 256
SB_HEADS = 4
SP_HEADS = 4
IDX_HEADS = 8
IDX_DIM = 32
TOPK_MAX = 256
DF_HEADS = 4
N_BUCKETS = 32
MAX_DISTANCE = 128
D_FF = 2816
EPS = 1e-6
LOG2E = math.log2(math.e)
QK_SCALE = HEAD_DIM ** -0.5 * LOG2E
M_INIT = -1e30
BOUND_SLACK = 1.02
UNDERFLOW_GUARD = 2.0 ** -100
INT_MIN = -2 ** 31
I16_MIN = -2 ** 15
COUNT_ROWS = 32
DIGIT_ROWS = 64
SCORE_GROUP = 4
WINDOW_GROUP = 3
SB_FAR = 2
SP_FAR = 2
DF_FAR = 4

G_SB, G_SP, G_DF = 0, 1024, 2048
Q_SB, K_SB, V_SB = 3072, 3328, 3584
Q_SP, K_SP, V_SP = 3840, 4096, 4352
Q_IX, KW_IX = 4608, 4864
Q_DF, K_DF, V_DF = 5120, 5632, 6144
NZ = 6656
W_IX_LANE = IDX_DIM

VMEM_LIMIT = 56 * 1024 * 1024

F32 = jnp.float32
BF16 = jnp.bfloat16
NT_DIMS = (((1,), (1,)), ((), ()))


def _nt_dot(a, b):
    return lax.dot_general(a, b, NT_DIMS, preferred_element_type=F32)


def _dot(a, b):
    return jnp.dot(a, b, preferred_element_type=F32)


def _params(*sem):
    return pltpu.CompilerParams(dimension_semantics=sem, vmem_limit_bytes=VMEM_LIMIT)


def _pick_rows(m, cap):
    for c in (2048, 1024, 512, 256):
        if c <= cap and m % c == 0:
            return c
    raise ValueError(f"row count {m} is not a multiple of {QB}")


def _bucket_np(rel):
    n = np.maximum(rel, 0)
    max_exact = N_BUCKETS // 2
    nf = np.maximum(n, 1).astype(np.float32)
    large = max_exact + (np.log(nf / np.float32(max_exact)) / np.float32(math.log(MAX_DISTANCE / max_exact))
                         * np.float32(N_BUCKETS - max_exact)).astype(np.int32)
    return np.where(n < max_exact, n, np.minimum(large, N_BUCKETS - 1)).astype(np.int32)


def _bucket_tile():
    tq = np.arange(QB)[:, None]
    c = np.arange(2 * QB)[None, :]
    return _bucket_np(tq - c + QB)


def _sb_prefix_matrix():
    sp = np.arange(2 * BLK)[:, None] % BLK
    c = np.arange(2 * BLK)[None, :]
    return np.where(c < BLK, sp > c, True).astype(np.float32)


def _ix_select_matrices():
    c = np.arange(QB)[:, None]
    col = np.arange(QB)[None, :]
    rep = ((c < IDX_DIM) & (c == col % IDX_DIM)).astype(np.float32)
    col8 = np.arange(IDX_HEADS * BLK)[None, :]
    bcast = (c == W_IX_LANE + col8 // BLK).astype(np.float32)
    before = (col < c).astype(np.float32)
    return rep, bcast, before


def _lane_iota(shape):
    return lax.broadcasted_iota(jnp.int32, shape, len(shape) - 1)


def _row_iota(shape):
    return lax.broadcasted_iota(jnp.int32, shape, 0)


def _head_rmsnorm128(xf, gain):
    lo = _lane_iota((1, BLK)) < HEAD_DIM
    ss = xf * xf
    s_lo = jnp.sum(jnp.where(lo, ss, 0.0), axis=-1, keepdims=True)
    s_hi = jnp.sum(jnp.where(lo, 0.0, ss), axis=-1, keepdims=True)
    ms = jnp.where(lo, s_lo, s_hi) * (1.0 / HEAD_DIM)
    return xf * lax.rsqrt(ms + EPS) * gain


def _bias_tile(tab_ref, bk, head):
    far = tab_ref[N_BUCKETS - 1, head]
    acc = jnp.zeros(bk.shape, F32)
    for b in range(N_BUCKETS - 1):
        acc = jnp.where(bk == b, (tab_ref[b, head] - far) * LOG2E, acc)
    return acc


def _softmax_piece_t(st, vt, m_ref, l_ref, acc_ref):
    m_old = m_ref[0:1]
    m_new = jnp.maximum(m_old, jnp.max(st, axis=0, keepdims=True))
    alpha = jnp.exp2(m_old - m_new)
    p = jnp.exp2(st - m_new)
    l_ref[0:1] = alpha * l_ref[0:1] + jnp.sum(p, axis=0, keepdims=True)
    m_ref[0:1] = m_new
    acc_ref[...] = alpha * acc_ref[...] + _dot(vt, p.astype(BF16))


def _online_pieces_t(pieces, m_ref, l_ref, acc_ref):
    for st, vt in pieces:
        _softmax_piece_t(st, vt, m_ref, l_ref, acc_ref)


def _bounded_pieces_t(pieces, bound, l_ref, acc_ref):
    ps = [jnp.exp2(st - bound) for st, _ in pieces]
    l_ref[...] += functools.reduce(jnp.add, [jnp.sum(p.reshape(p.shape[0] // 8, 8, p.shape[1]), axis=0) for p in ps])
    acc_ref[...] += functools.reduce(jnp.add, [_dot(vt, p.astype(BF16)) for (_, vt), p in zip(pieces, ps)])


def _attend_windows(i, width, far_fn, near_fn, step):
    n_far = jnp.maximum(i - 1, 0)
    n_win = (n_far + width - 1) // width

    def far(p):
        hi = n_far - p * width
        return far_fn(jnp.maximum(hi - width, 0), hi)

    def body(g, carry):
        step([far(WINDOW_GROUP * g + n) for n in range(WINDOW_GROUP)])
        return carry

    lax.fori_loop(0, n_win // WINDOW_GROUP, body, 0)
    first_left = WINDOW_GROUP * (n_win // WINDOW_GROUP)
    for n_left in range(WINDOW_GROUP):
        @pl.when((n_win - first_left == n_left) & (i >= 1))
        def _(n_left=n_left):
            step([far(first_left + n) for n in range(n_left)] + [near_fn(i - 1, 2)])

    @pl.when(i == 0)
    def _():
        step([near_fn(0, 1)])


def _for_far_pieces(n_far, width, piece_fn):
    def body(p, carry):
        hi = n_far - p * width
        piece_fn(jnp.maximum(hi - width, 0), hi)
        return carry

    lax.fori_loop(0, (n_far + width - 1) // width, body, 0)


def _in_proj_kernel(h_ref, g_ref, w_ref, o_ref, *, tn):
    x = h_ref[...]
    ms = jnp.mean(x * x, axis=-1, keepdims=True)
    u = (x * lax.rsqrt(ms + EPS) * g_ref[...]).astype(BF16)
    for c in range(o_ref.shape[1] // tn):
        cols = slice(c * tn, (c + 1) * tn)
        o_ref[:, cols] = _dot(u, w_ref[:, cols]).astype(o_ref.dtype)


def _in_proj(h, gain, w):
    m, d = h.shape
    n = w.shape[1]
    tm = _pick_rows(m, 512)
    return pl.pallas_call(
        functools.partial(_in_proj_kernel, tn=512),
        out_shape=jax.ShapeDtypeStruct((m, n), BF16),
        grid=(m // tm,),
        in_specs=[pl.BlockSpec((tm, d), lambda i: (i, 0)),
                  pl.BlockSpec((1, d), lambda i: (0, 0)),
                  pl.BlockSpec((d, n), lambda i: (0, 0), pipeline_mode=pl.Buffered(1))],
        out_specs=pl.BlockSpec((tm, n), lambda i: (i, 0)),
        compiler_params=_params("parallel"),
        name="in_proj",
    )(h, gain.reshape(1, d), w)


def _sb_kernel(q_ref, k_ref, v_ref, u_ref, o_ref, tot_ref, acc_ref, *, nq, tail_q):
    i = pl.program_id(1)
    nh = SB_HEADS

    def query_block(qt):
        head_of_lane = _lane_iota((qt, nh * HEAD_DIM)) // HEAD_DIM
        q = q_ref[0, :qt].astype(F32) * QK_SCALE
        qs = jnp.concatenate([jnp.where(head_of_lane == h, q, 0.0) for h in range(nh)], axis=0).astype(BF16)
        tot = tot_ref.at[:nh * qt]
        acc = acc_ref.at[:nh * qt]
        tot[...] = jnp.zeros_like(tot)
        acc[...] = jnp.zeros_like(acc)

        def piece(start, n_blocks, hi=None):
            r = pl.multiple_of(start * QB, QB)
            z_all = _nt_dot(qs, k_ref[0, pl.ds(r, n_blocks * QB), :])
            if hi is not None:
                col = _lane_iota((1, n_blocks * QB))
                z_all = z_all + jnp.where(col < (hi - start) * QB, 0.0, -jnp.inf)
            run = tot[...]
            n_sub = n_blocks * QB // BLK
            ws = [None] * n_sub
            for c in reversed(range(n_sub)):
                z = z_all[:, c * BLK:(c + 1) * BLK]
                sp = jnp.maximum(z, 0.0) + jnp.log2(1.0 + jnp.exp2(-jnp.abs(z)))
                l1m = -sp
                if hi is None:
                    mask = (_lane_iota(z.shape) + c * BLK) < (_row_iota(z.shape) & (qt - 1))
                    l1m = jnp.where(mask, l1m, 0.0)
                l1m_hi = l1m.astype(BF16)
                l1m_lo = (l1m - l1m_hi.astype(F32)).astype(BF16)
                rs = _dot(jnp.concatenate([l1m_hi, l1m_lo], axis=1), u_ref[...])
                w = jnp.exp2((z - sp) + rs[:, :BLK] + run)
                if hi is None:
                    w = jnp.where(mask, w, 0.0)
                ws[c] = w.astype(BF16)
                run = run + rs[:, BLK:]
            tot[...] = run
            acc[...] += _dot(jnp.concatenate(ws, axis=1), v_ref[0, pl.ds(r, n_blocks * QB), :])

        piece(i, 1)
        _for_far_pieces(i, SB_FAR, lambda start, hi: piece(start, SB_FAR, hi))

        a = acc[...]
        out = a[:qt]
        for h in range(1, nh):
            out = jnp.where(head_of_lane == h, a[h * qt:(h + 1) * qt], out)
        o_ref[0, :qt] = out.astype(o_ref.dtype)
        if qt < QB:
            o_ref[0, qt:] = jnp.zeros((QB - qt, nh * HEAD_DIM), o_ref.dtype)

    if tail_q == QB:
        query_block(QB)
    else:
        @pl.when(i < nq - 1)
        def _():
            query_block(QB)

        @pl.when(i == nq - 1)
        def _():
            query_block(tail_q)


def _tail_rows(t, tp, granule):
    real = t - (tp - QB)
    return min(QB, -(-real // granule) * granule)


def _sb_attn(z3, u_mat, t):
    b, tp, _ = z3.shape
    nq = tp // QB
    assert nq >= SB_FAR
    w = SB_HEADS * HEAD_DIM
    return pl.pallas_call(
        functools.partial(_sb_kernel, nq=nq, tail_q=_tail_rows(t, tp, 64)),
        out_shape=jax.ShapeDtypeStruct((b, tp, w), BF16),
        grid=(b, nq),
        in_specs=[pl.BlockSpec((1, QB, w), lambda bb, i: (bb, i, Q_SB // w)),
                  pl.BlockSpec((1, tp, w), lambda bb, i: (bb, 0, K_SB // w)),
                  pl.BlockSpec((1, tp, w), lambda bb, i: (bb, 0, V_SB // w)),
                  pl.BlockSpec((2 * BLK, 2 * BLK), lambda bb, i: (0, 0))],
        out_specs=pl.BlockSpec((1, QB, w), lambda bb, i: (bb, i, 0)),
        scratch_shapes=[pltpu.VMEM((SB_HEADS * QB, BLK), F32),
                        pltpu.VMEM((SB_HEADS * QB, w), F32)],
        compiler_params=_params("parallel", "arbitrary"),
        name="sb_attn",
    )(z3, z3, z3, u_mat)


def _dft_kernel(tab_ref, q_ref, k_ref, v_ref, qg_ref, kg_ref, bkt_ref, lamv_ref, sub_ref, o_ref,
                kn_ref, vt_ref, bias_ref, kmax_ref, m_ref, l_ref, lsum_ref, acc_ref,
                *, nq, tail_q, lam_init, head_off):
    h = pl.program_id(1)
    i = pl.program_id(2)
    lo = _lane_iota((1, BLK)) < HEAD_DIM

    @pl.when(i == 0)
    def _prep():
        kmax_ref[...] = jnp.zeros_like(kmax_ref)

        def kbody(j, carry):
            r = pl.multiple_of(j * QB, QB)
            kf = k_ref[0, pl.ds(r, QB), :].astype(F32)
            kn = _head_rmsnorm128(kf, kg_ref[...]).astype(BF16)
            kn_ref[pl.ds(r, QB), :] = kn
            ksq = kn.astype(F32) ** 2
            half_norms = jnp.maximum(jnp.sum(jnp.where(lo, ksq, 0.0), axis=-1, keepdims=True),
                                     jnp.sum(jnp.where(lo, 0.0, ksq), axis=-1, keepdims=True))
            kmax_ref[...] = jnp.maximum(kmax_ref[...], jnp.max(half_norms))
            vt_ref[j] = v_ref[0, pl.ds(r, QB), :].astype(F32).T.astype(BF16)
            return carry

        lax.fori_loop(0, nq, kbody, 0)
        bias_ref[...] = _bias_tile(tab_ref, bkt_ref[...], head_off + h)
        far_b = tab_ref[N_BUCKETS - 1, head_off + h]
        bias_max = jnp.float32(0.0)
        for b in range(N_BUCKETS - 1):
            bias_max = jnp.maximum(bias_max, (tab_ref[b, head_off + h] - far_b) * LOG2E)
        kmax_ref[1:2] = jnp.full((1, BLK), bias_max, F32)

    def query_block(qt):
        m, l, lsum, acc = (ref.at[:, :2 * qt] for ref in (m_ref, l_ref, lsum_ref, acc_ref))
        qn = _head_rmsnorm128(q_ref[0, :qt].astype(F32), qg_ref[...]) * QK_SCALE
        qs = jnp.concatenate([jnp.where(lo, qn, 0.0), jnp.where(lo, 0.0, qn)], axis=0).astype(BF16)

        q_sq = _nt_dot(jnp.ones((8, BLK), BF16), (qs.astype(F32) ** 2).astype(BF16))[0:1]
        bound = jnp.sqrt(q_sq * kmax_ref[0:1, 0:1]) * BOUND_SLACK + (kmax_ref[1:2, 0:1] + BOUND_SLACK)

        def scores(first, n):
            return _nt_dot(kn_ref[pl.ds(pl.multiple_of(first * QB, QB), n * QB), :], qs)

        def values(first, n):
            return jnp.concatenate([vt_ref[first + c] for c in range(n)], axis=1)

        def far_piece(start, hi):
            st = scores(start, DF_FAR)
            new_keys = _row_iota(st.shape) < (hi - start) * QB
            return jnp.where(new_keys, st, -jnp.inf), values(start, DF_FAR)

        def near_piece(first, n):
            b = bias_ref[(2 - n) * QB:, :qt]
            st = scores(first, n) + jnp.concatenate([b, b], axis=1)
            causal = (_row_iota(st.shape) - (n - 1) * QB) <= (_lane_iota(st.shape) & (qt - 1))
            return jnp.where(causal, st, -jnp.inf), values(first, n)

        def attend(step):
            _attend_windows(i, DF_FAR, far_piece, near_piece, step)

        l[...] = jnp.zeros_like(l)
        acc[...] = jnp.zeros_like(acc)
        attend(lambda pieces: _bounded_pieces_t(pieces, bound, l, acc))
        lsum[0:1] = jnp.sum(l[...], axis=0, keepdims=True)

        @pl.when(jnp.min(lsum[0:1]) < UNDERFLOW_GUARD)
        def _():
            m[...] = jnp.full(m.shape, M_INIT, F32)
            l[...] = jnp.zeros_like(l)
            acc[...] = jnp.zeros_like(acc)
            attend(lambda pieces: _online_pieces_t(pieces, m, l, acc))
            lsum[0:1] = l[0:1]

        a = acc[...] / lsum[0:1]
        lv = lamv_ref[...]
        lam = (jnp.exp(jnp.sum(lv[0:1] * lv[1:2], axis=-1, keepdims=True))
               - jnp.exp(jnp.sum(lv[2:3] * lv[3:4], axis=-1, keepdims=True)) + lam_init)
        y = (a[:, :qt] - lam * a[:, qt:]).T
        y = y * lax.rsqrt(jnp.mean(y * y, axis=-1, keepdims=True) + EPS) * sub_ref[...]
        o_ref[0, :qt] = (y * (1.0 - lam_init)).astype(o_ref.dtype)
        if qt < QB:
            o_ref[0, qt:] = jnp.zeros((QB - qt, BLK), o_ref.dtype)

    if tail_q == QB:
        query_block(QB)
    else:
        @pl.when(i < nq - 1)
        def _():
            query_block(QB)

        @pl.when(i == nq - 1)
        def _():
            query_block(tail_q)


def _dft_attn(z3, rel_bias, qg, kg, bkt_t, lamv, subln, lam_init, t):
    b, tp, _ = z3.shape
    nq = tp // QB
    assert nq >= DF_FAR
    kern = functools.partial(_dft_kernel, nq=nq, tail_q=_tail_rows(t, tp, 64), lam_init=lam_init,
                             head_off=SP_HEADS)
    vec = lambda bb, h, i: (0, 0)
    return pl.pallas_call(
        kern,
        out_shape=jax.ShapeDtypeStruct((b, tp, DF_HEADS * BLK), BF16),
        grid=(b, DF_HEADS, nq),
        in_specs=[pl.BlockSpec(memory_space=pltpu.SMEM),
                  pl.BlockSpec((1, QB, BLK), lambda bb, h, i: (bb, i, Q_DF // BLK + h)),
                  pl.BlockSpec((1, tp, BLK), lambda bb, h, i: (bb, 0, K_DF // BLK + h)),
                  pl.BlockSpec((1, tp, BLK), lambda bb, h, i: (bb, 0, V_DF // BLK + h)),
                  pl.BlockSpec((1, BLK), vec),
                  pl.BlockSpec((1, BLK), vec),
                  pl.BlockSpec((2 * QB, QB), vec),
                  pl.BlockSpec((8, BLK), vec),
                  pl.BlockSpec((1, BLK), vec)],
        out_specs=pl.BlockSpec((1, QB, BLK), lambda bb, h, i: (bb, i, h)),
        scratch_shapes=[pltpu.VMEM((tp, BLK), BF16),
                        pltpu.VMEM((nq, BLK, QB), BF16),
                        pltpu.VMEM((2 * QB, QB), F32),
                        pltpu.VMEM((8, BLK), F32),
                        pltpu.VMEM((8, 2 * QB), F32),
                        pltpu.VMEM((8, 2 * QB), F32),
                        pltpu.VMEM((8, 2 * QB), F32),
                        pltpu.VMEM((BLK, 2 * QB), F32)],
        compiler_params=_params("parallel", "parallel", "arbitrary"),
        name="df_attn",
    )(rel_bias, z3, z3, z3, qg, kg, bkt_t, lamv, subln)


def _sp_kernel(tab_ref, q_ref, k_ref, v_ref, qix_ref, kwq_ref, kwk_ref, qg_ref, kg_ref, bkt_ref,
               rep_ref, bcast_ref, before_ref, o_ref,
               kn_ref, kx_ref, vt_ref, bias_ref, kmax_ref, keys_ref, dig_ref, thr_ref, wb_ref,
               m_ref, l_ref, lsum_ref, acc_ref, *, nq, top_k):
    i = pl.program_id(1)
    nh = SP_HEADS
    w = nh * HEAD_DIM
    lane = _lane_iota((QB, w))
    head_of_lane = lane // HEAD_DIM

    def norm256(xf, g):
        return jnp.concatenate([_head_rmsnorm128(xf[:, :BLK], g[:, :BLK]),
                                _head_rmsnorm128(xf[:, BLK:], g[:, BLK:])], axis=1)

    def dup(x):
        return jnp.concatenate([x, x], axis=1)

    @pl.when(i == 0)
    def _prep():
        kmax_ref[...] = jnp.zeros_like(kmax_ref)

        def kbody(c, carry):
            r = pl.multiple_of(c * QB, QB)
            kn = norm256(k_ref[0, pl.ds(r, QB), :].astype(F32), kg_ref[...]).astype(BF16)
            kn_ref[pl.ds(r, QB), :] = kn
            ksq = kn.astype(F32) ** 2
            head_norms = functools.reduce(jnp.maximum, [
                jnp.sum(jnp.where(head_of_lane == h, ksq, 0.0), axis=-1, keepdims=True) for h in range(nh)])
            kmax_ref[0:1] = jnp.maximum(kmax_ref[0:1], jnp.max(head_norms))
            kx_ref[pl.ds(r, QB), :] = _dot(kwk_ref[0, pl.ds(r, QB), :], rep_ref[...]).astype(BF16)
            vt_ref[c] = v_ref[0, pl.ds(r, QB), :].astype(F32).T.astype(BF16)
            return carry

        lax.fori_loop(0, nq, kbody, 0)
        bias_max = jnp.float32(0.0)
        for h in range(nh):
            bias_ref[:, h * QB:(h + 1) * QB] = _bias_tile(tab_ref, bkt_ref[...], h)
            for b in range(N_BUCKETS - 1):
                bias_max = jnp.maximum(bias_max, (tab_ref[b, h] - tab_ref[N_BUCKETS - 1, h]) * LOG2E)
        kmax_ref[1:2] = jnp.full((1, BLK), bias_max, F32)

    key_causal = _row_iota((QB, QB)) <= _lane_iota((QB, QB))

    wb_ref[...] = _dot(kwq_ref[0], bcast_ref[...])
    qix = qix_ref[0].astype(F32)
    ix_head = lane // IDX_DIM
    qx = jnp.concatenate([jnp.where(ix_head == h, qix, 0.0) for h in range(IDX_HEADS)], axis=0).astype(BF16)

    def score_tiles(blocks, last_is_diagonal):
        dots = [_nt_dot(qx, kx_ref[pl.ds(pl.multiple_of(j * QB, QB), QB), :]) for j in blocks]
        for n, (j, d) in enumerate(zip(blocks, dots)):
            sc = jnp.zeros((QB, QB), F32)
            for h in range(IDX_HEADS):
                sc = sc + dup(wb_ref[:, h * BLK:(h + 1) * BLK]) * jnp.maximum(d[h * QB:(h + 1) * QB], 0.0)
            sc = jnp.where(sc == 0.0, 0.0, sc).T
            bits = lax.bitcast_convert_type(sc, jnp.int32)
            key = jnp.where(bits < 0, bits ^ jnp.int32(0x7FFFFFFF), bits)
            if last_is_diagonal and n == len(blocks) - 1:
                key = jnp.where(key_causal, key, jnp.int32(INT_MIN))
            keys_ref[j] = key
            dig_ref[j] = jnp.right_shift(key, 16).astype(jnp.int16)

    def group_body(g, c):
        score_tiles([SCORE_GROUP * g + n for n in range(SCORE_GROUP)], False)
        return c

    lax.fori_loop(0, i // SCORE_GROUP, group_body, 0)
    first_left = SCORE_GROUP * (i // SCORE_GROUP)
    for n_left in range(1, SCORE_GROUP + 1):
        @pl.when(i - first_left == n_left - 1)
        def _(n_left=n_left):
            score_tiles([first_left + n for n in range(n_left)], True)

    nblk = i + 1

    def count(pred):
        def cbody(j, c):
            hit = jnp.where(pred(keys_ref[j]), 1.0, 0.0)
            return c + jnp.sum(hit.reshape(QB // COUNT_ROWS, COUNT_ROWS, QB), axis=0)

        c = lax.fori_loop(0, nblk, cbody, jnp.zeros((COUNT_ROWS, QB), F32))
        return jnp.sum(c, axis=0, keepdims=True)

    def count16(pred):
        def cbody(j, c):
            hit = jnp.where(pred(dig_ref[j]), jnp.int16(1), jnp.int16(0)).reshape(QB // DIGIT_ROWS, DIGIT_ROWS, QB)
            return c + functools.reduce(jnp.add, [hit[r] for r in range(QB // DIGIT_ROWS)])

        c = lax.fori_loop(0, nblk, cbody, jnp.zeros((DIGIT_ROWS, QB), jnp.int16))
        return jnp.sum(c.astype(F32), axis=0, keepdims=True)

    n_all = jnp.zeros((1, QB), F32) + (nblk * QB).astype(F32)

    def bisect16(need):
        def bit_body(b, carry):
            cur, cnt_cur = carry
            cand = cur + jnp.left_shift(jnp.int32(1), 15 - b)
            cand16 = cand.astype(jnp.int16)
            cnt = count16(lambda d: d >= cand16)
            ok = cnt >= need
            return jnp.where(ok, cand, cur), jnp.where(ok, cnt, cnt_cur)

        return lax.fori_loop(0, 16, bit_body, (jnp.full((1, QB), I16_MIN, jnp.int32), n_all))

    t_hi, c_ge_hi = bisect16(float(top_k))
    t_hi16 = t_hi.astype(jnp.int16)
    c_gt_hi = count16(lambda d: d > t_hi16)
    base = jnp.left_shift(t_hi, 16)

    def low_digits(j, c):
        y = keys_ref[j] - base
        dig_ref[j] = jnp.where(jnp.right_shift(y, 16) == 0, y + I16_MIN, I16_MIN).astype(jnp.int16)
        return c

    lax.fori_loop(0, nblk, low_digits, 0)
    t_lo, c_ge_lo = bisect16(float(top_k) - c_gt_hi)
    thr = base + (t_lo - I16_MIN)
    cge = c_gt_hi + jnp.where(t_lo > I16_MIN, c_ge_lo, c_ge_hi - c_gt_hi)
    thr_ref[...] = jnp.broadcast_to(thr, thr_ref.shape)

    tie = jnp.where((cge > float(top_k)) & (thr > INT_MIN), 1, 0)

    @pl.when(jnp.max(tie) > 0)
    def _ties():
        need = float(top_k) - count(lambda k: k > thr)

        def tbody(j, run):
            kj = keys_ref[j]
            eq = kj == thr
            eqf = jnp.where(eq, 1.0, 0.0)
            rank = run + _dot(before_ref[...], eqf.astype(BF16))
            keep = jnp.where(kj > thr, 1, jnp.where(eq & (rank < need), 1, -1))
            keys_ref[j] = keep.astype(jnp.int32)
            return run + jnp.sum(eqf, axis=0, keepdims=True)

        lax.fori_loop(0, nblk, tbody, jnp.zeros((1, QB), F32))
        thr_ref[...] = jnp.zeros_like(thr_ref)

    qn = norm256(q_ref[0].astype(F32), qg_ref[...]) * QK_SCALE
    qs = jnp.concatenate([jnp.where(head_of_lane == h, qn, 0.0) for h in range(nh)], axis=0).astype(BF16)
    q_sq = _nt_dot(jnp.ones((8, w), BF16), (qs.astype(F32) ** 2).astype(BF16))[0:1]
    bound = jnp.sqrt(q_sq * kmax_ref[0:1, 0:1]) * BOUND_SLACK + (kmax_ref[1:2, 0:1] + BOUND_SLACK)

    thr_sel = thr_ref[0:1, :]

    def as_mask(sel):
        return lax.bitcast_convert_type(jnp.where(sel, 0.0, -jnp.inf), jnp.int32)

    def mask_body(j, c):
        keys_ref[j] = as_mask(keys_ref[j] >= thr_sel)
        return c

    lax.fori_loop(0, i, mask_body, 0)
    keys_ref[i] = as_mask((keys_ref[i] >= thr_sel) & key_causal)

    def scores(first, n):
        return _nt_dot(kn_ref[pl.ds(pl.multiple_of(first * QB, QB), n * QB), :], qs)

    def values(first, n):
        return jnp.concatenate([vt_ref[first + c] for c in range(n)], axis=1)

    def selection(first, n):
        return jnp.concatenate([lax.bitcast_convert_type(keys_ref[first + c], F32) for c in range(n)],
                               axis=0)

    def far_piece(start, hi):
        m = selection(start, SP_FAR)
        m = jnp.where(_row_iota(m.shape) < (hi - start) * QB, m, -jnp.inf)
        return scores(start, SP_FAR) + jnp.concatenate([m] * nh, axis=1), values(start, SP_FAR)

    def near_piece(first, n):
        m = selection(first, n)
        return (scores(first, n) + (jnp.concatenate([m] * nh, axis=1) + bias_ref[(2 - n) * QB:, :]),
                values(first, n))

    def attend(step):
        _attend_windows(i, SP_FAR, far_piece, near_piece, step)

    l_ref[...] = jnp.zeros_like(l_ref)
    acc_ref[...] = jnp.zeros_like(acc_ref)
    attend(lambda pieces: _bounded_pieces_t(pieces, bound, l_ref, acc_ref))
    lsum_ref[0:1] = jnp.sum(l_ref[...], axis=0, keepdims=True)

    @pl.when(jnp.min(lsum_ref[0:1]) < UNDERFLOW_GUARD)
    def _():
        m_ref[...] = jnp.full(m_ref.shape, M_INIT, F32)
        l_ref[...] = jnp.zeros_like(l_ref)
        acc_ref[...] = jnp.zeros_like(acc_ref)
        attend(lambda pieces: _online_pieces_t(pieces, m_ref, l_ref, acc_ref))
        lsum_ref[0:1] = l_ref[0:1]

    a = acc_ref[...] / lsum_ref[0:1]
    out_t = jnp.concatenate([a[h * HEAD_DIM:(h + 1) * HEAD_DIM, h * QB:(h + 1) * QB] for h in range(nh)], axis=0)
    o_ref[0] = out_t.T.astype(o_ref.dtype)


def _sp_attn(z3, rel_bias, qg, kg, bkt_t, rep, bcast, before, top_k):
    b, tp, _ = z3.shape
    nq = tp // QB
    assert nq >= SP_FAR
    w = SP_HEADS * HEAD_DIM
    kern = functools.partial(_sp_kernel, nq=nq, top_k=top_k)
    c2 = lambda bb, i: (0, 0)
    return pl.pallas_call(
        kern,
        out_shape=jax.ShapeDtypeStruct((b, tp, w), BF16),
        grid=(b, nq),
        in_specs=[pl.BlockSpec(memory_space=pltpu.SMEM),
                  pl.BlockSpec((1, QB, w), lambda bb, i: (bb, i, Q_SP // w)),
                  pl.BlockSpec((1, tp, w), lambda bb, i: (bb, 0, K_SP // w)),
                  pl.BlockSpec((1, tp, w), lambda bb, i: (bb, 0, V_SP // w)),
                  pl.BlockSpec((1, QB, w), lambda bb, i: (bb, i, Q_IX // w)),
                  pl.BlockSpec((1, QB, w), lambda bb, i: (bb, i, KW_IX // w)),
                  pl.BlockSpec((1, tp, w), lambda bb, i: (bb, 0, KW_IX // w)),
                  pl.BlockSpec((1, w), c2),
                  pl.BlockSpec((1, w), c2),
                  pl.BlockSpec((2 * QB, QB), c2),
                  pl.BlockSpec((w, w), c2),
                  pl.BlockSpec((w, IDX_HEADS * BLK), c2),
                  pl.BlockSpec((QB, QB), c2)],
        out_specs=pl.BlockSpec((1, QB, w), lambda bb, i: (bb, i, 0)),
        scratch_shapes=[pltpu.VMEM((tp, w), BF16),
                        pltpu.VMEM((tp, w), BF16),
                        pltpu.VMEM((nq, w, QB), BF16),
                        pltpu.VMEM((2 * QB, SP_HEADS * QB), F32),
                        pltpu.VMEM((8, BLK), F32),
                        pltpu.VMEM((nq, QB, QB), jnp.int32),
                        pltpu.VMEM((nq, QB, QB), jnp.int16),
                        pltpu.VMEM((8, QB), jnp.int32),
                        pltpu.VMEM((QB, IDX_HEADS * BLK), F32),
                        pltpu.VMEM((8, SP_HEADS * QB), F32),
                        pltpu.VMEM((8, SP_HEADS * QB), F32),
                        pltpu.VMEM((8, SP_HEADS * QB), F32),
                        pltpu.VMEM((w, SP_HEADS * QB), F32)],
        compiler_params=_params("parallel", "arbitrary"),
        name="sp_attn",
    )(rel_bias, z3, z3, z3, z3, z3, z3, qg, kg, bkt_t, rep, bcast, before)


def _mix_kernel(h_ref, gsb_ref, gsp_ref, gdf_ref, bg_ref, ysb_ref, ysp_ref, ydf_ref,
                wsb_ref, wsp_ref, wdf_ref, wo_ref, o_ref):
    def branch(g_ref, k, y_ref, w_ref):
        gate = jax.nn.sigmoid(g_ref[...].astype(F32) + bg_ref[:, k * D_MODEL:(k + 1) * D_MODEL])
        return gate * _dot(y_ref[...], w_ref[...])

    merged = (branch(gsb_ref, 0, ysb_ref, wsb_ref) + branch(gsp_ref, 1, ysp_ref, wsp_ref)
              + branch(gdf_ref, 2, ydf_ref, wdf_ref))
    o_ref[...] = h_ref[...] + _dot(merged.astype(BF16), wo_ref[...])


def _mix_out(h, z, b_gate, y_sb, y_sp, y_df, w_sb, w_sp, w_df, w_o):
    m, d = h.shape
    tm = _pick_rows(m, 512)
    row = lambda i: (i, 0)
    fixed = lambda i: (0, 0)
    return pl.pallas_call(
        _mix_kernel,
        out_shape=jax.ShapeDtypeStruct((m, d), F32),
        grid=(m // tm,),
        in_specs=[pl.BlockSpec((tm, d), row),
                  pl.BlockSpec((tm, d), lambda i: (i, G_SB // D_MODEL)),
                  pl.BlockSpec((tm, d), lambda i: (i, G_SP // D_MODEL)),
                  pl.BlockSpec((tm, d), lambda i: (i, G_DF // D_MODEL)),
                  pl.BlockSpec((1, 3 * d), fixed),
                  pl.BlockSpec((tm, y_sb.shape[1]), row),
                  pl.BlockSpec((tm, y_sp.shape[1]), row),
                  pl.BlockSpec((tm, y_df.shape[1]), row),
                  pl.BlockSpec(w_sb.shape, fixed),
                  pl.BlockSpec(w_sp.shape, fixed),
                  pl.BlockSpec(w_df.shape, fixed),
                  pl.BlockSpec(w_o.shape, fixed)],
        out_specs=pl.BlockSpec((tm, d), row),
        compiler_params=_params("parallel"),
        name="mix_out",
    )(h, z, z, z, b_gate.reshape(1, 3 * d), y_sb, y_sp, y_df, w_sb, w_sp, w_df, w_o)


def _ffn_kernel(h_ref, g_ref, wu_ref, cw_ref, cb_ref, wd_ref, o_ref, gbuf_ref, carry_ref, *, tm, tp, tf):
    r = pl.program_id(0)
    x = h_ref[...]
    ms = jnp.mean(x * x, axis=-1, keepdims=True)
    u = (x * lax.rsqrt(ms + EPS) * g_ref[...]).astype(BF16)

    @pl.when(r == 0)
    def _():
        carry_ref[...] = jnp.zeros_like(carry_ref)

    seq_start = lax.rem(tp - lax.rem(r * tm, tp), tp)
    local = lax.broadcasted_iota(jnp.int32, (tm, 1), 0)
    tap1 = local != seq_start
    tap2 = tap1 & (local != seq_start + 1)
    out = x
    for f in range(D_FF // tf):
        cols = slice(f * tf, (f + 1) * tf)
        gate = _dot(u, wu_ref[:, cols])
        val = _dot(u, wu_ref[:, D_FF + f * tf:D_FF + (f + 1) * tf])
        gbuf_ref[0:8] = carry_ref[f]
        gbuf_ref[8:8 + tm] = gate
        carry_ref[f] = gate[tm - 8:tm]
        g1 = jnp.where(tap1, gbuf_ref[7:7 + tm], 0.0)
        g2 = jnp.where(tap2, gbuf_ref[6:6 + tm], 0.0)
        conv = cb_ref[:, cols] + cw_ref[0:1, cols] * g2 + cw_ref[1:2, cols] * g1 + cw_ref[2:3, cols] * gate
        act = conv * jax.nn.sigmoid(conv) * val
        out = out + _dot(act.astype(BF16), wd_ref[cols, :])
    o_ref[...] = out


def _ffn(h, gain, w_up, conv_w, conv_b, w_down, tp):
    m, d = h.shape
    tm = _pick_rows(m, 512)
    assert tm <= tp
    tf = D_FF // 2
    nf = D_FF // tf
    kern = functools.partial(_ffn_kernel, tm=tm, tp=tp, tf=tf)
    fixed = lambda r: (0, 0)
    resident = pl.Buffered(1)
    return pl.pallas_call(
        kern,
        out_shape=jax.ShapeDtypeStruct((m, d), F32),
        grid=(m // tm,),
        in_specs=[pl.BlockSpec((tm, d), lambda r: (r, 0)),
                  pl.BlockSpec((1, d), fixed),
                  pl.BlockSpec((d, 2 * D_FF), fixed, pipeline_mode=resident),
                  pl.BlockSpec((8, D_FF), fixed),
                  pl.BlockSpec((1, D_FF), fixed),
                  pl.BlockSpec((D_FF, d), fixed, pipeline_mode=resident)],
        out_specs=pl.BlockSpec((tm, d), lambda r: (r, 0)),
        scratch_shapes=[pltpu.VMEM((tm + 8, tf), F32),
                        pltpu.VMEM((nf, 8, tf), F32)],
        compiler_params=_params("arbitrary"),
        name="conv_ffn",
    )(h, gain.reshape(1, d), w_up, conv_w, conv_b.reshape(1, D_FF), w_down)


def _permute_w_in(w):
    n_attn = KW_IX - Q_SB + IDX_DIM + IDX_HEADS
    n_gate = 3 * D_MODEL
    n_df = 3 * DF_HEADS * 2 * HEAD_DIM
    gates = w[:, n_attn + n_df:]
    attn = w[:, :n_attn]
    pad = jnp.zeros((w.shape[0], Q_DF - Q_SB - n_attn), w.dtype)
    df = w[:, n_attn:n_attn + n_df]
    out = jnp.concatenate([gates, attn, pad, df], axis=1)
    assert gates.shape[1] == n_gate and out.shape[1] == NZ
    return out


def kernel(x, meta_tokens, rel_bias, attn_norm, w_in, b_gate, q_norm_sp, k_norm_sp, q_norm_df, k_norm_df, lam_q1, lam_k1, lam_q2, lam_k2, subln_df, w_br_sb, w_br_sp, w_br_df, w_out, ffn_norm, w_up, conv_w, conv_b, w_down):
    b, s, d = x.shape
    depth = w_in.shape[0]
    t = N_META + s
    tp = -(-t // QB) * QB
    top_k = min(TOPK_MAX, t // 4)
    m = b * tp

    meta = jnp.broadcast_to(meta_tokens[None].astype(x.dtype), (b, N_META, d))
    h = jnp.concatenate([meta, x, jnp.zeros((b, tp - t, d), x.dtype)], axis=1).reshape(m, d)

    bkt_t = jnp.asarray(np.ascontiguousarray(_bucket_tile().T))
    u_mat = jnp.asarray(_sb_prefix_matrix(), BF16)
    rep, bcast, before = (jnp.asarray(a, BF16) for a in _ix_select_matrices())
    rel_bias = rel_bias.astype(F32)

    for l in range(depth):
        lam_init = 0.8 - 0.6 * math.exp(-0.3 * l)
        z = _in_proj(h, attn_norm[l], _permute_w_in(w_in[l]).astype(BF16))
        z3 = z.reshape(b, tp, NZ)
        y_sb = _sb_attn(z3, u_mat, t)
        y_sp = _sp_attn(z3, rel_bias,
                        jnp.tile(q_norm_sp[l].astype(F32), SP_HEADS).reshape(1, -1),
                        jnp.tile(k_norm_sp[l].astype(F32), SP_HEADS).reshape(1, -1),
                        bkt_t, rep, bcast, before, top_k)
        lamv = jnp.zeros((8, BLK), F32).at[:4, :HEAD_DIM].set(
            jnp.stack([lam_q1[l], lam_k1[l], lam_q2[l], lam_k2[l]]).astype(F32))
        y_df = _dft_attn(z3, rel_bias,
                         jnp.tile(q_norm_df[l].astype(F32), 2).reshape(1, -1),
                         jnp.tile(k_norm_df[l].astype(F32), 2).reshape(1, -1),
                         bkt_t, lamv, subln_df[l].astype(F32).reshape(1, -1), lam_init, t)
        h = _mix_out(h, z, b_gate[l], y_sb.reshape(m, -1), y_sp.reshape(m, -1), y_df.reshape(m, -1),
                     w_br_sb[l].astype(BF16), w_br_sp[l].astype(BF16), w_br_df[l].astype(BF16),
                     w_out[l].astype(BF16))
        cw = jnp.zeros((8, D_FF), F32).at[:conv_w.shape[1]].set(conv_w[l])
        h = _ffn(h, ffn_norm[l], w_up[l].astype(BF16), cw, conv_b[l], w_down[l].astype(BF16), tp)

    return h.reshape(b, tp, d)[:, N_META:t]
```

```python
import functools
import math

import numpy as np
import jax
import jax.numpy as jnp
from jax import lax
from jax.experimental import pallas as pl
from jax.experimental.pallas import tpu as pltpu

D_MODEL = 1024
HEAD_DIM = 64
N_META = 16
BLK = 128
QB = 256
SB_HEADS = 4
SP_HEADS = 4
IDX_HEADS = 8
IDX_DIM = 32
TOPK_MAX = 256
DF_HEADS = 4
N_BUCKETS = 32
MAX_DISTANCE = 128
D_FF = 2816
EPS = 1e-6
LOG2E = math.log2(math.e)
QK_SCALE = HEAD_DIM ** -0.5 * LOG2E
M_INIT = -1e30
BOUND_SLACK = 1.02
UNDERFLOW_GUARD = 2.0 ** -100
INT_MIN = -2 ** 31
I16_MIN = -2 ** 15
COUNT_ROWS = 32
DIGIT_ROWS = 64
SCORE_GROUP = 4
SP_WINDOW_GROUP = 3
DF_WINDOW_GROUP = 2
DF_GROUP = 2
SB_FAR = 2
SP_FAR = 2
DF_FAR = 4

G_SB, G_SP, G_DF = 0, 1024, 2048
Q_SB, K_SB, V_SB = 3072, 3328, 3584
Q_SP, K_SP, V_SP = 3840, 4096, 4352
Q_IX, KW_IX = 4608, 4864
Q_DF, K_DF, V_DF = 5120, 5632, 6144
NZ = 6656
W_IX_LANE = IDX_DIM

VMEM_LIMIT = 56 * 1024 * 1024

F32 = jnp.float32
BF16 = jnp.bfloat16
NT_DIMS = (((1,), (1,)), ((), ()))


def _nt_dot(a, b):
    return lax.dot_general(a, b, NT_DIMS, preferred_element_type=F32)


def _dot(a, b):
    return jnp.dot(a, b, preferred_element_type=F32)


def _params(*sem):
    return pltpu.CompilerParams(dimension_semantics=sem, vmem_limit_bytes=VMEM_LIMIT)


def _pick_rows(m, cap):
    for c in (2048, 1024, 512, 256):
        if c <= cap and m % c == 0:
            return c
    raise ValueError(f"row count {m} is not a multiple of {QB}")


def _bucket_np(rel):
    n = np.maximum(rel, 0)
    max_exact = N_BUCKETS // 2
    nf = np.maximum(n, 1).astype(np.float32)
    large = max_exact + (np.log(nf / np.float32(max_exact)) / np.float32(math.log(MAX_DISTANCE / max_exact))
                         * np.float32(N_BUCKETS - max_exact)).astype(np.int32)
    return np.where(n < max_exact, n, np.minimum(large, N_BUCKETS - 1)).astype(np.int32)


def _bucket_tile():
    tq = np.arange(QB)[:, None]
    c = np.arange(2 * QB)[None, :]
    return _bucket_np(tq - c + QB)


def _sb_prefix_matrix():
    sp = np.arange(2 * BLK)[:, None] % BLK
    c = np.arange(2 * BLK)[None, :]
    return np.where(c < BLK, sp > c, True).astype(np.float32)


def _ix_select_matrices():
    c = np.arange(QB)[:, None]
    col = np.arange(QB)[None, :]
    rep = ((c < IDX_DIM) & (c == col % IDX_DIM)).astype(np.float32)
    col8 = np.arange(IDX_HEADS * BLK)[None, :]
    bcast = (c == W_IX_LANE + col8 // BLK).astype(np.float32)
    before = (col < c).astype(np.float32)
    return rep, bcast, before


def _lane_iota(shape):
    return lax.broadcasted_iota(jnp.int32, shape, len(shape) - 1)


def _row_iota(shape):
    return lax.broadcasted_iota(jnp.int32, shape, 0)


def _head_rmsnorm128(xf, gain):
    lo = _lane_iota((1, BLK)) < HEAD_DIM
    ss = xf * xf
    s_lo = jnp.sum(jnp.where(lo, ss, 0.0), axis=-1, keepdims=True)
    s_hi = jnp.sum(jnp.where(lo, 0.0, ss), axis=-1, keepdims=True)
    ms = jnp.where(lo, s_lo, s_hi) * (1.0 / HEAD_DIM)
    return xf * lax.rsqrt(ms + EPS) * gain


def _bias_tile(tab_ref, bk, head):
    far = tab_ref[N_BUCKETS - 1, head]
    acc = jnp.zeros(bk.shape, F32)
    for b in range(N_BUCKETS - 1):
        acc = jnp.where(bk == b, (tab_ref[b, head] - far) * LOG2E, acc)
    return acc


def _softmax_piece_t(st, vt, m_ref, l_ref, acc_ref):
    m_old = m_ref[0:1]
    m_new = jnp.maximum(m_old, jnp.max(st, axis=0, keepdims=True))
    alpha = jnp.exp2(m_old - m_new)
    p = jnp.exp2(st - m_new)
    l_ref[0:1] = alpha * l_ref[0:1] + jnp.sum(p, axis=0, keepdims=True)
    m_ref[0:1] = m_new
    acc_ref[...] = alpha * acc_ref[...] + _dot(vt, p.astype(BF16))


def _online_pieces_t(pieces, m_refs, l_refs, acc_refs):
    for st, vt, c in pieces:
        _softmax_piece_t(st, vt, m_refs[c], l_refs[c], acc_refs[c])


def _bounded_pieces_t(pieces, bounds, l_refs, acc_refs):
    ps = [jnp.exp2(st - bounds[c]) for st, _, c in pieces]
    for c in sorted({c for _, _, c in pieces}):
        mine = [(vt, p) for (_, vt, cc), p in zip(pieces, ps) if cc == c]
        l_refs[c][...] += functools.reduce(
            jnp.add, [jnp.sum(p.reshape(p.shape[0] // 8, 8, p.shape[1]), axis=0) for _, p in mine])
        acc_refs[c][...] += functools.reduce(jnp.add, [_dot(vt, p.astype(BF16)) for vt, p in mine])


def _attend_windows(i, width, group, far_fn, near_fn, step):
    n_far = jnp.maximum(i - 1, 0)
    n_win = (n_far + width - 1) // width

    def far(p):
        hi = n_far - p * width
        return far_fn(jnp.maximum(hi - width, 0), hi)

    def body(g, carry):
        step(sum([far(group * g + n) for n in range(group)], []))
        return carry

    lax.fori_loop(0, n_win // group, body, 0)
    first_left = group * (n_win // group)
    for n_left in range(group):
        @pl.when((n_win - first_left == n_left) & (i >= 1))
        def _(n_left=n_left):
            step(sum([far(first_left + n) for n in range(n_left)], []) + near_fn(i - 1, 2))

    @pl.when(i == 0)
    def _():
        step(near_fn(0, 1))


def _for_far_pieces(n_far, width, piece_fn):
    def body(p, carry):
        hi = n_far - p * width
        piece_fn(jnp.maximum(hi - width, 0), hi)
        return carry

    lax.fori_loop(0, (n_far + width - 1) // width, body, 0)


def _in_proj_kernel(h_ref, g_ref, w_ref, o_ref, *, tn):
    x = h_ref[...]
    ms = jnp.mean(x * x, axis=-1, keepdims=True)
    u = (x * lax.rsqrt(ms + EPS) * g_ref[...]).astype(BF16)
    for c in range(o_ref.shape[1] // tn):
        cols = slice(c * tn, (c + 1) * tn)
        o_ref[:, cols] = _dot(u, w_ref[:, cols]).astype(o_ref.dtype)


def _in_proj(h, gain, w):
    m, d = h.shape
    n = w.shape[1]
    tm = _pick_rows(m, 512)
    return pl.pallas_call(
        functools.partial(_in_proj_kernel, tn=512),
        out_shape=jax.ShapeDtypeStruct((m, n), BF16),
        grid=(m // tm,),
        in_specs=[pl.BlockSpec((tm, d), lambda i: (i, 0)),
                  pl.BlockSpec((1, d), lambda i: (0, 0)),
                  pl.BlockSpec((d, n), lambda i: (0, 0), pipeline_mode=pl.Buffered(1))],
        out_specs=pl.BlockSpec((tm, n), lambda i: (i, 0)),
        compiler_params=_params("parallel"),
        name="in_proj",
    )(h, gain.reshape(1, d), w)


def _sb_kernel(q_ref, k_ref, v_ref, u_ref, o_ref, tot_ref, acc_ref, *, nq, tail_q):
    i = pl.program_id(1)
    nh = SB_HEADS

    def query_block(qt):
        head_of_lane = _lane_iota((qt, nh * HEAD_DIM)) // HEAD_DIM
        q = q_ref[0, :qt].astype(F32) * QK_SCALE
        qs = jnp.concatenate([jnp.where(head_of_lane == h, q, 0.0) for h in range(nh)], axis=0).astype(BF16)
        tot = tot_ref.at[:nh * qt]
        acc = acc_ref.at[:nh * qt]
        tot[...] = jnp.zeros_like(tot)
        acc[...] = jnp.zeros_like(acc)

        def piece(start, n_blocks, hi=None):
            r = pl.multiple_of(start * QB, QB)
            z_all = _nt_dot(qs, k_ref[0, pl.ds(r, n_blocks * QB), :])
            if hi is not None:
                col = _lane_iota((1, n_blocks * QB))
                z_all = z_all + jnp.where(col < (hi - start) * QB, 0.0, -jnp.inf)
            run = tot[...]
            n_sub = n_blocks * QB // BLK
            ws = [None] * n_sub
            for c in reversed(range(n_sub)):
                z = z_all[:, c * BLK:(c + 1) * BLK]
                sp = jnp.maximum(z, 0.0) + jnp.log2(1.0 + jnp.exp2(-jnp.abs(z)))
                l1m = -sp
                if hi is None:
                    mask = (_lane_iota(z.shape) + c * BLK) < (_row_iota(z.shape) & (qt - 1))
                    l1m = jnp.where(mask, l1m, 0.0)
                l1m_hi = l1m.astype(BF16)
                l1m_lo = (l1m - l1m_hi.astype(F32)).astype(BF16)
                rs = _dot(jnp.concatenate([l1m_hi, l1m_lo], axis=1), u_ref[...])
                w = jnp.exp2((z - sp) + rs[:, :BLK] + run)
                if hi is None:
                    w = jnp.where(mask, w, 0.0)
                ws[c] = w.astype(BF16)
                run = run + rs[:, BLK:]
            tot[...] = run
            acc[...] += _dot(jnp.concatenate(ws, axis=1), v_ref[0, pl.ds(r, n_blocks * QB), :])

        piece(i, 1)
        _for_far_pieces(i, SB_FAR, lambda start, hi: piece(start, SB_FAR, hi))

        a = acc[...]
        out = a[:qt]
        for h in range(1, nh):
            out = jnp.where(head_of_lane == h, a[h * qt:(h + 1) * qt], out)
        o_ref[0, :qt] = out.astype(o_ref.dtype)
        if qt < QB:
            o_ref[0, qt:] = jnp.zeros((QB - qt, nh * HEAD_DIM), o_ref.dtype)

    if tail_q == QB:
        query_block(QB)
    else:
        @pl.when(i < nq - 1)
        def _():
            query_block(QB)

        @pl.when(i == nq - 1)
        def _():
            query_block(tail_q)


def _tail_rows(t, tp, granule):
    real = t - (tp - QB)
    return min(QB, -(-real // granule) * granule)


def _sb_attn(z3, u_mat, t):
    b, tp, _ = z3.shape
    nq = tp // QB
    assert nq >= SB_FAR
    w = SB_HEADS * HEAD_DIM
    return pl.pallas_call(
        functools.partial(_sb_kernel, nq=nq, tail_q=_tail_rows(t, tp, 64)),
        out_shape=jax.ShapeDtypeStruct((b, tp, w), BF16),
        grid=(b, nq),
        in_specs=[pl.BlockSpec((1, QB, w), lambda bb, i: (bb, i, Q_SB // w)),
                  pl.BlockSpec((1, tp, w), lambda bb, i: (bb, 0, K_SB // w)),
                  pl.BlockSpec((1, tp, w), lambda bb, i: (bb, 0, V_SB // w)),
                  pl.BlockSpec((2 * BLK, 2 * BLK), lambda bb, i: (0, 0))],
        out_specs=pl.BlockSpec((1, QB, w), lambda bb, i: (bb, i, 0)),
        scratch_shapes=[pltpu.VMEM((SB_HEADS * QB, BLK), F32),
                        pltpu.VMEM((SB_HEADS * QB, w), F32)],
        compiler_params=_params("parallel", "arbitrary"),
        name="sb_attn",
    )(z3, z3, z3, u_mat)


def _dft_kernel(tab_ref, q_ref, k_ref, v_ref, qg_ref, kg_ref, bkt_ref, lamv_ref, sub_ref, o_ref,
                kn_ref, vt_ref, bias_ref, kmax_ref, m_ref, l_ref, lsum_ref, acc_ref,
                *, nq, tail_q, lam_init, head_off):
    g = pl.program_id(1)
    i = pl.program_id(2)
    lo = _lane_iota((1, BLK)) < HEAD_DIM
    heads = range(DF_GROUP)
    lanes = [slice(c * BLK, (c + 1) * BLK) for c in heads]

    @pl.when(i == 0)
    def _prep():
        kmax_ref[...] = jnp.zeros_like(kmax_ref)

        def kbody(j, carry):
            r = pl.multiple_of(j * QB, QB)
            for c in heads:
                kf = k_ref[0, pl.ds(r, QB), lanes[c]].astype(F32)
                kn = _head_rmsnorm128(kf, kg_ref[...]).astype(BF16)
                kn_ref[pl.ds(r, QB), lanes[c]] = kn
                ksq = kn.astype(F32) ** 2
                half_norms = jnp.maximum(jnp.sum(jnp.where(lo, ksq, 0.0), axis=-1, keepdims=True),
                                         jnp.sum(jnp.where(lo, 0.0, ksq), axis=-1, keepdims=True))
                kmax_ref[c] = jnp.maximum(kmax_ref[c], jnp.max(half_norms))
            vt_ref[j] = v_ref[0, pl.ds(r, QB), :].astype(F32).T.astype(BF16)
            return carry

        lax.fori_loop(0, nq, kbody, 0)
        for c in heads:
            head = head_off + g * DF_GROUP + c
            bias_ref[c] = _bias_tile(tab_ref, bkt_ref[...], head)
            bias_max = jnp.float32(0.0)
            for b in range(N_BUCKETS - 1):
                bias_max = jnp.maximum(bias_max, (tab_ref[b, head] - tab_ref[N_BUCKETS - 1, head]) * LOG2E)
            kmax_ref[c, 1:2] = jnp.full((1, BLK), bias_max, F32)

    def query_block(qt):
        m, l, lsum, acc = ([ref.at[c, :, :2 * qt] for c in heads] for ref in (m_ref, l_ref, lsum_ref, acc_ref))
        qs, bound = [], []
        for c in heads:
            qn = _head_rmsnorm128(q_ref[0, :qt, lanes[c]].astype(F32), qg_ref[...]) * QK_SCALE
            qs.append(jnp.concatenate([jnp.where(lo, qn, 0.0), jnp.where(lo, 0.0, qn)], axis=0).astype(BF16))
            q_sq = _nt_dot(jnp.ones((8, BLK), BF16), (qs[c].astype(F32) ** 2).astype(BF16))[0:1]
            bound.append(jnp.sqrt(q_sq * kmax_ref[c, 0:1, 0:1]) * BOUND_SLACK
                         + (kmax_ref[c, 1:2, 0:1] + BOUND_SLACK))

        def scores(c, first, n):
            rows = pl.ds(pl.multiple_of(first * QB, QB), n * QB)
            return _nt_dot(kn_ref[rows, lanes[c]], qs[c])

        def values(c, first, n):
            return jnp.concatenate([vt_ref[first + b, lanes[c], :] for b in range(n)], axis=1)

        def far_piece(start, hi):
            pieces = []
            for c in heads:
                st = scores(c, start, DF_FAR)
                new_keys = _row_iota(st.shape) < (hi - start) * QB
                pieces.append((jnp.where(new_keys, st, -jnp.inf), values(c, start, DF_FAR), c))
            return pieces

        def near_piece(first, n):
            pieces = []
            for c in heads:
                b = bias_ref[c, (2 - n) * QB:, :qt]
                st = scores(c, first, n) + jnp.concatenate([b, b], axis=1)
                causal = (_row_iota(st.shape) - (n - 1) * QB) <= (_lane_iota(st.shape) & (qt - 1))
                pieces.append((jnp.where(causal, st, -jnp.inf), values(c, first, n), c))
            return pieces

        def attend(step):
            _attend_windows(i, DF_FAR, DF_WINDOW_GROUP, far_piece, near_piece, step)

        for c in heads:
            l[c][...] = jnp.zeros_like(l[c])
            acc[c][...] = jnp.zeros_like(acc[c])
        attend(lambda pieces: _bounded_pieces_t(pieces, bound, l, acc))
        for c in heads:
            lsum[c][0:1] = jnp.sum(l[c][...], axis=0, keepdims=True)

        @pl.when(functools.reduce(jnp.minimum, [jnp.min(lsum[c][0:1]) for c in heads]) < UNDERFLOW_GUARD)
        def _():
            for c in heads:
                m[c][...] = jnp.full(m[c].shape, M_INIT, F32)
                l[c][...] = jnp.zeros_like(l[c])
                acc[c][...] = jnp.zeros_like(acc[c])
            attend(lambda pieces: _online_pieces_t(pieces, m, l, acc))
            for c in heads:
                lsum[c][0:1] = l[c][0:1]

        lv = lamv_ref[...]
        lam = (jnp.exp(jnp.sum(lv[0:1] * lv[1:2], axis=-1, keepdims=True))
               - jnp.exp(jnp.sum(lv[2:3] * lv[3:4], axis=-1, keepdims=True)) + lam_init)
        for c in heads:
            a = acc[c][...] / lsum[c][0:1]
            y = (a[:, :qt] - lam * a[:, qt:]).T
            y = y * lax.rsqrt(jnp.mean(y * y, axis=-1, keepdims=True) + EPS) * sub_ref[...]
            o_ref[0, :qt, lanes[c]] = (y * (1.0 - lam_init)).astype(o_ref.dtype)
        if qt < QB:
            o_ref[0, qt:] = jnp.zeros((QB - qt, DF_GROUP * BLK), o_ref.dtype)

    if tail_q == QB:
        query_block(QB)
    else:
        @pl.when(i < nq - 1)
        def _():
            query_block(QB)

        @pl.when(i == nq - 1)
        def _():
            query_block(tail_q)


def _dft_attn(z3, rel_bias, qg, kg, bkt_t, lamv, subln, lam_init, t):
    b, tp, _ = z3.shape
    nq = tp // QB
    assert nq >= DF_FAR
    kern = functools.partial(_dft_kernel, nq=nq, tail_q=_tail_rows(t, tp, 64), lam_init=lam_init,
                             head_off=SP_HEADS)
    vec = lambda bb, g, i: (0, 0)
    w = DF_GROUP * BLK
    per_head = lambda rows, cols: pltpu.VMEM((DF_GROUP, rows, cols), F32)
    return pl.pallas_call(
        kern,
        out_shape=jax.ShapeDtypeStruct((b, tp, DF_HEADS * BLK), BF16),
        grid=(b, DF_HEADS // DF_GROUP, nq),
        in_specs=[pl.BlockSpec(memory_space=pltpu.SMEM),
                  pl.BlockSpec((1, QB, w), lambda bb, g, i: (bb, i, Q_DF // w + g)),
                  pl.BlockSpec((1, tp, w), lambda bb, g, i: (bb, 0, K_DF // w + g)),
                  pl.BlockSpec((1, tp, w), lambda bb, g, i: (bb, 0, V_DF // w + g)),
                  pl.BlockSpec((1, BLK), vec),
                  pl.BlockSpec((1, BLK), vec),
                  pl.BlockSpec((2 * QB, QB), vec),
                  pl.BlockSpec((8, BLK), vec),
                  pl.BlockSpec((1, BLK), vec)],
        out_specs=pl.BlockSpec((1, QB, w), lambda bb, g, i: (bb, i, g)),
        scratch_shapes=[pltpu.VMEM((tp, w), BF16),
                        pltpu.VMEM((nq, w, QB), BF16),
                        per_head(2 * QB, QB),
                        per_head(8, BLK),
                        per_head(8, 2 * QB),
                        per_head(8, 2 * QB),
                        per_head(8, 2 * QB),
                        per_head(BLK, 2 * QB)],
        compiler_params=_params("parallel", "parallel", "arbitrary"),
        name="df_attn",
    )(rel_bias, z3, z3, z3, qg, kg, bkt_t, lamv, subln)


def _sp_kernel(tab_ref, q_ref, k_ref, v_ref, qix_ref, kwq_ref, kwk_ref, qg_ref, kg_ref, bkt_ref,
               rep_ref, bcast_ref, before_ref, o_ref,
               kn_ref, kx_ref, vt_ref, bias_ref, kmax_ref, keys_ref, dig_ref, thr_ref, wb_ref,
               m_ref, l_ref, lsum_ref, acc_ref, *, nq, top_k):
    i = pl.program_id(1)
    nh = SP_HEADS
    w = nh * HEAD_DIM
    lane = _lane_iota((QB, w))
    head_of_lane = lane // HEAD_DIM

    def norm256(xf, g):
        return jnp.concatenate([_head_rmsnorm128(xf[:, :BLK], g[:, :BLK]),
                                _head_rmsnorm128(xf[:, BLK:], g[:, BLK:])], axis=1)

    def dup(x):
        return jnp.concatenate([x, x], axis=1)

    @pl.when(i == 0)
    def _prep():
        kmax_ref[...] = jnp.zeros_like(kmax_ref)

        def kbody(c, carry):
            r = pl.multiple_of(c * QB, QB)
            kn = norm256(k_ref[0, pl.ds(r, QB), :].astype(F32), kg_ref[...]).astype(BF16)
            kn_ref[pl.ds(r, QB), :] = kn
            ksq = kn.astype(F32) ** 2
            head_norms = functools.reduce(jnp.maximum, [
                jnp.sum(jnp.where(head_of_lane == h, ksq, 0.0), axis=-1, keepdims=True) for h in range(nh)])
            kmax_ref[0:1] = jnp.maximum(kmax_ref[0:1], jnp.max(head_norms))
            kx_ref[pl.ds(r, QB), :] = _dot(kwk_ref[0, pl.ds(r, QB), :], rep_ref[...]).astype(BF16)
            vt_ref[c] = v_ref[0, pl.ds(r, QB), :].astype(F32).T.astype(BF16)
            return carry

        lax.fori_loop(0, nq, kbody, 0)
        bias_max = jnp.float32(0.0)
        for h in range(nh):
            bias_ref[:, h * QB:(h + 1) * QB] = _bias_tile(tab_ref, bkt_ref[...], h)
            for b in range(N_BUCKETS - 1):
                bias_max = jnp.maximum(bias_max, (tab_ref[b, h] - tab_ref[N_BUCKETS - 1, h]) * LOG2E)
        kmax_ref[1:2] = jnp.full((1, BLK), bias_max, F32)

    key_causal = _row_iota((QB, QB)) <= _lane_iota((QB, QB))

    wb_ref[...] = _dot(kwq_ref[0], bcast_ref[...])
    qix = qix_ref[0].astype(F32)
    ix_head = lane // IDX_DIM
    qx = jnp.concatenate([jnp.where(ix_head == h, qix, 0.0) for h in range(IDX_HEADS)], axis=0).astype(BF16)

    def score_tiles(blocks, last_is_diagonal):
        dots = [_nt_dot(qx, kx_ref[pl.ds(pl.multiple_of(j * QB, QB), QB), :]) for j in blocks]
        for n, (j, d) in enumerate(zip(blocks, dots)):
            sc = jnp.zeros((QB, QB), F32)
            for h in range(IDX_HEADS):
                sc = sc + dup(wb_ref[:, h * BLK:(h + 1) * BLK]) * jnp.maximum(d[h * QB:(h + 1) * QB], 0.0)
            sc = jnp.where(sc == 0.0, 0.0, sc).T
            bits = lax.bitcast_convert_type(sc, jnp.int32)
            key = jnp.where(bits < 0, bits ^ jnp.int32(0x7FFFFFFF), bits)
            if last_is_diagonal and n == len(blocks) - 1:
                key = jnp.where(key_causal, key, jnp.int32(INT_MIN))
            keys_ref[j] = key
            dig_ref[j] = jnp.right_shift(key, 16).astype(jnp.int16)

    def group_body(g, c):
        score_tiles([SCORE_GROUP * g + n for n in range(SCORE_GROUP)], False)
        return c

    lax.fori_loop(0, i // SCORE_GROUP, group_body, 0)
    first_left = SCORE_GROUP * (i // SCORE_GROUP)
    for n_left in range(1, SCORE_GROUP + 1):
        @pl.when(i - first_left == n_left - 1)
        def _(n_left=n_left):
            score_tiles([first_left + n for n in range(n_left)], True)

    nblk = i + 1

    def count(pred):
        def cbody(j, c):
            hit = jnp.where(pred(keys_ref[j]), 1.0, 0.0)
            return c + jnp.sum(hit.reshape(QB // COUNT_ROWS, COUNT_ROWS, QB), axis=0)

        c = lax.fori_loop(0, nblk, cbody, jnp.zeros((COUNT_ROWS, QB), F32))
        return jnp.sum(c, axis=0, keepdims=True)

    def count16(pred):
        def cbody(j, c):
            hit = jnp.where(pred(dig_ref[j]), jnp.int16(1), jnp.int16(0)).reshape(QB // DIGIT_ROWS, DIGIT_ROWS, QB)
            return c + functools.reduce(jnp.add, [hit[r] for r in range(QB // DIGIT_ROWS)])

        c = lax.fori_loop(0, nblk, cbody, jnp.zeros((DIGIT_ROWS, QB), jnp.int16))
        return jnp.sum(c.astype(F32), axis=0, keepdims=True)

    n_all = jnp.zeros((1, QB), F32) + (nblk * QB).astype(F32)

    def bisect16(need):
        def bit_body(b, carry):
            cur, cnt_cur = carry
            cand = cur + jnp.left_shift(jnp.int32(1), 15 - b)
            cand16 = cand.astype(jnp.int16)
            cnt = count16(lambda d: d >= cand16)
            ok = cnt >= need
            return jnp.where(ok, cand, cur), jnp.where(ok, cnt, cnt_cur)

        return lax.fori_loop(0, 16, bit_body, (jnp.full((1, QB), I16_MIN, jnp.int32), n_all))

    t_hi, c_ge_hi = bisect16(float(top_k))
    t_hi16 = t_hi.astype(jnp.int16)
    c_gt_hi = count16(lambda d: d > t_hi16)
    base = jnp.left_shift(t_hi, 16)

    def low_digits(j, c):
        y = keys_ref[j] - base
        dig_ref[j] = jnp.where(jnp.right_shift(y, 16) == 0, y + I16_MIN, I16_MIN).astype(jnp.int16)
        return c

    lax.fori_loop(0, nblk, low_digits, 0)
    t_lo, c_ge_lo = bisect16(float(top_k) - c_gt_hi)
    thr = base + (t_lo - I16_MIN)
    cge = c_gt_hi + jnp.where(t_lo > I16_MIN, c_ge_lo, c_ge_hi - c_gt_hi)
    thr_ref[...] = jnp.broadcast_to(thr, thr_ref.shape)

    tie = jnp.where((cge > float(top_k)) & (thr > INT_MIN), 1, 0)

    @pl.when(jnp.max(tie) > 0)
    def _ties():
        need = float(top_k) - count(lambda k: k > thr)

        def tbody(j, run):
            kj = keys_ref[j]
            eq = kj == thr
            eqf = jnp.where(eq, 1.0, 0.0)
            rank = run + _dot(before_ref[...], eqf.astype(BF16))
            keep = jnp.where(kj > thr, 1, jnp.where(eq & (rank < need), 1, -1))
            keys_ref[j] = keep.astype(jnp.int32)
            return run + jnp.sum(eqf, axis=0, keepdims=True)

        lax.fori_loop(0, nblk, tbody, jnp.zeros((1, QB), F32))
        thr_ref[...] = jnp.zeros_like(thr_ref)

    qn = norm256(q_ref[0].astype(F32), qg_ref[...]) * QK_SCALE
    qs = jnp.concatenate([jnp.where(head_of_lane == h, qn, 0.0) for h in range(nh)], axis=0).astype(BF16)
    q_sq = _nt_dot(jnp.ones((8, w), BF16), (qs.astype(F32) ** 2).astype(BF16))[0:1]
    bound = jnp.sqrt(q_sq * kmax_ref[0:1, 0:1]) * BOUND_SLACK + (kmax_ref[1:2, 0:1] + BOUND_SLACK)

    thr_sel = thr_ref[0:1, :]

    def as_mask(sel):
        return lax.bitcast_convert_type(jnp.where(sel, 0.0, -jnp.inf), jnp.int32)

    def mask_body(j, c):
        keys_ref[j] = as_mask(keys_ref[j] >= thr_sel)
        return c

    lax.fori_loop(0, i, mask_body, 0)
    keys_ref[i] = as_mask((keys_ref[i] >= thr_sel) & key_causal)

    def scores(first, n):
        return _nt_dot(kn_ref[pl.ds(pl.multiple_of(first * QB, QB), n * QB), :], qs)

    def values(first, n):
        return jnp.concatenate([vt_ref[first + c] for c in range(n)], axis=1)

    def selection(first, n):
        return jnp.concatenate([lax.bitcast_convert_type(keys_ref[first + c], F32) for c in range(n)],
                               axis=0)

    def far_piece(start, hi):
        m = selection(start, SP_FAR)
        m = jnp.where(_row_iota(m.shape) < (hi - start) * QB, m, -jnp.inf)
        return [(scores(start, SP_FAR) + jnp.concatenate([m] * nh, axis=1), values(start, SP_FAR), 0)]

    def near_piece(first, n):
        m = selection(first, n)
        return [(scores(first, n) + (jnp.concatenate([m] * nh, axis=1) + bias_ref[(2 - n) * QB:, :]),
                 values(first, n), 0)]

    def attend(step):
        _attend_windows(i, SP_FAR, SP_WINDOW_GROUP, far_piece, near_piece, step)

    l_ref[...] = jnp.zeros_like(l_ref)
    acc_ref[...] = jnp.zeros_like(acc_ref)
    attend(lambda pieces: _bounded_pieces_t(pieces, [bound], [l_ref], [acc_ref]))
    lsum_ref[0:1] = jnp.sum(l_ref[...], axis=0, keepdims=True)

    @pl.when(jnp.min(lsum_ref[0:1]) < UNDERFLOW_GUARD)
    def _():
        m_ref[...] = jnp.full(m_ref.shape, M_INIT, F32)
        l_ref[...] = jnp.zeros_like(l_ref)
        acc_ref[...] = jnp.zeros_like(acc_ref)
        attend(lambda pieces: _online_pieces_t(pieces, [m_ref], [l_ref], [acc_ref]))
        lsum_ref[0:1] = l_ref[0:1]

    a = acc_ref[...] / lsum_ref[0:1]
    out_t = jnp.concatenate([a[h * HEAD_DIM:(h + 1) * HEAD_DIM, h * QB:(h + 1) * QB] for h in range(nh)], axis=0)
    o_ref[0] = out_t.T.astype(o_ref.dtype)


def _sp_attn(z3, rel_bias, qg, kg, bkt_t, rep, bcast, before, top_k):
    b, tp, _ = z3.shape
    nq = tp // QB
    assert nq >= SP_FAR
    w = SP_HEADS * HEAD_DIM
    kern = functools.partial(_sp_kernel, nq=nq, top_k=top_k)
    c2 = lambda bb, i: (0, 0)
    return pl.pallas_call(
        kern,
        out_shape=jax.ShapeDtypeStruct((b, tp, w), BF16),
        grid=(b, nq),
        in_specs=[pl.BlockSpec(memory_space=pltpu.SMEM),
                  pl.BlockSpec((1, QB, w), lambda bb, i: (bb, i, Q_SP // w)),
                  pl.BlockSpec((1, tp, w), lambda bb, i: (bb, 0, K_SP // w)),
                  pl.BlockSpec((1, tp, w), lambda bb, i: (bb, 0, V_SP // w)),
                  pl.BlockSpec((1, QB, w), lambda bb, i: (bb, i, Q_IX // w)),
                  pl.BlockSpec((1, QB, w), lambda bb, i: (bb, i, KW_IX // w)),
                  pl.BlockSpec((1, tp, w), lambda bb, i: (bb, 0, KW_IX // w)),
                  pl.BlockSpec((1, w), c2),
                  pl.BlockSpec((1, w), c2),
                  pl.BlockSpec((2 * QB, QB), c2),
                  pl.BlockSpec((w, w), c2),
                  pl.BlockSpec((w, IDX_HEADS * BLK), c2),
                  pl.BlockSpec((QB, QB), c2)],
        out_specs=pl.BlockSpec((1, QB, w), lambda bb, i: (bb, i, 0)),
        scratch_shapes=[pltpu.VMEM((tp, w), BF16),
                        pltpu.VMEM((tp, w), BF16),
                        pltpu.VMEM((nq, w, QB), BF16),
                        pltpu.VMEM((2 * QB, SP_HEADS * QB), F32),
                        pltpu.VMEM((8, BLK), F32),
                        pltpu.VMEM((nq, QB, QB), jnp.int32),
                        pltpu.VMEM((nq, QB, QB), jnp.int16),
                        pltpu.VMEM((8, QB), jnp.int32),
                        pltpu.VMEM((QB, IDX_HEADS * BLK), F32),
                        pltpu.VMEM((8, SP_HEADS * QB), F32),
                        pltpu.VMEM((8, SP_HEADS * QB), F32),
                        pltpu.VMEM((8, SP_HEADS * QB), F32),
                        pltpu.VMEM((w, SP_HEADS * QB), F32)],
        compiler_params=_params("parallel", "arbitrary"),
        name="sp_attn",
    )(rel_bias, z3, z3, z3, z3, z3, z3, qg, kg, bkt_t, rep, bcast, before)


def _mix_kernel(h_ref, gsb_ref, gsp_ref, gdf_ref, bg_ref, ysb_ref, ysp_ref, ydf_ref,
                wsb_ref, wsp_ref, wdf_ref, wo_ref, o_ref):
    def branch(g_ref, k, y_ref, w_ref):
        gate = jax.nn.sigmoid(g_ref[...].astype(F32) + bg_ref[:, k * D_MODEL:(k + 1) * D_MODEL])
        return gate * _dot(y_ref[...], w_ref[...])

    merged = (branch(gsb_ref, 0, ysb_ref, wsb_ref) + branch(gsp_ref, 1, ysp_ref, wsp_ref)
              + branch(gdf_ref, 2, ydf_ref, wdf_ref))
    o_ref[...] = h_ref[...] + _dot(merged.astype(BF16), wo_ref[...])


def _mix_out(h, z, b_gate, y_sb, y_sp, y_df, w_sb, w_sp, w_df, w_o):
    m, d = h.shape
    tm = _pick_rows(m, 512)
    row = lambda i: (i, 0)
    fixed = lambda i: (0, 0)
    return pl.pallas_call(
        _mix_kernel,
        out_shape=jax.ShapeDtypeStruct((m, d), F32),
        grid=(m // tm,),
        in_specs=[pl.BlockSpec((tm, d), row),
                  pl.BlockSpec((tm, d), lambda i: (i, G_SB // D_MODEL)),
                  pl.BlockSpec((tm, d), lambda i: (i, G_SP // D_MODEL)),
                  pl.BlockSpec((tm, d), lambda i: (i, G_DF // D_MODEL)),
                  pl.BlockSpec((1, 3 * d), fixed),
                  pl.BlockSpec((tm, y_sb.shape[1]), row),
                  pl.BlockSpec((tm, y_sp.shape[1]), row),
                  pl.BlockSpec((tm, y_df.shape[1]), row),
                  pl.BlockSpec(w_sb.shape, fixed),
                  pl.BlockSpec(w_sp.shape, fixed),
                  pl.BlockSpec(w_df.shape, fixed),
                  pl.BlockSpec(w_o.shape, fixed)],
        out_specs=pl.BlockSpec((tm, d), row),
        compiler_params=_params("parallel"),
        name="mix_out",
    )(h, z, z, z, b_gate.reshape(1, 3 * d), y_sb, y_sp, y_df, w_sb, w_sp, w_df, w_o)


def _ffn_kernel(h_ref, g_ref, wu_ref, cw_ref, cb_ref, wd_ref, o_ref, gbuf_ref, carry_ref, *, tm, tp, tf):
    r = pl.program_id(0)
    x = h_ref[...]
    ms = jnp.mean(x * x, axis=-1, keepdims=True)
    u = (x * lax.rsqrt(ms + EPS) * g_ref[...]).astype(BF16)

    @pl.when(r == 0)
    def _():
        carry_ref[...] = jnp.zeros_like(carry_ref)

    seq_start = lax.rem(tp - lax.rem(r * tm, tp), tp)
    local = lax.broadcasted_iota(jnp.int32, (tm, 1), 0)
    tap1 = local != seq_start
    tap2 = tap1 & (local != seq_start + 1)
    out = x
    for f in range(D_FF // tf):
        cols = slice(f * tf, (f + 1) * tf)
        gate = _dot(u, wu_ref[:, cols])
        val = _dot(u, wu_ref[:, D_FF + f * tf:D_FF + (f + 1) * tf])
        gbuf_ref[0:8] = carry_ref[f]
        gbuf_ref[8:8 + tm] = gate
        carry_ref[f] = gate[tm - 8:tm]
        g1 = jnp.where(tap1, gbuf_ref[7:7 + tm], 0.0)
        g2 = jnp.where(tap2, gbuf_ref[6:6 + tm], 0.0)
        conv = cb_ref[:, cols] + cw_ref[0:1, cols] * g2 + cw_ref[1:2, cols] * g1 + cw_ref[2:3, cols] * gate
        act = conv * jax.nn.sigmoid(conv) * val
        out = out + _dot(act.astype(BF16), wd_ref[cols, :])
    o_ref[...] = out


def _ffn(h, gain, w_up, conv_w, conv_b, w_down, tp):
    m, d = h.shape
    tm = _pick_rows(m, 512)
    assert tm <= tp
    tf = D_FF // 2
    nf = D_FF // tf
    kern = functools.partial(_ffn_kernel, tm=tm, tp=tp, tf=tf)
    fixed = lambda r: (0, 0)
    resident = pl.Buffered(1)
    return pl.pallas_call(
        kern,
        out_shape=jax.ShapeDtypeStruct((m, d), F32),
        grid=(m // tm,),
        in_specs=[pl.BlockSpec((tm, d), lambda r: (r, 0)),
                  pl.BlockSpec((1, d), fixed),
                  pl.BlockSpec((d, 2 * D_FF), fixed, pipeline_mode=resident),
                  pl.BlockSpec((8, D_FF), fixed),
                  pl.BlockSpec((1, D_FF), fixed),
                  pl.BlockSpec((D_FF, d), fixed, pipeline_mode=resident)],
        out_specs=pl.BlockSpec((tm, d), lambda r: (r, 0)),
        scratch_shapes=[pltpu.VMEM((tm + 8, tf), F32),
                        pltpu.VMEM((nf, 8, tf), F32)],
        compiler_params=_params("arbitrary"),
        name="conv_ffn",
    )(h, gain.reshape(1, d), w_up, conv_w, conv_b.reshape(1, D_FF), w_down)


def _permute_w_in(w):
    n_attn = KW_IX - Q_SB + IDX_DIM + IDX_HEADS
    n_gate = 3 * D_MODEL
    n_df = 3 * DF_HEADS * 2 * HEAD_DIM
    gates = w[:, n_attn + n_df:]
    attn = w[:, :n_attn]
    pad = jnp.zeros((w.shape[0], Q_DF - Q_SB - n_attn), w.dtype)
    df = w[:, n_attn:n_attn + n_df]
    out = jnp.concatenate([gates, attn, pad, df], axis=1)
    assert gates.shape[1] == n_gate and out.shape[1] == NZ
    return out


def kernel(x, meta_tokens, rel_bias, attn_norm, w_in, b_gate, q_norm_sp, k_norm_sp, q_norm_df, k_norm_df, lam_q1, lam_k1, lam_q2, lam_k2, subln_df, w_br_sb, w_br_sp, w_br_df, w_out, ffn_norm, w_up, conv_w, conv_b, w_down):
    b, s, d = x.shape
    depth = w_in.shape[0]
    t = N_META + s
    tp = -(-t // QB) * QB
    top_k = min(TOPK_MAX, t // 4)
    m = b * tp

    meta = jnp.broadcast_to(meta_tokens[None].astype(x.dtype), (b, N_META, d))
    h = jnp.concatenate([meta, x, jnp.zeros((b, tp - t, d), x.dtype)], axis=1).reshape(m, d)

    bkt_t = jnp.asarray(np.ascontiguousarray(_bucket_tile().T))
    u_mat = jnp.asarray(_sb_prefix_matrix(), BF16)
    rep, bcast, before = (jnp.asarray(a, BF16) for a in _ix_select_matrices())
    rel_bias = rel_bias.astype(F32)

    for l in range(depth):
        lam_init = 0.8 - 0.6 * math.exp(-0.3 * l)
        z = _in_proj(h, attn_norm[l], _permute_w_in(w_in[l]).astype(BF16))
        z3 = z.reshape(b, tp, NZ)
        y_sb = _sb_attn(z3, u_mat, t)
        y_sp = _sp_attn(z3, rel_bias,
                        jnp.tile(q_norm_sp[l].astype(F32), SP_HEADS).reshape(1, -1),
                        jnp.tile(k_norm_sp[l].astype(F32), SP_HEADS).reshape(1, -1),
                        bkt_t, rep, bcast, before, top_k)
        lamv = jnp.zeros((8, BLK), F32).at[:4, :HEAD_DIM].set(
            jnp.stack([lam_q1[l], lam_k1[l], lam_q2[l], lam_k2[l]]).astype(F32))
        y_df = _dft_attn(z3, rel_bias,
                         jnp.tile(q_norm_df[l].astype(F32), 2).reshape(1, -1),
                         jnp.tile(k_norm_df[l].astype(F32), 2).reshape(1, -1),
                         bkt_t, lamv, subln_df[l].astype(F32).reshape(1, -1), lam_init, t)
        h = _mix_out(h, z, b_gate[l], y_sb.reshape(m, -1), y_sp.reshape(m, -1), y_df.reshape(m, -1),
                     w_br_sb[l].astype(BF16), w_br_sp[l].astype(BF16), w_br_df[l].astype(BF16),
                     w_out[l].astype(BF16))
        cw = jnp.zeros((8, D_FF), F32).at[:conv_w.shape[1]].set(conv_w[l])
        h = _ffn(h, ffn_norm[l], w_up[l].astype(BF16), cw, conv_b[l], w_down[l].astype(BF16), tp)

    return h.reshape(b, tp, d)[:, N_META:t]
```

```python
import functools
import math

import numpy as np
import jax
import jax.numpy as jnp
from jax import lax
from jax.experimental import pallas as pl
from jax.experimental.pallas import tpu as pltpu

D_MODEL = 1024
HEAD_DIM = 64
N_META = 16
BLK = 128
QB = 256
SB_HEADS = 4
SP_HEADS = 4
IDX_HEADS = 8
IDX_DIM = 32
TOPK_MAX = 256
DF_HEADS = 4
N_BUCKETS = 32
MAX_DISTANCE = 128
D_FF = 2816
EPS = 1e-6
LOG2E = math.log2(math.e)
QK_SCALE = HEAD_DIM ** -0.5 * LOG2E
M_INIT = -1e30
BOUND_SLACK = 1.02
UNDERFLOW_GUARD = 2.0 ** -100
INT_MIN = -2 ** 31
I16_MIN = -2 ** 15
COUNT_ROWS = 32
DIGIT_ROWS = 64
SCORE_GROUP = 4
SP_WINDOW_GROUP = 3
DF_WINDOW_GROUP = 1
DF_GROUP = 4
SB_FAR = 2
SP_FAR = 2
DF_FAR = 4

G_SB, G_SP, G_DF = 0, 1024, 2048
Q_SB, K_SB, V_SB = 3072, 3328, 3584
Q_SP, K_SP, V_SP = 3840, 4096, 4352
Q_IX, KW_IX = 4608, 4864
Q_DF, K_DF, V_DF = 5120, 5632, 6144
NZ = 6656
W_IX_LANE = IDX_DIM

VMEM_LIMIT = 56 * 1024 * 1024

F32 = jnp.float32
BF16 = jnp.bfloat16
NT_DIMS = (((1,), (1,)), ((), ()))


def _nt_dot(a, b):
    return lax.dot_general(a, b, NT_DIMS, preferred_element_type=F32)


def _dot(a, b):
    return jnp.dot(a, b, preferred_element_type=F32)


def _params(*sem):
    return pltpu.CompilerParams(dimension_semantics=sem, vmem_limit_bytes=VMEM_LIMIT)


def _pick_rows(m, cap):
    for c in (2048, 1024, 512, 256):
        if c <= cap and m % c == 0:
            return c
    raise ValueError(f"row count {m} is not a multiple of {QB}")


def _bucket_np(rel):
    n = np.maximum(rel, 0)
    max_exact = N_BUCKETS // 2
    nf = np.maximum(n, 1).astype(np.float32)
    large = max_exact + (np.log(nf / np.float32(max_exact)) / np.float32(math.log(MAX_DISTANCE / max_exact))
                         * np.float32(N_BUCKETS - max_exact)).astype(np.int32)
    return np.where(n < max_exact, n, np.minimum(large, N_BUCKETS - 1)).astype(np.int32)


def _bucket_tile():
    tq = np.arange(QB)[:, None]
    c = np.arange(2 * QB)[None, :]
    return _bucket_np(tq - c + QB)


def _sb_prefix_matrix():
    sp = np.arange(2 * BLK)[:, None] % BLK
    c = np.arange(2 * BLK)[None, :]
    return np.where(c < BLK, sp > c, True).astype(np.float32)


def _ix_select_matrices():
    c = np.arange(QB)[:, None]
    col = np.arange(QB)[None, :]
    rep = ((c < IDX_DIM) & (c == col % IDX_DIM)).astype(np.float32)
    col8 = np.arange(IDX_HEADS * BLK)[None, :]
    bcast = (c == W_IX_LANE + col8 // BLK).astype(np.float32)
    before = (col < c).astype(np.float32)
    return rep, bcast, before


def _lane_iota(shape):
    return lax.broadcasted_iota(jnp.int32, shape, len(shape) - 1)


def _row_iota(shape):
    return lax.broadcasted_iota(jnp.int32, shape, 0)


def _head_rmsnorm128(xf, gain):
    lo = _lane_iota((1, BLK)) < HEAD_DIM
    ss = xf * xf
    s_lo = jnp.sum(jnp.where(lo, ss, 0.0), axis=-1, keepdims=True)
    s_hi = jnp.sum(jnp.where(lo, 0.0, ss), axis=-1, keepdims=True)
    ms = jnp.where(lo, s_lo, s_hi) * (1.0 / HEAD_DIM)
    return xf * lax.rsqrt(ms + EPS) * gain


def _bias_tile(tab_ref, bk, head):
    far = tab_ref[N_BUCKETS - 1, head]
    acc = jnp.zeros(bk.shape, F32)
    for b in range(N_BUCKETS - 1):
        acc = jnp.where(bk == b, (tab_ref[b, head] - far) * LOG2E, acc)
    return acc


def _softmax_piece_t(st, vt, m_ref, l_ref, acc_ref):
    m_old = m_ref[0:1]
    m_new = jnp.maximum(m_old, jnp.max(st, axis=0, keepdims=True))
    alpha = jnp.exp2(m_old - m_new)
    p = jnp.exp2(st - m_new)
    l_ref[0:1] = alpha * l_ref[0:1] + jnp.sum(p, axis=0, keepdims=True)
    m_ref[0:1] = m_new
    acc_ref[...] = alpha * acc_ref[...] + _dot(vt, p.astype(BF16))


def _online_pieces_t(pieces, m_refs, l_refs, acc_refs):
    for st, vt, c in pieces:
        _softmax_piece_t(st, vt, m_refs[c], l_refs[c], acc_refs[c])


def _bounded_pieces_t(pieces, bounds, l_refs, acc_refs):
    ps = [jnp.exp2(st - bounds[c]) for st, _, c in pieces]
    for c in sorted({c for _, _, c in pieces}):
        mine = [(vt, p) for (_, vt, cc), p in zip(pieces, ps) if cc == c]
        l_refs[c][...] += functools.reduce(
            jnp.add, [jnp.sum(p.reshape(p.shape[0] // 8, 8, p.shape[1]), axis=0) for _, p in mine])
        acc_refs[c][...] += functools.reduce(jnp.add, [_dot(vt, p.astype(BF16)) for vt, p in mine])


def _attend_windows(i, width, group, far_fn, near_fn, step):
    n_far = jnp.maximum(i - 1, 0)
    n_win = (n_far + width - 1) // width

    def far(p):
        hi = n_far - p * width
        return far_fn(jnp.maximum(hi - width, 0), hi)

    def body(g, carry):
        step(sum([far(group * g + n) for n in range(group)], []))
        return carry

    lax.fori_loop(0, n_win // group, body, 0)
    first_left = group * (n_win // group)
    for n_left in range(group):
        @pl.when((n_win - first_left == n_left) & (i >= 1))
        def _(n_left=n_left):
            step(sum([far(first_left + n) for n in range(n_left)], []) + near_fn(i - 1, 2))

    @pl.when(i == 0)
    def _():
        step(near_fn(0, 1))


def _for_far_pieces(n_far, width, piece_fn):
    def body(p, carry):
        hi = n_far - p * width
        piece_fn(jnp.maximum(hi - width, 0), hi)
        return carry

    lax.fori_loop(0, (n_far + width - 1) // width, body, 0)


def _in_proj_kernel(h_ref, g_ref, w_ref, o_ref, *, tn):
    x = h_ref[...]
    ms = jnp.mean(x * x, axis=-1, keepdims=True)
    u = (x * lax.rsqrt(ms + EPS) * g_ref[...]).astype(BF16)
    for c in range(o_ref.shape[1] // tn):
        cols = slice(c * tn, (c + 1) * tn)
        o_ref[:, cols] = _dot(u, w_ref[:, cols]).astype(o_ref.dtype)


def _in_proj(h, gain, w):
    m, d = h.shape
    n = w.shape[1]
    tm = _pick_rows(m, 512)
    return pl.pallas_call(
        functools.partial(_in_proj_kernel, tn=512),
        out_shape=jax.ShapeDtypeStruct((m, n), BF16),
        grid=(m // tm,),
        in_specs=[pl.BlockSpec((tm, d), lambda i: (i, 0)),
                  pl.BlockSpec((1, d), lambda i: (0, 0)),
                  pl.BlockSpec((d, n), lambda i: (0, 0), pipeline_mode=pl.Buffered(1))],
        out_specs=pl.BlockSpec((tm, n), lambda i: (i, 0)),
        compiler_params=_params("parallel"),
        name="in_proj",
    )(h, gain.reshape(1, d), w)


def _sb_kernel(q_ref, k_ref, v_ref, u_ref, o_ref, tot_ref, acc_ref, *, nq, tail_q):
    i = pl.program_id(1)
    nh = SB_HEADS

    def query_block(qt):
        head_of_lane = _lane_iota((qt, nh * HEAD_DIM)) // HEAD_DIM
        q = q_ref[0, :qt].astype(F32) * QK_SCALE
        qs = jnp.concatenate([jnp.where(head_of_lane == h, q, 0.0) for h in range(nh)], axis=0).astype(BF16)
        tot = tot_ref.at[:nh * qt]
        acc = acc_ref.at[:nh * qt]
        tot[...] = jnp.zeros_like(tot)
        acc[...] = jnp.zeros_like(acc)

        def piece(start, n_blocks, hi=None):
            r = pl.multiple_of(start * QB, QB)
            z_all = _nt_dot(qs, k_ref[0, pl.ds(r, n_blocks * QB), :])
            if hi is not None:
                col = _lane_iota((1, n_blocks * QB))
                z_all = z_all + jnp.where(col < (hi - start) * QB, 0.0, -jnp.inf)
            run = tot[...]
            n_sub = n_blocks * QB // BLK
            ws = [None] * n_sub
            for c in reversed(range(n_sub)):
                z = z_all[:, c * BLK:(c + 1) * BLK]
                sp = jnp.maximum(z, 0.0) + jnp.log2(1.0 + jnp.exp2(-jnp.abs(z)))
                l1m = -sp
                if hi is None:
                    mask = (_lane_iota(z.shape) + c * BLK) < (_row_iota(z.shape) & (qt - 1))
                    l1m = jnp.where(mask, l1m, 0.0)
                l1m_hi = l1m.astype(BF16)
                l1m_lo = (l1m - l1m_hi.astype(F32)).astype(BF16)
                rs = _dot(jnp.concatenate([l1m_hi, l1m_lo], axis=1), u_ref[...])
                w = jnp.exp2((z - sp) + rs[:, :BLK] + run)
                if hi is None:
                    w = jnp.where(mask, w, 0.0)
                ws[c] = w.astype(BF16)
                run = run + rs[:, BLK:]
            tot[...] = run
            acc[...] += _dot(jnp.concatenate(ws, axis=1), v_ref[0, pl.ds(r, n_blocks * QB), :])

        piece(i, 1)
        _for_far_pieces(i, SB_FAR, lambda start, hi: piece(start, SB_FAR, hi))

        a = acc[...]
        out = a[:qt]
        for h in range(1, nh):
            out = jnp.where(head_of_lane == h, a[h * qt:(h + 1) * qt], out)
        o_ref[0, :qt] = out.astype(o_ref.dtype)
        if qt < QB:
            o_ref[0, qt:] = jnp.zeros((QB - qt, nh * HEAD_DIM), o_ref.dtype)

    if tail_q == QB:
        query_block(QB)
    else:
        @pl.when(i < nq - 1)
        def _():
            query_block(QB)

        @pl.when(i == nq - 1)
        def _():
            query_block(tail_q)


def _tail_rows(t, tp, granule):
    real = t - (tp - QB)
    return min(QB, -(-real // granule) * granule)


def _sb_attn(z3, u_mat, t):
    b, tp, _ = z3.shape
    nq = tp // QB
    assert nq >= SB_FAR
    w = SB_HEADS * HEAD_DIM
    return pl.pallas_call(
        functools.partial(_sb_kernel, nq=nq, tail_q=_tail_rows(t, tp, 64)),
        out_shape=jax.ShapeDtypeStruct((b, tp, w), BF16),
        grid=(b, nq),
        in_specs=[pl.BlockSpec((1, QB, w), lambda bb, i: (bb, i, Q_SB // w)),
                  pl.BlockSpec((1, tp, w), lambda bb, i: (bb, 0, K_SB // w)),
                  pl.BlockSpec((1, tp, w), lambda bb, i: (bb, 0, V_SB // w)),
                  pl.BlockSpec((2 * BLK, 2 * BLK), lambda bb, i: (0, 0))],
        out_specs=pl.BlockSpec((1, QB, w), lambda bb, i: (bb, i, 0)),
        scratch_shapes=[pltpu.VMEM((SB_HEADS * QB, BLK), F32),
                        pltpu.VMEM((SB_HEADS * QB, w), F32)],
        compiler_params=_params("parallel", "arbitrary"),
        name="sb_attn",
    )(z3, z3, z3, u_mat)


def _dft_kernel(tab_ref, q_ref, k_ref, v_ref, qg_ref, kg_ref, bkt_ref, lamv_ref, sub_ref, o_ref,
                kn_ref, vt_ref, bias_ref, kmax_ref, m_ref, l_ref, lsum_ref, acc_ref,
                *, nq, tail_q, lam_init, head_off):
    g = pl.program_id(1)
    i = pl.program_id(2)
    lo = _lane_iota((1, BLK)) < HEAD_DIM
    heads = range(DF_GROUP)
    lanes = [slice(c * BLK, (c + 1) * BLK) for c in heads]

    @pl.when(i == 0)
    def _prep():
        kmax_ref[...] = jnp.zeros_like(kmax_ref)

        def kbody(j, carry):
            r = pl.multiple_of(j * QB, QB)
            for c in heads:
                kf = k_ref[0, pl.ds(r, QB), lanes[c]].astype(F32)
                kn = _head_rmsnorm128(kf, kg_ref[...]).astype(BF16)
                kn_ref[pl.ds(r, QB), lanes[c]] = kn
                ksq = kn.astype(F32) ** 2
                half_norms = jnp.maximum(jnp.sum(jnp.where(lo, ksq, 0.0), axis=-1, keepdims=True),
                                         jnp.sum(jnp.where(lo, 0.0, ksq), axis=-1, keepdims=True))
                kmax_ref[c] = jnp.maximum(kmax_ref[c], jnp.max(half_norms))
            vt_ref[j] = v_ref[0, pl.ds(r, QB), :].astype(F32).T.astype(BF16)
            return carry

        lax.fori_loop(0, nq, kbody, 0)
        for c in heads:
            head = head_off + g * DF_GROUP + c
            bias_ref[c] = _bias_tile(tab_ref, bkt_ref[...], head)
            bias_max = jnp.float32(0.0)
            for b in range(N_BUCKETS - 1):
                bias_max = jnp.maximum(bias_max, (tab_ref[b, head] - tab_ref[N_BUCKETS - 1, head]) * LOG2E)
            kmax_ref[c, 1:2] = jnp.full((1, BLK), bias_max, F32)

    def query_block(qt):
        m, l, lsum, acc = ([ref.at[c, :, :2 * qt] for c in heads] for ref in (m_ref, l_ref, lsum_ref, acc_ref))
        qs, bound = [], []
        for c in heads:
            qn = _head_rmsnorm128(q_ref[0, :qt, lanes[c]].astype(F32), qg_ref[...]) * QK_SCALE
            qs.append(jnp.concatenate([jnp.where(lo, qn, 0.0), jnp.where(lo, 0.0, qn)], axis=0).astype(BF16))
            q_sq = _nt_dot(jnp.ones((8, BLK), BF16), (qs[c].astype(F32) ** 2).astype(BF16))[0:1]
            bound.append(jnp.sqrt(q_sq * kmax_ref[c, 0:1, 0:1]) * BOUND_SLACK
                         + (kmax_ref[c, 1:2, 0:1] + BOUND_SLACK))

        def scores(c, first, n):
            rows = pl.ds(pl.multiple_of(first * QB, QB), n * QB)
            return _nt_dot(kn_ref[rows, lanes[c]], qs[c])

        def values(c, first, n):
            return jnp.concatenate([vt_ref[first + b, lanes[c], :] for b in range(n)], axis=1)

        def far_piece(start, hi):
            pieces = []
            for c in heads:
                st = scores(c, start, DF_FAR)
                new_keys = _row_iota(st.shape) < (hi - start) * QB
                pieces.append((jnp.where(new_keys, st, -jnp.inf), values(c, start, DF_FAR), c))
            return pieces

        def near_piece(first, n):
            pieces = []
            for c in heads:
                b = bias_ref[c, (2 - n) * QB:, :qt]
                st = scores(c, first, n) + jnp.concatenate([b, b], axis=1)
                causal = (_row_iota(st.shape) - (n - 1) * QB) <= (_lane_iota(st.shape) & (qt - 1))
                pieces.append((jnp.where(causal, st, -jnp.inf), values(c, first, n), c))
            return pieces

        def attend(step):
            _attend_windows(i, DF_FAR, DF_WINDOW_GROUP, far_piece, near_piece, step)

        for c in heads:
            l[c][...] = jnp.zeros_like(l[c])
            acc[c][...] = jnp.zeros_like(acc[c])
        attend(lambda pieces: _bounded_pieces_t(pieces, bound, l, acc))
        for c in heads:
            lsum[c][0:1] = jnp.sum(l[c][...], axis=0, keepdims=True)

        @pl.when(functools.reduce(jnp.minimum, [jnp.min(lsum[c][0:1]) for c in heads]) < UNDERFLOW_GUARD)
        def _():
            for c in heads:
                m[c][...] = jnp.full(m[c].shape, M_INIT, F32)
                l[c][...] = jnp.zeros_like(l[c])
                acc[c][...] = jnp.zeros_like(acc[c])
            attend(lambda pieces: _online_pieces_t(pieces, m, l, acc))
            for c in heads:
                lsum[c][0:1] = l[c][0:1]

        lv = lamv_ref[...]
        lam = (jnp.exp(jnp.sum(lv[0:1] * lv[1:2], axis=-1, keepdims=True))
               - jnp.exp(jnp.sum(lv[2:3] * lv[3:4], axis=-1, keepdims=True)) + lam_init)
        for c in heads:
            a = acc[c][...] / lsum[c][0:1]
            y = (a[:, :qt] - lam * a[:, qt:]).T
            y = y * lax.rsqrt(jnp.mean(y * y, axis=-1, keepdims=True) + EPS) * sub_ref[...]
            o_ref[0, :qt, lanes[c]] = (y * (1.0 - lam_init)).astype(o_ref.dtype)
        if qt < QB:
            o_ref[0, qt:] = jnp.zeros((QB - qt, DF_GROUP * BLK), o_ref.dtype)

    if tail_q == QB:
        query_block(QB)
    else:
        @pl.when(i < nq - 1)
        def _():
            query_block(QB)

        @pl.when(i == nq - 1)
        def _():
            query_block(tail_q)


def _dft_attn(z3, rel_bias, qg, kg, bkt_t, lamv, subln, lam_init, t):
    b, tp, _ = z3.shape
    nq = tp // QB
    assert nq >= DF_FAR
    kern = functools.partial(_dft_kernel, nq=nq, tail_q=_tail_rows(t, tp, 64), lam_init=lam_init,
                             head_off=SP_HEADS)
    vec = lambda bb, g, i: (0, 0)
    w = DF_GROUP * BLK
    per_head = lambda rows, cols: pltpu.VMEM((DF_GROUP, rows, cols), F32)
    return pl.pallas_call(
        kern,
        out_shape=jax.ShapeDtypeStruct((b, tp, DF_HEADS * BLK), BF16),
        grid=(b, DF_HEADS // DF_GROUP, nq),
        in_specs=[pl.BlockSpec(memory_space=pltpu.SMEM),
                  pl.BlockSpec((1, QB, w), lambda bb, g, i: (bb, i, Q_DF // w + g)),
                  pl.BlockSpec((1, tp, w), lambda bb, g, i: (bb, 0, K_DF // w + g)),
                  pl.BlockSpec((1, tp, w), lambda bb, g, i: (bb, 0, V_DF // w + g)),
                  pl.BlockSpec((1, BLK), vec),
                  pl.BlockSpec((1, BLK), vec),
                  pl.BlockSpec((2 * QB, QB), vec),
                  pl.BlockSpec((8, BLK), vec),
                  pl.BlockSpec((1, BLK), vec)],
        out_specs=pl.BlockSpec((1, QB, w), lambda bb, g, i: (bb, i, g)),
        scratch_shapes=[pltpu.VMEM((tp, w), BF16),
                        pltpu.VMEM((nq, w, QB), BF16),
                        per_head(2 * QB, QB),
                        per_head(8, BLK),
                        per_head(8, 2 * QB),
                        per_head(8, 2 * QB),
                        per_head(8, 2 * QB),
                        per_head(BLK, 2 * QB)],
        compiler_params=_params("parallel", "parallel", "arbitrary"),
        name="df_attn",
    )(rel_bias, z3, z3, z3, qg, kg, bkt_t, lamv, subln)


def _sp_kernel(tab_ref, q_ref, k_ref, v_ref, qix_ref, kwq_ref, kwk_ref, qg_ref, kg_ref, bkt_ref,
               rep_ref, bcast_ref, before_ref, o_ref,
               kn_ref, kx_ref, vt_ref, bias_ref, kmax_ref, keys_ref, dig_ref, thr_ref, wb_ref,
               m_ref, l_ref, lsum_ref, acc_ref, *, nq, top_k):
    i = pl.program_id(1)
    nh = SP_HEADS
    w = nh * HEAD_DIM
    lane = _lane_iota((QB, w))
    head_of_lane = lane // HEAD_DIM

    def norm256(xf, g):
        return jnp.concatenate([_head_rmsnorm128(xf[:, :BLK], g[:, :BLK]),
                                _head_rmsnorm128(xf[:, BLK:], g[:, BLK:])], axis=1)

    def dup(x):
        return jnp.concatenate([x, x], axis=1)

    @pl.when(i == 0)
    def _prep():
        kmax_ref[...] = jnp.zeros_like(kmax_ref)

        def kbody(c, carry):
            r = pl.multiple_of(c * QB, QB)
            kn = norm256(k_ref[0, pl.ds(r, QB), :].astype(F32), kg_ref[...]).astype(BF16)
            kn_ref[pl.ds(r, QB), :] = kn
            ksq = kn.astype(F32) ** 2
            head_norms = functools.reduce(jnp.maximum, [
                jnp.sum(jnp.where(head_of_lane == h, ksq, 0.0), axis=-1, keepdims=True) for h in range(nh)])
            kmax_ref[0:1] = jnp.maximum(kmax_ref[0:1], jnp.max(head_norms))
            kx_ref[pl.ds(r, QB), :] = _dot(kwk_ref[0, pl.ds(r, QB), :], rep_ref[...]).astype(BF16)
            vt_ref[c] = v_ref[0, pl.ds(r, QB), :].astype(F32).T.astype(BF16)
            return carry

        lax.fori_loop(0, nq, kbody, 0)
        bias_max = jnp.float32(0.0)
        for h in range(nh):
            bias_ref[:, h * QB:(h + 1) * QB] = _bias_tile(tab_ref, bkt_ref[...], h)
            for b in range(N_BUCKETS - 1):
                bias_max = jnp.maximum(bias_max, (tab_ref[b, h] - tab_ref[N_BUCKETS - 1, h]) * LOG2E)
        kmax_ref[1:2] = jnp.full((1, BLK), bias_max, F32)

    key_causal = _row_iota((QB, QB)) <= _lane_iota((QB, QB))

    wb_ref[...] = _dot(kwq_ref[0], bcast_ref[...])
    qix = qix_ref[0].astype(F32)
    ix_head = lane // IDX_DIM
    qx = jnp.concatenate([jnp.where(ix_head == h, qix, 0.0) for h in range(IDX_HEADS)], axis=0).astype(BF16)

    def score_tiles(blocks, last_is_diagonal):
        dots = [_nt_dot(qx, kx_ref[pl.ds(pl.multiple_of(j * QB, QB), QB), :]) for j in blocks]
        for n, (j, d) in enumerate(zip(blocks, dots)):
            sc = jnp.zeros((QB, QB), F32)
            for h in range(IDX_HEADS):
                sc = sc + dup(wb_ref[:, h * BLK:(h + 1) * BLK]) * jnp.maximum(d[h * QB:(h + 1) * QB], 0.0)
            sc = jnp.where(sc == 0.0, 0.0, sc).T
            bits = lax.bitcast_convert_type(sc, jnp.int32)
            key = jnp.where(bits < 0, bits ^ jnp.int32(0x7FFFFFFF), bits)
            if last_is_diagonal and n == len(blocks) - 1:
                key = jnp.where(key_causal, key, jnp.int32(INT_MIN))
            keys_ref[j] = key
            dig_ref[j] = jnp.right_shift(key, 16).astype(jnp.int16)

    def group_body(g, c):
        score_tiles([SCORE_GROUP * g + n for n in range(SCORE_GROUP)], False)
        return c

    lax.fori_loop(0, i // SCORE_GROUP, group_body, 0)
    first_left = SCORE_GROUP * (i // SCORE_GROUP)
    for n_left in range(1, SCORE_GROUP + 1):
        @pl.when(i - first_left == n_left - 1)
        def _(n_left=n_left):
            score_tiles([first_left + n for n in range(n_left)], True)

    nblk = i + 1

    def count(pred):
        def cbody(j, c):
            hit = jnp.where(pred(keys_ref[j]), 1.0, 0.0)
            return c + jnp.sum(hit.reshape(QB // COUNT_ROWS, COUNT_ROWS, QB), axis=0)

        c = lax.fori_loop(0, nblk, cbody, jnp.zeros((COUNT_ROWS, QB), F32))
        return jnp.sum(c, axis=0, keepdims=True)

    def count16(pred):
        def cbody(j, c):
            hit = jnp.where(pred(dig_ref[j]), jnp.int16(1), jnp.int16(0)).reshape(QB // DIGIT_ROWS, DIGIT_ROWS, QB)
            return c + functools.reduce(jnp.add, [hit[r] for r in range(QB // DIGIT_ROWS)])

        c = lax.fori_loop(0, nblk, cbody, jnp.zeros((DIGIT_ROWS, QB), jnp.int16))
        return jnp.sum(c.astype(F32), axis=0, keepdims=True)

    n_all = jnp.zeros((1, QB), F32) + (nblk * QB).astype(F32)

    def bisect16(need):
        def bit_body(b, carry):
            cur, cnt_cur = carry
            cand = cur + jnp.left_shift(jnp.int32(1), 15 - b)
            cand16 = cand.astype(jnp.int16)
            cnt = count16(lambda d: d >= cand16)
            ok = cnt >= need
            return jnp.where(ok, cand, cur), jnp.where(ok, cnt, cnt_cur)

        return lax.fori_loop(0, 16, bit_body, (jnp.full((1, QB), I16_MIN, jnp.int32), n_all))

    t_hi, c_ge_hi = bisect16(float(top_k))
    t_hi16 = t_hi.astype(jnp.int16)
    c_gt_hi = count16(lambda d: d > t_hi16)
    base = jnp.left_shift(t_hi, 16)

    def low_digits(j, c):
        y = keys_ref[j] - base
        dig_ref[j] = jnp.where(jnp.right_shift(y, 16) == 0, y + I16_MIN, I16_MIN).astype(jnp.int16)
        return c

    lax.fori_loop(0, nblk, low_digits, 0)
    t_lo, c_ge_lo = bisect16(float(top_k) - c_gt_hi)
    thr = base + (t_lo - I16_MIN)
    cge = c_gt_hi + jnp.where(t_lo > I16_MIN, c_ge_lo, c_ge_hi - c_gt_hi)
    thr_ref[...] = jnp.broadcast_to(thr, thr_ref.shape)

    tie = jnp.where((cge > float(top_k)) & (thr > INT_MIN), 1, 0)

    @pl.when(jnp.max(tie) > 0)
    def _ties():
        need = float(top_k) - count(lambda k: k > thr)

        def tbody(j, run):
            kj = keys_ref[j]
            eq = kj == thr
            eqf = jnp.where(eq, 1.0, 0.0)
            rank = run + _dot(before_ref[...], eqf.astype(BF16))
            keep = jnp.where(kj > thr, 1, jnp.where(eq & (rank < need), 1, -1))
            keys_ref[j] = keep.astype(jnp.int32)
            return run + jnp.sum(eqf, axis=0, keepdims=True)

        lax.fori_loop(0, nblk, tbody, jnp.zeros((1, QB), F32))
        thr_ref[...] = jnp.zeros_like(thr_ref)

    qn = norm256(q_ref[0].astype(F32), qg_ref[...]) * QK_SCALE
    qs = jnp.concatenate([jnp.where(head_of_lane == h, qn, 0.0) for h in range(nh)], axis=0).astype(BF16)
    q_sq = _nt_dot(jnp.ones((8, w), BF16), (qs.astype(F32) ** 2).astype(BF16))[0:1]
    bound = jnp.sqrt(q_sq * kmax_ref[0:1, 0:1]) * BOUND_SLACK + (kmax_ref[1:2, 0:1] + BOUND_SLACK)

    thr_sel = thr_ref[0:1, :]

    def as_mask(sel):
        return lax.bitcast_convert_type(jnp.where(sel, 0.0, -jnp.inf), jnp.int32)

    def mask_body(j, c):
        keys_ref[j] = as_mask(keys_ref[j] >= thr_sel)
        return c

    lax.fori_loop(0, i, mask_body, 0)
    keys_ref[i] = as_mask((keys_ref[i] >= thr_sel) & key_causal)

    def scores(first, n):
        return _nt_dot(kn_ref[pl.ds(pl.multiple_of(first * QB, QB), n * QB), :], qs)

    def values(first, n):
        return jnp.concatenate([vt_ref[first + c] for c in range(n)], axis=1)

    def selection(first, n):
        return jnp.concatenate([lax.bitcast_convert_type(keys_ref[first + c], F32) for c in range(n)],
                               axis=0)

    def far_piece(start, hi):
        m = selection(start, SP_FAR)
        m = jnp.where(_row_iota(m.shape) < (hi - start) * QB, m, -jnp.inf)
        return [(scores(start, SP_FAR) + jnp.concatenate([m] * nh, axis=1), values(start, SP_FAR), 0)]

    def near_piece(first, n):
        m = selection(first, n)
        return [(scores(first, n) + (jnp.concatenate([m] * nh, axis=1) + bias_ref[(2 - n) * QB:, :]),
                 values(first, n), 0)]

    def attend(step):
        _attend_windows(i, SP_FAR, SP_WINDOW_GROUP, far_piece, near_piece, step)

    l_ref[...] = jnp.zeros_like(l_ref)
    acc_ref[...] = jnp.zeros_like(acc_ref)
    attend(lambda pieces: _bounded_pieces_t(pieces, [bound], [l_ref], [acc_ref]))
    lsum_ref[0:1] = jnp.sum(l_ref[...], axis=0, keepdims=True)

    @pl.when(jnp.min(lsum_ref[0:1]) < UNDERFLOW_GUARD)
    def _():
        m_ref[...] = jnp.full(m_ref.shape, M_INIT, F32)
        l_ref[...] = jnp.zeros_like(l_ref)
        acc_ref[...] = jnp.zeros_like(acc_ref)
        attend(lambda pieces: _online_pieces_t(pieces, [m_ref], [l_ref], [acc_ref]))
        lsum_ref[0:1] = l_ref[0:1]

    a = acc_ref[...] / lsum_ref[0:1]
    out_t = jnp.concatenate([a[h * HEAD_DIM:(h + 1) * HEAD_DIM, h * QB:(h + 1) * QB] for h in range(nh)], axis=0)
    o_ref[0] = out_t.T.astype(o_ref.dtype)


def _sp_attn(z3, rel_bias, qg, kg, bkt_t, rep, bcast, before, top_k):
    b, tp, _ = z3.shape
    nq = tp // QB
    assert nq >= SP_FAR
    w = SP_HEADS * HEAD_DIM
    kern = functools.partial(_sp_kernel, nq=nq, top_k=top_k)
    c2 = lambda bb, i: (0, 0)
    return pl.pallas_call(
        kern,
        out_shape=jax.ShapeDtypeStruct((b, tp, w), BF16),
        grid=(b, nq),
        in_specs=[pl.BlockSpec(memory_space=pltpu.SMEM),
                  pl.BlockSpec((1, QB, w), lambda bb, i: (bb, i, Q_SP // w)),
                  pl.BlockSpec((1, tp, w), lambda bb, i: (bb, 0, K_SP // w)),
                  pl.BlockSpec((1, tp, w), lambda bb, i: (bb, 0, V_SP // w)),
                  pl.BlockSpec((1, QB, w), lambda bb, i: (bb, i, Q_IX // w)),
                  pl.BlockSpec((1, QB, w), lambda bb, i: (bb, i, KW_IX // w)),
                  pl.BlockSpec((1, tp, w), lambda bb, i: (bb, 0, KW_IX // w)),
                  pl.BlockSpec((1, w), c2),
                  pl.BlockSpec((1, w), c2),
                  pl.BlockSpec((2 * QB, QB), c2),
                  pl.BlockSpec((w, w), c2),
                  pl.BlockSpec((w, IDX_HEADS * BLK), c2),
                  pl.BlockSpec((QB, QB), c2)],
        out_specs=pl.BlockSpec((1, QB, w), lambda bb, i: (bb, i, 0)),
        scratch_shapes=[pltpu.VMEM((tp, w), BF16),
                        pltpu.VMEM((tp, w), BF16),
                        pltpu.VMEM((nq, w, QB), BF16),
                        pltpu.VMEM((2 * QB, SP_HEADS * QB), F32),
                        pltpu.VMEM((8, BLK), F32),
                        pltpu.VMEM((nq, QB, QB), jnp.int32),
                        pltpu.VMEM((nq, QB, QB), jnp.int16),
                        pltpu.VMEM((8, QB), jnp.int32),
                        pltpu.VMEM((QB, IDX_HEADS * BLK), F32),
                        pltpu.VMEM((8, SP_HEADS * QB), F32),
                        pltpu.VMEM((8, SP_HEADS * QB), F32),
                        pltpu.VMEM((8, SP_HEADS * QB), F32),
                        pltpu.VMEM((w, SP_HEADS * QB), F32)],
        compiler_params=_params("parallel", "arbitrary"),
        name="sp_attn",
    )(rel_bias, z3, z3, z3, z3, z3, z3, qg, kg, bkt_t, rep, bcast, before)


def _mix_kernel(h_ref, gsb_ref, gsp_ref, gdf_ref, bg_ref, ysb_ref, ysp_ref, ydf_ref,
                wsb_ref, wsp_ref, wdf_ref, wo_ref, o_ref):
    def branch(g_ref, k, y_ref, w_ref):
        gate = jax.nn.sigmoid(g_ref[...].astype(F32) + bg_ref[:, k * D_MODEL:(k + 1) * D_MODEL])
        return gate * _dot(y_ref[...], w_ref[...])

    merged = (branch(gsb_ref, 0, ysb_ref, wsb_ref) + branch(gsp_ref, 1, ysp_ref, wsp_ref)
              + branch(gdf_ref, 2, ydf_ref, wdf_ref))
    o_ref[...] = h_ref[...] + _dot(merged.astype(BF16), wo_ref[...])


def _mix_out(h, z, b_gate, y_sb, y_sp, y_df, w_sb, w_sp, w_df, w_o):
    m, d = h.shape
    tm = _pick_rows(m, 512)
    row = lambda i: (i, 0)
    fixed = lambda i: (0, 0)
    return pl.pallas_call(
        _mix_kernel,
        out_shape=jax.ShapeDtypeStruct((m, d), F32),
        grid=(m // tm,),
        in_specs=[pl.BlockSpec((tm, d), row),
                  pl.BlockSpec((tm, d), lambda i: (i, G_SB // D_MODEL)),
                  pl.BlockSpec((tm, d), lambda i: (i, G_SP // D_MODEL)),
                  pl.BlockSpec((tm, d), lambda i: (i, G_DF // D_MODEL)),
                  pl.BlockSpec((1, 3 * d), fixed),
                  pl.BlockSpec((tm, y_sb.shape[1]), row),
                  pl.BlockSpec((tm, y_sp.shape[1]), row),
                  pl.BlockSpec((tm, y_df.shape[1]), row),
                  pl.BlockSpec(w_sb.shape, fixed),
                  pl.BlockSpec(w_sp.shape, fixed),
                  pl.BlockSpec(w_df.shape, fixed),
                  pl.BlockSpec(w_o.shape, fixed)],
        out_specs=pl.BlockSpec((tm, d), row),
        compiler_params=_params("parallel"),
        name="mix_out",
    )(h, z, z, z, b_gate.reshape(1, 3 * d), y_sb, y_sp, y_df, w_sb, w_sp, w_df, w_o)


def _ffn_kernel(h_ref, g_ref, wu_ref, cw_ref, cb_ref, wd_ref, o_ref, gbuf_ref, carry_ref, *, tm, tp, tf):
    r = pl.program_id(0)
    x = h_ref[...]
    ms = jnp.mean(x * x, axis=-1, keepdims=True)
    u = (x * lax.rsqrt(ms + EPS) * g_ref[...]).astype(BF16)

    @pl.when(r == 0)
    def _():
        carry_ref[...] = jnp.zeros_like(carry_ref)

    seq_start = lax.rem(tp - lax.rem(r * tm, tp), tp)
    local = lax.broadcasted_iota(jnp.int32, (tm, 1), 0)
    tap1 = local != seq_start
    tap2 = tap1 & (local != seq_start + 1)
    out = x
    for f in range(D_FF // tf):
        cols = slice(f * tf, (f + 1) * tf)
        gate = _dot(u, wu_ref[:, cols])
        val = _dot(u, wu_ref[:, D_FF + f * tf:D_FF + (f + 1) * tf])
        gbuf_ref[0:8] = carry_ref[f]
        gbuf_ref[8:8 + tm] = gate
        carry_ref[f] = gate[tm - 8:tm]
        g1 = jnp.where(tap1, gbuf_ref[7:7 + tm], 0.0)
        g2 = jnp.where(tap2, gbuf_ref[6:6 + tm], 0.0)
        conv = cb_ref[:, cols] + cw_ref[0:1, cols] * g2 + cw_ref[1:2, cols] * g1 + cw_ref[2:3, cols] * gate
        act = conv * jax.nn.sigmoid(conv) * val
        out = out + _dot(act.astype(BF16), wd_ref[cols, :])
    o_ref[...] = out


def _ffn(h, gain, w_up, conv_w, conv_b, w_down, tp):
    m, d = h.shape
    tm = _pick_rows(m, 512)
    assert tm <= tp
    tf = D_FF // 2
    nf = D_FF // tf
    kern = functools.partial(_ffn_kernel, tm=tm, tp=tp, tf=tf)
    fixed = lambda r: (0, 0)
    resident = pl.Buffered(1)
    return pl.pallas_call(
        kern,
        out_shape=jax.ShapeDtypeStruct((m, d), F32),
        grid=(m // tm,),
        in_specs=[pl.BlockSpec((tm, d), lambda r: (r, 0)),
                  pl.BlockSpec((1, d), fixed),
                  pl.BlockSpec((d, 2 * D_FF), fixed, pipeline_mode=resident),
                  pl.BlockSpec((8, D_FF), fixed),
                  pl.BlockSpec((1, D_FF), fixed),
                  pl.BlockSpec((D_FF, d), fixed, pipeline_mode=resident)],
        out_specs=pl.BlockSpec((tm, d), lambda r: (r, 0)),
        scratch_shapes=[pltpu.VMEM((tm + 8, tf), F32),
                        pltpu.VMEM((nf, 8, tf), F32)],
        compiler_params=_params("arbitrary"),
        name="conv_ffn",
    )(h, gain.reshape(1, d), w_up, conv_w, conv_b.reshape(1, D_FF), w_down)


def _permute_w_in(w):
    n_attn = KW_IX - Q_SB + IDX_DIM + IDX_HEADS
    n_gate = 3 * D_MODEL
    n_df = 3 * DF_HEADS * 2 * HEAD_DIM
    gates = w[:, n_attn + n_df:]
    attn = w[:, :n_attn]
    pad = jnp.zeros((w.shape[0], Q_DF - Q_SB - n_attn), w.dtype)
    df = w[:, n_attn:n_attn + n_df]
    out = jnp.concatenate([gates, attn, pad, df], axis=1)
    assert gates.shape[1] == n_gate and out.shape[1] == NZ
    return out


def kernel(x, meta_tokens, rel_bias, attn_norm, w_in, b_gate, q_norm_sp, k_norm_sp, q_norm_df, k_norm_df, lam_q1, lam_k1, lam_q2, lam_k2, subln_df, w_br_sb, w_br_sp, w_br_df, w_out, ffn_norm, w_up, conv_w, conv_b, w_down):
    b, s, d = x.shape
    depth = w_in.shape[0]
    t = N_META + s
    tp = -(-t // QB) * QB
    top_k = min(TOPK_MAX, t // 4)
    m = b * tp

    meta = jnp.broadcast_to(meta_tokens[None].astype(x.dtype), (b, N_META, d))
    h = jnp.concatenate([meta, x, jnp.zeros((b, tp - t, d), x.dtype)], axis=1).reshape(m, d)

    bkt_t = jnp.asarray(np.ascontiguousarray(_bucket_tile().T))
    u_mat = jnp.asarray(_sb_prefix_matrix(), BF16)
    rep, bcast, before = (jnp.asarray(a, BF16) for a in _ix_select_matrices())
    rel_bias = rel_bias.astype(F32)

    for l in range(depth):
        lam_init = 0.8 - 0.6 * math.exp(-0.3 * l)
        z = _in_proj(h, attn_norm[l], _permute_w_in(w_in[l]).astype(BF16))
        z3 = z.reshape(b, tp, NZ)
        y_sb = _sb_attn(z3, u_mat, t)
        y_sp = _sp_attn(z3, rel_bias,
                        jnp.tile(q_norm_sp[l].astype(F32), SP_HEADS).reshape(1, -1),
                        jnp.tile(k_norm_sp[l].astype(F32), SP_HEADS).reshape(1, -1),
                        bkt_t, rep, bcast, before, top_k)
        lamv = jnp.zeros((8, BLK), F32).at[:4, :HEAD_DIM].set(
            jnp.stack([lam_q1[l], lam_k1[l], lam_q2[l], lam_k2[l]]).astype(F32))
        y_df = _dft_attn(z3, rel_bias,
                         jnp.tile(q_norm_df[l].astype(F32), 2).reshape(1, -1),
                         jnp.tile(k_norm_df[l].astype(F32), 2).reshape(1, -1),
                         bkt_t, lamv, subln_df[l].astype(F32).reshape(1, -1), lam_init, t)
        h = _mix_out(h, z, b_gate[l], y_sb.reshape(m, -1), y_sp.reshape(m, -1), y_df.reshape(m, -1),
                     w_br_sb[l].astype(BF16), w_br_sp[l].astype(BF16), w_br_df[l].astype(BF16),
                     w_out[l].astype(BF16))
        cw = jnp.zeros((8, D_FF), F32).at[:conv_w.shape[1]].set(conv_w[l])
        h = _ffn(h, ffn_norm[l], w_up[l].astype(BF16), cw, conv_b[l], w_down[l].astype(BF16), tp)

    return h.reshape(b, tp, d)[:, N_META:t]
```

```python
import functools
import math

import numpy as np
import jax
import jax.numpy as jnp
from jax import lax
from jax.experimental import pallas as pl
from jax.experimental.pallas import tpu as pltpu

D_MODEL = 1024
HEAD_DIM = 64
N_META = 16
BLK = 128
QB = 256
SB_HEADS = 4
SP_HEADS = 4
IDX_HEADS = 8
IDX_DIM = 32
TOPK_MAX = 256
DF_HEADS = 4
N_BUCKETS = 32
MAX_DISTANCE = 128
D_FF = 2816
EPS = 1e-6
LOG2E = math.log2(math.e)
QK_SCALE = HEAD_DIM ** -0.5 * LOG2E
M_INIT = -1e30
BOUND_SLACK = 1.02
UNDERFLOW_GUARD = 2.0 ** -100
INT_MIN = -2 ** 31
I16_MIN = -2 ** 15
COUNT_ROWS = 32
DIGIT_ROWS = 64
SCORE_GROUP = 4
SP_WINDOW_GROUP = 3
DF_WINDOW_GROUP = 1
DF_GROUP = 4
SB_FAR = 2
SB_WINDOW_GROUP = 2
SP_FAR = 2
DF_FAR = 4

G_SB, G_SP, G_DF = 0, 1024, 2048
Q_SB, K_SB, V_SB = 3072, 3328, 3584
Q_SP, K_SP, V_SP = 3840, 4096, 4352
Q_IX, KW_IX = 4608, 4864
Q_DF, K_DF, V_DF = 5120, 5632, 6144
NZ = 6656
W_IX_LANE = IDX_DIM

VMEM_LIMIT = 56 * 1024 * 1024

F32 = jnp.float32
BF16 = jnp.bfloat16
NT_DIMS = (((1,), (1,)), ((), ()))


def _nt_dot(a, b):
    return lax.dot_general(a, b, NT_DIMS, preferred_element_type=F32)


def _dot(a, b):
    return jnp.dot(a, b, preferred_element_type=F32)


def _params(*sem):
    return pltpu.CompilerParams(dimension_semantics=sem, vmem_limit_bytes=VMEM_LIMIT)


def _pick_rows(m, cap):
    for c in (2048, 1024, 512, 256):
        if c <= cap and m % c == 0:
            return c
    raise ValueError(f"row count {m} is not a multiple of {QB}")


def _bucket_np(rel):
    n = np.maximum(rel, 0)
    max_exact = N_BUCKETS // 2
    nf = np.maximum(n, 1).astype(np.float32)
    large = max_exact + (np.log(nf / np.float32(max_exact)) / np.float32(math.log(MAX_DISTANCE / max_exact))
                         * np.float32(N_BUCKETS - max_exact)).astype(np.int32)
    return np.where(n < max_exact, n, np.minimum(large, N_BUCKETS - 1)).astype(np.int32)


def _bucket_tile():
    tq = np.arange(QB)[:, None]
    c = np.arange(2 * QB)[None, :]
    return _bucket_np(tq - c + QB)


def _sb_prefix_matrix():
    sp = np.arange(2 * BLK)[:, None] % BLK
    c = np.arange(2 * BLK)[None, :]
    return np.where(c < BLK, sp > c, True).astype(np.float32)


def _ix_select_matrices():
    c = np.arange(QB)[:, None]
    col = np.arange(QB)[None, :]
    rep = ((c < IDX_DIM) & (c == col % IDX_DIM)).astype(np.float32)
    col8 = np.arange(IDX_HEADS * BLK)[None, :]
    bcast = (c == W_IX_LANE + col8 // BLK).astype(np.float32)
    before = (col < c).astype(np.float32)
    return rep, bcast, before


def _lane_iota(shape):
    return lax.broadcasted_iota(jnp.int32, shape, len(shape) - 1)


def _row_iota(shape):
    return lax.broadcasted_iota(jnp.int32, shape, 0)


def _head_rmsnorm128(xf, gain):
    lo = _lane_iota((1, BLK)) < HEAD_DIM
    ss = xf * xf
    s_lo = jnp.sum(jnp.where(lo, ss, 0.0), axis=-1, keepdims=True)
    s_hi = jnp.sum(jnp.where(lo, 0.0, ss), axis=-1, keepdims=True)
    ms = jnp.where(lo, s_lo, s_hi) * (1.0 / HEAD_DIM)
    return xf * lax.rsqrt(ms + EPS) * gain


def _bias_tile(tab_ref, bk, head):
    far = tab_ref[N_BUCKETS - 1, head]
    acc = jnp.zeros(bk.shape, F32)
    for b in range(N_BUCKETS - 1):
        acc = jnp.where(bk == b, (tab_ref[b, head] - far) * LOG2E, acc)
    return acc


def _softmax_piece_t(st, vt, m_ref, l_ref, acc_ref):
    m_old = m_ref[0:1]
    m_new = jnp.maximum(m_old, jnp.max(st, axis=0, keepdims=True))
    alpha = jnp.exp2(m_old - m_new)
    p = jnp.exp2(st - m_new)
    l_ref[0:1] = alpha * l_ref[0:1] + jnp.sum(p, axis=0, keepdims=True)
    m_ref[0:1] = m_new
    acc_ref[...] = alpha * acc_ref[...] + _dot(vt, p.astype(BF16))


def _online_pieces_t(pieces, m_refs, l_refs, acc_refs):
    for st, vt, c in pieces:
        _softmax_piece_t(st, vt, m_refs[c], l_refs[c], acc_refs[c])


def _bounded_pieces_t(pieces, bounds, l_refs, acc_refs):
    ps = [jnp.exp2(st - bounds[c]) for st, _, c in pieces]
    for c in sorted({c for _, _, c in pieces}):
        mine = [(vt, p) for (_, vt, cc), p in zip(pieces, ps) if cc == c]
        l_refs[c][...] += functools.reduce(
            jnp.add, [jnp.sum(p.reshape(p.shape[0] // 8, 8, p.shape[1]), axis=0) for _, p in mine])
        acc_refs[c][...] += functools.reduce(jnp.add, [_dot(vt, p.astype(BF16)) for vt, p in mine])


def _attend_windows(i, width, group, far_fn, near_fn, step):
    n_far = jnp.maximum(i - 1, 0)
    n_win = (n_far + width - 1) // width

    def far(p):
        hi = n_far - p * width
        return far_fn(jnp.maximum(hi - width, 0), hi)

    def body(g, carry):
        step(sum([far(group * g + n) for n in range(group)], []))
        return carry

    lax.fori_loop(0, n_win // group, body, 0)
    first_left = group * (n_win // group)
    for n_left in range(group):
        @pl.when((n_win - first_left == n_left) & (i >= 1))
        def _(n_left=n_left):
            step(sum([far(first_left + n) for n in range(n_left)], []) + near_fn(i - 1, 2))

    @pl.when(i == 0)
    def _():
        step(near_fn(0, 1))


def _for_far_pieces(n_far, width, group, piece_fn):
    n_win = (n_far + width - 1) // width

    def window(p):
        hi = n_far - p * width
        piece_fn(jnp.maximum(hi - width, 0), hi)

    def body(g, carry):
        for n in range(group):
            window(group * g + n)
        return carry

    lax.fori_loop(0, n_win // group, body, 0)
    first_left = group * (n_win // group)
    for n_left in range(1, group):
        @pl.when(n_win - first_left == n_left)
        def _(n_left=n_left):
            for n in range(n_left):
                window(first_left + n)


def _in_proj_kernel(h_ref, g_ref, w_ref, o_ref, *, tn):
    x = h_ref[...]
    ms = jnp.mean(x * x, axis=-1, keepdims=True)
    u = (x * lax.rsqrt(ms + EPS) * g_ref[...]).astype(BF16)
    for c in range(o_ref.shape[1] // tn):
        cols = slice(c * tn, (c + 1) * tn)
        o_ref[:, cols] = _dot(u, w_ref[:, cols]).astype(o_ref.dtype)


def _in_proj(h, gain, w):
    m, d = h.shape
    n = w.shape[1]
    tm = _pick_rows(m, 512)
    return pl.pallas_call(
        functools.partial(_in_proj_kernel, tn=512),
        out_shape=jax.ShapeDtypeStruct((m, n), BF16),
        grid=(m // tm,),
        in_specs=[pl.BlockSpec((tm, d), lambda i: (i, 0)),
                  pl.BlockSpec((1, d), lambda i: (0, 0)),
                  pl.BlockSpec((d, n), lambda i: (0, 0), pipeline_mode=pl.Buffered(1))],
        out_specs=pl.BlockSpec((tm, n), lambda i: (i, 0)),
        compiler_params=_params("parallel"),
        name="in_proj",
    )(h, gain.reshape(1, d), w)


def _sb_kernel(q_ref, k_ref, v_ref, u_ref, o_ref, tot_ref, acc_ref, *, nq, tail_q):
    i = pl.program_id(1)
    nh = SB_HEADS

    def query_block(qt):
        head_of_lane = _lane_iota((qt, nh * HEAD_DIM)) // HEAD_DIM
        q = q_ref[0, :qt].astype(F32) * QK_SCALE
        qs = jnp.concatenate([jnp.where(head_of_lane == h, q, 0.0) for h in range(nh)], axis=0).astype(BF16)
        tot = tot_ref.at[:nh * qt]
        acc = acc_ref.at[:nh * qt]
        tot[...] = jnp.zeros_like(tot)
        acc[...] = jnp.zeros_like(acc)

        def piece(start, n_blocks, hi=None):
            r = pl.multiple_of(start * QB, QB)
            z_all = _nt_dot(qs, k_ref[0, pl.ds(r, n_blocks * QB), :])
            if hi is not None:
                col = _lane_iota((1, n_blocks * QB))
                z_all = z_all + jnp.where(col < (hi - start) * QB, 0.0, -jnp.inf)
            run = tot[...]
            n_sub = n_blocks * QB // BLK
            ws = [None] * n_sub
            for c in reversed(range(n_sub)):
                z = z_all[:, c * BLK:(c + 1) * BLK]
                sp = jnp.maximum(z, 0.0) + jnp.log2(1.0 + jnp.exp2(-jnp.abs(z)))
                l1m = -sp
                if hi is None:
                    mask = (_lane_iota(z.shape) + c * BLK) < (_row_iota(z.shape) & (qt - 1))
                    l1m = jnp.where(mask, l1m, 0.0)
                l1m_hi = l1m.astype(BF16)
                l1m_lo = (l1m - l1m_hi.astype(F32)).astype(BF16)
                rs = _dot(jnp.concatenate([l1m_hi, l1m_lo], axis=1), u_ref[...])
                w = jnp.exp2((z - sp) + rs[:, :BLK] + run)
                if hi is None:
                    w = jnp.where(mask, w, 0.0)
                ws[c] = w.astype(BF16)
                run = run + rs[:, BLK:]
            tot[...] = run
            acc[...] += _dot(jnp.concatenate(ws, axis=1), v_ref[0, pl.ds(r, n_blocks * QB), :])

        piece(i, 1)
        _for_far_pieces(i, SB_FAR, SB_WINDOW_GROUP, lambda start, hi: piece(start, SB_FAR, hi))

        a = acc[...]
        out = a[:qt]
        for h in range(1, nh):
            out = jnp.where(head_of_lane == h, a[h * qt:(h + 1) * qt], out)
        o_ref[0, :qt] = out.astype(o_ref.dtype)
        if qt < QB:
            o_ref[0, qt:] = jnp.zeros((QB - qt, nh * HEAD_DIM), o_ref.dtype)

    if tail_q == QB:
        query_block(QB)
    else:
        @pl.when(i < nq - 1)
        def _():
            query_block(QB)

        @pl.when(i == nq - 1)
        def _():
            query_block(tail_q)


def _tail_rows(t, tp, granule):
    real = t - (tp - QB)
    return min(QB, -(-real // granule) * granule)


def _sb_attn(z3, u_mat, t):
    b, tp, _ = z3.shape
    nq = tp // QB
    assert nq >= SB_FAR
    w = SB_HEADS * HEAD_DIM
    return pl.pallas_call(
        functools.partial(_sb_kernel, nq=nq, tail_q=_tail_rows(t, tp, 64)),
        out_shape=jax.ShapeDtypeStruct((b, tp, w), BF16),
        grid=(b, nq),
        in_specs=[pl.BlockSpec((1, QB, w), lambda bb, i: (bb, i, Q_SB // w)),
                  pl.BlockSpec((1, tp, w), lambda bb, i: (bb, 0, K_SB // w)),
                  pl.BlockSpec((1, tp, w), lambda bb, i: (bb, 0, V_SB // w)),
                  pl.BlockSpec((2 * BLK, 2 * BLK), lambda bb, i: (0, 0))],
        out_specs=pl.BlockSpec((1, QB, w), lambda bb, i: (bb, i, 0)),
        scratch_shapes=[pltpu.VMEM((SB_HEADS * QB, BLK), F32),
                        pltpu.VMEM((SB_HEADS * QB, w), F32)],
        compiler_params=_params("parallel", "arbitrary"),
        name="sb_attn",
    )(z3, z3, z3, u_mat)


def _dft_kernel(tab_ref, q_ref, k_ref, v_ref, qg_ref, kg_ref, bkt_ref, lamv_ref, sub_ref, o_ref,
                kn_ref, vt_ref, bias_ref, kmax_ref, m_ref, l_ref, lsum_ref, acc_ref,
                *, nq, tail_q, lam_init, head_off):
    g = pl.program_id(1)
    i = pl.program_id(2)
    lo = _lane_iota((1, BLK)) < HEAD_DIM
    heads = range(DF_GROUP)
    lanes = [slice(c * BLK, (c + 1) * BLK) for c in heads]

    @pl.when(i == 0)
    def _prep():
        kmax_ref[...] = jnp.zeros_like(kmax_ref)

        def kbody(j, carry):
            r = pl.multiple_of(j * QB, QB)
            for c in heads:
                kf = k_ref[0, pl.ds(r, QB), lanes[c]].astype(F32)
                kn = _head_rmsnorm128(kf, kg_ref[...]).astype(BF16)
                kn_ref[pl.ds(r, QB), lanes[c]] = kn
                ksq = kn.astype(F32) ** 2
                half_norms = jnp.maximum(jnp.sum(jnp.where(lo, ksq, 0.0), axis=-1, keepdims=True),
                                         jnp.sum(jnp.where(lo, 0.0, ksq), axis=-1, keepdims=True))
                kmax_ref[c] = jnp.maximum(kmax_ref[c], jnp.max(half_norms))
            vt_ref[j] = v_ref[0, pl.ds(r, QB), :].astype(F32).T.astype(BF16)
            return carry

        lax.fori_loop(0, nq, kbody, 0)
        for c in heads:
            head = head_off + g * DF_GROUP + c
            bias_ref[c] = _bias_tile(tab_ref, bkt_ref[...], head)
            bias_max = jnp.float32(0.0)
            for b in range(N_BUCKETS - 1):
                bias_max = jnp.maximum(bias_max, (tab_ref[b, head] - tab_ref[N_BUCKETS - 1, head]) * LOG2E)
            kmax_ref[c, 1:2] = jnp.full((1, BLK), bias_max, F32)

    def query_block(qt):
        m, l, lsum, acc = ([ref.at[c, :, :2 * qt] for c in heads] for ref in (m_ref, l_ref, lsum_ref, acc_ref))
        qs, bound = [], []
        for c in heads:
            qn = _head_rmsnorm128(q_ref[0, :qt, lanes[c]].astype(F32), qg_ref[...]) * QK_SCALE
            qs.append(jnp.concatenate([jnp.where(lo, qn, 0.0), jnp.where(lo, 0.0, qn)], axis=0).astype(BF16))
            q_sq = _nt_dot(jnp.ones((8, BLK), BF16), (qs[c].astype(F32) ** 2).astype(BF16))[0:1]
            bound.append(jnp.sqrt(q_sq * kmax_ref[c, 0:1, 0:1]) * BOUND_SLACK
                         + (kmax_ref[c, 1:2, 0:1] + BOUND_SLACK))

        def scores(c, first, n):
            rows = pl.ds(pl.multiple_of(first * QB, QB), n * QB)
            return _nt_dot(kn_ref[rows, lanes[c]], qs[c])

        def values(c, first, n):
            return jnp.concatenate([vt_ref[first + b, lanes[c], :] for b in range(n)], axis=1)

        def far_piece(start, hi):
            pieces = []
            for c in heads:
                st = scores(c, start, DF_FAR)
                new_keys = _row_iota(st.shape) < (hi - start) * QB
                pieces.append((jnp.where(new_keys, st, -jnp.inf), values(c, start, DF_FAR), c))
            return pieces

        def near_piece(first, n):
            pieces = []
            for c in heads:
                b = bias_ref[c, (2 - n) * QB:, :qt]
                st = scores(c, first, n) + jnp.concatenate([b, b], axis=1)
                causal = (_row_iota(st.shape) - (n - 1) * QB) <= (_lane_iota(st.shape) & (qt - 1))
                pieces.append((jnp.where(causal, st, -jnp.inf), values(c, first, n), c))
            return pieces

        def attend(step):
            _attend_windows(i, DF_FAR, DF_WINDOW_GROUP, far_piece, near_piece, step)

        for c in heads:
            l[c][...] = jnp.zeros_like(l[c])
            acc[c][...] = jnp.zeros_like(acc[c])
        attend(lambda pieces: _bounded_pieces_t(pieces, bound, l, acc))
        for c in heads:
            lsum[c][0:1] = jnp.sum(l[c][...], axis=0, keepdims=True)

        @pl.when(functools.reduce(jnp.minimum, [jnp.min(lsum[c][0:1]) for c in heads]) < UNDERFLOW_GUARD)
        def _():
            for c in heads:
                m[c][...] = jnp.full(m[c].shape, M_INIT, F32)
                l[c][...] = jnp.zeros_like(l[c])
                acc[c][...] = jnp.zeros_like(acc[c])
            attend(lambda pieces: _online_pieces_t(pieces, m, l, acc))
            for c in heads:
                lsum[c][0:1] = l[c][0:1]

        lv = lamv_ref[...]
        lam = (jnp.exp(jnp.sum(lv[0:1] * lv[1:2], axis=-1, keepdims=True))
               - jnp.exp(jnp.sum(lv[2:3] * lv[3:4], axis=-1, keepdims=True)) + lam_init)
        for c in heads:
            a = acc[c][...] / lsum[c][0:1]
            y = (a[:, :qt] - lam * a[:, qt:]).T
            y = y * lax.rsqrt(jnp.mean(y * y, axis=-1, keepdims=True) + EPS) * sub_ref[...]
            o_ref[0, :qt, lanes[c]] = (y * (1.0 - lam_init)).astype(o_ref.dtype)
        if qt < QB:
            o_ref[0, qt:] = jnp.zeros((QB - qt, DF_GROUP * BLK), o_ref.dtype)

    if tail_q == QB:
        query_block(QB)
    else:
        @pl.when(i < nq - 1)
        def _():
            query_block(QB)

        @pl.when(i == nq - 1)
        def _():
            query_block(tail_q)


def _dft_attn(z3, rel_bias, qg, kg, bkt_t, lamv, subln, lam_init, t):
    b, tp, _ = z3.shape
    nq = tp // QB
    assert nq >= DF_FAR
    kern = functools.partial(_dft_kernel, nq=nq, tail_q=_tail_rows(t, tp, 64), lam_init=lam_init,
                             head_off=SP_HEADS)
    vec = lambda bb, g, i: (0, 0)
    w = DF_GROUP * BLK
    per_head = lambda rows, cols: pltpu.VMEM((DF_GROUP, rows, cols), F32)
    return pl.pallas_call(
        kern,
        out_shape=jax.ShapeDtypeStruct((b, tp, DF_HEADS * BLK), BF16),
        grid=(b, DF_HEADS // DF_GROUP, nq),
        in_specs=[pl.BlockSpec(memory_space=pltpu.SMEM),
                  pl.BlockSpec((1, QB, w), lambda bb, g, i: (bb, i, Q_DF // w + g)),
                  pl.BlockSpec((1, tp, w), lambda bb, g, i: (bb, 0, K_DF // w + g)),
                  pl.BlockSpec((1, tp, w), lambda bb, g, i: (bb, 0, V_DF // w + g)),
                  pl.BlockSpec((1, BLK), vec),
                  pl.BlockSpec((1, BLK), vec),
                  pl.BlockSpec((2 * QB, QB), vec),
                  pl.BlockSpec((8, BLK), vec),
                  pl.BlockSpec((1, BLK), vec)],
        out_specs=pl.BlockSpec((1, QB, w), lambda bb, g, i: (bb, i, g)),
        scratch_shapes=[pltpu.VMEM((tp, w), BF16),
                        pltpu.VMEM((nq, w, QB), BF16),
                        per_head(2 * QB, QB),
                        per_head(8, BLK),
                        per_head(8, 2 * QB),
                        per_head(8, 2 * QB),
                        per_head(8, 2 * QB),
                        per_head(BLK, 2 * QB)],
        compiler_params=_params("parallel", "parallel", "arbitrary"),
        name="df_attn",
    )(rel_bias, z3, z3, z3, qg, kg, bkt_t, lamv, subln)


def _sp_kernel(tab_ref, q_ref, k_ref, v_ref, qix_ref, kwq_ref, kwk_ref, qg_ref, kg_ref, bkt_ref,
               rep_ref, bcast_ref, before_ref, o_ref,
               kn_ref, kx_ref, vt_ref, bias_ref, kmax_ref, keys_ref, dig_ref, thr_ref, wb_ref,
               m_ref, l_ref, lsum_ref, acc_ref, *, nq, top_k):
    i = pl.program_id(1)
    nh = SP_HEADS
    w = nh * HEAD_DIM
    lane = _lane_iota((QB, w))
    head_of_lane = lane // HEAD_DIM

    def norm256(xf, g):
        return jnp.concatenate([_head_rmsnorm128(xf[:, :BLK], g[:, :BLK]),
                                _head_rmsnorm128(xf[:, BLK:], g[:, BLK:])], axis=1)

    def dup(x):
        return jnp.concatenate([x, x], axis=1)

    @pl.when(i == 0)
    def _prep():
        kmax_ref[...] = jnp.zeros_like(kmax_ref)

        def kbody(c, carry):
            r = pl.multiple_of(c * QB, QB)
            kn = norm256(k_ref[0, pl.ds(r, QB), :].astype(F32), kg_ref[...]).astype(BF16)
            kn_ref[pl.ds(r, QB), :] = kn
            ksq = kn.astype(F32) ** 2
            head_norms = functools.reduce(jnp.maximum, [
                jnp.sum(jnp.where(head_of_lane == h, ksq, 0.0), axis=-1, keepdims=True) for h in range(nh)])
            kmax_ref[0:1] = jnp.maximum(kmax_ref[0:1], jnp.max(head_norms))
            kx_ref[pl.ds(r, QB), :] = _dot(kwk_ref[0, pl.ds(r, QB), :], rep_ref[...]).astype(BF16)
            vt_ref[c] = v_ref[0, pl.ds(r, QB), :].astype(F32).T.astype(BF16)
            return carry

        lax.fori_loop(0, nq, kbody, 0)
        bias_max = jnp.float32(0.0)
        for h in range(nh):
            bias_ref[:, h * QB:(h + 1) * QB] = _bias_tile(tab_ref, bkt_ref[...], h)
            for b in range(N_BUCKETS - 1):
                bias_max = jnp.maximum(bias_max, (tab_ref[b, h] - tab_ref[N_BUCKETS - 1, h]) * LOG2E)
        kmax_ref[1:2] = jnp.full((1, BLK), bias_max, F32)

    key_causal = _row_iota((QB, QB)) <= _lane_iota((QB, QB))

    wb_ref[...] = _dot(kwq_ref[0], bcast_ref[...])
    qix = qix_ref[0].astype(F32)
    ix_head = lane // IDX_DIM
    qx = jnp.concatenate([jnp.where(ix_head == h, qix, 0.0) for h in range(IDX_HEADS)], axis=0).astype(BF16)

    def score_tiles(blocks, last_is_diagonal):
        dots = [_nt_dot(qx, kx_ref[pl.ds(pl.multiple_of(j * QB, QB), QB), :]) for j in blocks]
        for n, (j, d) in enumerate(zip(blocks, dots)):
            sc = jnp.zeros((QB, QB), F32)
            for h in range(IDX_HEADS):
                sc = sc + dup(wb_ref[:, h * BLK:(h + 1) * BLK]) * jnp.maximum(d[h * QB:(h + 1) * QB], 0.0)
            sc = jnp.where(sc == 0.0, 0.0, sc).T
            bits = lax.bitcast_convert_type(sc, jnp.int32)
            key = jnp.where(bits < 0, bits ^ jnp.int32(0x7FFFFFFF), bits)
            if last_is_diagonal and n == len(blocks) - 1:
                key = jnp.where(key_causal, key, jnp.int32(INT_MIN))
            keys_ref[j] = key
            dig_ref[j] = jnp.right_shift(key, 16).astype(jnp.int16)

    def group_body(g, c):
        score_tiles([SCORE_GROUP * g + n for n in range(SCORE_GROUP)], False)
        return c

    lax.fori_loop(0, i // SCORE_GROUP, group_body, 0)
    first_left = SCORE_GROUP * (i // SCORE_GROUP)
    for n_left in range(1, SCORE_GROUP + 1):
        @pl.when(i - first_left == n_left - 1)
        def _(n_left=n_left):
            score_tiles([first_left + n for n in range(n_left)], True)

    nblk = i + 1

    def count(pred):
        def cbody(j, c):
            hit = jnp.where(pred(keys_ref[j]), 1.0, 0.0)
            return c + jnp.sum(hit.reshape(QB // COUNT_ROWS, COUNT_ROWS, QB), axis=0)

        c = lax.fori_loop(0, nblk, cbody, jnp.zeros((COUNT_ROWS, QB), F32))
        return jnp.sum(c, axis=0, keepdims=True)

    def count16(pred):
        def cbody(j, c):
            hit = jnp.where(pred(dig_ref[j]), jnp.int16(1), jnp.int16(0)).reshape(QB // DIGIT_ROWS, DIGIT_ROWS, QB)
            return c + functools.reduce(jnp.add, [hit[r] for r in range(QB // DIGIT_ROWS)])

        c = lax.fori_loop(0, nblk, cbody, jnp.zeros((DIGIT_ROWS, QB), jnp.int16))
        return jnp.sum(c.astype(F32), axis=0, keepdims=True)

    n_all = jnp.zeros((1, QB), F32) + (nblk * QB).astype(F32)

    def bisect16(need):
        def bit_body(b, carry):
            cur, cnt_cur = carry
            cand = cur + jnp.left_shift(jnp.int32(1), 15 - b)
            cand16 = cand.astype(jnp.int16)
            cnt = count16(lambda d: d >= cand16)
            ok = cnt >= need
            return jnp.where(ok, cand, cur), jnp.where(ok, cnt, cnt_cur)

        return lax.fori_loop(0, 16, bit_body, (jnp.full((1, QB), I16_MIN, jnp.int32), n_all))

    t_hi, c_ge_hi = bisect16(float(top_k))
    t_hi16 = t_hi.astype(jnp.int16)
    c_gt_hi = count16(lambda d: d > t_hi16)
    base = jnp.left_shift(t_hi, 16)

    def low_digits(j, c):
        y = keys_ref[j] - base
        dig_ref[j] = jnp.where(jnp.right_shift(y, 16) == 0, y + I16_MIN, I16_MIN).astype(jnp.int16)
        return c

    lax.fori_loop(0, nblk, low_digits, 0)
    t_lo, c_ge_lo = bisect16(float(top_k) - c_gt_hi)
    thr = base + (t_lo - I16_MIN)
    cge = c_gt_hi + jnp.where(t_lo > I16_MIN, c_ge_lo, c_ge_hi - c_gt_hi)
    thr_ref[...] = jnp.broadcast_to(thr, thr_ref.shape)

    tie = jnp.where((cge > float(top_k)) & (thr > INT_MIN), 1, 0)

    @pl.when(jnp.max(tie) > 0)
    def _ties():
        need = float(top_k) - count(lambda k: k > thr)

        def tbody(j, run):
            kj = keys_ref[j]
            eq = kj == thr
            eqf = jnp.where(eq, 1.0, 0.0)
            rank = run + _dot(before_ref[...], eqf.astype(BF16))
            keep = jnp.where(kj > thr, 1, jnp.where(eq & (rank < need), 1, -1))
            keys_ref[j] = keep.astype(jnp.int32)
            return run + jnp.sum(eqf, axis=0, keepdims=True)

        lax.fori_loop(0, nblk, tbody, jnp.zeros((1, QB), F32))
        thr_ref[...] = jnp.zeros_like(thr_ref)

    qn = norm256(q_ref[0].astype(F32), qg_ref[...]) * QK_SCALE
    qs = jnp.concatenate([jnp.where(head_of_lane == h, qn, 0.0) for h in range(nh)], axis=0).astype(BF16)
    q_sq = _nt_dot(jnp.ones((8, w), BF16), (qs.astype(F32) ** 2).astype(BF16))[0:1]
    bound = jnp.sqrt(q_sq * kmax_ref[0:1, 0:1]) * BOUND_SLACK + (kmax_ref[1:2, 0:1] + BOUND_SLACK)

    thr_sel = thr_ref[0:1, :]

    def as_mask(sel):
        return lax.bitcast_convert_type(jnp.where(sel, 0.0, -jnp.inf), jnp.int32)

    def mask_body(j, c):
        keys_ref[j] = as_mask(keys_ref[j] >= thr_sel)
        return c

    lax.fori_loop(0, i, mask_body, 0)
    keys_ref[i] = as_mask((keys_ref[i] >= thr_sel) & key_causal)

    def scores(first, n):
        return _nt_dot(kn_ref[pl.ds(pl.multiple_of(first * QB, QB), n * QB), :], qs)

    def values(first, n):
        return jnp.concatenate([vt_ref[first + c] for c in range(n)], axis=1)

    def selection(first, n):
        return jnp.concatenate([lax.bitcast_convert_type(keys_ref[first + c], F32) for c in range(n)],
                               axis=0)

    def far_piece(start, hi):
        m = selection(start, SP_FAR)
        m = jnp.where(_row_iota(m.shape) < (hi - start) * QB, m, -jnp.inf)
        return [(scores(start, SP_FAR) + jnp.concatenate([m] * nh, axis=1), values(start, SP_FAR), 0)]

    def near_piece(first, n):
        m = selection(first, n)
        return [(scores(first, n) + (jnp.concatenate([m] * nh, axis=1) + bias_ref[(2 - n) * QB:, :]),
                 values(first, n), 0)]

    def attend(step):
        _attend_windows(i, SP_FAR, SP_WINDOW_GROUP, far_piece, near_piece, step)

    l_ref[...] = jnp.zeros_like(l_ref)
    acc_ref[...] = jnp.zeros_like(acc_ref)
    attend(lambda pieces: _bounded_pieces_t(pieces, [bound], [l_ref], [acc_ref]))
    lsum_ref[0:1] = jnp.sum(l_ref[...], axis=0, keepdims=True)

    @pl.when(jnp.min(lsum_ref[0:1]) < UNDERFLOW_GUARD)
    def _():
        m_ref[...] = jnp.full(m_ref.shape, M_INIT, F32)
        l_ref[...] = jnp.zeros_like(l_ref)
        acc_ref[...] = jnp.zeros_like(acc_ref)
        attend(lambda pieces: _online_pieces_t(pieces, [m_ref], [l_ref], [acc_ref]))
        lsum_ref[0:1] = l_ref[0:1]

    a = acc_ref[...] / lsum_ref[0:1]
    out_t = jnp.concatenate([a[h * HEAD_DIM:(h + 1) * HEAD_DIM, h * QB:(h + 1) * QB] for h in range(nh)], axis=0)
    o_ref[0] = out_t.T.astype(o_ref.dtype)


def _sp_attn(z3, rel_bias, qg, kg, bkt_t, rep, bcast, before, top_k):
    b, tp, _ = z3.shape
    nq = tp // QB
    assert nq >= SP_FAR
    w = SP_HEADS * HEAD_DIM
    kern = functools.partial(_sp_kernel, nq=nq, top_k=top_k)
    c2 = lambda bb, i: (0, 0)
    return pl.pallas_call(
        kern,
        out_shape=jax.ShapeDtypeStruct((b, tp, w), BF16),
        grid=(b, nq),
        in_specs=[pl.BlockSpec(memory_space=pltpu.SMEM),
                  pl.BlockSpec((1, QB, w), lambda bb, i: (bb, i, Q_SP // w)),
                  pl.BlockSpec((1, tp, w), lambda bb, i: (bb, 0, K_SP // w)),
                  pl.BlockSpec((1, tp, w), lambda bb, i: (bb, 0, V_SP // w)),
                  pl.BlockSpec((1, QB, w), lambda bb, i: (bb, i, Q_IX // w)),
                  pl.BlockSpec((1, QB, w), lambda bb, i: (bb, i, KW_IX // w)),
                  pl.BlockSpec((1, tp, w), lambda bb, i: (bb, 0, KW_IX // w)),
                  pl.BlockSpec((1, w), c2),
                  pl.BlockSpec((1, w), c2),
                  pl.BlockSpec((2 * QB, QB), c2),
                  pl.BlockSpec((w, w), c2),
                  pl.BlockSpec((w, IDX_HEADS * BLK), c2),
                  pl.BlockSpec((QB, QB), c2)],
        out_specs=pl.BlockSpec((1, QB, w), lambda bb, i: (bb, i, 0)),
        scratch_shapes=[pltpu.VMEM((tp, w), BF16),
                        pltpu.VMEM((tp, w), BF16),
                        pltpu.VMEM((nq, w, QB), BF16),
                        pltpu.VMEM((2 * QB, SP_HEADS * QB), F32),
                        pltpu.VMEM((8, BLK), F32),
                        pltpu.VMEM((nq, QB, QB), jnp.int32),
                        pltpu.VMEM((nq, QB, QB), jnp.int16),
                        pltpu.VMEM((8, QB), jnp.int32),
                        pltpu.VMEM((QB, IDX_HEADS * BLK), F32),
                        pltpu.VMEM((8, SP_HEADS * QB), F32),
                        pltpu.VMEM((8, SP_HEADS * QB), F32),
                        pltpu.VMEM((8, SP_HEADS * QB), F32),
                        pltpu.VMEM((w, SP_HEADS * QB), F32)],
        compiler_params=_params("parallel", "arbitrary"),
        name="sp_attn",
    )(rel_bias, z3, z3, z3, z3, z3, z3, qg, kg, bkt_t, rep, bcast, before)


def _mix_kernel(h_ref, gsb_ref, gsp_ref, gdf_ref, bg_ref, ysb_ref, ysp_ref, ydf_ref,
                wsb_ref, wsp_ref, wdf_ref, wo_ref, o_ref):
    def branch(g_ref, k, y_ref, w_ref):
        gate = jax.nn.sigmoid(g_ref[...].astype(F32) + bg_ref[:, k * D_MODEL:(k + 1) * D_MODEL])
        return gate * _dot(y_ref[...], w_ref[...])

    merged = (branch(gsb_ref, 0, ysb_ref, wsb_ref) + branch(gsp_ref, 1, ysp_ref, wsp_ref)
              + branch(gdf_ref, 2, ydf_ref, wdf_ref))
    o_ref[...] = h_ref[...] + _dot(merged.astype(BF16), wo_ref[...])


def _mix_out(h, z, b_gate, y_sb, y_sp, y_df, w_sb, w_sp, w_df, w_o):
    m, d = h.shape
    tm = _pick_rows(m, 512)
    row = lambda i: (i, 0)
    fixed = lambda i: (0, 0)
    return pl.pallas_call(
        _mix_kernel,
        out_shape=jax.ShapeDtypeStruct((m, d), F32),
        grid=(m // tm,),
        in_specs=[pl.BlockSpec((tm, d), row),
                  pl.BlockSpec((tm, d), lambda i: (i, G_SB // D_MODEL)),
                  pl.BlockSpec((tm, d), lambda i: (i, G_SP // D_MODEL)),
                  pl.BlockSpec((tm, d), lambda i: (i, G_DF // D_MODEL)),
                  pl.BlockSpec((1, 3 * d), fixed),
                  pl.BlockSpec((tm, y_sb.shape[1]), row),
                  pl.BlockSpec((tm, y_sp.shape[1]), row),
                  pl.BlockSpec((tm, y_df.shape[1]), row),
                  pl.BlockSpec(w_sb.shape, fixed),
                  pl.BlockSpec(w_sp.shape, fixed),
                  pl.BlockSpec(w_df.shape, fixed),
                  pl.BlockSpec(w_o.shape, fixed)],
        out_specs=pl.BlockSpec((tm, d), row),
        compiler_params=_params("parallel"),
        name="mix_out",
    )(h, z, z, z, b_gate.reshape(1, 3 * d), y_sb, y_sp, y_df, w_sb, w_sp, w_df, w_o)


def _ffn_kernel(h_ref, g_ref, wu_ref, cw_ref, cb_ref, wd_ref, o_ref, gbuf_ref, carry_ref, *, tm, tp, tf):
    r = pl.program_id(0)
    x = h_ref[...]
    ms = jnp.mean(x * x, axis=-1, keepdims=True)
    u = (x * lax.rsqrt(ms + EPS) * g_ref[...]).astype(BF16)

    @pl.when(r == 0)
    def _():
        carry_ref[...] = jnp.zeros_like(carry_ref)

    seq_start = lax.rem(tp - lax.rem(r * tm, tp), tp)
    local = lax.broadcasted_iota(jnp.int32, (tm, 1), 0)
    tap1 = local != seq_start
    tap2 = tap1 & (local != seq_start + 1)
    out = x
    for f in range(D_FF // tf):
        cols = slice(f * tf, (f + 1) * tf)
        gate = _dot(u, wu_ref[:, cols])
        val = _dot(u, wu_ref[:, D_FF + f * tf:D_FF + (f + 1) * tf])
        gbuf_ref[0:8] = carry_ref[f]
        gbuf_ref[8:8 + tm] = gate
        carry_ref[f] = gate[tm - 8:tm]
        g1 = jnp.where(tap1, gbuf_ref[7:7 + tm], 0.0)
        g2 = jnp.where(tap2, gbuf_ref[6:6 + tm], 0.0)
        conv = cb_ref[:, cols] + cw_ref[0:1, cols] * g2 + cw_ref[1:2, cols] * g1 + cw_ref[2:3, cols] * gate
        act = conv * jax.nn.sigmoid(conv) * val
        out = out + _dot(act.astype(BF16), wd_ref[cols, :])
    o_ref[...] = out


def _ffn(h, gain, w_up, conv_w, conv_b, w_down, tp):
    m, d = h.shape
    tm = _pick_rows(m, 512)
    assert tm <= tp
    tf = D_FF // 2
    nf = D_FF // tf
    kern = functools.partial(_ffn_kernel, tm=tm, tp=tp, tf=tf)
    fixed = lambda r: (0, 0)
    resident = pl.Buffered(1)
    return pl.pallas_call(
        kern,
        out_shape=jax.ShapeDtypeStruct((m, d), F32),
        grid=(m // tm,),
        in_specs=[pl.BlockSpec((tm, d), lambda r: (r, 0)),
                  pl.BlockSpec((1, d), fixed),
                  pl.BlockSpec((d, 2 * D_FF), fixed, pipeline_mode=resident),
                  pl.BlockSpec((8, D_FF), fixed),
                  pl.BlockSpec((1, D_FF), fixed),
                  pl.BlockSpec((D_FF, d), fixed, pipeline_mode=resident)],
        out_specs=pl.BlockSpec((tm, d), lambda r: (r, 0)),
        scratch_shapes=[pltpu.VMEM((tm + 8, tf), F32),
                        pltpu.VMEM((nf, 8, tf), F32)],
        compiler_params=_params("arbitrary"),
        name="conv_ffn",
    )(h, gain.reshape(1, d), w_up, conv_w, conv_b.reshape(1, D_FF), w_down)


def _permute_w_in(w):
    n_attn = KW_IX - Q_SB + IDX_DIM + IDX_HEADS
    n_gate = 3 * D_MODEL
    n_df = 3 * DF_HEADS * 2 * HEAD_DIM
    gates = w[:, n_attn + n_df:]
    attn = w[:, :n_attn]
    pad = jnp.zeros((w.shape[0], Q_DF - Q_SB - n_attn), w.dtype)
    df = w[:, n_attn:n_attn + n_df]
    out = jnp.concatenate([gates, attn, pad, df], axis=1)
    assert gates.shape[1] == n_gate and out.shape[1] == NZ
    return out


def kernel(x, meta_tokens, rel_bias, attn_norm, w_in, b_gate, q_norm_sp, k_norm_sp, q_norm_df, k_norm_df, lam_q1, lam_k1, lam_q2, lam_k2, subln_df, w_br_sb, w_br_sp, w_br_df, w_out, ffn_norm, w_up, conv_w, conv_b, w_down):
    b, s, d = x.shape
    depth = w_in.shape[0]
    t = N_META + s
    tp = -(-t // QB) * QB
    top_k = min(TOPK_MAX, t // 4)
    m = b * tp

    meta = jnp.broadcast_to(meta_tokens[None].astype(x.dtype), (b, N_META, d))
    h = jnp.concatenate([meta, x, jnp.zeros((b, tp - t, d), x.dtype)], axis=1).reshape(m, d)

    bkt_t = jnp.asarray(np.ascontiguousarray(_bucket_tile().T))
    u_mat = jnp.asarray(_sb_prefix_matrix(), BF16)
    rep, bcast, before = (jnp.asarray(a, BF16) for a in _ix_select_matrices())
    rel_bias = rel_bias.astype(F32)

    for l in range(depth):
        lam_init = 0.8 - 0.6 * math.exp(-0.3 * l)
        z = _in_proj(h, attn_norm[l], _permute_w_in(w_in[l]).astype(BF16))
        z3 = z.reshape(b, tp, NZ)
        y_sb = _sb_attn(z3, u_mat, t)
        y_sp = _sp_attn(z3, rel_bias,
                        jnp.tile(q_norm_sp[l].astype(F32), SP_HEADS).reshape(1, -1),
                        jnp.tile(k_norm_sp[l].astype(F32), SP_HEADS).reshape(1, -1),
                        bkt_t, rep, bcast, before, top_k)
        lamv = jnp.zeros((8, BLK), F32).at[:4, :HEAD_DIM].set(
            jnp.stack([lam_q1[l], lam_k1[l], lam_q2[l], lam_k2[l]]).astype(F32))
        y_df = _dft_attn(z3, rel_bias,
                         jnp.tile(q_norm_df[l].astype(F32), 2).reshape(1, -1),
                         jnp.tile(k_norm_df[l].astype(F32), 2).reshape(1, -1),
                         bkt_t, lamv, subln_df[l].astype(F32).reshape(1, -1), lam_init, t)
        h = _mix_out(h, z, b_gate[l], y_sb.reshape(m, -1), y_sp.reshape(m, -1), y_df.reshape(m, -1),
                     w_br_sb[l].astype(BF16), w_br_sp[l].astype(BF16), w_br_df[l].astype(BF16),
                     w_out[l].astype(BF16))
        cw = jnp.zeros((8, D_FF), F32).at[:conv_w.shape[1]].set(conv_w[l])
        h = _ffn(h, ffn_norm[l], w_up[l].astype(BF16), cw, conv_b[l], w_down[l].astype(BF16), tp)

    return h.reshape(b, tp, d)[:, N_META:t]
```

```python
import functools
import math

import numpy as np
import jax
import jax.numpy as jnp
from jax import lax
from jax.experimental import pallas as pl
from jax.experimental.pallas import tpu as pltpu

D_MODEL = 1024
HEAD_DIM = 64
N_META = 16
BLK = 128
QB = 256
SB_HEADS = 4
SP_HEADS = 4
IDX_HEADS = 8
IDX_DIM = 32
TOPK_MAX = 256
DF_HEADS = 4
N_BUCKETS = 32
MAX_DISTANCE = 128
D_FF = 2816
EPS = 1e-6
LOG2E = math.log2(math.e)
QK_SCALE = HEAD_DIM ** -0.5 * LOG2E
M_INIT = -1e30
BOUND_SLACK = 1.02
UNDERFLOW_GUARD = 2.0 ** -100
INT_MIN = -2 ** 31
I16_MIN = -2 ** 15
COUNT_ROWS = 32
DIGIT_ROWS = 64
SCORE_GROUP = 4
SP_WINDOW_GROUP = 3
DF_WINDOW_GROUP = 1
DF_GROUP = 4
SB_FAR = 2
SB_WINDOW_GROUP = 3
SP_FAR = 2
DF_FAR = 4

G_SB, G_SP, G_DF = 0, 1024, 2048
Q_SB, K_SB, V_SB = 3072, 3328, 3584
Q_SP, K_SP, V_SP = 3840, 4096, 4352
Q_IX, KW_IX = 4608, 4864
Q_DF, K_DF, V_DF = 5120, 5632, 6144
NZ = 6656
W_IX_LANE = IDX_DIM

VMEM_LIMIT = 56 * 1024 * 1024

F32 = jnp.float32
BF16 = jnp.bfloat16
NT_DIMS = (((1,), (1,)), ((), ()))


def _nt_dot(a, b):
    return lax.dot_general(a, b, NT_DIMS, preferred_element_type=F32)


def _dot(a, b):
    return jnp.dot(a, b, preferred_element_type=F32)


def _params(*sem):
    return pltpu.CompilerParams(dimension_semantics=sem, vmem_limit_bytes=VMEM_LIMIT)


def _pick_rows(m, cap):
    for c in (2048, 1024, 512, 256):
        if c <= cap and m % c == 0:
            return c
    raise ValueError(f"row count {m} is not a multiple of {QB}")


def _bucket_np(rel):
    n = np.maximum(rel, 0)
    max_exact = N_BUCKETS // 2
    nf = np.maximum(n, 1).astype(np.float32)
    large = max_exact + (np.log(nf / np.float32(max_exact)) / np.float32(math.log(MAX_DISTANCE / max_exact))
                         * np.float32(N_BUCKETS - max_exact)).astype(np.int32)
    return np.where(n < max_exact, n, np.minimum(large, N_BUCKETS - 1)).astype(np.int32)


def _bucket_tile():
    tq = np.arange(QB)[:, None]
    c = np.arange(2 * QB)[None, :]
    return _bucket_np(tq - c + QB)


def _sb_prefix_matrix():
    sp = np.arange(2 * BLK)[:, None] % BLK
    c = np.arange(2 * BLK)[None, :]
    return np.where(c < BLK, sp > c, True).astype(np.float32)


def _ix_select_matrices():
    c = np.arange(QB)[:, None]
    col = np.arange(QB)[None, :]
    rep = ((c < IDX_DIM) & (c == col % IDX_DIM)).astype(np.float32)
    col8 = np.arange(IDX_HEADS * BLK)[None, :]
    bcast = (c == W_IX_LANE + col8 // BLK).astype(np.float32)
    before = (col < c).astype(np.float32)
    return rep, bcast, before


def _lane_iota(shape):
    return lax.broadcasted_iota(jnp.int32, shape, len(shape) - 1)


def _row_iota(shape):
    return lax.broadcasted_iota(jnp.int32, shape, 0)


def _head_rmsnorm128(xf, gain):
    lo = _lane_iota((1, BLK)) < HEAD_DIM
    ss = xf * xf
    s_lo = jnp.sum(jnp.where(lo, ss, 0.0), axis=-1, keepdims=True)
    s_hi = jnp.sum(jnp.where(lo, 0.0, ss), axis=-1, keepdims=True)
    ms = jnp.where(lo, s_lo, s_hi) * (1.0 / HEAD_DIM)
    return xf * lax.rsqrt(ms + EPS) * gain


def _bias_tile(tab_ref, bk, head):
    far = tab_ref[N_BUCKETS - 1, head]
    acc = jnp.zeros(bk.shape, F32)
    for b in range(N_BUCKETS - 1):
        acc = jnp.where(bk == b, (tab_ref[b, head] - far) * LOG2E, acc)
    return acc


def _softmax_piece_t(st, vt, m_ref, l_ref, acc_ref):
    m_old = m_ref[0:1]
    m_new = jnp.maximum(m_old, jnp.max(st, axis=0, keepdims=True))
    alpha = jnp.exp2(m_old - m_new)
    p = jnp.exp2(st - m_new)
    l_ref[0:1] = alpha * l_ref[0:1] + jnp.sum(p, axis=0, keepdims=True)
    m_ref[0:1] = m_new
    acc_ref[...] = alpha * acc_ref[...] + _dot(vt, p.astype(BF16))


def _online_pieces_t(pieces, m_refs, l_refs, acc_refs):
    for st, vt, c in pieces:
        _softmax_piece_t(st, vt, m_refs[c], l_refs[c], acc_refs[c])


def _bounded_pieces_t(pieces, bounds, l_refs, acc_refs):
    ps = [jnp.exp2(st - bounds[c]) for st, _, c in pieces]
    for c in sorted({c for _, _, c in pieces}):
        mine = [(vt, p) for (_, vt, cc), p in zip(pieces, ps) if cc == c]
        l_refs[c][...] += functools.reduce(
            jnp.add, [jnp.sum(p.reshape(p.shape[0] // 8, 8, p.shape[1]), axis=0) for _, p in mine])
        acc_refs[c][...] += functools.reduce(jnp.add, [_dot(vt, p.astype(BF16)) for vt, p in mine])


def _attend_windows(i, width, group, far_fn, near_fn, step):
    n_far = jnp.maximum(i - 1, 0)
    n_win = (n_far + width - 1) // width

    def far(p):
        hi = n_far - p * width
        return far_fn(jnp.maximum(hi - width, 0), hi)

    def body(g, carry):
        step(sum([far(group * g + n) for n in range(group)], []))
        return carry

    lax.fori_loop(0, n_win // group, body, 0)
    first_left = group * (n_win // group)
    for n_left in range(group):
        @pl.when((n_win - first_left == n_left) & (i >= 1))
        def _(n_left=n_left):
            step(sum([far(first_left + n) for n in range(n_left)], []) + near_fn(i - 1, 2))

    @pl.when(i == 0)
    def _():
        step(near_fn(0, 1))


def _for_far_pieces(n_far, width, group, piece_fn):
    n_win = (n_far + width - 1) // width

    def window(p):
        hi = n_far - p * width
        piece_fn(jnp.maximum(hi - width, 0), hi)

    def body(g, carry):
        for n in range(group):
            window(group * g + n)
        return carry

    lax.fori_loop(0, n_win // group, body, 0)
    first_left = group * (n_win // group)
    for n_left in range(1, group):
        @pl.when(n_win - first_left == n_left)
        def _(n_left=n_left):
            for n in range(n_left):
                window(first_left + n)


def _in_proj_kernel(h_ref, g_ref, w_ref, o_ref, *, tn):
    x = h_ref[...]
    ms = jnp.mean(x * x, axis=-1, keepdims=True)
    u = (x * lax.rsqrt(ms + EPS) * g_ref[...]).astype(BF16)
    for c in range(o_ref.shape[1] // tn):
        cols = slice(c * tn, (c + 1) * tn)
        o_ref[:, cols] = _dot(u, w_ref[:, cols]).astype(o_ref.dtype)


def _in_proj(h, gain, w):
    m, d = h.shape
    n = w.shape[1]
    tm = _pick_rows(m, 512)
    return pl.pallas_call(
        functools.partial(_in_proj_kernel, tn=512),
        out_shape=jax.ShapeDtypeStruct((m, n), BF16),
        grid=(m // tm,),
        in_specs=[pl.BlockSpec((tm, d), lambda i: (i, 0)),
                  pl.BlockSpec((1, d), lambda i: (0, 0)),
                  pl.BlockSpec((d, n), lambda i: (0, 0), pipeline_mode=pl.Buffered(1))],
        out_specs=pl.BlockSpec((tm, n), lambda i: (i, 0)),
        compiler_params=_params("parallel"),
        name="in_proj",
    )(h, gain.reshape(1, d), w)


def _sb_kernel(q_ref, k_ref, v_ref, u_ref, o_ref, tot_ref, acc_ref, *, nq, tail_q):
    i = pl.program_id(1)
    nh = SB_HEADS

    def query_block(qt):
        head_of_lane = _lane_iota((qt, nh * HEAD_DIM)) // HEAD_DIM
        q = q_ref[0, :qt].astype(F32) * QK_SCALE
        qs = jnp.concatenate([jnp.where(head_of_lane == h, q, 0.0) for h in range(nh)], axis=0).astype(BF16)
        tot = tot_ref.at[:nh * qt]
        acc = acc_ref.at[:nh * qt]
        tot[...] = jnp.zeros_like(tot)
        acc[...] = jnp.zeros_like(acc)

        def piece(start, n_blocks, hi=None):
            r = pl.multiple_of(start * QB, QB)
            z_all = _nt_dot(qs, k_ref[0, pl.ds(r, n_blocks * QB), :])
            if hi is not None:
                col = _lane_iota((1, n_blocks * QB))
                z_all = z_all + jnp.where(col < (hi - start) * QB, 0.0, -jnp.inf)
            run = tot[...]
            n_sub = n_blocks * QB // BLK
            ws = [None] * n_sub
            for c in reversed(range(n_sub)):
                z = z_all[:, c * BLK:(c + 1) * BLK]
                sp = jnp.maximum(z, 0.0) + jnp.log2(1.0 + jnp.exp2(-jnp.abs(z)))
                l1m = -sp
                if hi is None:
                    mask = (_lane_iota(z.shape) + c * BLK) < (_row_iota(z.shape) & (qt - 1))
                    l1m = jnp.where(mask, l1m, 0.0)
                l1m_hi = l1m.astype(BF16)
                l1m_lo = (l1m - l1m_hi.astype(F32)).astype(BF16)
                rs = _dot(jnp.concatenate([l1m_hi, l1m_lo], axis=1), u_ref[...])
                w = jnp.exp2((z - sp) + rs[:, :BLK] + run)
                if hi is None:
                    w = jnp.where(mask, w, 0.0)
                ws[c] = w.astype(BF16)
                run = run + rs[:, BLK:]
            tot[...] = run
            acc[...] += _dot(jnp.concatenate(ws, axis=1), v_ref[0, pl.ds(r, n_blocks * QB), :])

        piece(i, 1)
        _for_far_pieces(i, SB_FAR, SB_WINDOW_GROUP, lambda start, hi: piece(start, SB_FAR, hi))

        a = acc[...]
        out = a[:qt]
        for h in range(1, nh):
            out = jnp.where(head_of_lane == h, a[h * qt:(h + 1) * qt], out)
        o_ref[0, :qt] = out.astype(o_ref.dtype)
        if qt < QB:
            o_ref[0, qt:] = jnp.zeros((QB - qt, nh * HEAD_DIM), o_ref.dtype)

    if tail_q == QB:
        query_block(QB)
    else:
        @pl.when(i < nq - 1)
        def _():
            query_block(QB)

        @pl.when(i == nq - 1)
        def _():
            query_block(tail_q)


def _tail_rows(t, tp, granule):
    real = t - (tp - QB)
    return min(QB, -(-real // granule) * granule)


def _sb_attn(z3, u_mat, t):
    b, tp, _ = z3.shape
    nq = tp // QB
    assert nq >= SB_FAR
    w = SB_HEADS * HEAD_DIM
    return pl.pallas_call(
        functools.partial(_sb_kernel, nq=nq, tail_q=_tail_rows(t, tp, 64)),
        out_shape=jax.ShapeDtypeStruct((b, tp, w), BF16),
        grid=(b, nq),
        in_specs=[pl.BlockSpec((1, QB, w), lambda bb, i: (bb, i, Q_SB // w)),
                  pl.BlockSpec((1, tp, w), lambda bb, i: (bb, 0, K_SB // w)),
                  pl.BlockSpec((1, tp, w), lambda bb, i: (bb, 0, V_SB // w)),
                  pl.BlockSpec((2 * BLK, 2 * BLK), lambda bb, i: (0, 0))],
        out_specs=pl.BlockSpec((1, QB, w), lambda bb, i: (bb, i, 0)),
        scratch_shapes=[pltpu.VMEM((SB_HEADS * QB, BLK), F32),
                        pltpu.VMEM((SB_HEADS * QB, w), F32)],
        compiler_params=_params("parallel", "arbitrary"),
        name="sb_attn",
    )(z3, z3, z3, u_mat)


def _dft_kernel(tab_ref, q_ref, k_ref, v_ref, qg_ref, kg_ref, bkt_ref, lamv_ref, sub_ref, o_ref,
                kn_ref, vt_ref, bias_ref, kmax_ref, m_ref, l_ref, lsum_ref, acc_ref,
                *, nq, tail_q, lam_init, head_off):
    g = pl.program_id(1)
    i = pl.program_id(2)
    lo = _lane_iota((1, BLK)) < HEAD_DIM
    heads = range(DF_GROUP)
    lanes = [slice(c * BLK, (c + 1) * BLK) for c in heads]

    @pl.when(i == 0)
    def _prep():
        kmax_ref[...] = jnp.zeros_like(kmax_ref)

        def kbody(j, carry):
            r = pl.multiple_of(j * QB, QB)
            for c in heads:
                kf = k_ref[0, pl.ds(r, QB), lanes[c]].astype(F32)
                kn = _head_rmsnorm128(kf, kg_ref[...]).astype(BF16)
                kn_ref[pl.ds(r, QB), lanes[c]] = kn
                ksq = kn.astype(F32) ** 2
                half_norms = jnp.maximum(jnp.sum(jnp.where(lo, ksq, 0.0), axis=-1, keepdims=True),
                                         jnp.sum(jnp.where(lo, 0.0, ksq), axis=-1, keepdims=True))
                kmax_ref[c] = jnp.maximum(kmax_ref[c], jnp.max(half_norms))
            vt_ref[j] = v_ref[0, pl.ds(r, QB), :].astype(F32).T.astype(BF16)
            return carry

        lax.fori_loop(0, nq, kbody, 0)
        for c in heads:
            head = head_off + g * DF_GROUP + c
            bias_ref[c] = _bias_tile(tab_ref, bkt_ref[...], head)
            bias_max = jnp.float32(0.0)
            for b in range(N_BUCKETS - 1):
                bias_max = jnp.maximum(bias_max, (tab_ref[b, head] - tab_ref[N_BUCKETS - 1, head]) * LOG2E)
            kmax_ref[c, 1:2] = jnp.full((1, BLK), bias_max, F32)

    def query_block(qt):
        m, l, lsum, acc = ([ref.at[c, :, :2 * qt] for c in heads] for ref in (m_ref, l_ref, lsum_ref, acc_ref))
        qs, bound = [], []
        for c in heads:
            qn = _head_rmsnorm128(q_ref[0, :qt, lanes[c]].astype(F32), qg_ref[...]) * QK_SCALE
            qs.append(jnp.concatenate([jnp.where(lo, qn, 0.0), jnp.where(lo, 0.0, qn)], axis=0).astype(BF16))
            q_sq = _nt_dot(jnp.ones((8, BLK), BF16), (qs[c].astype(F32) ** 2).astype(BF16))[0:1]
            bound.append(jnp.sqrt(q_sq * kmax_ref[c, 0:1, 0:1]) * BOUND_SLACK
                         + (kmax_ref[c, 1:2, 0:1] + BOUND_SLACK))

        def scores(c, first, n):
            rows = pl.ds(pl.multiple_of(first * QB, QB), n * QB)
            return _nt_dot(kn_ref[rows, lanes[c]], qs[c])

        def values(c, first, n):
            return jnp.concatenate([vt_ref[first + b, lanes[c], :] for b in range(n)], axis=1)

        def far_piece(start, hi):
            pieces = []
            for c in heads:
                st = scores(c, start, DF_FAR)
                new_keys = _row_iota(st.shape) < (hi - start) * QB
                pieces.append((jnp.where(new_keys, st, -jnp.inf), values(c, start, DF_FAR), c))
            return pieces

        def near_piece(first, n):
            pieces = []
            for c in heads:
                b = bias_ref[c, (2 - n) * QB:, :qt]
                st = scores(c, first, n) + jnp.concatenate([b, b], axis=1)
                causal = (_row_iota(st.shape) - (n - 1) * QB) <= (_lane_iota(st.shape) & (qt - 1))
                pieces.append((jnp.where(causal, st, -jnp.inf), values(c, first, n), c))
            return pieces

        def attend(step):
            _attend_windows(i, DF_FAR, DF_WINDOW_GROUP, far_piece, near_piece, step)

        for c in heads:
            l[c][...] = jnp.zeros_like(l[c])
            acc[c][...] = jnp.zeros_like(acc[c])
        attend(lambda pieces: _bounded_pieces_t(pieces, bound, l, acc))
        for c in heads:
            lsum[c][0:1] = jnp.sum(l[c][...], axis=0, keepdims=True)

        @pl.when(functools.reduce(jnp.minimum, [jnp.min(lsum[c][0:1]) for c in heads]) < UNDERFLOW_GUARD)
        def _():
            for c in heads:
                m[c][...] = jnp.full(m[c].shape, M_INIT, F32)
                l[c][...] = jnp.zeros_like(l[c])
                acc[c][...] = jnp.zeros_like(acc[c])
            attend(lambda pieces: _online_pieces_t(pieces, m, l, acc))
            for c in heads:
                lsum[c][0:1] = l[c][0:1]

        lv = lamv_ref[...]
        lam = (jnp.exp(jnp.sum(lv[0:1] * lv[1:2], axis=-1, keepdims=True))
               - jnp.exp(jnp.sum(lv[2:3] * lv[3:4], axis=-1, keepdims=True)) + lam_init)
        for c in heads:
            a = acc[c][...] / lsum[c][0:1]
            y = (a[:, :qt] - lam * a[:, qt:]).T
            y = y * lax.rsqrt(jnp.mean(y * y, axis=-1, keepdims=True) + EPS) * sub_ref[...]
            o_ref[0, :qt, lanes[c]] = (y * (1.0 - lam_init)).astype(o_ref.dtype)
        if qt < QB:
            o_ref[0, qt:] = jnp.zeros((QB - qt, DF_GROUP * BLK), o_ref.dtype)

    if tail_q == QB:
        query_block(QB)
    else:
        @pl.when(i < nq - 1)
        def _():
            query_block(QB)

        @pl.when(i == nq - 1)
        def _():
            query_block(tail_q)


def _dft_attn(z3, rel_bias, qg, kg, bkt_t, lamv, subln, lam_init, t):
    b, tp, _ = z3.shape
    nq = tp // QB
    assert nq >= DF_FAR
    kern = functools.partial(_dft_kernel, nq=nq, tail_q=_tail_rows(t, tp, 64), lam_init=lam_init,
                             head_off=SP_HEADS)
    vec = lambda bb, g, i: (0, 0)
    w = DF_GROUP * BLK
    per_head = lambda rows, cols: pltpu.VMEM((DF_GROUP, rows, cols), F32)
    return pl.pallas_call(
        kern,
        out_shape=jax.ShapeDtypeStruct((b, tp, DF_HEADS * BLK), BF16),
        grid=(b, DF_HEADS // DF_GROUP, nq),
        in_specs=[pl.BlockSpec(memory_space=pltpu.SMEM),
                  pl.BlockSpec((1, QB, w), lambda bb, g, i: (bb, i, Q_DF // w + g)),
                  pl.BlockSpec((1, tp, w), lambda bb, g, i: (bb, 0, K_DF // w + g)),
                  pl.BlockSpec((1, tp, w), lambda bb, g, i: (bb, 0, V_DF // w + g)),
                  pl.BlockSpec((1, BLK), vec),
                  pl.BlockSpec((1, BLK), vec),
                  pl.BlockSpec((2 * QB, QB), vec),
                  pl.BlockSpec((8, BLK), vec),
                  pl.BlockSpec((1, BLK), vec)],
        out_specs=pl.BlockSpec((1, QB, w), lambda bb, g, i: (bb, i, g)),
        scratch_shapes=[pltpu.VMEM((tp, w), BF16),
                        pltpu.VMEM((nq, w, QB), BF16),
                        per_head(2 * QB, QB),
                        per_head(8, BLK),
                        per_head(8, 2 * QB),
                        per_head(8, 2 * QB),
                        per_head(8, 2 * QB),
                        per_head(BLK, 2 * QB)],
        compiler_params=_params("parallel", "parallel", "arbitrary"),
        name="df_attn",
    )(rel_bias, z3, z3, z3, qg, kg, bkt_t, lamv, subln)


def _sp_kernel(tab_ref, q_ref, k_ref, v_ref, qix_ref, kwq_ref, kwk_ref, qg_ref, kg_ref, bkt_ref,
               rep_ref, bcast_ref, before_ref, o_ref,
               kn_ref, kx_ref, vt_ref, bias_ref, kmax_ref, keys_ref, dig_ref, thr_ref, wb_ref,
               m_ref, l_ref, lsum_ref, acc_ref, *, nq, top_k):
    i = pl.program_id(1)
    nh = SP_HEADS
    w = nh * HEAD_DIM
    lane = _lane_iota((QB, w))
    head_of_lane = lane // HEAD_DIM

    def norm256(xf, g):
        return jnp.concatenate([_head_rmsnorm128(xf[:, :BLK], g[:, :BLK]),
                                _head_rmsnorm128(xf[:, BLK:], g[:, BLK:])], axis=1)

    def dup(x):
        return jnp.concatenate([x, x], axis=1)

    @pl.when(i == 0)
    def _prep():
        kmax_ref[...] = jnp.zeros_like(kmax_ref)

        def kbody(c, carry):
            r = pl.multiple_of(c * QB, QB)
            kn = norm256(k_ref[0, pl.ds(r, QB), :].astype(F32), kg_ref[...]).astype(BF16)
            kn_ref[pl.ds(r, QB), :] = kn
            ksq = kn.astype(F32) ** 2
            head_norms = functools.reduce(jnp.maximum, [
                jnp.sum(jnp.where(head_of_lane == h, ksq, 0.0), axis=-1, keepdims=True) for h in range(nh)])
            kmax_ref[0:1] = jnp.maximum(kmax_ref[0:1], jnp.max(head_norms))
            kx_ref[pl.ds(r, QB), :] = _dot(kwk_ref[0, pl.ds(r, QB), :], rep_ref[...]).astype(BF16)
            vt_ref[c] = v_ref[0, pl.ds(r, QB), :].astype(F32).T.astype(BF16)
            return carry

        lax.fori_loop(0, nq, kbody, 0)
        bias_max = jnp.float32(0.0)
        for h in range(nh):
            bias_ref[:, h * QB:(h + 1) * QB] = _bias_tile(tab_ref, bkt_ref[...], h)
            for b in range(N_BUCKETS - 1):
                bias_max = jnp.maximum(bias_max, (tab_ref[b, h] - tab_ref[N_BUCKETS - 1, h]) * LOG2E)
        kmax_ref[1:2] = jnp.full((1, BLK), bias_max, F32)

    key_causal = _row_iota((QB, QB)) <= _lane_iota((QB, QB))

    wb_ref[...] = _dot(kwq_ref[0], bcast_ref[...])
    qix = qix_ref[0].astype(F32)
    ix_head = lane // IDX_DIM
    qx = jnp.concatenate([jnp.where(ix_head == h, qix, 0.0) for h in range(IDX_HEADS)], axis=0).astype(BF16)

    def score_tiles(blocks, last_is_diagonal):
        dots = [_nt_dot(qx, kx_ref[pl.ds(pl.multiple_of(j * QB, QB), QB), :]) for j in blocks]
        for n, (j, d) in enumerate(zip(blocks, dots)):
            sc = jnp.zeros((QB, QB), F32)
            for h in range(IDX_HEADS):
                sc = sc + dup(wb_ref[:, h * BLK:(h + 1) * BLK]) * jnp.maximum(d[h * QB:(h + 1) * QB], 0.0)
            sc = jnp.where(sc == 0.0, 0.0, sc).T
            bits = lax.bitcast_convert_type(sc, jnp.int32)
            key = jnp.where(bits < 0, bits ^ jnp.int32(0x7FFFFFFF), bits)
            if last_is_diagonal and n == len(blocks) - 1:
                key = jnp.where(key_causal, key, jnp.int32(INT_MIN))
            keys_ref[j] = key
            dig_ref[j] = jnp.right_shift(key, 16).astype(jnp.int16)

    def group_body(g, c):
        score_tiles([SCORE_GROUP * g + n for n in range(SCORE_GROUP)], False)
        return c

    lax.fori_loop(0, i // SCORE_GROUP, group_body, 0)
    first_left = SCORE_GROUP * (i // SCORE_GROUP)
    for n_left in range(1, SCORE_GROUP + 1):
        @pl.when(i - first_left == n_left - 1)
        def _(n_left=n_left):
            score_tiles([first_left + n for n in range(n_left)], True)

    nblk = i + 1

    def count(pred):
        def cbody(j, c):
            hit = jnp.where(pred(keys_ref[j]), 1.0, 0.0)
            return c + jnp.sum(hit.reshape(QB // COUNT_ROWS, COUNT_ROWS, QB), axis=0)

        c = lax.fori_loop(0, nblk, cbody, jnp.zeros((COUNT_ROWS, QB), F32))
        return jnp.sum(c, axis=0, keepdims=True)

    def count16(pred):
        def cbody(j, c):
            hit = jnp.where(pred(dig_ref[j]), jnp.int16(1), jnp.int16(0)).reshape(QB // DIGIT_ROWS, DIGIT_ROWS, QB)
            return c + functools.reduce(jnp.add, [hit[r] for r in range(QB // DIGIT_ROWS)])

        c = lax.fori_loop(0, nblk, cbody, jnp.zeros((DIGIT_ROWS, QB), jnp.int16))
        return jnp.sum(c.astype(F32), axis=0, keepdims=True)

    n_all = jnp.zeros((1, QB), F32) + (nblk * QB).astype(F32)

    def bisect16(need):
        def bit_body(b, carry):
            cur, cnt_cur = carry
            cand = cur + jnp.left_shift(jnp.int32(1), 15 - b)
            cand16 = cand.astype(jnp.int16)
            cnt = count16(lambda d: d >= cand16)
            ok = cnt >= need
            return jnp.where(ok, cand, cur), jnp.where(ok, cnt, cnt_cur)

        return lax.fori_loop(0, 16, bit_body, (jnp.full((1, QB), I16_MIN, jnp.int32), n_all))

    t_hi, c_ge_hi = bisect16(float(top_k))
    t_hi16 = t_hi.astype(jnp.int16)
    c_gt_hi = count16(lambda d: d > t_hi16)
    base = jnp.left_shift(t_hi, 16)

    def low_digits(j, c):
        y = keys_ref[j] - base
        dig_ref[j] = jnp.where(jnp.right_shift(y, 16) == 0, y + I16_MIN, I16_MIN).astype(jnp.int16)
        return c

    lax.fori_loop(0, nblk, low_digits, 0)
    t_lo, c_ge_lo = bisect16(float(top_k) - c_gt_hi)
    thr = base + (t_lo - I16_MIN)
    cge = c_gt_hi + jnp.where(t_lo > I16_MIN, c_ge_lo, c_ge_hi - c_gt_hi)
    thr_ref[...] = jnp.broadcast_to(thr, thr_ref.shape)

    tie = jnp.where((cge > float(top_k)) & (thr > INT_MIN), 1, 0)

    @pl.when(jnp.max(tie) > 0)
    def _ties():
        need = float(top_k) - count(lambda k: k > thr)

        def tbody(j, run):
            kj = keys_ref[j]
            eq = kj == thr
            eqf = jnp.where(eq, 1.0, 0.0)
            rank = run + _dot(before_ref[...], eqf.astype(BF16))
            keep = jnp.where(kj > thr, 1, jnp.where(eq & (rank < need), 1, -1))
            keys_ref[j] = keep.astype(jnp.int32)
            return run + jnp.sum(eqf, axis=0, keepdims=True)

        lax.fori_loop(0, nblk, tbody, jnp.zeros((1, QB), F32))
        thr_ref[...] = jnp.zeros_like(thr_ref)

    qn = norm256(q_ref[0].astype(F32), qg_ref[...]) * QK_SCALE
    qs = jnp.concatenate([jnp.where(head_of_lane == h, qn, 0.0) for h in range(nh)], axis=0).astype(BF16)
    q_sq = _nt_dot(jnp.ones((8, w), BF16), (qs.astype(F32) ** 2).astype(BF16))[0:1]
    bound = jnp.sqrt(q_sq * kmax_ref[0:1, 0:1]) * BOUND_SLACK + (kmax_ref[1:2, 0:1] + BOUND_SLACK)

    thr_sel = thr_ref[0:1, :]

    def as_mask(sel):
        return lax.bitcast_convert_type(jnp.where(sel, 0.0, -jnp.inf), jnp.int32)

    def mask_body(j, c):
        keys_ref[j] = as_mask(keys_ref[j] >= thr_sel)
        return c

    lax.fori_loop(0, i, mask_body, 0)
    keys_ref[i] = as_mask((keys_ref[i] >= thr_sel) & key_causal)

    def scores(first, n):
        return _nt_dot(kn_ref[pl.ds(pl.multiple_of(first * QB, QB), n * QB), :], qs)

    def values(first, n):
        return jnp.concatenate([vt_ref[first + c] for c in range(n)], axis=1)

    def selection(first, n):
        return jnp.concatenate([lax.bitcast_convert_type(keys_ref[first + c], F32) for c in range(n)],
                               axis=0)

    def far_piece(start, hi):
        m = selection(start, SP_FAR)
        m = jnp.where(_row_iota(m.shape) < (hi - start) * QB, m, -jnp.inf)
        return [(scores(start, SP_FAR) + jnp.concatenate([m] * nh, axis=1), values(start, SP_FAR), 0)]

    def near_piece(first, n):
        m = selection(first, n)
        return [(scores(first, n) + (jnp.concatenate([m] * nh, axis=1) + bias_ref[(2 - n) * QB:, :]),
                 values(first, n), 0)]

    def attend(step):
        _attend_windows(i, SP_FAR, SP_WINDOW_GROUP, far_piece, near_piece, step)

    l_ref[...] = jnp.zeros_like(l_ref)
    acc_ref[...] = jnp.zeros_like(acc_ref)
    attend(lambda pieces: _bounded_pieces_t(pieces, [bound], [l_ref], [acc_ref]))
    lsum_ref[0:1] = jnp.sum(l_ref[...], axis=0, keepdims=True)

    @pl.when(jnp.min(lsum_ref[0:1]) < UNDERFLOW_GUARD)
    def _():
        m_ref[...] = jnp.full(m_ref.shape, M_INIT, F32)
        l_ref[...] = jnp.zeros_like(l_ref)
        acc_ref[...] = jnp.zeros_like(acc_ref)
        attend(lambda pieces: _online_pieces_t(pieces, [m_ref], [l_ref], [acc_ref]))
        lsum_ref[0:1] = l_ref[0:1]

    a = acc_ref[...] / lsum_ref[0:1]
    out_t = jnp.concatenate([a[h * HEAD_DIM:(h + 1) * HEAD_DIM, h * QB:(h + 1) * QB] for h in range(nh)], axis=0)
    o_ref[0] = out_t.T.astype(o_ref.dtype)


def _sp_attn(z3, rel_bias, qg, kg, bkt_t, rep, bcast, before, top_k):
    b, tp, _ = z3.shape
    nq = tp // QB
    assert nq >= SP_FAR
    w = SP_HEADS * HEAD_DIM
    kern = functools.partial(_sp_kernel, nq=nq, top_k=top_k)
    c2 = lambda bb, i: (0, 0)
    return pl.pallas_call(
        kern,
        out_shape=jax.ShapeDtypeStruct((b, tp, w), BF16),
        grid=(b, nq),
        in_specs=[pl.BlockSpec(memory_space=pltpu.SMEM),
                  pl.BlockSpec((1, QB, w), lambda bb, i: (bb, i, Q_SP // w)),
                  pl.BlockSpec((1, tp, w), lambda bb, i: (bb, 0, K_SP // w)),
                  pl.BlockSpec((1, tp, w), lambda bb, i: (bb, 0, V_SP // w)),
                  pl.BlockSpec((1, QB, w), lambda bb, i: (bb, i, Q_IX // w)),
                  pl.BlockSpec((1, QB, w), lambda bb, i: (bb, i, KW_IX // w)),
                  pl.BlockSpec((1, tp, w), lambda bb, i: (bb, 0, KW_IX // w)),
                  pl.BlockSpec((1, w), c2),
                  pl.BlockSpec((1, w), c2),
                  pl.BlockSpec((2 * QB, QB), c2),
                  pl.BlockSpec((w, w), c2),
                  pl.BlockSpec((w, IDX_HEADS * BLK), c2),
                  pl.BlockSpec((QB, QB), c2)],
        out_specs=pl.BlockSpec((1, QB, w), lambda bb, i: (bb, i, 0)),
        scratch_shapes=[pltpu.VMEM((tp, w), BF16),
                        pltpu.VMEM((tp, w), BF16),
                        pltpu.VMEM((nq, w, QB), BF16),
                        pltpu.VMEM((2 * QB, SP_HEADS * QB), F32),
                        pltpu.VMEM((8, BLK), F32),
                        pltpu.VMEM((nq, QB, QB), jnp.int32),
                        pltpu.VMEM((nq, QB, QB), jnp.int16),
                        pltpu.VMEM((8, QB), jnp.int32),
                        pltpu.VMEM((QB, IDX_HEADS * BLK), F32),
                        pltpu.VMEM((8, SP_HEADS * QB), F32),
                        pltpu.VMEM((8, SP_HEADS * QB), F32),
                        pltpu.VMEM((8, SP_HEADS * QB), F32),
                        pltpu.VMEM((w, SP_HEADS * QB), F32)],
        compiler_params=_params("parallel", "arbitrary"),
        name="sp_attn",
    )(rel_bias, z3, z3, z3, z3, z3, z3, qg, kg, bkt_t, rep, bcast, before)


def _mix_kernel(h_ref, gsb_ref, gsp_ref, gdf_ref, bg_ref, ysb_ref, ysp_ref, ydf_ref,
                wsb_ref, wsp_ref, wdf_ref, wo_ref, o_ref):
    def branch(g_ref, k, y_ref, w_ref):
        gate = jax.nn.sigmoid(g_ref[...].astype(F32) + bg_ref[:, k * D_MODEL:(k + 1) * D_MODEL])
        return gate * _dot(y_ref[...], w_ref[...])

    merged = (branch(gsb_ref, 0, ysb_ref, wsb_ref) + branch(gsp_ref, 1, ysp_ref, wsp_ref)
              + branch(gdf_ref, 2, ydf_ref, wdf_ref))
    o_ref[...] = h_ref[...] + _dot(merged.astype(BF16), wo_ref[...])


def _mix_out(h, z, b_gate, y_sb, y_sp, y_df, w_sb, w_sp, w_df, w_o):
    m, d = h.shape
    tm = _pick_rows(m, 512)
    row = lambda i: (i, 0)
    fixed = lambda i: (0, 0)
    return pl.pallas_call(
        _mix_kernel,
        out_shape=jax.ShapeDtypeStruct((m, d), F32),
        grid=(m // tm,),
        in_specs=[pl.BlockSpec((tm, d), row),
                  pl.BlockSpec((tm, d), lambda i: (i, G_SB // D_MODEL)),
                  pl.BlockSpec((tm, d), lambda i: (i, G_SP // D_MODEL)),
                  pl.BlockSpec((tm, d), lambda i: (i, G_DF // D_MODEL)),
                  pl.BlockSpec((1, 3 * d), fixed),
                  pl.BlockSpec((tm, y_sb.shape[1]), row),
                  pl.BlockSpec((tm, y_sp.shape[1]), row),
                  pl.BlockSpec((tm, y_df.shape[1]), row),
                  pl.BlockSpec(w_sb.shape, fixed),
                  pl.BlockSpec(w_sp.shape, fixed),
                  pl.BlockSpec(w_df.shape, fixed),
                  pl.BlockSpec(w_o.shape, fixed)],
        out_specs=pl.BlockSpec((tm, d), row),
        compiler_params=_params("parallel"),
        name="mix_out",
    )(h, z, z, z, b_gate.reshape(1, 3 * d), y_sb, y_sp, y_df, w_sb, w_sp, w_df, w_o)


def _ffn_kernel(h_ref, g_ref, wu_ref, cw_ref, cb_ref, wd_ref, o_ref, gbuf_ref, carry_ref, *, tm, tp, tf):
    r = pl.program_id(0)
    x = h_ref[...]
    ms = jnp.mean(x * x, axis=-1, keepdims=True)
    u = (x * lax.rsqrt(ms + EPS) * g_ref[...]).astype(BF16)

    @pl.when(r == 0)
    def _():
        carry_ref[...] = jnp.zeros_like(carry_ref)

    seq_start = lax.rem(tp - lax.rem(r * tm, tp), tp)
    local = lax.broadcasted_iota(jnp.int32, (tm, 1), 0)
    tap1 = local != seq_start
    tap2 = tap1 & (local != seq_start + 1)
    out = x
    for f in range(D_FF // tf):
        cols = slice(f * tf, (f + 1) * tf)
        gate = _dot(u, wu_ref[:, cols])
        val = _dot(u, wu_ref[:, D_FF + f * tf:D_FF + (f + 1) * tf])
        gbuf_ref[0:8] = carry_ref[f]
        gbuf_ref[8:8 + tm] = gate
        carry_ref[f] = gate[tm - 8:tm]
        g1 = jnp.where(tap1, gbuf_ref[7:7 + tm], 0.0)
        g2 = jnp.where(tap2, gbuf_ref[6:6 + tm], 0.0)
        conv = cb_ref[:, cols] + cw_ref[0:1, cols] * g2 + cw_ref[1:2, cols] * g1 + cw_ref[2:3, cols] * gate
        act = conv * jax.nn.sigmoid(conv) * val
        out = out + _dot(act.astype(BF16), wd_ref[cols, :])
    o_ref[...] = out


def _ffn(h, gain, w_up, conv_w, conv_b, w_down, tp):
    m, d = h.shape
    tm = _pick_rows(m, 512)
    assert tm <= tp
    tf = D_FF // 2
    nf = D_FF // tf
    kern = functools.partial(_ffn_kernel, tm=tm, tp=tp, tf=tf)
    fixed = lambda r: (0, 0)
    resident = pl.Buffered(1)
    return pl.pallas_call(
        kern,
        out_shape=jax.ShapeDtypeStruct((m, d), F32),
        grid=(m // tm,),
        in_specs=[pl.BlockSpec((tm, d), lambda r: (r, 0)),
                  pl.BlockSpec((1, d), fixed),
                  pl.BlockSpec((d, 2 * D_FF), fixed, pipeline_mode=resident),
                  pl.BlockSpec((8, D_FF), fixed),
                  pl.BlockSpec((1, D_FF), fixed),
                  pl.BlockSpec((D_FF, d), fixed, pipeline_mode=resident)],
        out_specs=pl.BlockSpec((tm, d), lambda r: (r, 0)),
        scratch_shapes=[pltpu.VMEM((tm + 8, tf), F32),
                        pltpu.VMEM((nf, 8, tf), F32)],
        compiler_params=_params("arbitrary"),
        name="conv_ffn",
    )(h, gain.reshape(1, d), w_up, conv_w, conv_b.reshape(1, D_FF), w_down)


def _permute_w_in(w):
    n_attn = KW_IX - Q_SB + IDX_DIM + IDX_HEADS
    n_gate = 3 * D_MODEL
    n_df = 3 * DF_HEADS * 2 * HEAD_DIM
    gates = w[:, n_attn + n_df:]
    attn = w[:, :n_attn]
    pad = jnp.zeros((w.shape[0], Q_DF - Q_SB - n_attn), w.dtype)
    df = w[:, n_attn:n_attn + n_df]
    out = jnp.concatenate([gates, attn, pad, df], axis=1)
    assert gates.shape[1] == n_gate and out.shape[1] == NZ
    return out


def kernel(x, meta_tokens, rel_bias, attn_norm, w_in, b_gate, q_norm_sp, k_norm_sp, q_norm_df, k_norm_df, lam_q1, lam_k1, lam_q2, lam_k2, subln_df, w_br_sb, w_br_sp, w_br_df, w_out, ffn_norm, w_up, conv_w, conv_b, w_down):
    b, s, d = x.shape
    depth = w_in.shape[0]
    t = N_META + s
    tp = -(-t // QB) * QB
    top_k = min(TOPK_MAX, t // 4)
    m = b * tp

    meta = jnp.broadcast_to(meta_tokens[None].astype(x.dtype), (b, N_META, d))
    h = jnp.concatenate([meta, x, jnp.zeros((b, tp - t, d), x.dtype)], axis=1).reshape(m, d)

    bkt_t = jnp.asarray(np.ascontiguousarray(_bucket_tile().T))
    u_mat = jnp.asarray(_sb_prefix_matrix(), BF16)
    rep, bcast, before = (jnp.asarray(a, BF16) for a in _ix_select_matrices())
    rel_bias = rel_bias.astype(F32)

    for l in range(depth):
        lam_init = 0.8 - 0.6 * math.exp(-0.3 * l)
        z = _in_proj(h, attn_norm[l], _permute_w_in(w_in[l]).astype(BF16))
        z3 = z.reshape(b, tp, NZ)
        y_sb = _sb_attn(z3, u_mat, t)
        y_sp = _sp_attn(z3, rel_bias,
                        jnp.tile(q_norm_sp[l].astype(F32), SP_HEADS).reshape(1, -1),
                        jnp.tile(k_norm_sp[l].astype(F32), SP_HEADS).reshape(1, -1),
                        bkt_t, rep, bcast, before, top_k)
        lamv = jnp.zeros((8, BLK), F32).at[:4, :HEAD_DIM].set(
            jnp.stack([lam_q1[l], lam_k1[l], lam_q2[l], lam_k2[l]]).astype(F32))
        y_df = _dft_attn(z3, rel_bias,
                         jnp.tile(q_norm_df[l].astype(F32), 2).reshape(1, -1),
                         jnp.tile(k_norm_df[l].astype(F32), 2).reshape(1, -1),
                         bkt_t, lamv, subln_df[l].astype(F32).reshape(1, -1), lam_init, t)
        h = _mix_out(h, z, b_gate[l], y_sb.reshape(m, -1), y_sp.reshape(m, -1), y_df.reshape(m, -1),
                     w_br_sb[l].astype(BF16), w_br_sp[l].astype(BF16), w_br_df[l].astype(BF16),
                     w_out[l].astype(BF16))
        cw = jnp.zeros((8, D_FF), F32).at[:conv_w.shape[1]].set(conv_w[l])
        h = _ffn(h, ffn_norm[l], w_up[l].astype(BF16), cw, conv_b[l], w_down[l].astype(BF16), tp)

    return h.reshape(b, tp, d)[:, N_META:t]
```

```python
import functools
import math

import numpy as np
import jax
import jax.numpy as jnp
from jax import lax
from jax.experimental import pallas as pl
from jax.experimental.pallas import tpu as pltpu

D_MODEL = 1024
HEAD_DIM = 64
N_META = 16
BLK = 128
QB = 256
SB_HEADS = 4
SP_HEADS = 4
IDX_HEADS = 8
IDX_DIM = 32
TOPK_MAX = 256
DF_HEADS = 4
N_BUCKETS = 32
MAX_DISTANCE = 128
D_FF = 2816
EPS = 1e-6
LOG2E = math.log2(math.e)
QK_SCALE = HEAD_DIM ** -0.5 * LOG2E
M_INIT = -1e30
BOUND_SLACK = 1.02
UNDERFLOW_GUARD = 2.0 ** -100
INT_MIN = -2 ** 31
I16_MIN = -2 ** 15
COUNT_ROWS = 32
DIGIT_ROWS = 64
SCORE_GROUP = 4
SP_WINDOW_GROUP = 3
DF_WINDOW_GROUP = 1
DF_GROUP = 4
SB_FAR = 2
SB_WINDOW_GROUP = 3
SP_FAR = 2
DF_FAR = 4

G_SB, G_SP, G_DF = 0, 1024, 2048
Q_SB, K_SB, V_SB = 3072, 3328, 3584
Q_SP, K_SP, V_SP = 3840, 4096, 4352
Q_IX, KW_IX = 4608, 4864
Q_DF, K_DF, V_DF = 5120, 5632, 6144
NZ = 6656
W_IX_LANE = IDX_DIM

VMEM_LIMIT = 56 * 1024 * 1024

F32 = jnp.float32
BF16 = jnp.bfloat16
NT_DIMS = (((1,), (1,)), ((), ()))


def _nt_dot(a, b):
    return lax.dot_general(a, b, NT_DIMS, preferred_element_type=F32)


def _dot(a, b):
    return jnp.dot(a, b, preferred_element_type=F32)


def _params(*sem):
    return pltpu.CompilerParams(dimension_semantics=sem, vmem_limit_bytes=VMEM_LIMIT)


def _pick_rows(m, cap):
    for c in (2048, 1024, 512, 256):
        if c <= cap and m % c == 0:
            return c
    raise ValueError(f"row count {m} is not a multiple of {QB}")


def _bucket_np(rel):
    n = np.maximum(rel, 0)
    max_exact = N_BUCKETS // 2
    nf = np.maximum(n, 1).astype(np.float32)
    large = max_exact + (np.log(nf / np.float32(max_exact)) / np.float32(math.log(MAX_DISTANCE / max_exact))
                         * np.float32(N_BUCKETS - max_exact)).astype(np.int32)
    return np.where(n < max_exact, n, np.minimum(large, N_BUCKETS - 1)).astype(np.int32)


def _bucket_tile():
    tq = np.arange(QB)[:, None]
    c = np.arange(2 * QB)[None, :]
    return _bucket_np(tq - c + QB)


def _sb_prefix_matrix():
    sp = np.arange(2 * BLK)[:, None] % BLK
    c = np.arange(2 * BLK)[None, :]
    return np.where(c < BLK, sp > c, True).astype(np.float32)


def _ix_select_matrices():
    c = np.arange(QB)[:, None]
    col = np.arange(QB)[None, :]
    rep = ((c < IDX_DIM) & (c == col % IDX_DIM)).astype(np.float32)
    col8 = np.arange(IDX_HEADS * BLK)[None, :]
    bcast = (c == W_IX_LANE + col8 // BLK).astype(np.float32)
    before = (col < c).astype(np.float32)
    return rep, bcast, before


def _lane_iota(shape):
    return lax.broadcasted_iota(jnp.int32, shape, len(shape) - 1)


def _row_iota(shape):
    return lax.broadcasted_iota(jnp.int32, shape, 0)


def _head_rmsnorm128(xf, gain):
    lo = _lane_iota((1, BLK)) < HEAD_DIM
    ss = xf * xf
    s_lo = jnp.sum(jnp.where(lo, ss, 0.0), axis=-1, keepdims=True)
    s_hi = jnp.sum(jnp.where(lo, 0.0, ss), axis=-1, keepdims=True)
    ms = jnp.where(lo, s_lo, s_hi) * (1.0 / HEAD_DIM)
    return xf * lax.rsqrt(ms + EPS) * gain


def _bias_tile(tab_ref, bk, head):
    far = tab_ref[N_BUCKETS - 1, head]
    acc = jnp.zeros(bk.shape, F32)
    for b in range(N_BUCKETS - 1):
        acc = jnp.where(bk == b, (tab_ref[b, head] - far) * LOG2E, acc)
    return acc


def _softmax_piece_t(st, vt, m_ref, l_ref, acc_ref):
    m_old = m_ref[0:1]
    m_new = jnp.maximum(m_old, jnp.max(st, axis=0, keepdims=True))
    alpha = jnp.exp2(m_old - m_new)
    p = jnp.exp2(st - m_new)
    l_ref[0:1] = alpha * l_ref[0:1] + jnp.sum(p, axis=0, keepdims=True)
    m_ref[0:1] = m_new
    acc_ref[...] = alpha * acc_ref[...] + _dot(vt, p.astype(BF16))


def _online_pieces_t(pieces, m_refs, l_refs, acc_refs):
    for st, vt, c in pieces:
        _softmax_piece_t(st, vt, m_refs[c], l_refs[c], acc_refs[c])


def _bounded_pieces_t(pieces, bounds, l_refs, acc_refs):
    ps = [jnp.exp2(st - bounds[c]) for st, _, c in pieces]
    for c in sorted({c for _, _, c in pieces}):
        mine = [(vt, p) for (_, vt, cc), p in zip(pieces, ps) if cc == c]
        l_refs[c][...] += functools.reduce(
            jnp.add, [jnp.sum(p.reshape(p.shape[0] // 8, 8, p.shape[1]), axis=0) for _, p in mine])
        acc_refs[c][...] += functools.reduce(jnp.add, [_dot(vt, p.astype(BF16)) for vt, p in mine])


def _attend_windows(i, width, group, far_fn, near_fn, step):
    n_far = jnp.maximum(i - 1, 0)
    n_win = (n_far + width - 1) // width

    def far(p):
        hi = n_far - p * width
        return far_fn(jnp.maximum(hi - width, 0), hi)

    def body(g, carry):
        step(sum([far(group * g + n) for n in range(group)], []))
        return carry

    lax.fori_loop(0, n_win // group, body, 0)
    first_left = group * (n_win // group)
    for n_left in range(group):
        @pl.when((n_win - first_left == n_left) & (i >= 1))
        def _(n_left=n_left):
            step(sum([far(first_left + n) for n in range(n_left)], []) + near_fn(i - 1, 2))

    @pl.when(i == 0)
    def _():
        step(near_fn(0, 1))


def _for_far_pieces(n_far, width, group, piece_fn):
    n_win = (n_far + width - 1) // width

    def window(p):
        hi = n_far - p * width
        piece_fn(jnp.maximum(hi - width, 0), hi)

    def body(g, carry):
        for n in range(group):
            window(group * g + n)
        return carry

    lax.fori_loop(0, n_win // group, body, 0)
    first_left = group * (n_win // group)
    for n_left in range(1, group):
        @pl.when(n_win - first_left == n_left)
        def _(n_left=n_left):
            for n in range(n_left):
                window(first_left + n)


def _in_proj_kernel(h_ref, g_ref, w_ref, o_ref, *, tn):
    x = h_ref[...]
    ms = jnp.mean(x * x, axis=-1, keepdims=True)
    u = (x * lax.rsqrt(ms + EPS) * g_ref[...]).astype(BF16)
    for c in range(o_ref.shape[1] // tn):
        cols = slice(c * tn, (c + 1) * tn)
        o_ref[:, cols] = _dot(u, w_ref[:, cols]).astype(o_ref.dtype)


def _in_proj(h, gain, w):
    m, d = h.shape
    n = w.shape[1]
    tm = _pick_rows(m, 512)
    return pl.pallas_call(
        functools.partial(_in_proj_kernel, tn=512),
        out_shape=jax.ShapeDtypeStruct((m, n), BF16),
        grid=(m // tm,),
        in_specs=[pl.BlockSpec((tm, d), lambda i: (i, 0)),
                  pl.BlockSpec((1, d), lambda i: (0, 0)),
                  pl.BlockSpec((d, n), lambda i: (0, 0), pipeline_mode=pl.Buffered(1))],
        out_specs=pl.BlockSpec((tm, n), lambda i: (i, 0)),
        compiler_params=_params("parallel"),
        name="in_proj",
    )(h, gain.reshape(1, d), w)


def _sb_kernel(q_ref, k_ref, v_ref, u_ref, o_ref, tot_ref, acc_ref, *, nq, tail_q):
    i = pl.program_id(1)
    nh = SB_HEADS

    def query_block(qt):
        head_of_lane = _lane_iota((qt, nh * HEAD_DIM)) // HEAD_DIM
        q = q_ref[0, :qt].astype(F32) * QK_SCALE
        qs = jnp.concatenate([jnp.where(head_of_lane == h, q, 0.0) for h in range(nh)], axis=0).astype(BF16)
        tot = tot_ref.at[:nh * qt]
        acc = acc_ref.at[:nh * qt]
        tot[...] = jnp.zeros_like(tot)
        acc[...] = jnp.zeros_like(acc)

        def piece(start, n_blocks, hi=None):
            r = pl.multiple_of(start * QB, QB)
            z_all = _nt_dot(qs, k_ref[0, pl.ds(r, n_blocks * QB), :])
            if hi is not None:
                col = _lane_iota((1, n_blocks * QB))
                z_all = z_all + jnp.where(col < (hi - start) * QB, 0.0, -jnp.inf)
            run = tot[...]
            n_sub = n_blocks * QB // BLK
            ws = [None] * n_sub
            for c in reversed(range(n_sub)):
                z = z_all[:, c * BLK:(c + 1) * BLK]
                sp = jnp.maximum(z, 0.0) + jnp.log2(1.0 + jnp.exp2(-jnp.abs(z)))
                l1m = -sp
                if hi is None:
                    mask = (_lane_iota(z.shape) + c * BLK) < (_row_iota(z.shape) & (qt - 1))
                    l1m = jnp.where(mask, l1m, 0.0)
                l1m_hi = l1m.astype(BF16)
                l1m_lo = (l1m - l1m_hi.astype(F32)).astype(BF16)
                rs = _dot(jnp.concatenate([l1m_hi, l1m_lo], axis=1), u_ref[...])
                w = jnp.exp2((z - sp) + rs[:, :BLK] + run)
                if hi is None:
                    w = jnp.where(mask, w, 0.0)
                ws[c] = w.astype(BF16)
                run = run + rs[:, BLK:]
            tot[...] = run
            acc[...] += _dot(jnp.concatenate(ws, axis=1), v_ref[0, pl.ds(r, n_blocks * QB), :])

        piece(i, 1)
        _for_far_pieces(i, SB_FAR, SB_WINDOW_GROUP, lambda start, hi: piece(start, SB_FAR, hi))

        a = acc[...]
        out = a[:qt]
        for h in range(1, nh):
            out = jnp.where(head_of_lane == h, a[h * qt:(h + 1) * qt], out)
        o_ref[0, :qt] = out.astype(o_ref.dtype)
        if qt < QB:
            o_ref[0, qt:] = jnp.zeros((QB - qt, nh * HEAD_DIM), o_ref.dtype)

    if tail_q == QB:
        query_block(QB)
    else:
        @pl.when(i < nq - 1)
        def _():
            query_block(QB)

        @pl.when(i == nq - 1)
        def _():
            query_block(tail_q)


def _tail_rows(t, tp, granule):
    real = t - (tp - QB)
    return min(QB, -(-real // granule) * granule)


def _sb_attn(z3, u_mat, t):
    b, tp, _ = z3.shape
    nq = tp // QB
    assert nq >= SB_FAR
    w = SB_HEADS * HEAD_DIM
    return pl.pallas_call(
        functools.partial(_sb_kernel, nq=nq, tail_q=_tail_rows(t, tp, 64)),
        out_shape=jax.ShapeDtypeStruct((b, tp, w), BF16),
        grid=(b, nq),
        in_specs=[pl.BlockSpec((1, QB, w), lambda bb, i: (bb, i, Q_SB // w)),
                  pl.BlockSpec((1, tp, w), lambda bb, i: (bb, 0, K_SB // w)),
                  pl.BlockSpec((1, tp, w), lambda bb, i: (bb, 0, V_SB // w)),
                  pl.BlockSpec((2 * BLK, 2 * BLK), lambda bb, i: (0, 0))],
        out_specs=pl.BlockSpec((1, QB, w), lambda bb, i: (bb, i, 0)),
        scratch_shapes=[pltpu.VMEM((SB_HEADS * QB, BLK), F32),
                        pltpu.VMEM((SB_HEADS * QB, w), F32)],
        compiler_params=_params("parallel", "arbitrary"),
        name="sb_attn",
    )(z3, z3, z3, u_mat)


def _dft_kernel(tab_ref, q_ref, k_ref, v_ref, qg_ref, kg_ref, bkt_ref, lamv_ref, sub_ref, o_ref,
                kn_ref, vt_ref, bias_ref, kmax_ref, m_ref, l_ref, lsum_ref, acc_ref,
                *, nq, tail_q, lam_init, head_off):
    g = pl.program_id(1)
    i = pl.program_id(2)
    lo = _lane_iota((1, BLK)) < HEAD_DIM
    heads = range(DF_GROUP)
    lanes = [slice(c * BLK, (c + 1) * BLK) for c in heads]

    @pl.when(i == 0)
    def _prep():
        kmax_ref[...] = jnp.zeros_like(kmax_ref)

        def kbody(j, carry):
            r = pl.multiple_of(j * QB, QB)
            for c in heads:
                kf = k_ref[0, pl.ds(r, QB), lanes[c]].astype(F32)
                kn = _head_rmsnorm128(kf, kg_ref[...]).astype(BF16)
                kn_ref[pl.ds(r, QB), lanes[c]] = kn
                ksq = kn.astype(F32) ** 2
                half_norms = jnp.maximum(jnp.sum(jnp.where(lo, ksq, 0.0), axis=-1, keepdims=True),
                                         jnp.sum(jnp.where(lo, 0.0, ksq), axis=-1, keepdims=True))
                kmax_ref[c] = jnp.maximum(kmax_ref[c], jnp.max(half_norms))
            vt_ref[j] = v_ref[0, pl.ds(r, QB), :].astype(F32).T.astype(BF16)
            return carry

        lax.fori_loop(0, nq, kbody, 0)
        for c in heads:
            head = head_off + g * DF_GROUP + c
            bias_ref[c] = _bias_tile(tab_ref, bkt_ref[...], head)
            bias_max = jnp.float32(0.0)
            for b in range(N_BUCKETS - 1):
                bias_max = jnp.maximum(bias_max, (tab_ref[b, head] - tab_ref[N_BUCKETS - 1, head]) * LOG2E)
            kmax_ref[c, 1:2] = jnp.full((1, BLK), bias_max, F32)

    def query_block(qt):
        m, l, lsum, acc = ([ref.at[c, :, :2 * qt] for c in heads] for ref in (m_ref, l_ref, lsum_ref, acc_ref))
        qs, bound = [], []
        for c in heads:
            qn = _head_rmsnorm128(q_ref[0, :qt, lanes[c]].astype(F32), qg_ref[...]) * QK_SCALE
            qs.append(jnp.concatenate([jnp.where(lo, qn, 0.0), jnp.where(lo, 0.0, qn)], axis=0).astype(BF16))
            q_sq = _nt_dot(jnp.ones((8, BLK), BF16), (qs[c].astype(F32) ** 2).astype(BF16))[0:1]
            bound.append(jnp.sqrt(q_sq * kmax_ref[c, 0:1, 0:1]) * BOUND_SLACK
                         + (kmax_ref[c, 1:2, 0:1] + BOUND_SLACK))

        def scores(c, first, n):
            rows = pl.ds(pl.multiple_of(first * QB, QB), n * QB)
            return _nt_dot(kn_ref[rows, lanes[c]], qs[c])

        def values(c, first, n):
            return jnp.concatenate([vt_ref[first + b, lanes[c], :] for b in range(n)], axis=1)

        def far_piece(start, hi):
            pieces = []
            for c in heads:
                st = scores(c, start, DF_FAR)
                new_keys = _row_iota(st.shape) < (hi - start) * QB
                pieces.append((jnp.where(new_keys, st, -jnp.inf), values(c, start, DF_FAR), c))
            return pieces

        def near_piece(first, n):
            pieces = []
            for c in heads:
                b = bias_ref[c, (2 - n) * QB:, :qt]
                st = scores(c, first, n) + jnp.concatenate([b, b], axis=1)
                causal = (_row_iota(st.shape) - (n - 1) * QB) <= (_lane_iota(st.shape) & (qt - 1))
                pieces.append((jnp.where(causal, st, -jnp.inf), values(c, first, n), c))
            return pieces

        def attend(step):
            _attend_windows(i, DF_FAR, DF_WINDOW_GROUP, far_piece, near_piece, step)

        for c in heads:
            l[c][...] = jnp.zeros_like(l[c])
            acc[c][...] = jnp.zeros_like(acc[c])
        attend(lambda pieces: _bounded_pieces_t(pieces, bound, l, acc))
        for c in heads:
            lsum[c][0:1] = jnp.sum(l[c][...], axis=0, keepdims=True)

        @pl.when(functools.reduce(jnp.minimum, [jnp.min(lsum[c][0:1]) for c in heads]) < UNDERFLOW_GUARD)
        def _():
            for c in heads:
                m[c][...] = jnp.full(m[c].shape, M_INIT, F32)
                l[c][...] = jnp.zeros_like(l[c])
                acc[c][...] = jnp.zeros_like(acc[c])
            attend(lambda pieces: _online_pieces_t(pieces, m, l, acc))
            for c in heads:
                lsum[c][0:1] = l[c][0:1]

        lv = lamv_ref[...]
        lam = (jnp.exp(jnp.sum(lv[0:1] * lv[1:2], axis=-1, keepdims=True))
               - jnp.exp(jnp.sum(lv[2:3] * lv[3:4], axis=-1, keepdims=True)) + lam_init)
        for c in heads:
            a = acc[c][...] / lsum[c][0:1]
            y = (a[:, :qt] - lam * a[:, qt:]).T
            y = y * lax.rsqrt(jnp.mean(y * y, axis=-1, keepdims=True) + EPS) * sub_ref[...]
            o_ref[0, :qt, lanes[c]] = (y * (1.0 - lam_init)).astype(o_ref.dtype)
        if qt < QB:
            o_ref[0, qt:] = jnp.zeros((QB - qt, DF_GROUP * BLK), o_ref.dtype)

    if tail_q == QB:
        query_block(QB)
    else:
        @pl.when(i < nq - 1)
        def _():
            query_block(QB)

        @pl.when(i == nq - 1)
        def _():
            query_block(tail_q)


def _dft_attn(z3, rel_bias, qg, kg, bkt_t, lamv, subln, lam_init, t):
    b, tp, _ = z3.shape
    nq = tp // QB
    assert nq >= DF_FAR
    kern = functools.partial(_dft_kernel, nq=nq, tail_q=_tail_rows(t, tp, 64), lam_init=lam_init,
                             head_off=SP_HEADS)
    vec = lambda bb, g, i: (0, 0)
    w = DF_GROUP * BLK
    per_head = lambda rows, cols: pltpu.VMEM((DF_GROUP, rows, cols), F32)
    return pl.pallas_call(
        kern,
        out_shape=jax.ShapeDtypeStruct((b, tp, DF_HEADS * BLK), BF16),
        grid=(b, DF_HEADS // DF_GROUP, nq),
        in_specs=[pl.BlockSpec(memory_space=pltpu.SMEM),
                  pl.BlockSpec((1, QB, w), lambda bb, g, i: (bb, i, Q_DF // w + g)),
                  pl.BlockSpec((1, tp, w), lambda bb, g, i: (bb, 0, K_DF // w + g)),
                  pl.BlockSpec((1, tp, w), lambda bb, g, i: (bb, 0, V_DF // w + g)),
                  pl.BlockSpec((1, BLK), vec),
                  pl.BlockSpec((1, BLK), vec),
                  pl.BlockSpec((2 * QB, QB), vec),
                  pl.BlockSpec((8, BLK), vec),
                  pl.BlockSpec((1, BLK), vec)],
        out_specs=pl.BlockSpec((1, QB, w), lambda bb, g, i: (bb, i, g)),
        scratch_shapes=[pltpu.VMEM((tp, w), BF16),
                        pltpu.VMEM((nq, w, QB), BF16),
                        per_head(2 * QB, QB),
                        per_head(8, BLK),
                        per_head(8, 2 * QB),
                        per_head(8, 2 * QB),
                        per_head(8, 2 * QB),
                        per_head(BLK, 2 * QB)],
        compiler_params=_params("parallel", "parallel", "arbitrary"),
        name="df_attn",
    )(rel_bias, z3, z3, z3, qg, kg, bkt_t, lamv, subln)


def _sp_kernel(tab_ref, q_ref, k_ref, v_ref, qix_ref, kwq_ref, kwk_ref, qg_ref, kg_ref, bkt_ref,
               rep_ref, bcast_ref, before_ref, o_ref,
               kn_ref, kx_ref, vt_ref, bias_ref, kmax_ref, keys_ref, dig_ref, thr_ref, wb_ref,
               m_ref, l_ref, lsum_ref, acc_ref, *, nq, top_k):
    i = pl.program_id(1)
    nh = SP_HEADS
    w = nh * HEAD_DIM
    lane = _lane_iota((QB, w))
    head_of_lane = lane // HEAD_DIM

    def norm256(xf, g):
        return jnp.concatenate([_head_rmsnorm128(xf[:, :BLK], g[:, :BLK]),
                                _head_rmsnorm128(xf[:, BLK:], g[:, BLK:])], axis=1)

    def dup(x):
        return jnp.concatenate([x, x], axis=1)

    @pl.when(i == 0)
    def _prep():
        kmax_ref[...] = jnp.zeros_like(kmax_ref)

        def kbody(c, carry):
            r = pl.multiple_of(c * QB, QB)
            kn = norm256(k_ref[0, pl.ds(r, QB), :].astype(F32), kg_ref[...]).astype(BF16)
            kn_ref[pl.ds(r, QB), :] = kn
            ksq = kn.astype(F32) ** 2
            head_norms = functools.reduce(jnp.maximum, [
                jnp.sum(jnp.where(head_of_lane == h, ksq, 0.0), axis=-1, keepdims=True) for h in range(nh)])
            kmax_ref[0:1] = jnp.maximum(kmax_ref[0:1], jnp.max(head_norms))
            kx_ref[pl.ds(r, QB), :] = _dot(kwk_ref[0, pl.ds(r, QB), :], rep_ref[...]).astype(BF16)
            vt_ref[c] = v_ref[0, pl.ds(r, QB), :].astype(F32).T.astype(BF16)
            return carry

        lax.fori_loop(0, nq, kbody, 0)
        bias_max = jnp.float32(0.0)
        for h in range(nh):
            bias_ref[:, h * QB:(h + 1) * QB] = _bias_tile(tab_ref, bkt_ref[...], h)
            for b in range(N_BUCKETS - 1):
                bias_max = jnp.maximum(bias_max, (tab_ref[b, h] - tab_ref[N_BUCKETS - 1, h]) * LOG2E)
        kmax_ref[1:2] = jnp.full((1, BLK), bias_max, F32)

    key_causal = _row_iota((QB, QB)) <= _lane_iota((QB, QB))

    wb_ref[...] = _dot(kwq_ref[0], bcast_ref[...])
    qix = qix_ref[0].astype(F32)
    ix_head = lane // IDX_DIM
    qx = jnp.concatenate([jnp.where(ix_head == h, qix, 0.0) for h in range(IDX_HEADS)], axis=0).astype(BF16)

    def score_tiles(blocks, last_is_diagonal):
        dots = [_nt_dot(qx, kx_ref[pl.ds(pl.multiple_of(j * QB, QB), QB), :]) for j in blocks]
        for n, (j, d) in enumerate(zip(blocks, dots)):
            sc = jnp.zeros((QB, QB), F32)
            for h in range(IDX_HEADS):
                sc = sc + dup(wb_ref[:, h * BLK:(h + 1) * BLK]) * jnp.maximum(d[h * QB:(h + 1) * QB], 0.0)
            sc = jnp.where(sc == 0.0, 0.0, sc).T
            bits = lax.bitcast_convert_type(sc, jnp.int32)
            key = jnp.where(bits < 0, bits ^ jnp.int32(0x7FFFFFFF), bits)
            if last_is_diagonal and n == len(blocks) - 1:
                key = jnp.where(key_causal, key, jnp.int32(INT_MIN))
            keys_ref[j] = key
            dig_ref[j] = jnp.right_shift(key, 16).astype(jnp.int16)

    def group_body(g, c):
        score_tiles([SCORE_GROUP * g + n for n in range(SCORE_GROUP)], False)
        return c

    lax.fori_loop(0, i // SCORE_GROUP, group_body, 0)
    first_left = SCORE_GROUP * (i // SCORE_GROUP)
    for n_left in range(1, SCORE_GROUP + 1):
        @pl.when(i - first_left == n_left - 1)
        def _(n_left=n_left):
            score_tiles([first_left + n for n in range(n_left)], True)

    nblk = i + 1

    def count(pred):
        def cbody(j, c):
            hit = jnp.where(pred(keys_ref[j]), 1.0, 0.0)
            return c + jnp.sum(hit.reshape(QB // COUNT_ROWS, COUNT_ROWS, QB), axis=0)

        c = lax.fori_loop(0, nblk, cbody, jnp.zeros((COUNT_ROWS, QB), F32))
        return jnp.sum(c, axis=0, keepdims=True)

    def count16(pred):
        def cbody(j, c):
            hit = jnp.where(pred(dig_ref[j]), jnp.int16(1), jnp.int16(0)).reshape(QB // DIGIT_ROWS, DIGIT_ROWS, QB)
            return c + functools.reduce(jnp.add, [hit[r] for r in range(QB // DIGIT_ROWS)])

        c = lax.fori_loop(0, nblk, cbody, jnp.zeros((DIGIT_ROWS, QB), jnp.int16))
        return jnp.sum(c.astype(F32), axis=0, keepdims=True)

    n_all = jnp.zeros((1, QB), F32) + (nblk * QB).astype(F32)

    def bisect16(need):
        def bit_body(b, carry):
            cur, cnt_cur = carry
            cand = cur + jnp.left_shift(jnp.int32(1), 15 - b)
            cand16 = cand.astype(jnp.int16)
            cnt = count16(lambda d: d >= cand16)
            ok = cnt >= need
            return jnp.where(ok, cand, cur), jnp.where(ok, cnt, cnt_cur)

        return lax.fori_loop(0, 16, bit_body, (jnp.full((1, QB), I16_MIN, jnp.int32), n_all))

    t_hi, c_ge_hi = bisect16(float(top_k))
    t_hi16 = t_hi.astype(jnp.int16)
    c_gt_hi = count16(lambda d: d > t_hi16)
    base = jnp.left_shift(t_hi, 16)

    def low_digits(j, c):
        y = keys_ref[j] - base
        dig_ref[j] = jnp.where(jnp.right_shift(y, 16) == 0, y + I16_MIN, I16_MIN).astype(jnp.int16)
        return c

    lax.fori_loop(0, nblk, low_digits, 0)
    t_lo, c_ge_lo = bisect16(float(top_k) - c_gt_hi)
    thr = base + (t_lo - I16_MIN)
    cge = c_gt_hi + jnp.where(t_lo > I16_MIN, c_ge_lo, c_ge_hi - c_gt_hi)
    thr_ref[...] = jnp.broadcast_to(thr, thr_ref.shape)

    tie = jnp.where((cge > float(top_k)) & (thr > INT_MIN), 1, 0)

    @pl.when(jnp.max(tie) > 0)
    def _ties():
        need = float(top_k) - count(lambda k: k > thr)

        def tbody(j, run):
            kj = keys_ref[j]
            eq = kj == thr
            eqf = jnp.where(eq, 1.0, 0.0)
            rank = run + _dot(before_ref[...], eqf.astype(BF16))
            keep = jnp.where(kj > thr, 1, jnp.where(eq & (rank < need), 1, -1))
            keys_ref[j] = keep.astype(jnp.int32)
            return run + jnp.sum(eqf, axis=0, keepdims=True)

        lax.fori_loop(0, nblk, tbody, jnp.zeros((1, QB), F32))
        thr_ref[...] = jnp.zeros_like(thr_ref)

    qn = norm256(q_ref[0].astype(F32), qg_ref[...]) * QK_SCALE
    qs = jnp.concatenate([jnp.where(head_of_lane == h, qn, 0.0) for h in range(nh)], axis=0).astype(BF16)
    q_sq = _nt_dot(jnp.ones((8, w), BF16), (qs.astype(F32) ** 2).astype(BF16))[0:1]
    bound = jnp.sqrt(q_sq * kmax_ref[0:1, 0:1]) * BOUND_SLACK + (kmax_ref[1:2, 0:1] + BOUND_SLACK)

    thr_sel = thr_ref[0:1, :]

    def as_mask(sel):
        return lax.bitcast_convert_type(jnp.where(sel, 0.0, -jnp.inf), jnp.int32)

    def mask_body(j, c):
        keys_ref[j] = as_mask(keys_ref[j] >= thr_sel)
        return c

    lax.fori_loop(0, i, mask_body, 0)
    keys_ref[i] = as_mask((keys_ref[i] >= thr_sel) & key_causal)

    def scores(first, n):
        return _nt_dot(kn_ref[pl.ds(pl.multiple_of(first * QB, QB), n * QB), :], qs)

    def values(first, n):
        return jnp.concatenate([vt_ref[first + c] for c in range(n)], axis=1)

    def selection(first, n):
        return jnp.concatenate([lax.bitcast_convert_type(keys_ref[first + c], F32) for c in range(n)],
                               axis=0)

    def far_piece(start, hi):
        m = selection(start, SP_FAR)
        m = jnp.where(_row_iota(m.shape) < (hi - start) * QB, m, -jnp.inf)
        return [(scores(start, SP_FAR) + jnp.concatenate([m] * nh, axis=1), values(start, SP_FAR), 0)]

    def near_piece(first, n):
        m = selection(first, n)
        return [(scores(first, n) + (jnp.concatenate([m] * nh, axis=1) + bias_ref[(2 - n) * QB:, :]),
                 values(first, n), 0)]

    def attend(step):
        _attend_windows(i, SP_FAR, SP_WINDOW_GROUP, far_piece, near_piece, step)

    l_ref[...] = jnp.zeros_like(l_ref)
    acc_ref[...] = jnp.zeros_like(acc_ref)
    attend(lambda pieces: _bounded_pieces_t(pieces, [bound], [l_ref], [acc_ref]))
    lsum_ref[0:1] = jnp.sum(l_ref[...], axis=0, keepdims=True)

    @pl.when(jnp.min(lsum_ref[0:1]) < UNDERFLOW_GUARD)
    def _():
        m_ref[...] = jnp.full(m_ref.shape, M_INIT, F32)
        l_ref[...] = jnp.zeros_like(l_ref)
        acc_ref[...] = jnp.zeros_like(acc_ref)
        attend(lambda pieces: _online_pieces_t(pieces, [m_ref], [l_ref], [acc_ref]))
        lsum_ref[0:1] = l_ref[0:1]

    a = acc_ref[...] / lsum_ref[0:1]
    out_t = jnp.concatenate([a[h * HEAD_DIM:(h + 1) * HEAD_DIM, h * QB:(h + 1) * QB] for h in range(nh)], axis=0)
    o_ref[0] = out_t.T.astype(o_ref.dtype)


def _sp_attn(z3, rel_bias, qg, kg, bkt_t, rep, bcast, before, top_k):
    b, tp, _ = z3.shape
    nq = tp // QB
    assert nq >= SP_FAR
    w = SP_HEADS * HEAD_DIM
    kern = functools.partial(_sp_kernel, nq=nq, top_k=top_k)
    c2 = lambda bb, i: (0, 0)
    return pl.pallas_call(
        kern,
        out_shape=jax.ShapeDtypeStruct((b, tp, w), BF16),
        grid=(b, nq),
        in_specs=[pl.BlockSpec(memory_space=pltpu.SMEM),
                  pl.BlockSpec((1, QB, w), lambda bb, i: (bb, i, Q_SP // w)),
                  pl.BlockSpec((1, tp, w), lambda bb, i: (bb, 0, K_SP // w)),
                  pl.BlockSpec((1, tp, w), lambda bb, i: (bb, 0, V_SP // w)),
                  pl.BlockSpec((1, QB, w), lambda bb, i: (bb, i, Q_IX // w)),
                  pl.BlockSpec((1, QB, w), lambda bb, i: (bb, i, KW_IX // w)),
                  pl.BlockSpec((1, tp, w), lambda bb, i: (bb, 0, KW_IX // w)),
                  pl.BlockSpec((1, w), c2),
                  pl.BlockSpec((1, w), c2),
                  pl.BlockSpec((2 * QB, QB), c2),
                  pl.BlockSpec((w, w), c2),
                  pl.BlockSpec((w, IDX_HEADS * BLK), c2),
                  pl.BlockSpec((QB, QB), c2)],
        out_specs=pl.BlockSpec((1, QB, w), lambda bb, i: (bb, i, 0)),
        scratch_shapes=[pltpu.VMEM((tp, w), BF16),
                        pltpu.VMEM((tp, w), BF16),
                        pltpu.VMEM((nq, w, QB), BF16),
                        pltpu.VMEM((2 * QB, SP_HEADS * QB), F32),
                        pltpu.VMEM((8, BLK), F32),
                        pltpu.VMEM((nq, QB, QB), jnp.int32),
                        pltpu.VMEM((nq, QB, QB), jnp.int16),
                        pltpu.VMEM((8, QB), jnp.int32),
                        pltpu.VMEM((QB, IDX_HEADS * BLK), F32),
                        pltpu.VMEM((8, SP_HEADS * QB), F32),
                        pltpu.VMEM((8, SP_HEADS * QB), F32),
                        pltpu.VMEM((8, SP_HEADS * QB), F32),
                        pltpu.VMEM((w, SP_HEADS * QB), F32)],
        compiler_params=_params("parallel", "arbitrary"),
        name="sp_attn",
    )(rel_bias, z3, z3, z3, z3, z3, z3, qg, kg, bkt_t, rep, bcast, before)


def _mix_kernel(h_ref, gsb_ref, gsp_ref, gdf_ref, bg_ref, ysb_ref, ysp_ref, ydf_ref,
                wsb_ref, wsp_ref, wdf_ref, wo_ref, o_ref):
    def branch(g_ref, k, y_ref, w_ref):
        gate = jax.nn.sigmoid(g_ref[...].astype(F32) + bg_ref[:, k * D_MODEL:(k + 1) * D_MODEL])
        return gate * _dot(y_ref[...], w_ref[...])

    merged = (branch(gsb_ref, 0, ysb_ref, wsb_ref) + branch(gsp_ref, 1, ysp_ref, wsp_ref)
              + branch(gdf_ref, 2, ydf_ref, wdf_ref))
    o_ref[...] = h_ref[...] + _dot(merged.astype(BF16), wo_ref[...])


def _mix_out(h, z, b_gate, y_sb, y_sp, y_df, w_sb, w_sp, w_df, w_o):
    m, d = h.shape
    tm = _pick_rows(m, 512)
    row = lambda i: (i, 0)
    fixed = lambda i: (0, 0)
    return pl.pallas_call(
        _mix_kernel,
        out_shape=jax.ShapeDtypeStruct((m, d), F32),
        grid=(m // tm,),
        in_specs=[pl.BlockSpec((tm, d), row),
                  pl.BlockSpec((tm, d), lambda i: (i, G_SB // D_MODEL)),
                  pl.BlockSpec((tm, d), lambda i: (i, G_SP // D_MODEL)),
                  pl.BlockSpec((tm, d), lambda i: (i, G_DF // D_MODEL)),
                  pl.BlockSpec((1, 3 * d), fixed),
                  pl.BlockSpec((tm, y_sb.shape[1]), row),
                  pl.BlockSpec((tm, y_sp.shape[1]), row),
                  pl.BlockSpec((tm, y_df.shape[1]), row),
                  pl.BlockSpec(w_sb.shape, fixed),
                  pl.BlockSpec(w_sp.shape, fixed),
                  pl.BlockSpec(w_df.shape, fixed),
                  pl.BlockSpec(w_o.shape, fixed)],
        out_specs=pl.BlockSpec((tm, d), row),
        compiler_params=_params("parallel"),
        name="mix_out",
    )(h, z, z, z, b_gate.reshape(1, 3 * d), y_sb, y_sp, y_df, w_sb, w_sp, w_df, w_o)


def _mix_ffn_kernel(h_ref, gsb_ref, gsp_ref, gdf_ref, bg_ref, ysb_ref, ysp_ref, ydf_ref,
                    wsb_ref, wsp_ref, wdf_ref, wo_ref, g_ref, wu_ref, cw_ref, cb_ref, wd_ref, o_ref,
                    gbuf_ref, carry_ref, *, tm, tp, tf):
    def branch(gate_ref, k, y_ref, w_ref):
        gate = jax.nn.sigmoid(gate_ref[...].astype(F32) + bg_ref[:, k * D_MODEL:(k + 1) * D_MODEL])
        return gate * _dot(y_ref[...], w_ref[...])

    merged = (branch(gsb_ref, 0, ysb_ref, wsb_ref) + branch(gsp_ref, 1, ysp_ref, wsp_ref)
              + branch(gdf_ref, 2, ydf_ref, wdf_ref))
    x = h_ref[...] + _dot(merged.astype(BF16), wo_ref[...])
    _ffn_block(x, g_ref, wu_ref, cw_ref, cb_ref, wd_ref, o_ref, gbuf_ref, carry_ref, tm=tm, tp=tp, tf=tf)


def _ffn_kernel(h_ref, g_ref, wu_ref, cw_ref, cb_ref, wd_ref, o_ref, gbuf_ref, carry_ref, *, tm, tp, tf):
    _ffn_block(h_ref[...], g_ref, wu_ref, cw_ref, cb_ref, wd_ref, o_ref, gbuf_ref, carry_ref, tm=tm, tp=tp, tf=tf)


def _ffn_block(x, g_ref, wu_ref, cw_ref, cb_ref, wd_ref, o_ref, gbuf_ref, carry_ref, *, tm, tp, tf):
    r = pl.program_id(0)
    ms = jnp.mean(x * x, axis=-1, keepdims=True)
    u = (x * lax.rsqrt(ms + EPS) * g_ref[...]).astype(BF16)

    @pl.when(r == 0)
    def _():
        carry_ref[...] = jnp.zeros_like(carry_ref)

    seq_start = lax.rem(tp - lax.rem(r * tm, tp), tp)
    local = lax.broadcasted_iota(jnp.int32, (tm, 1), 0)
    tap1 = local != seq_start
    tap2 = tap1 & (local != seq_start + 1)
    out = x
    for f in range(D_FF // tf):
        cols = slice(f * tf, (f + 1) * tf)
        gate = _dot(u, wu_ref[:, cols])
        val = _dot(u, wu_ref[:, D_FF + f * tf:D_FF + (f + 1) * tf])
        gbuf_ref[0:8] = carry_ref[f]
        gbuf_ref[8:8 + tm] = gate
        carry_ref[f] = gate[tm - 8:tm]
        g1 = jnp.where(tap1, gbuf_ref[7:7 + tm], 0.0)
        g2 = jnp.where(tap2, gbuf_ref[6:6 + tm], 0.0)
        conv = cb_ref[:, cols] + cw_ref[0:1, cols] * g2 + cw_ref[1:2, cols] * g1 + cw_ref[2:3, cols] * gate
        act = conv * jax.nn.sigmoid(conv) * val
        out = out + _dot(act.astype(BF16), wd_ref[cols, :])
    o_ref[...] = out


def _ffn(h, gain, w_up, conv_w, conv_b, w_down, tp):
    m, d = h.shape
    tm = _pick_rows(m, 512)
    assert tm <= tp
    tf = D_FF // 2
    nf = D_FF // tf
    kern = functools.partial(_ffn_kernel, tm=tm, tp=tp, tf=tf)
    fixed = lambda r: (0, 0)
    resident = pl.Buffered(1)
    return pl.pallas_call(
        kern,
        out_shape=jax.ShapeDtypeStruct((m, d), F32),
        grid=(m // tm,),
        in_specs=[pl.BlockSpec((tm, d), lambda r: (r, 0)),
                  pl.BlockSpec((1, d), fixed),
                  pl.BlockSpec((d, 2 * D_FF), fixed, pipeline_mode=resident),
                  pl.BlockSpec((8, D_FF), fixed),
                  pl.BlockSpec((1, D_FF), fixed),
                  pl.BlockSpec((D_FF, d), fixed, pipeline_mode=resident)],
        out_specs=pl.BlockSpec((tm, d), lambda r: (r, 0)),
        scratch_shapes=[pltpu.VMEM((tm + 8, tf), F32),
                        pltpu.VMEM((nf, 8, tf), F32)],
        compiler_params=_params("arbitrary"),
        name="conv_ffn",
    )(h, gain.reshape(1, d), w_up, conv_w, conv_b.reshape(1, D_FF), w_down)


def _mix_ffn(h, z, b_gate, y_sb, y_sp, y_df, w_sb, w_sp, w_df, w_o, gain, w_up, conv_w, conv_b, w_down, tp):
    m, d = h.shape
    tm = _pick_rows(m, 512)
    assert tm <= tp
    tf = D_FF // 2
    nf = D_FF // tf
    row = lambda r: (r, 0)
    fixed = lambda r: (0, 0)
    resident = pl.Buffered(1)
    weight = lambda w: pl.BlockSpec(w.shape, fixed, pipeline_mode=resident)
    return pl.pallas_call(
        functools.partial(_mix_ffn_kernel, tm=tm, tp=tp, tf=tf),
        out_shape=jax.ShapeDtypeStruct((m, d), F32),
        grid=(m // tm,),
        in_specs=[pl.BlockSpec((tm, d), row),
                  pl.BlockSpec((tm, d), lambda r: (r, G_SB // D_MODEL)),
                  pl.BlockSpec((tm, d), lambda r: (r, G_SP // D_MODEL)),
                  pl.BlockSpec((tm, d), lambda r: (r, G_DF // D_MODEL)),
                  pl.BlockSpec((1, 3 * d), fixed),
                  pl.BlockSpec((tm, y_sb.shape[1]), row),
                  pl.BlockSpec((tm, y_sp.shape[1]), row),
                  pl.BlockSpec((tm, y_df.shape[1]), row),
                  weight(w_sb), weight(w_sp), weight(w_df), weight(w_o),
                  pl.BlockSpec((1, d), fixed),
                  weight(w_up),
                  pl.BlockSpec((8, D_FF), fixed),
                  pl.BlockSpec((1, D_FF), fixed),
                  weight(w_down)],
        out_specs=pl.BlockSpec((tm, d), row),
        scratch_shapes=[pltpu.VMEM((tm + 8, tf), F32),
                        pltpu.VMEM((nf, 8, tf), F32)],
        compiler_params=_params("arbitrary"),
        name="mix_ffn",
    )(h, z, z, z, b_gate.reshape(1, 3 * d), y_sb, y_sp, y_df, w_sb, w_sp, w_df, w_o,
      gain.reshape(1, d), w_up, conv_w, conv_b.reshape(1, D_FF), w_down)


def _permute_w_in(w):
    n_attn = KW_IX - Q_SB + IDX_DIM + IDX_HEADS
    n_gate = 3 * D_MODEL
    n_df = 3 * DF_HEADS * 2 * HEAD_DIM
    gates = w[:, n_attn + n_df:]
    attn = w[:, :n_attn]
    pad = jnp.zeros((w.shape[0], Q_DF - Q_SB - n_attn), w.dtype)
    df = w[:, n_attn:n_attn + n_df]
    out = jnp.concatenate([gates, attn, pad, df], axis=1)
    assert gates.shape[1] == n_gate and out.shape[1] == NZ
    return out


def kernel(x, meta_tokens, rel_bias, attn_norm, w_in, b_gate, q_norm_sp, k_norm_sp, q_norm_df, k_norm_df, lam_q1, lam_k1, lam_q2, lam_k2, subln_df, w_br_sb, w_br_sp, w_br_df, w_out, ffn_norm, w_up, conv_w, conv_b, w_down):
    b, s, d = x.shape
    depth = w_in.shape[0]
    t = N_META + s
    tp = -(-t // QB) * QB
    top_k = min(TOPK_MAX, t // 4)
    m = b * tp

    meta = jnp.broadcast_to(meta_tokens[None].astype(x.dtype), (b, N_META, d))
    h = jnp.concatenate([meta, x, jnp.zeros((b, tp - t, d), x.dtype)], axis=1).reshape(m, d)

    bkt_t = jnp.asarray(np.ascontiguousarray(_bucket_tile().T))
    u_mat = jnp.asarray(_sb_prefix_matrix(), BF16)
    rep, bcast, before = (jnp.asarray(a, BF16) for a in _ix_select_matrices())
    rel_bias = rel_bias.astype(F32)

    for l in range(depth):
        lam_init = 0.8 - 0.6 * math.exp(-0.3 * l)
        z = _in_proj(h, attn_norm[l], _permute_w_in(w_in[l]).astype(BF16))
        z3 = z.reshape(b, tp, NZ)
        y_sb = _sb_attn(z3, u_mat, t)
        y_sp = _sp_attn(z3, rel_bias,
                        jnp.tile(q_norm_sp[l].astype(F32), SP_HEADS).reshape(1, -1),
                        jnp.tile(k_norm_sp[l].astype(F32), SP_HEADS).reshape(1, -1),
                        bkt_t, rep, bcast, before, top_k)
        lamv = jnp.zeros((8, BLK), F32).at[:4, :HEAD_DIM].set(
            jnp.stack([lam_q1[l], lam_k1[l], lam_q2[l], lam_k2[l]]).astype(F32))
        y_df = _dft_attn(z3, rel_bias,
                         jnp.tile(q_norm_df[l].astype(F32), 2).reshape(1, -1),
                         jnp.tile(k_norm_df[l].astype(F32), 2).reshape(1, -1),
                         bkt_t, lamv, subln_df[l].astype(F32).reshape(1, -1), lam_init, t)
        cw = jnp.zeros((8, D_FF), F32).at[:conv_w.shape[1]].set(conv_w[l])
        h = _mix_ffn(h, z, b_gate[l], y_sb.reshape(m, -1), y_sp.reshape(m, -1), y_df.reshape(m, -1),
                     w_br_sb[l].astype(BF16), w_br_sp[l].astype(BF16), w_br_df[l].astype(BF16),
                     w_out[l].astype(BF16), ffn_norm[l], w_up[l].astype(BF16), cw, conv_b[l],
                     w_down[l].astype(BF16), tp)

    return h.reshape(b, tp, d)[:, N_META:t]
```
